```python
import math
import jax, jax.numpy as jnp
from jax import lax
import numpy as np

D_MODEL = 1024
BATCH = 2
SEQ = 16384
DEPTH = 2

HEAD_DIM = 64
ROPE_THETA = 10000.0
NORM_EPS = 1e-6
N_MIXERS = 2
N_MOD = 6
A_Q_HEADS = D_MODEL // HEAD_DIM
A_KV_HEADS = A_Q_HEADS // 4
A_RADIUS = 128
B_GROUPS = ((128, 1), (512, 4), (2048, 16))
B_HEADS = D_MODEL // (2 * HEAD_DIM)
N_EXPERTS = 16
N_EXPERT_GROUPS = 4
TOP_K = 2
D_EXPERT = D_MODEL // 2
N_A_LAYERS = (DEPTH + 1) // 2
N_B_LAYERS = DEPTH // 2

kernel_name = 'hybrid_window_dilated_moe_encoder'


def rms_norm(x, g):
    xf = x.astype(jnp.float32)
    y = xf * lax.rsqrt(jnp.mean(xf * xf, axis=-1, keepdims=True) + NORM_EPS)
    return (y * g.astype(jnp.float32)).astype(x.dtype)


def rope_tables(positions):
    inv_freq = ROPE_THETA ** (-jnp.arange(0, HEAD_DIM, 2, dtype=jnp.float32) / HEAD_DIM)
    ang = positions.astype(jnp.float32)[..., None] * inv_freq
    return jnp.cos(ang)[:, :, None, :], jnp.sin(ang)[:, :, None, :]


def apply_rope(x, cos, sin):
    x1, x2 = jnp.split(x.astype(jnp.float32), 2, axis=-1)
    return jnp.concatenate([x1 * cos - x2 * sin, x2 * cos + x1 * sin], axis=-1).astype(x.dtype)


def banded_attention(q, k, v, radius, sink=None):
    bt, L, hq, dh = q.shape
    hkv = k.shape[2]
    g = hq // hkv
    nb = -(-L // radius)
    lp = nb * radius
    pad = lp - L
    qb = jnp.pad(q, ((0, 0), (0, pad), (0, 0), (0, 0))).reshape(bt, nb, radius, hkv, g, dh)
    kp = jnp.pad(k, ((0, 0), (radius, radius + pad), (0, 0), (0, 0))).reshape(bt, nb + 2, radius, hkv, dh)
    vp = jnp.pad(v, ((0, 0), (radius, radius + pad), (0, 0), (0, 0))).reshape(bt, nb + 2, radius, hkv, dh)
    kb = jnp.concatenate([kp[:, :nb], kp[:, 1:nb + 1], kp[:, 2:nb + 2]], axis=2)
    vb = jnp.concatenate([vp[:, :nb], vp[:, 1:nb + 1], vp[:, 2:nb + 2]], axis=2)
    s = jnp.einsum('bnqhgd,bnkhd->bnhgqk', qb, kb, preferred_element_type=jnp.float32) * (dh ** -0.5)
    qpos = jnp.arange(lp).reshape(nb, radius, 1)
    kpos = (jnp.arange(nb)[:, None, None] - 1) * radius + jnp.arange(3 * radius)[None, None, :]
    valid = (jnp.abs(kpos - qpos) <= radius) & (kpos >= 0) & (kpos < L)
    s = jnp.where(valid[None, :, None, None], s, -jnp.inf)
    m = jnp.max(s, axis=-1, keepdims=True)
    if sink is not None:
        sink_b = sink.astype(jnp.float32).reshape(1, 1, hkv, g, 1, 1)
        m = jnp.maximum(m, sink_b)
    p = jnp.exp(s - m)
    denom = jnp.sum(p, axis=-1, keepdims=True)
    if sink is not None:
        denom = denom + jnp.exp(sink_b - m)
    o = jnp.einsum('bnhgqk,bnkhd->bnqhgd', p.astype(v.dtype), vb, preferred_element_type=jnp.float32)
    o = o / jnp.transpose(denom[..., 0], (0, 1, 4, 2, 3))[..., None]
    lse = jnp.transpose((m + jnp.log(denom))[..., 0], (0, 1, 4, 2, 3)).reshape(bt, lp, hq)[:, :L]
    return o.reshape(bt, lp, hq, dh)[:, :L].astype(q.dtype), lse


def mixer_a(h, cos, sin, w_qkv, w_o, sink):
    b, s, _ = h.shape
    qkv = h @ w_qkv
    q, k, v = jnp.split(qkv, [A_Q_HEADS * HEAD_DIM, (A_Q_HEADS + A_KV_HEADS) * HEAD_DIM], axis=-1)
    q = apply_rope(q.reshape(b, s, A_Q_HEADS, HEAD_DIM), cos, sin)
    k = apply_rope(k.reshape(b, s, A_KV_HEADS, HEAD_DIM), cos, sin)
    v = v.reshape(b, s, A_KV_HEADS, HEAD_DIM)
    o, _ = banded_attention(q, k, v, A_RADIUS, sink)
    return o.reshape(b, s, A_Q_HEADS * HEAD_DIM) @ w_o


def dilated_group_attention(q, k, v, dilation, radius):
    b, s, hh, dh = q.shape
    L = s // dilation

    def to_strided(t):
        return t.reshape(b, L, dilation, hh, dh).transpose(0, 2, 1, 3, 4).reshape(b * dilation, L, hh, dh)

    o, lse = banded_attention(to_strided(q), to_strided(k), to_strided(v), radius)
    o = o.reshape(b, dilation, L, hh, dh).transpose(0, 2, 1, 3, 4).reshape(b, s, hh, dh)
    lse = lse.reshape(b, dilation, L, hh).transpose(0, 2, 1, 3).reshape(b, s, hh)
    return o, lse


def mixer_b(h, cos, sin, w_qkv, w_o):
    b, s, _ = h.shape
    qkv = (h @ w_qkv).reshape(b, s, len(B_GROUPS), 3, B_HEADS, HEAD_DIM)
    outs, lses = [], []
    for gi, (window, dilation) in enumerate(B_GROUPS):
        q = apply_rope(qkv[:, :, gi, 0], cos, sin)
        k = apply_rope(qkv[:, :, gi, 1], cos, sin)
        o, lse = dilated_group_attention(q, k, qkv[:, :, gi, 2], dilation, window // (2 * dilation))
        outs.append(o)
        lses.append(lse)
    w = jax.nn.softmax(jnp.stack(lses, axis=0), axis=0)
    o = jnp.einsum('gbsh,gbshd->bshd', w, jnp.stack(outs, axis=0).astype(jnp.float32)).astype(h.dtype)
    return o.reshape(b, s, B_HEADS * HEAD_DIM) @ w_o


def route(h_flat, router_w, router_bias):
    scores = jax.nn.sigmoid(h_flat.astype(jnp.float32) @ router_w.astype(jnp.float32))
    biased = scores + router_bias.astype(jnp.float32)
    per_group = N_EXPERTS // N_EXPERT_GROUPS
    group_score = jnp.sum(lax.top_k(biased.reshape(-1, N_EXPERT_GROUPS, per_group), 2)[0], axis=-1)
    gsel = jnp.argmax(group_score, axis=-1)
    in_group = (jnp.arange(N_EXPERTS) // per_group)[None, :] == gsel[:, None]
    _, idx = lax.top_k(jnp.where(in_group, biased, -jnp.inf), TOP_K)
    w = jnp.take_along_axis(scores, idx, axis=-1)
    w = w / jnp.sum(w, axis=-1, keepdims=True)
    return jnp.sum(jax.nn.one_hot(idx, N_EXPERTS, dtype=jnp.float32) * w[..., None], axis=1)


def moe_ffn(h, combine, w_gate, w_up, w_down):
    b, s, d = h.shape
    t = h.reshape(b * s, d)
    y = jnp.zeros_like(t)
    for e in range(N_EXPERTS):
        a = jax.nn.silu(t @ w_gate[e]) * (t @ w_up[e])
        y = y + combine[:, e:e + 1].astype(t.dtype) * (a @ w_down[e])
    return y.reshape(b, s, d)


def setup_inputs(seed: int = 0) -> dict:
    key = jax.random.key(seed)
    ks = jax.random.split(key, 18)

    def nrm(k, shape, scale):
        return jax.random.normal(k, shape, jnp.float32) * scale

    x = nrm(ks[0], (BATCH, SEQ, D_MODEL), 1.0)
    c = nrm(ks[1], (BATCH, D_MODEL), 1.0)
    positions = jnp.arange(SEQ, dtype=jnp.int32)[None, :] + jax.random.randint(ks[2], (BATCH, 1), 0, 4096, dtype=jnp.int32)
    ada_w = nrm(ks[3], (DEPTH, D_MODEL, N_MOD * D_MODEL), 0.5 * D_MODEL ** -0.5)
    ada_b = nrm(ks[4], (DEPTH, N_MOD * D_MODEL), 0.02)
    norm_mix_g = 1.0 + nrm(ks[5], (DEPTH, D_MODEL), 0.05)
    norm_ffn_g = 1.0 + nrm(ks[6], (DEPTH, D_MODEL), 0.05)
    a_w_qkv = nrm(ks[7], (N_A_LAYERS, D_MODEL, (A_Q_HEADS + 2 * A_KV_HEADS) * HEAD_DIM), D_MODEL ** -0.5)
    a_w_o = nrm(ks[8], (N_A_LAYERS, A_Q_HEADS * HEAD_DIM, D_MODEL), (A_Q_HEADS * HEAD_DIM) ** -0.5)
    a_sink = nrm(ks[9], (N_A_LAYERS, A_Q_HEADS), 1.0)
    b_w_qkv = nrm(ks[10], (N_B_LAYERS, D_MODEL, len(B_GROUPS) * 3 * B_HEADS * HEAD_DIM), D_MODEL ** -0.5)
    b_w_o = nrm(ks[11], (N_B_LAYERS, B_HEADS * HEAD_DIM, D_MODEL), (B_HEADS * HEAD_DIM) ** -0.5)
    router_w = nrm(ks[12], (D_MODEL, N_EXPERTS), D_MODEL ** -0.5)
    router_bias = nrm(ks[13], (N_EXPERTS,), 0.01)
    exp_w_gate = nrm(ks[14], (DEPTH, N_EXPERTS, D_MODEL, D_EXPERT), D_MODEL ** -0.5)
    exp_w_up = nrm(ks[15], (DEPTH, N_EXPERTS, D_MODEL, D_EXPERT), D_MODEL ** -0.5)
    exp_w_down = nrm(ks[16], (DEPTH, N_EXPERTS, D_EXPERT, D_MODEL), D_EXPERT ** -0.5)
    final_norm_g = 1.0 + nrm(ks[17], (D_MODEL,), 0.05)
    return {'x': x, 'c': c, 'positions': positions, 'ada_w': ada_w, 'ada_b': ada_b,
            'norm_mix_g': norm_mix_g, 'norm_ffn_g': norm_ffn_g,
            'a_w_qkv': a_w_qkv, 'a_w_o': a_w_o, 'a_sink': a_sink,
            'b_w_qkv': b_w_qkv, 'b_w_o': b_w_o,
            'router_w': router_w, 'router_bias': router_bias,
            'exp_w_gate': exp_w_gate, 'exp_w_up': exp_w_up, 'exp_w_down': exp_w_down,
            'final_norm_g': final_norm_g}


def reference(x, c, positions, ada_w, ada_b, norm_mix_g, norm_ffn_g, a_w_qkv, a_w_o, a_sink,
              b_w_qkv, b_w_o, router_w, router_bias, exp_w_gate, exp_w_up, exp_w_down, final_norm_g):
    cos, sin = rope_tables(positions)
    mod_in = jax.nn.silu(c)
    for i in range(DEPTH):
        mod = mod_in @ ada_w[i] + ada_b[i]
        sh_m, sc_m, g_m, sh_f, sc_f, g_f = [t[:, None, :] for t in jnp.split(mod, N_MOD, axis=-1)]
        h = rms_norm(x, norm_mix_g[i]) * (1.0 + sc_m) + sh_m
        j = i // N_MIXERS
        if i % N_MIXERS == 0:
            mix = mixer_a(h, cos, sin, a_w_qkv[j], a_w_o[j], a_sink[j])
        else:
            mix = mixer_b(h, cos, sin, b_w_qkv[j], b_w_o[j])
        x = x + g_m * mix
        h = rms_norm(x, norm_ffn_g[i]) * (1.0 + sc_f) + sh_f
        combine = route(h.reshape(-1, D_MODEL), router_w, router_bias)
        x = x + g_f * moe_ffn(h, combine, exp_w_gate[i], exp_w_up[i], exp_w_down[i])
    return rms_norm(x, final_norm_g)
```

```python
import functools

import jax
import jax.numpy as jnp
from jax import lax
from jax.experimental import pallas as pl
from jax.experimental.pallas import tpu as pltpu

F32, BF16, I32 = jnp.float32, jnp.bfloat16, jnp.int32

D_MODEL = 1024
HEAD_DIM = 64
ROPE_THETA = 10000.0
NORM_EPS = 1e-6
A_Q_HEADS = 16
A_KV_HEADS = 4
A_GROUP = A_Q_HEADS // A_KV_HEADS
A_RADIUS = 128
B_GROUPS = ((128, 1), (512, 4), (2048, 16))
B_HEADS = 8
B_RADIUS = 64
B_WIDTH = B_HEADS * HEAD_DIM
N_EXPERTS = 16
N_EXPERT_GROUPS = 4
PER_GROUP = N_EXPERTS // N_EXPERT_GROUPS
D_EXPERT = D_MODEL // 2
PAIRS = ((0, 1), (0, 2), (0, 3), (1, 2), (1, 3), (2, 3))
N_CLASSES = N_EXPERT_GROUPS * len(PAIRS)
NEG = -1e30

VMEM_LIMIT = 56 * 1024 * 1024
TM = 512
ATT1_CHUNK = 1024
FFN_TM = 256


def _cparams(*sem):
    return pltpu.CompilerParams(dimension_semantics=sem, vmem_limit_bytes=VMEM_LIMIT)


def _rms_mod(x, g, sc, sh):
    ms = jnp.mean(x * x, axis=-1, keepdims=True)
    return (x * lax.rsqrt(ms + NORM_EPS)) * g * (1.0 + sc) + sh


def _rope_apply(y, cos, sin_signed, lane_lo):
    sw = jnp.where(lane_lo, pltpu.roll(y, 96, 1), pltpu.roll(y, 32, 1))
    return y * cos + sw * sin_signed


def _mod_kernel(c_ref, w_ref, b_ref, o_ref):
    c = c_ref[...]
    s = c * jax.nn.sigmoid(c)
    o_ref[0] = jnp.dot(s, w_ref[0], preferred_element_type=F32,
                       precision=lax.Precision.HIGHEST) + b_ref[0]


def _modulation(c, ada_w, ada_b):
    depth, d, n = ada_w.shape
    bsz = c.shape[0]
    cp = jnp.zeros((8, d), F32).at[:bsz].set(c)
    tn = 1536
    out = pl.pallas_call(
        _mod_kernel,
        grid=(depth, n // tn),
        in_specs=[pl.BlockSpec((8, d), lambda l, j: (0, 0)),
                  pl.BlockSpec((1, d, tn), lambda l, j: (l, 0, j)),
                  pl.BlockSpec((1, 1, tn), lambda l, j: (l, 0, j))],
        out_specs=pl.BlockSpec((1, 8, tn), lambda l, j: (l, 0, j)),
        out_shape=jax.ShapeDtypeStruct((depth, 8, n), F32),
        compiler_params=_cparams("arbitrary", "arbitrary"),
        name="modulation",
    )(cp, ada_w, ada_b.reshape(depth, 1, n))
    return out[:, :bsz]


def _rope_kernel(pos_ref, invf_ref, sign_ref, cos_ref, sin_ref):
    ang = pos_ref[0].astype(F32) * invf_ref[...]
    cos_ref[0] = jnp.cos(ang)
    sin_ref[0] = jnp.sin(ang) * sign_ref[...]


def _rope_tables(positions):
    bsz, s = positions.shape
    inv_freq = ROPE_THETA ** (-jnp.arange(0, HEAD_DIM, 2, dtype=F32) / HEAD_DIM)
    invf = jnp.tile(inv_freq, 4).reshape(1, 128)
    half = HEAD_DIM // 2
    sign = jnp.tile(jnp.concatenate([-jnp.ones((half,), F32), jnp.ones((half,), F32)]), 2).reshape(1, 128)
    ts = TM
    return pl.pallas_call(
        _rope_kernel,
        grid=(bsz, s // ts),
        in_specs=[pl.BlockSpec((1, ts, 1), lambda b, i: (b, i, 0)),
                  pl.BlockSpec((1, 128), lambda b, i: (0, 0)),
                  pl.BlockSpec((1, 128), lambda b, i: (0, 0))],
        out_specs=[pl.BlockSpec((1, ts, 128), lambda b, i: (b, i, 0))] * 2,
        out_shape=[jax.ShapeDtypeStruct((bsz, s, 128), F32)] * 2,
        compiler_params=_cparams("parallel", "parallel"),
        name="rope_tables",
    )(positions.reshape(bsz, s, 1), invf, sign)


def _qkv0_kernel(x_ref, g_ref, sc_ref, sh_ref, w_ref, cos_ref, sin_ref,
                 q_ref, k_ref, v_ref, hb_scr):
    hb_scr[...] = _rms_mod(x_ref[0], g_ref[...], sc_ref[0], sh_ref[0]).astype(BF16)
    cos = cos_ref[0]
    sin = sin_ref[0]
    lane_lo = (lax.broadcasted_iota(I32, cos.shape, 1) % HEAD_DIM) < (HEAD_DIM // 2)
    nq = A_Q_HEADS * HEAD_DIM
    nkv = A_KV_HEADS * HEAD_DIM
    for c in range(nq // 256):
        y = jnp.dot(hb_scr[...], w_ref[:, 256 * c:256 * (c + 1)], preferred_element_type=F32)
        for cc in range(2):
            r = _rope_apply(y[:, 128 * cc:128 * (cc + 1)], cos, sin, lane_lo) * (HEAD_DIM ** -0.5)
            q_ref[0, :, 256 * c + 128 * cc:256 * c + 128 * (cc + 1)] = r.astype(BF16)
    y = jnp.dot(hb_scr[...], w_ref[:, nq:nq + nkv], preferred_element_type=F32)
    for cc in range(nkv // 128):
        r = _rope_apply(y[:, 128 * cc:128 * (cc + 1)], cos, sin, lane_lo)
        k_ref[0, :, 128 * cc:128 * (cc + 1)] = r.astype(BF16)
    y = jnp.dot(hb_scr[...], w_ref[:, nq + nkv:nq + 2 * nkv], preferred_element_type=F32)
    v_ref[0] = y.astype(BF16)


def _qkv0(x, g, sc, sh, w, cos, sin):
    bsz, s, d = x.shape
    nq = A_Q_HEADS * HEAD_DIM
    nkv = A_KV_HEADS * HEAD_DIM
    tm = TM
    tok = lambda w_: pl.BlockSpec((1, tm, w_), lambda b, i: (b, i, 0))
    per_b = pl.BlockSpec((1, 1, d), lambda b, i: (b, 0, 0))
    return pl.pallas_call(
        _qkv0_kernel,
        grid=(bsz, s // tm),
        in_specs=[tok(d), pl.BlockSpec((1, d), lambda b, i: (0, 0)), per_b, per_b,
                  pl.BlockSpec(w.shape, lambda b, i: (0, 0)), tok(128), tok(128)],
        out_specs=[tok(nq), tok(nkv), tok(nkv)],
        out_shape=[jax.ShapeDtypeStruct((bsz, s, nq), BF16),
                   jax.ShapeDtypeStruct((bsz, s, nkv), BF16),
                   jax.ShapeDtypeStruct((bsz, s, nkv), BF16)],
        scratch_shapes=[pltpu.VMEM((tm, d), BF16)],
        compiler_params=_cparams("parallel", "parallel"),
        name="qkv0",
    )(x, g.reshape(1, d), sc, sh, w, cos, sin)


def _att0_kernel(sink_ref, q_ref, kc_ref, kp_ref, kn_ref, vc_ref, vp_ref, vn_ref,
                 o_ref, kf, vf, *, tq, seq):
    i = pl.program_id(1)
    r = A_RADIUS
    kf[0:r] = kp_ref[0]
    kf[r:r + tq] = kc_ref[0]
    kf[r + tq:r + tq + r] = kn_ref[0]
    vf[0:r] = vp_ref[0]
    vf[r:r + tq] = vc_ref[0]
    vf[r + tq:r + tq + r] = vn_ref[0]

    ii = lax.broadcasted_iota(I32, (r, 3 * r), 0)
    jj = lax.broadcasted_iota(I32, (r, 3 * r), 1)
    rel = jj - ii
    band = (rel >= 0) & (rel <= 2 * r)

    def block(j, carry):
        q0 = pl.multiple_of(j * r, r)
        kpos = i * tq + q0 - r + jj
        bias = jnp.where(band & (kpos >= 0) & (kpos < seq), 0.0, NEG)
        for g in range(A_KV_HEADS):
            kb = kf[pl.ds(q0, 3 * r), HEAD_DIM * g:HEAD_DIM * (g + 1)]
            vb = vf[pl.ds(q0, 3 * r), HEAD_DIM * g:HEAD_DIM * (g + 1)]
            heads = [A_GROUP * g + t for t in range(A_GROUP)]
            q4 = jnp.concatenate(
                [q_ref[0, pl.ds(q0, r), HEAD_DIM * h:HEAD_DIM * (h + 1)] for h in heads], axis=0)
            s4 = lax.dot_general(q4, kb, (((1,), (1,)), ((), ())), preferred_element_type=F32)
            ps, ls = [], []
            for t, h in enumerate(heads):
                s = s4[r * t:r * (t + 1)] + bias
                sink = sink_ref[h]
                m = jnp.maximum(jnp.max(s, axis=-1, keepdims=True), sink)
                p = jnp.exp(s - m)
                ls.append(jnp.sum(p, axis=-1, keepdims=True) + jnp.exp(sink - m))
                ps.append(p.astype(BF16))
            o4 = jnp.dot(jnp.concatenate(ps, axis=0), vb, preferred_element_type=F32)
            for t, h in enumerate(heads):
                o = o4[r * t:r * (t + 1)] / ls[t]
                o_ref[0, pl.ds(q0, r), HEAD_DIM * h:HEAD_DIM * (h + 1)] = o.astype(BF16)
        return carry

    lax.fori_loop(0, tq // r, block, 0)


def _att0(q, k, v, sink):
    bsz, s, nq = q.shape
    nkv = k.shape[-1]
    tq = TM
    r = A_RADIUS
    per = tq // r
    last = s // r - 1
    cur = lambda w_: pl.BlockSpec((1, tq, w_), lambda b, i: (b, i, 0))
    prev = pl.BlockSpec((1, r, nkv), lambda b, i: (b, jnp.maximum(i * per - 1, 0), 0))
    nxt = pl.BlockSpec((1, r, nkv), lambda b, i: (b, jnp.minimum(i * per + per, last), 0))
    return pl.pallas_call(
        functools.partial(_att0_kernel, tq=tq, seq=s),
        grid=(bsz, s // tq),
        in_specs=[pl.BlockSpec(memory_space=pltpu.SMEM),
                  cur(nq), cur(nkv), prev, nxt, cur(nkv), prev, nxt],
        out_specs=cur(nq),
        out_shape=jax.ShapeDtypeStruct((bsz, s, nq), BF16),
        scratch_shapes=[pltpu.VMEM((tq + 2 * r, nkv), BF16), pltpu.VMEM((tq + 2 * r, nkv), BF16)],
        compiler_params=_cparams("parallel", "parallel"),
        name="att0",
    )(sink, q, k, k, k, v, v, v)


def _post_kernel(o_ref, wo_ref, x_ref, gm_ref, gf_ref, scf_ref, shf_ref, rw_ref, rb_ref,
                 x1_ref, h2_ref, cls_ref):
    mix = jnp.dot(o_ref[0], wo_ref[...], preferred_element_type=F32)
    x1 = x_ref[0] + gm_ref[0] * mix
    x1_ref[0] = x1
    h2 = _rms_mod(x1, gf_ref[...], scf_ref[0], shf_ref[0]).astype(BF16)
    h2_ref[0] = h2
    tq = h2.shape[0]
    logits = lax.dot_general(rw_ref[...], h2, (((1,), (1,)), ((), ())), preferred_element_type=F32)
    scores = jax.nn.sigmoid(logits)
    biased = scores + rb_ref[...]
    pj = [biased[8 * j:8 * (j + 1)] for j in range(PER_GROUP)]
    sel = []
    for j in range(PER_GROUP):
        beaten = jnp.zeros((8, tq), F32)
        for j2 in range(PER_GROUP):
            if j2 == j:
                continue
            beats = (pj[j2] > pj[j]) | (pj[j2] == pj[j]) if j2 < j else (pj[j2] > pj[j])
            beaten = beaten + jnp.where(beats, 1.0, 0.0)
        sel.append(beaten < 2.0)
    gscore = jnp.zeros((8, tq), F32)
    for j in range(PER_GROUP):
        gscore = gscore + jnp.where(sel[j], pj[j], 0.0)
    gi = lax.broadcasted_iota(I32, (8, tq), 0).astype(F32)
    gmax = jnp.max(gscore, axis=0, keepdims=True)
    gidx = jnp.min(jnp.where(gscore == gmax, gi, 8.0), axis=0, keepdims=True)
    onehot = gi == gidx
    f = [jnp.max(jnp.where(onehot & sel[j], 1.0, 0.0), axis=0, keepdims=True) > 0.5
         for j in range(PER_GROUP)]
    pair = jnp.where(f[0], jnp.where(f[1], 0.0, jnp.where(f[2], 1.0, 2.0)),
                     jnp.where(f[1], jnp.where(f[2], 3.0, 4.0), 5.0))
    cls = (gidx * float(len(PAIRS)) + pair).astype(I32)
    cls_ref[0] = jnp.broadcast_to(cls, (8, tq))


def _post(o, wo, x, gm, gf, scf, shf, rw32, rb32):
    bsz, s, d = x.shape
    do = o.shape[-1]
    tq = TM
    nt = s // tq
    tok = lambda w_: pl.BlockSpec((1, tq, w_), lambda b, i: (b, i, 0))
    per_b = pl.BlockSpec((1, 1, d), lambda b, i: (b, 0, 0))
    full = lambda a: pl.BlockSpec(a.shape, lambda b, i: (0,) * a.ndim)
    gf2 = gf.reshape(1, d)
    return pl.pallas_call(
        _post_kernel,
        grid=(bsz, nt),
        in_specs=[tok(do), full(wo), tok(d), per_b, full(gf2), per_b, per_b, full(rw32), full(rb32)],
        out_specs=[tok(d), tok(d), pl.BlockSpec((1, 8, tq), lambda b, i: (b * nt + i, 0, 0))],
        out_shape=[jax.ShapeDtypeStruct((bsz, s, d), F32),
                   jax.ShapeDtypeStruct((bsz, s, d), BF16),
                   jax.ShapeDtypeStruct((bsz * nt, 8, tq), I32)],
        compiler_params=_cparams("parallel", "parallel"),
        name="post_attention",
    )(o, wo, x, gm, gf2, scf, shf, rw32, rb32)


def _rank_kernel(cls_ref, rank_ref, cnt_ref, carry):
    i = pl.program_id(0)

    @pl.when(i == 0)
    def _():
        carry[...] = jnp.zeros_like(carry)

    tq = cls_ref.shape[-1]
    cls = cls_ref[0][0:1, :]
    oh = lax.broadcasted_iota(I32, (32, tq), 0) == cls
    ohf = jnp.where(oh, 1.0, 0.0)
    before = lax.broadcasted_iota(I32, (tq, tq), 0) < lax.broadcasted_iota(I32, (tq, tq), 1)
    upper = jnp.where(before, 1.0, 0.0).astype(BF16)
    within = jnp.dot(ohf.astype(BF16), upper, preferred_element_type=F32)
    c = carry[...]
    tot = jnp.sum(jnp.where(oh, within + c[:, 0:1], 0.0), axis=0, keepdims=True)
    rank_ref[0] = jnp.broadcast_to(tot.astype(I32), (8, tq))
    newc = c + jnp.sum(ohf, axis=1, keepdims=True)
    carry[...] = newc
    cnt_ref[...] = newc


def _class_ranks(cls3):
    nt, _, tq = cls3.shape
    blk = pl.BlockSpec((1, 8, tq), lambda i: (i, 0, 0))
    return pl.pallas_call(
        _rank_kernel,
        grid=(nt,),
        in_specs=[blk],
        out_specs=[blk, pl.BlockSpec((32, 128), lambda i: (0, 0))],
        out_shape=[jax.ShapeDtypeStruct((nt, 8, tq), I32), jax.ShapeDtypeStruct((32, 128), F32)],
        scratch_shapes=[pltpu.VMEM((32, 128), F32)],
        compiler_params=_cparams("arbitrary"),
        name="class_ranks",
    )(cls3)


def _ffn_kernel(e1_ref, e2_ref, nused_ref, hs_ref, wg1, wu1, wd1, rw1, wg2, wu2, wd2, rw2, ys_ref):
    j = pl.program_id(0)

    @pl.when(j < nused_ref[0])
    def _():
        x = hs_ref[...]
        xf = x.astype(F32)
        s1 = jax.nn.sigmoid(jnp.sum(xf * rw1[0].astype(F32), axis=-1, keepdims=True))
        s2 = jax.nn.sigmoid(jnp.sum(xf * rw2[0].astype(F32), axis=-1, keepdims=True))
        den = s1 + s2

        def expert(wg, wu, wd):
            gt = jnp.dot(x, wg[0], preferred_element_type=F32)
            up = jnp.dot(x, wu[0], preferred_element_type=F32)
            a = (gt * jax.nn.sigmoid(gt)) * up
            return jnp.dot(a.astype(BF16), wd[0], preferred_element_type=F32)

        y = (s1 / den) * expert(wg1, wu1, wd1) + (s2 / den) * expert(wg2, wu2, wd2)
        ys_ref[...] = y.astype(BF16)

    @pl.when(j >= nused_ref[0])
    def _():
        ys_ref[...] = jnp.zeros_like(ys_ref)


def _ffn(hs, e1, e2, nused, wg, wu, wd, rw3):
    npad, d = hs.shape
    tm = FFN_TM
    de = wg.shape[-1]
    row = pl.BlockSpec((tm, d), lambda j, e1, e2, nu: (jnp.minimum(j, nu[0] - 1), 0))
    first = lambda shape: pl.BlockSpec((1,) + shape, lambda j, e1, e2, nu: (e1[j], 0, 0))
    second = lambda shape: pl.BlockSpec((1,) + shape, lambda j, e1, e2, nu: (e2[j], 0, 0))
    grid_spec = pltpu.PrefetchScalarGridSpec(
        num_scalar_prefetch=3,
        grid=(npad // tm,),
        in_specs=[row,
                  first((d, de)), first((d, de)), first((de, d)), first((1, d)),
                  second((d, de)), second((d, de)), second((de, d)), second((1, d))],
        out_specs=pl.BlockSpec((tm, d), lambda j, e1, e2, nu: (j, 0)),
    )
    return pl.pallas_call(
        _ffn_kernel,
        grid_spec=grid_spec,
        out_shape=jax.ShapeDtypeStruct((npad, d), BF16),
        compiler_params=_cparams("arbitrary"),
        name="expert_ffn",
    )(e1, e2, nused, hs, wg, wu, wd, rw3, wg, wu, wd, rw3)


def _moe(h2, cls3, wg, wu, wd, rw3):
    bsz, s, d = h2.shape
    t = bsz * s
    tm = FFN_TM
    rank3, cnt = _class_ranks(cls3)
    cls = cls3[:, 0, :].reshape(t)
    rank = rank3[:, 0, :].reshape(t)
    counts = cnt[:N_CLASSES, 0].astype(I32)
    padded = ((counts + tm - 1) // tm) * tm
    ends = jnp.cumsum(padded)
    base = ends - padded
    pos = base[cls] + rank
    ntiles = t // tm + N_CLASSES
    npad = ntiles * tm
    tcls = jnp.minimum(jnp.searchsorted(ends, jnp.arange(ntiles, dtype=I32) * tm, side="right"),
                       N_CLASSES - 1).astype(I32)
    nused = (ends[-1:] // tm).astype(I32)
    grp = tcls // len(PAIRS)
    pr = tcls % len(PAIRS)
    e1 = (grp * PER_GROUP + jnp.array([p[0] for p in PAIRS], I32)[pr]).astype(I32)
    e2 = (grp * PER_GROUP + jnp.array([p[1] for p in PAIRS], I32)[pr]).astype(I32)
    perm = jnp.zeros((npad,), I32).at[pos].set(jnp.arange(t, dtype=I32))
    hs = jnp.take(h2.reshape(t, d), perm, axis=0)
    ys = _ffn(hs, e1, e2, nused, wg, wu, wd, rw3)
    return jnp.take(ys, pos, axis=0).reshape(bsz, s, d)


def _qkv1_kernel(x_ref, y_ref, gfp_ref, g_ref, sc_ref, sh_ref, w_ref, cos_ref, sin_ref,
                 x2_ref, *rest):
    outs = rest[:9]
    hb_scr, ysc = rest[9:]
    tm = x_ref.shape[1]
    x2 = x_ref[0] + gfp_ref[0] * y_ref[0].astype(F32)
    x2_ref[0] = x2
    hb_scr[...] = _rms_mod(x2, g_ref[...], sc_ref[0], sh_ref[0]).astype(BF16)
    cos = cos_ref[0]
    sin = sin_ref[0]
    lane_lo = (lax.broadcasted_iota(I32, cos.shape, 1) % HEAD_DIM) < (HEAD_DIM // 2)
    for gi, (_, dil) in enumerate(B_GROUPS):
        for j in range(3):
            c = gi * 3 + j
            y = jnp.dot(hb_scr[...], w_ref[:, B_WIDTH * c:B_WIDTH * (c + 1)], preferred_element_type=F32)
            out = outs[c]
            for cc in range(B_WIDTH // 128):
                lanes = slice(128 * cc, 128 * (cc + 1))
                r = y[:, lanes]
                if j < 2:
                    r = _rope_apply(r, cos, sin, lane_lo)
                if j == 0:
                    r = r * (HEAD_DIM ** -0.5)
                if dil == 1:
                    out[0, 0, :, lanes] = r.astype(BF16)
                else:
                    ysc[cc] = r
                    for rr in range(dil):
                        out[0, rr, :, lanes] = ysc[cc, pl.ds(rr, tm // dil, stride=dil), :].astype(BF16)


def _qkv1(x, y, gfp, g, sc, sh, w, cos, sin):
    bsz, s, d = x.shape
    tm = TM
    tok = lambda w_: pl.BlockSpec((1, tm, w_), lambda b, i: (b, i, 0))
    per_b = pl.BlockSpec((1, 1, d), lambda b, i: (b, 0, 0))
    out_specs = [tok(d)]
    out_shape = [jax.ShapeDtypeStruct((bsz, s, d), F32)]
    for _, dil in B_GROUPS:
        for _ in range(3):
            out_specs.append(pl.BlockSpec((1, dil, tm // dil, B_WIDTH), lambda b, i: (b, 0, i, 0)))
            out_shape.append(jax.ShapeDtypeStruct((bsz, dil, s // dil, B_WIDTH), BF16))
    return pl.pallas_call(
        _qkv1_kernel,
        grid=(bsz, s // tm),
        in_specs=[tok(d), tok(d), per_b, pl.BlockSpec((1, d), lambda b, i: (0, 0)), per_b, per_b,
                  pl.BlockSpec(w.shape, lambda b, i: (0, 0)), tok(128), tok(128)],
        out_specs=out_specs,
        out_shape=out_shape,
        scratch_shapes=[pltpu.VMEM((tm, d), BF16), pltpu.VMEM((B_WIDTH // 128, tm, 128), F32)],
        compiler_params=_cparams("parallel", "parallel"),
        name="qkv1",
    )(x, y, gfp, g.reshape(1, d), sc, sh, w, cos, sin)


def _att1_kernel(*refs, chunk, seq):
    ins = refs[:21]
    o_ref = refs[21]
    scr = refs[22:]
    kfs, vfs, nats = scr[0:3], scr[3:6], scr[6:9]
    n = pl.program_id(1)
    rad = B_RADIUS
    for gi, (_, dil) in enumerate(B_GROUPS):
        _, kc, kp, kn, vc, vp, vn = ins[7 * gi:7 * gi + 7]
        lseg = chunk // dil
        for full, (p_, c_, n_) in ((kfs[gi], (kp, kc, kn)), (vfs[gi], (vp, vc, vn))):
            full[:, 0:rad] = p_[0]
            full[:, rad:rad + lseg] = c_[0]
            full[:, rad + lseg:rad + lseg + rad] = n_[0]

    for h in range(B_HEADS):
        hs = slice(HEAD_DIM * h, HEAD_DIM * (h + 1))
        for gi, (_, dil) in enumerate(B_GROUPS):
            q_ref = ins[7 * gi]
            kf, vf, nat = kfs[gi], vfs[gi], nats[gi]
            lseg = chunk // dil
            qb = min(2 * rad, lseg)
            kb_n = qb + 2 * rad
            nblk = lseg // qb
            lstr = seq // dil
            ii = lax.broadcasted_iota(I32, (qb, kb_n), 0)
            jj = lax.broadcasted_iota(I32, (qb, kb_n), 1)
            rel = jj - ii
            band = (rel >= 0) & (rel <= 2 * rad)

            def body(t, carry, q_ref=q_ref, kf=kf, vf=vf, nat=nat, lseg=lseg, qb=qb, kb_n=kb_n,
                     nblk=nblk, lstr=lstr, jj=jj, band=band, dil=dil, hs=hs):
                res = t // nblk
                q0 = pl.multiple_of((t % nblk) * qb, qb)
                qh = q_ref[0, res, pl.ds(q0, qb), hs]
                kb = kf[res, pl.ds(q0, kb_n), hs]
                vb = vf[res, pl.ds(q0, kb_n), hs]
                s = lax.dot_general(qh, kb, (((1,), (1,)), ((), ())), preferred_element_type=F32)
                apos = n * lseg + q0 - rad + jj
                s = s + jnp.where(band & (apos >= 0) & (apos < lstr), 0.0, NEG)
                m = jnp.max(s, axis=-1, keepdims=True)
                p = jnp.exp(s - m)
                l = jnp.sum(p, axis=-1, keepdims=True)
                o = jnp.dot(p.astype(BF16), vb, preferred_element_type=F32) / l
                lse = m + jnp.log(l)
                tile = jnp.concatenate([o, jnp.broadcast_to(lse, (qb, HEAD_DIM))], axis=1)
                if dil == 1:
                    nat[pl.ds(q0, qb), :] = tile
                else:
                    nat[pl.ds(res + dil * q0, qb, stride=dil), :] = tile
                return carry

            lax.fori_loop(0, dil * nblk, body, 0)

        a = [nat[...] for nat in nats]
        ls = [pltpu.roll(x, HEAD_DIM, 1) for x in a]
        mx = jnp.maximum(jnp.maximum(ls[0], ls[1]), ls[2])
        w = [jnp.exp(x - mx) for x in ls]
        num = w[0] * a[0] + w[1] * a[1] + w[2] * a[2]
        res = num / (w[0] + w[1] + w[2])
        o_ref[0, :, hs] = res[:, 0:HEAD_DIM].astype(BF16)


def _att1(qkv):
    bsz = qkv[0].shape[0]
    seq = qkv[0].shape[1] * qkv[0].shape[2]
    chunk = ATT1_CHUNK
    rad = B_RADIUS
    in_specs, args, scratch = [], [], []
    for gi, (_, dil) in enumerate(B_GROUPS):
        q, k, v = qkv[3 * gi:3 * gi + 3]
        lseg = chunk // dil
        per = lseg // rad
        last = seq // dil // rad - 1
        cur = pl.BlockSpec((1, dil, lseg, B_WIDTH), lambda b, n: (b, 0, n, 0))
        prev = pl.BlockSpec((1, dil, rad, B_WIDTH),
                            lambda b, n, per=per: (b, 0, jnp.maximum(n * per - 1, 0), 0))
        nxt = pl.BlockSpec((1, dil, rad, B_WIDTH),
                           lambda b, n, per=per, last=last: (b, 0, jnp.minimum(n * per + per, last), 0))
        in_specs += [cur, cur, prev, nxt, cur, prev, nxt]
        args += [q, k, k, k, v, v, v]
    for _ in range(2):
        for _, dil in B_GROUPS:
            scratch.append(pltpu.VMEM((dil, chunk // dil + 2 * rad, B_WIDTH), BF16))
    for _ in B_GROUPS:
        scratch.append(pltpu.VMEM((chunk, 2 * HEAD_DIM), F32))
    return pl.pallas_call(
        functools.partial(_att1_kernel, chunk=chunk, seq=seq),
        grid=(bsz, seq // chunk),
        in_specs=in_specs,
        out_specs=pl.BlockSpec((1, chunk, B_WIDTH), lambda b, n: (b, n, 0)),
        out_shape=jax.ShapeDtypeStruct((bsz, seq, B_WIDTH), BF16),
        scratch_shapes=scratch,
        compiler_params=_cparams("parallel", "parallel"),
        name="att1",
    )(*args)


def _final_kernel(x_ref, y_ref, gfp_ref, g_ref, o_ref):
    x = x_ref[0] + gfp_ref[0] * y_ref[0].astype(F32)
    ms = jnp.mean(x * x, axis=-1, keepdims=True)
    o_ref[0] = (x * lax.rsqrt(ms + NORM_EPS)) * g_ref[...]


def _final(x, y, gfp, g):
    bsz, s, d = x.shape
    tm = TM
    tok = pl.BlockSpec((1, tm, d), lambda b, i: (b, i, 0))
    return pl.pallas_call(
        _final_kernel,
        grid=(bsz, s // tm),
        in_specs=[tok, tok, pl.BlockSpec((1, 1, d), lambda b, i: (b, 0, 0)),
                  pl.BlockSpec((1, d), lambda b, i: (0, 0))],
        out_specs=tok,
        out_shape=jax.ShapeDtypeStruct((bsz, s, d), F32),
        compiler_params=_cparams("parallel", "parallel"),
        name="final_norm",
    )(x, y, gfp, g.reshape(1, d))


def kernel(x, c, positions, ada_w, ada_b, norm_mix_g, norm_ffn_g, a_w_qkv, a_w_o, a_sink, b_w_qkv, b_w_o,
           router_w, router_bias, exp_w_gate, exp_w_up, exp_w_down, final_norm_g):
    bsz, s, d = x.shape
    assert d == D_MODEL and ada_w.shape[0] == 2 and s % ATT1_CHUNK == 0
    assert all(w_ // (2 * dil) == B_RADIUS for w_, dil in B_GROUPS)

    mod = _modulation(c, ada_w, ada_b)
    mods = [[mod[l][:, k * d:(k + 1) * d].reshape(bsz, 1, d) for k in range(6)] for l in range(2)]
    cos, sin = _rope_tables(positions)

    rw_t = router_w.T.astype(BF16)
    rows = jnp.array([PER_GROUP * g + j if g < N_EXPERT_GROUPS else 0
                      for j in range(PER_GROUP) for g in range(8)], I32)
    live = jnp.array([1.0 if g < N_EXPERT_GROUPS else 0.0
                      for j in range(PER_GROUP) for g in range(8)], F32)
    rw32 = (rw_t[rows].astype(F32) * live[:, None]).astype(BF16)
    rb32 = jnp.where(live > 0, router_bias.astype(F32)[rows], NEG).reshape(32, 1)
    rw3 = rw_t.reshape(N_EXPERTS, 1, d)
    wg, wu, wd = exp_w_gate.astype(BF16), exp_w_up.astype(BF16), exp_w_down.astype(BF16)

    sh_m, sc_m, g_m, sh_f, sc_f, g_f = mods[0]
    q, k, v = _qkv0(x, norm_mix_g[0], sc_m, sh_m, a_w_qkv[0].astype(BF16), cos, sin)
    o = _att0(q, k, v, a_sink[0].astype(F32))
    x1, h2, cls3 = _post(o, a_w_o[0].astype(BF16), x, g_m, norm_ffn_g[0], sc_f, sh_f, rw32, rb32)
    y = _moe(h2, cls3, wg[0], wu[0], wd[0], rw3)
    g_f_prev = g_f

    sh_m, sc_m, g_m, sh_f, sc_f, g_f = mods[1]
    outs = _qkv1(x1, y, g_f_prev, norm_mix_g[1], sc_m, sh_m, b_w_qkv[0].astype(BF16), cos, sin)
    x2, qkv = outs[0], outs[1:]
    o = _att1(qkv)
    x3, h2, cls3 = _post(o, b_w_o[0].astype(BF16), x2, g_m, norm_ffn_g[1], sc_f, sh_f, rw32, rb32)
    y = _moe(h2, cls3, wg[1], wu[1], wd[1], rw3)

    return _final(x3, y, g_f, final_norm_g)
```

```python
import functools

import jax
import jax.numpy as jnp
from jax import lax
from jax.experimental import pallas as pl
from jax.experimental.pallas import tpu as pltpu

F32, BF16, I32 = jnp.float32, jnp.bfloat16, jnp.int32

D_MODEL = 1024
HEAD_DIM = 64
ROPE_THETA = 10000.0
NORM_EPS = 1e-6
A_Q_HEADS = 16
A_KV_HEADS = 4
A_GROUP = A_Q_HEADS // A_KV_HEADS
A_RADIUS = 128
B_GROUPS = ((128, 1), (512, 4), (2048, 16))
B_HEADS = 8
B_RADIUS = 64
B_WIDTH = B_HEADS * HEAD_DIM
N_EXPERTS = 16
N_EXPERT_GROUPS = 4
PER_GROUP = N_EXPERTS // N_EXPERT_GROUPS
D_EXPERT = D_MODEL // 2
PAIRS = ((0, 1), (0, 2), (0, 3), (1, 2), (1, 3), (2, 3))
N_CLASSES = N_EXPERT_GROUPS * len(PAIRS)
NEG = -1e30

VMEM_LIMIT = 56 * 1024 * 1024
TM = 512
ATT1_CHUNK = 1024
FFN_TM = 256


def _cparams(*sem):
    return pltpu.CompilerParams(dimension_semantics=sem, vmem_limit_bytes=VMEM_LIMIT)


def _rms_mod(x, g, sc, sh):
    ms = jnp.mean(x * x, axis=-1, keepdims=True)
    return (x * lax.rsqrt(ms + NORM_EPS)) * g * (1.0 + sc) + sh


def _rope_apply(y, cos, sin_signed, lane_lo):
    sw = jnp.where(lane_lo, pltpu.roll(y, 96, 1), pltpu.roll(y, 32, 1))
    return y * cos + sw * sin_signed


def _dup_head(x, first_head, which):
    rolled = pltpu.roll(x, HEAD_DIM, 1)
    return jnp.where(first_head, x, rolled) if which == 0 else jnp.where(first_head, rolled, x)


def _mod_kernel(c_ref, w_ref, b_ref, o_ref):
    c = c_ref[...]
    s = c * jax.nn.sigmoid(c)
    o_ref[0] = jnp.dot(s, w_ref[0], preferred_element_type=F32,
                       precision=lax.Precision.HIGHEST) + b_ref[0]


def _modulation(c, ada_w, ada_b):
    depth, d, n = ada_w.shape
    bsz = c.shape[0]
    cp = jnp.zeros((8, d), F32).at[:bsz].set(c)
    tn = 1536
    out = pl.pallas_call(
        _mod_kernel,
        grid=(depth, n // tn),
        in_specs=[pl.BlockSpec((8, d), lambda l, j: (0, 0)),
                  pl.BlockSpec((1, d, tn), lambda l, j: (l, 0, j)),
                  pl.BlockSpec((1, 1, tn), lambda l, j: (l, 0, j))],
        out_specs=pl.BlockSpec((1, 8, tn), lambda l, j: (l, 0, j)),
        out_shape=jax.ShapeDtypeStruct((depth, 8, n), F32),
        compiler_params=_cparams("arbitrary", "arbitrary"),
        name="modulation",
    )(cp, ada_w, ada_b.reshape(depth, 1, n))
    return out[:, :bsz]


def _rope_kernel(pos_ref, invf_ref, sign_ref, cos_ref, sin_ref):
    ang = pos_ref[0].astype(F32) * invf_ref[...]
    cos_ref[0] = jnp.cos(ang)
    sin_ref[0] = jnp.sin(ang) * sign_ref[...]


def _rope_tables(positions):
    bsz, s = positions.shape
    inv_freq = ROPE_THETA ** (-jnp.arange(0, HEAD_DIM, 2, dtype=F32) / HEAD_DIM)
    invf = jnp.tile(inv_freq, 4).reshape(1, 128)
    half = HEAD_DIM // 2
    sign = jnp.tile(jnp.concatenate([-jnp.ones((half,), F32), jnp.ones((half,), F32)]), 2).reshape(1, 128)
    ts = TM
    return pl.pallas_call(
        _rope_kernel,
        grid=(bsz, s // ts),
        in_specs=[pl.BlockSpec((1, ts, 1), lambda b, i: (b, i, 0)),
                  pl.BlockSpec((1, 128), lambda b, i: (0, 0)),
                  pl.BlockSpec((1, 128), lambda b, i: (0, 0))],
        out_specs=[pl.BlockSpec((1, ts, 128), lambda b, i: (b, i, 0))] * 2,
        out_shape=[jax.ShapeDtypeStruct((bsz, s, 128), F32)] * 2,
        compiler_params=_cparams("parallel", "parallel"),
        name="rope_tables",
    )(positions.reshape(bsz, s, 1), invf, sign)


def _qkv0_kernel(x_ref, g_ref, sc_ref, sh_ref, w_ref, cos_ref, sin_ref,
                 q_ref, k_ref, v_ref, hb_scr):
    hb_scr[...] = _rms_mod(x_ref[0], g_ref[...], sc_ref[0], sh_ref[0]).astype(BF16)
    cos = cos_ref[0]
    sin = sin_ref[0]
    lane_lo = (lax.broadcasted_iota(I32, cos.shape, 1) % HEAD_DIM) < (HEAD_DIM // 2)
    nq = A_Q_HEADS * HEAD_DIM
    nkv = A_KV_HEADS * HEAD_DIM
    for c in range(nq // 256):
        y = jnp.dot(hb_scr[...], w_ref[:, 256 * c:256 * (c + 1)], preferred_element_type=F32)
        for cc in range(2):
            r = _rope_apply(y[:, 128 * cc:128 * (cc + 1)], cos, sin, lane_lo) * (HEAD_DIM ** -0.5)
            q_ref[0, :, 256 * c + 128 * cc:256 * c + 128 * (cc + 1)] = r.astype(BF16)
    first_head = lax.broadcasted_iota(I32, cos.shape, 1) < HEAD_DIM
    y = jnp.dot(hb_scr[...], w_ref[:, nq:nq + nkv], preferred_element_type=F32)
    for cc in range(nkv // 128):
        r = _rope_apply(y[:, 128 * cc:128 * (cc + 1)], cos, sin, lane_lo)
        for which in range(2):
            g = 2 * cc + which
            k_ref[0, :, 128 * g:128 * (g + 1)] = _dup_head(r, first_head, which).astype(BF16)
    y = jnp.dot(hb_scr[...], w_ref[:, nq + nkv:nq + 2 * nkv], preferred_element_type=F32)
    for cc in range(nkv // 128):
        r = y[:, 128 * cc:128 * (cc + 1)]
        for which in range(2):
            g = 2 * cc + which
            v_ref[0, :, 128 * g:128 * (g + 1)] = _dup_head(r, first_head, which).astype(BF16)


def _qkv0(x, g, sc, sh, w, cos, sin):
    bsz, s, d = x.shape
    nq = A_Q_HEADS * HEAD_DIM
    nkv = A_KV_HEADS * HEAD_DIM
    tm = TM
    tok = lambda w_: pl.BlockSpec((1, tm, w_), lambda b, i: (b, i, 0))
    per_b = pl.BlockSpec((1, 1, d), lambda b, i: (b, 0, 0))
    return pl.pallas_call(
        _qkv0_kernel,
        grid=(bsz, s // tm),
        in_specs=[tok(d), pl.BlockSpec((1, d), lambda b, i: (0, 0)), per_b, per_b,
                  pl.BlockSpec(w.shape, lambda b, i: (0, 0)), tok(128), tok(128)],
        out_specs=[tok(nq), tok(2 * nkv), tok(2 * nkv)],
        out_shape=[jax.ShapeDtypeStruct((bsz, s, nq), BF16),
                   jax.ShapeDtypeStruct((bsz, s, 2 * nkv), BF16),
                   jax.ShapeDtypeStruct((bsz, s, 2 * nkv), BF16)],
        scratch_shapes=[pltpu.VMEM((tm, d), BF16)],
        compiler_params=_cparams("parallel", "parallel"),
        name="qkv0",
    )(x, g.reshape(1, d), sc, sh, w, cos, sin)


def _att0_kernel(sink_ref, q_ref, kc_ref, kp_ref, kn_ref, vc_ref, vp_ref, vn_ref,
                 o_ref, kf, vf, *, tq, seq):
    i = pl.program_id(1)
    r = A_RADIUS
    ones = jnp.ones((tq + 2 * r, 128), BF16)
    for g in range(A_KV_HEADS):
        lanes = slice(128 * g, 128 * (g + 1))
        kf[g, 0:r] = kp_ref[0, :, lanes]
        kf[g, r:r + tq] = kc_ref[0, :, lanes]
        kf[g, r + tq:r + tq + r] = kn_ref[0, :, lanes]
        vf[g, 0:r, 0:128] = vp_ref[0, :, lanes]
        vf[g, r:r + tq, 0:128] = vc_ref[0, :, lanes]
        vf[g, r + tq:r + tq + r, 0:128] = vn_ref[0, :, lanes]
        vf[g, :, 128:256] = ones

    ii = lax.broadcasted_iota(I32, (r, r), 0)
    jj = lax.broadcasted_iota(I32, (r, r), 1)
    first_head = jj < HEAD_DIM
    zero = jnp.zeros((r, 128), BF16)

    def block(j, carry):
        q0 = pl.multiple_of(j * r, r)
        base = i * tq + q0
        bias_lo = jnp.where((jj >= ii) & (base - r + jj >= 0), 0.0, NEG)
        bias_hi = jnp.where((jj <= ii) & (base + r + jj < seq), 0.0, NEG)
        for g in range(A_KV_HEADS):
            kb = kf[g, pl.ds(q0, 3 * r), :]
            vb = vf[g, pl.ds(q0, 3 * r), :]
            qs = []
            for cc in range(2):
                c = 2 * g + cc
                qc = q_ref[0, pl.ds(q0, r), 128 * c:128 * (c + 1)]
                qs += [jnp.where(first_head, qc, zero), jnp.where(first_head, zero, qc)]
            s4 = lax.dot_general(jnp.concatenate(qs, axis=0), kb, (((1,), (1,)), ((), ())),
                                 preferred_element_type=F32)
            ps, ms = [], []
            for t in range(A_GROUP):
                s = s4[r * t:r * (t + 1)]
                a0 = s[:, 0:r] + bias_lo
                a1 = s[:, r:2 * r]
                a2 = s[:, 2 * r:3 * r] + bias_hi
                sink = sink_ref[A_GROUP * g + t]
                m = jnp.max(jnp.maximum(jnp.maximum(a0, a1), a2), axis=-1, keepdims=True)
                m = jnp.maximum(m, sink)
                ps.append(jnp.concatenate([jnp.exp(a0 - m), jnp.exp(a1 - m), jnp.exp(a2 - m)],
                                          axis=1).astype(BF16))
                ms.append(m)
            pv = jnp.dot(jnp.concatenate(ps, axis=0), vb, preferred_element_type=F32)
            os_ = []
            for t in range(A_GROUP):
                sink = sink_ref[A_GROUP * g + t]
                den = pv[r * t:r * (t + 1), 128:256] + jnp.exp(sink - ms[t])
                os_.append(pv[r * t:r * (t + 1), 0:128] / den)
            for cc in range(2):
                c = 2 * g + cc
                o_ref[0, pl.ds(q0, r), 128 * c:128 * (c + 1)] = jnp.where(
                    first_head, os_[2 * cc], os_[2 * cc + 1]).astype(BF16)
        return carry

    lax.fori_loop(0, tq // r, block, 0)


def _att0(q, k2, v2, sink):
    bsz, s, nq = q.shape
    nkv2 = k2.shape[-1]
    tq = TM
    r = A_RADIUS
    per = tq // r
    last = s // r - 1
    cur = lambda w_: pl.BlockSpec((1, tq, w_), lambda b, i: (b, i, 0))
    prev = pl.BlockSpec((1, r, nkv2), lambda b, i: (b, jnp.maximum(i * per - 1, 0), 0))
    nxt = pl.BlockSpec((1, r, nkv2), lambda b, i: (b, jnp.minimum(i * per + per, last), 0))
    return pl.pallas_call(
        functools.partial(_att0_kernel, tq=tq, seq=s),
        grid=(bsz, s // tq),
        in_specs=[pl.BlockSpec(memory_space=pltpu.SMEM),
                  cur(nq), cur(nkv2), prev, nxt, cur(nkv2), prev, nxt],
        out_specs=cur(nq),
        out_shape=jax.ShapeDtypeStruct((bsz, s, nq), BF16),
        scratch_shapes=[pltpu.VMEM((A_KV_HEADS, tq + 2 * r, 128), BF16),
                        pltpu.VMEM((A_KV_HEADS, tq + 2 * r, 256), BF16)],
        compiler_params=_cparams("parallel", "parallel"),
        name="att0",
    )(sink, q, k2, k2, k2, v2, v2, v2)


def _post_kernel(o_ref, wo_ref, x_ref, gm_ref, gf_ref, scf_ref, shf_ref, rw_ref, rb_ref,
                 x1_ref, h2_ref, cls_ref):
    mix = jnp.dot(o_ref[0], wo_ref[...], preferred_element_type=F32)
    x1 = x_ref[0] + gm_ref[0] * mix
    x1_ref[0] = x1
    h2 = _rms_mod(x1, gf_ref[...], scf_ref[0], shf_ref[0]).astype(BF16)
    h2_ref[0] = h2
    tq = h2.shape[0]
    logits = lax.dot_general(rw_ref[...], h2, (((1,), (1,)), ((), ())), preferred_element_type=F32)
    scores = jax.nn.sigmoid(logits)
    biased = scores + rb_ref[...]
    pj = [biased[8 * j:8 * (j + 1)] for j in range(PER_GROUP)]
    sel = []
    for j in range(PER_GROUP):
        beaten = jnp.zeros((8, tq), F32)
        for j2 in range(PER_GROUP):
            if j2 == j:
                continue
            beats = (pj[j2] > pj[j]) | (pj[j2] == pj[j]) if j2 < j else (pj[j2] > pj[j])
            beaten = beaten + jnp.where(beats, 1.0, 0.0)
        sel.append(beaten < 2.0)
    gscore = jnp.zeros((8, tq), F32)
    for j in range(PER_GROUP):
        gscore = gscore + jnp.where(sel[j], pj[j], 0.0)
    gi = lax.broadcasted_iota(I32, (8, tq), 0).astype(F32)
    gmax = jnp.max(gscore, axis=0, keepdims=True)
    gidx = jnp.min(jnp.where(gscore == gmax, gi, 8.0), axis=0, keepdims=True)
    onehot = gi == gidx
    f = [jnp.max(jnp.where(onehot & sel[j], 1.0, 0.0), axis=0, keepdims=True) > 0.5
         for j in range(PER_GROUP)]
    pair = jnp.where(f[0], jnp.where(f[1], 0.0, jnp.where(f[2], 1.0, 2.0)),
                     jnp.where(f[1], jnp.where(f[2], 3.0, 4.0), 5.0))
    cls = (gidx * float(len(PAIRS)) + pair).astype(I32)
    cls_ref[0] = jnp.broadcast_to(cls, (8, tq))


def _post(o, wo, x, gm, gf, scf, shf, rw32, rb32):
    bsz, s, d = x.shape
    do = o.shape[-1]
    tq = TM
    nt = s // tq
    tok = lambda w_: pl.BlockSpec((1, tq, w_), lambda b, i: (b, i, 0))
    per_b = pl.BlockSpec((1, 1, d), lambda b, i: (b, 0, 0))
    full = lambda a: pl.BlockSpec(a.shape, lambda b, i: (0,) * a.ndim)
    gf2 = gf.reshape(1, d)
    return pl.pallas_call(
        _post_kernel,
        grid=(bsz, nt),
        in_specs=[tok(do), full(wo), tok(d), per_b, full(gf2), per_b, per_b, full(rw32), full(rb32)],
        out_specs=[tok(d), tok(d), pl.BlockSpec((1, 8, tq), lambda b, i: (b * nt + i, 0, 0))],
        out_shape=[jax.ShapeDtypeStruct((bsz, s, d), F32),
                   jax.ShapeDtypeStruct((bsz, s, d), BF16),
                   jax.ShapeDtypeStruct((bsz * nt, 8, tq), I32)],
        compiler_params=_cparams("parallel", "parallel"),
        name="post_attention",
    )(o, wo, x, gm, gf2, scf, shf, rw32, rb32)


def _rank_kernel(cls_ref, rank_ref, cnt_ref, carry):
    i = pl.program_id(0)

    @pl.when(i == 0)
    def _():
        carry[...] = jnp.zeros_like(carry)

    tq = cls_ref.shape[-1]
    cls = cls_ref[0][0:1, :]
    oh = lax.broadcasted_iota(I32, (32, tq), 0) == cls
    ohf = jnp.where(oh, 1.0, 0.0)
    before = lax.broadcasted_iota(I32, (tq, tq), 0) < lax.broadcasted_iota(I32, (tq, tq), 1)
    upper = jnp.where(before, 1.0, 0.0).astype(BF16)
    within = jnp.dot(ohf.astype(BF16), upper, preferred_element_type=F32)
    c = carry[...]
    tot = jnp.sum(jnp.where(oh, within + c[:, 0:1], 0.0), axis=0, keepdims=True)
    rank_ref[0] = jnp.broadcast_to(tot.astype(I32), (8, tq))
    newc = c + jnp.sum(ohf, axis=1, keepdims=True)
    carry[...] = newc
    cnt_ref[...] = newc


def _class_ranks(cls3):
    nt, _, tq = cls3.shape
    blk = pl.BlockSpec((1, 8, tq), lambda i: (i, 0, 0))
    return pl.pallas_call(
        _rank_kernel,
        grid=(nt,),
        in_specs=[blk],
        out_specs=[blk, pl.BlockSpec((32, 128), lambda i: (0, 0))],
        out_shape=[jax.ShapeDtypeStruct((nt, 8, tq), I32), jax.ShapeDtypeStruct((32, 128), F32)],
        scratch_shapes=[pltpu.VMEM((32, 128), F32)],
        compiler_params=_cparams("arbitrary"),
        name="class_ranks",
    )(cls3)


def _ffn_kernel(e1_ref, e2_ref, nused_ref, hs_ref, wg1, wu1, wd1, rw1, wg2, wu2, wd2, rw2, ys_ref):
    j = pl.program_id(0)

    @pl.when(j < nused_ref[0])
    def _():
        x = hs_ref[...]
        xf = x.astype(F32)
        s1 = jax.nn.sigmoid(jnp.sum(xf * rw1[0].astype(F32), axis=-1, keepdims=True))
        s2 = jax.nn.sigmoid(jnp.sum(xf * rw2[0].astype(F32), axis=-1, keepdims=True))
        den = s1 + s2

        def expert(wg, wu, wd):
            gt = jnp.dot(x, wg[0], preferred_element_type=F32)
            up = jnp.dot(x, wu[0], preferred_element_type=F32)
            a = (gt * jax.nn.sigmoid(gt)) * up
            return jnp.dot(a.astype(BF16), wd[0], preferred_element_type=F32)

        y = (s1 / den) * expert(wg1, wu1, wd1) + (s2 / den) * expert(wg2, wu2, wd2)
        ys_ref[...] = y.astype(BF16)

    @pl.when(j >= nused_ref[0])
    def _():
        ys_ref[...] = jnp.zeros_like(ys_ref)


def _ffn(hs, e1, e2, nused, wg, wu, wd, rw3):
    npad, d = hs.shape
    tm = FFN_TM
    de = wg.shape[-1]
    row = pl.BlockSpec((tm, d), lambda j, e1, e2, nu: (jnp.minimum(j, nu[0] - 1), 0))
    first = lambda shape: pl.BlockSpec((1,) + shape, lambda j, e1, e2, nu: (e1[j], 0, 0))
    second = lambda shape: pl.BlockSpec((1,) + shape, lambda j, e1, e2, nu: (e2[j], 0, 0))
    grid_spec = pltpu.PrefetchScalarGridSpec(
        num_scalar_prefetch=3,
        grid=(npad // tm,),
        in_specs=[row,
                  first((d, de)), first((d, de)), first((de, d)), first((1, d)),
                  second((d, de)), second((d, de)), second((de, d)), second((1, d))],
        out_specs=pl.BlockSpec((tm, d), lambda j, e1, e2, nu: (j, 0)),
    )
    return pl.pallas_call(
        _ffn_kernel,
        grid_spec=grid_spec,
        out_shape=jax.ShapeDtypeStruct((npad, d), BF16),
        compiler_params=_cparams("arbitrary"),
        name="expert_ffn",
    )(e1, e2, nused, hs, wg, wu, wd, rw3, wg, wu, wd, rw3)


def _moe(h2, cls3, wg, wu, wd, rw3):
    bsz, s, d = h2.shape
    t = bsz * s
    tm = FFN_TM
    rank3, cnt = _class_ranks(cls3)
    cls = cls3[:, 0, :].reshape(t)
    rank = rank3[:, 0, :].reshape(t)
    counts = cnt[:N_CLASSES, 0].astype(I32)
    padded = ((counts + tm - 1) // tm) * tm
    ends = jnp.cumsum(padded)
    base = ends - padded
    pos = base[cls] + rank
    ntiles = t // tm + N_CLASSES
    npad = ntiles * tm
    tcls = jnp.minimum(jnp.searchsorted(ends, jnp.arange(ntiles, dtype=I32) * tm, side="right"),
                       N_CLASSES - 1).astype(I32)
    nused = (ends[-1:] // tm).astype(I32)
    grp = tcls // len(PAIRS)
    pr = tcls % len(PAIRS)
    e1 = (grp * PER_GROUP + jnp.array([p[0] for p in PAIRS], I32)[pr]).astype(I32)
    e2 = (grp * PER_GROUP + jnp.array([p[1] for p in PAIRS], I32)[pr]).astype(I32)
    perm = jnp.zeros((npad,), I32).at[pos].set(jnp.arange(t, dtype=I32))
    hs = jnp.take(h2.reshape(t, d), perm, axis=0)
    ys = _ffn(hs, e1, e2, nused, wg, wu, wd, rw3)
    return jnp.take(ys, pos, axis=0).reshape(bsz, s, d)


def _qkv1_kernel(x_ref, y_ref, gfp_ref, g_ref, sc_ref, sh_ref, w_ref, cos_ref, sin_ref,
                 x2_ref, *rest):
    outs = rest[:9]
    hb_scr, ysc = rest[9:]
    tm = x_ref.shape[1]
    x2 = x_ref[0] + gfp_ref[0] * y_ref[0].astype(F32)
    x2_ref[0] = x2
    hb_scr[...] = _rms_mod(x2, g_ref[...], sc_ref[0], sh_ref[0]).astype(BF16)
    cos = cos_ref[0]
    sin = sin_ref[0]
    lane_lo = (lax.broadcasted_iota(I32, cos.shape, 1) % HEAD_DIM) < (HEAD_DIM // 2)
    for gi, (_, dil) in enumerate(B_GROUPS):
        for j in range(3):
            c = gi * 3 + j
            y = jnp.dot(hb_scr[...], w_ref[:, B_WIDTH * c:B_WIDTH * (c + 1)], preferred_element_type=F32)
            out = outs[c]
            for cc in range(B_WIDTH // 128):
                lanes = slice(128 * cc, 128 * (cc + 1))
                r = y[:, lanes]
                if j < 2:
                    r = _rope_apply(r, cos, sin, lane_lo)
                if j == 0:
                    r = r * (HEAD_DIM ** -0.5)
                if dil == 1:
                    out[0, 0, :, lanes] = r.astype(BF16)
                else:
                    ysc[cc] = r
                    for rr in range(dil):
                        out[0, rr, :, lanes] = ysc[cc, pl.ds(rr, tm // dil, stride=dil), :].astype(BF16)


def _qkv1(x, y, gfp, g, sc, sh, w, cos, sin):
    bsz, s, d = x.shape
    tm = TM
    tok = lambda w_: pl.BlockSpec((1, tm, w_), lambda b, i: (b, i, 0))
    per_b = pl.BlockSpec((1, 1, d), lambda b, i: (b, 0, 0))
    out_specs = [tok(d)]
    out_shape = [jax.ShapeDtypeStruct((bsz, s, d), F32)]
    for _, dil in B_GROUPS:
        for _ in range(3):
            out_specs.append(pl.BlockSpec((1, dil, tm // dil, B_WIDTH), lambda b, i: (b, 0, i, 0)))
            out_shape.append(jax.ShapeDtypeStruct((bsz, dil, s // dil, B_WIDTH), BF16))
    return pl.pallas_call(
        _qkv1_kernel,
        grid=(bsz, s // tm),
        in_specs=[tok(d), tok(d), per_b, pl.BlockSpec((1, d), lambda b, i: (0, 0)), per_b, per_b,
                  pl.BlockSpec(w.shape, lambda b, i: (0, 0)), tok(128), tok(128)],
        out_specs=out_specs,
        out_shape=out_shape,
        scratch_shapes=[pltpu.VMEM((tm, d), BF16), pltpu.VMEM((B_WIDTH // 128, tm, 128), F32)],
        compiler_params=_cparams("parallel", "parallel"),
        name="qkv1",
    )(x, y, gfp, g.reshape(1, d), sc, sh, w, cos, sin)


def _att1_kernel(*refs, chunk, seq):
    ins = refs[:21]
    o_ref = refs[21]
    scr = refs[22:]
    kfs, vfs, nat_o, nat_l = scr[0:3], scr[3:6], scr[6:9], scr[9:12]
    n = pl.program_id(1)
    rad = B_RADIUS
    npair = B_WIDTH // 128
    for gi, (_, dil) in enumerate(B_GROUPS):
        _, kc, kp, kn, vc, vp, vn = ins[7 * gi:7 * gi + 7]
        lseg = chunk // dil
        for full, (p_, c_, n_) in ((kfs[gi], (kp, kc, kn)), (vfs[gi], (vp, vc, vn))):
            full[:, 0:rad] = p_[0]
            full[:, rad:rad + lseg] = c_[0]
            full[:, rad + lseg:rad + lseg + rad] = n_[0]

    for gi, (_, dil) in enumerate(B_GROUPS):
        q_ref = ins[7 * gi]
        kf, vf = kfs[gi], vfs[gi]
        lseg = chunk // dil
        qb = min(2 * rad, lseg)
        kb_n = qb + 2 * rad
        nblk = lseg // qb
        lstr = seq // dil
        ii = lax.broadcasted_iota(I32, (qb, kb_n), 0)
        jj = lax.broadcasted_iota(I32, (qb, kb_n), 1)
        rel = jj - ii
        band = (rel >= 0) & (rel <= 2 * rad)
        first_head = lax.broadcasted_iota(I32, (qb, 128), 1) < HEAD_DIM
        zero = jnp.zeros((qb, 128), BF16)
        ones = jnp.ones((kb_n, 128), BF16)

        def body(t, carry, gi=gi, q_ref=q_ref, kf=kf, vf=vf, lseg=lseg, qb=qb, kb_n=kb_n, nblk=nblk,
                 lstr=lstr, jj=jj, band=band, dil=dil, first_head=first_head, zero=zero, ones=ones):
            res = t // nblk
            q0 = pl.multiple_of((t % nblk) * qb, qb)
            apos = n * lseg + q0 - rad + jj
            bias = jnp.where(band & (apos >= 0) & (apos < lstr), 0.0, NEG)
            if dil == 1:
                rows = pl.ds(q0, qb)
            else:
                rows = pl.ds(res + dil * q0, qb, stride=dil)
            for c in range(npair):
                lanes = slice(128 * c, 128 * (c + 1))
                qc = q_ref[0, res, pl.ds(q0, qb), lanes]
                kb = kf[res, pl.ds(q0, kb_n), lanes]
                vb = jnp.concatenate([vf[res, pl.ds(q0, kb_n), lanes], ones], axis=1)
                qs = jnp.concatenate([jnp.where(first_head, qc, zero), jnp.where(first_head, zero, qc)],
                                     axis=0)
                s2 = lax.dot_general(qs, kb, (((1,), (1,)), ((), ())), preferred_element_type=F32)
                os_, ls_ = [], []
                for t2 in range(2):
                    s = s2[qb * t2:qb * (t2 + 1)] + bias
                    m = jnp.max(s, axis=-1, keepdims=True)
                    p = jnp.exp(s - m).astype(BF16)
                    pv = jnp.dot(p, vb, preferred_element_type=F32)
                    den = pv[:, 128:256]
                    os_.append(pv[:, 0:128] / den)
                    ls_.append(m + jnp.log(den))
                nat_o[gi][c, rows, :] = jnp.where(first_head, os_[0], os_[1])
                nat_l[gi][c, rows, :] = jnp.where(first_head, ls_[0], ls_[1])
            return carry

        lax.fori_loop(0, dil * nblk, body, 0)

    for c in range(npair):
        a = [nat_o[gi][c] for gi in range(3)]
        ls = [nat_l[gi][c] for gi in range(3)]
        mx = jnp.maximum(jnp.maximum(ls[0], ls[1]), ls[2])
        w = [jnp.exp(x - mx) for x in ls]
        num = w[0] * a[0] + w[1] * a[1] + w[2] * a[2]
        o_ref[0, :, 128 * c:128 * (c + 1)] = (num / (w[0] + w[1] + w[2])).astype(BF16)


def _att1(qkv):
    bsz = qkv[0].shape[0]
    seq = qkv[0].shape[1] * qkv[0].shape[2]
    chunk = ATT1_CHUNK
    rad = B_RADIUS
    in_specs, args, scratch = [], [], []
    for gi, (_, dil) in enumerate(B_GROUPS):
        q, k, v = qkv[3 * gi:3 * gi + 3]
        lseg = chunk // dil
        per = lseg // rad
        last = seq // dil // rad - 1
        cur = pl.BlockSpec((1, dil, lseg, B_WIDTH), lambda b, n: (b, 0, n, 0))
        prev = pl.BlockSpec((1, dil, rad, B_WIDTH),
                            lambda b, n, per=per: (b, 0, jnp.maximum(n * per - 1, 0), 0))
        nxt = pl.BlockSpec((1, dil, rad, B_WIDTH),
                           lambda b, n, per=per, last=last: (b, 0, jnp.minimum(n * per + per, last), 0))
        in_specs += [cur, cur, prev, nxt, cur, prev, nxt]
        args += [q, k, k, k, v, v, v]
    for _ in range(2):
        for _, dil in B_GROUPS:
            scratch.append(pltpu.VMEM((dil, chunk // dil + 2 * rad, B_WIDTH), BF16))
    for _ in range(2):
        for _ in B_GROUPS:
            scratch.append(pltpu.VMEM((B_WIDTH // 128, chunk, 128), F32))
    return pl.pallas_call(
        functools.partial(_att1_kernel, chunk=chunk, seq=seq),
        grid=(bsz, seq // chunk),
        in_specs=in_specs,
        out_specs=pl.BlockSpec((1, chunk, B_WIDTH), lambda b, n: (b, n, 0)),
        out_shape=jax.ShapeDtypeStruct((bsz, seq, B_WIDTH), BF16),
        scratch_shapes=scratch,
        compiler_params=_cparams("parallel", "parallel"),
        name="att1",
    )(*args)


def _final_kernel(x_ref, y_ref, gfp_ref, g_ref, o_ref):
    x = x_ref[0] + gfp_ref[0] * y_ref[0].astype(F32)
    ms = jnp.mean(x * x, axis=-1, keepdims=True)
    o_ref[0] = (x * lax.rsqrt(ms + NORM_EPS)) * g_ref[...]


def _final(x, y, gfp, g):
    bsz, s, d = x.shape
    tm = TM
    tok = pl.BlockSpec((1, tm, d), lambda b, i: (b, i, 0))
    return pl.pallas_call(
        _final_kernel,
        grid=(bsz, s // tm),
        in_specs=[tok, tok, pl.BlockSpec((1, 1, d), lambda b, i: (b, 0, 0)),
                  pl.BlockSpec((1, d), lambda b, i: (0, 0))],
        out_specs=tok,
        out_shape=jax.ShapeDtypeStruct((bsz, s, d), F32),
        compiler_params=_cparams("parallel", "parallel"),
        name="final_norm",
    )(x, y, gfp, g.reshape(1, d))


def kernel(x, c, positions, ada_w, ada_b, norm_mix_g, norm_ffn_g, a_w_qkv, a_w_o, a_sink, b_w_qkv, b_w_o,
           router_w, router_bias, exp_w_gate, exp_w_up, exp_w_down, final_norm_g):
    bsz, s, d = x.shape
    assert d == D_MODEL and ada_w.shape[0] == 2 and s % ATT1_CHUNK == 0
    assert all(w_ // (2 * dil) == B_RADIUS for w_, dil in B_GROUPS)

    mod = _modulation(c, ada_w, ada_b)
    mods = [[mod[l][:, k * d:(k + 1) * d].reshape(bsz, 1, d) for k in range(6)] for l in range(2)]
    cos, sin = _rope_tables(positions)

    rw_t = router_w.T.astype(BF16)
    rows = jnp.array([PER_GROUP * g + j if g < N_EXPERT_GROUPS else 0
                      for j in range(PER_GROUP) for g in range(8)], I32)
    live = jnp.array([1.0 if g < N_EXPERT_GROUPS else 0.0
                      for j in range(PER_GROUP) for g in range(8)], F32)
    rw32 = (rw_t[rows].astype(F32) * live[:, None]).astype(BF16)
    rb32 = jnp.where(live > 0, router_bias.astype(F32)[rows], NEG).reshape(32, 1)
    rw3 = rw_t.reshape(N_EXPERTS, 1, d)
    wg, wu, wd = exp_w_gate.astype(BF16), exp_w_up.astype(BF16), exp_w_down.astype(BF16)

    sh_m, sc_m, g_m, sh_f, sc_f, g_f = mods[0]
    q, k, v = _qkv0(x, norm_mix_g[0], sc_m, sh_m, a_w_qkv[0].astype(BF16), cos, sin)
    o = _att0(q, k, v, a_sink[0].astype(F32))
    x1, h2, cls3 = _post(o, a_w_o[0].astype(BF16), x, g_m, norm_ffn_g[0], sc_f, sh_f, rw32, rb32)
    y = _moe(h2, cls3, wg[0], wu[0], wd[0], rw3)
    g_f_prev = g_f

    sh_m, sc_m, g_m, sh_f, sc_f, g_f = mods[1]
    outs = _qkv1(x1, y, g_f_prev, norm_mix_g[1], sc_m, sh_m, b_w_qkv[0].astype(BF16), cos, sin)
    x2, qkv = outs[0], outs[1:]
    o = _att1(qkv)
    x3, h2, cls3 = _post(o, b_w_o[0].astype(BF16), x2, g_m, norm_ffn_g[1], sc_f, sh_f, rw32, rb32)
    y = _moe(h2, cls3, wg[1], wu[1], wd[1], rw3)

    return _final(x3, y, g_f, final_norm_g)
```

```python
import functools

import jax
import jax.numpy as jnp
from jax import lax
from jax.experimental import pallas as pl
from jax.experimental.pallas import tpu as pltpu
from jax.experimental.pallas import tpu_sc as plsc

F32, BF16, I32, U32 = jnp.float32, jnp.bfloat16, jnp.int32, jnp.uint32

D_MODEL = 1024
HEAD_DIM = 64
ROPE_THETA = 10000.0
NORM_EPS = 1e-6
A_Q_HEADS = 16
A_KV_HEADS = 4
A_GROUP = A_Q_HEADS // A_KV_HEADS
A_RADIUS = 128
B_GROUPS = ((128, 1), (512, 4), (2048, 16))
B_HEADS = 8
B_RADIUS = 64
B_WIDTH = B_HEADS * HEAD_DIM
N_EXPERTS = 16
N_EXPERT_GROUPS = 4
PER_GROUP = N_EXPERTS // N_EXPERT_GROUPS
D_EXPERT = D_MODEL // 2
PAIRS = ((0, 1), (0, 2), (0, 3), (1, 2), (1, 3), (2, 3))
N_CLASSES = N_EXPERT_GROUPS * len(PAIRS)
NEG = -1e30

VMEM_LIMIT = 56 * 1024 * 1024
TM = 512
ATT1_CHUNK = 1024
FFN_TM = 256
SC_CORES_V7X = 2
SC_SUBCORES_V7X = 16
SC_MAX_INDEX_VECTOR = 128


def _cparams(*sem):
    return pltpu.CompilerParams(dimension_semantics=sem, vmem_limit_bytes=VMEM_LIMIT)


def _rms_mod(x, g, sc, sh):
    ms = jnp.mean(x * x, axis=-1, keepdims=True)
    return (x * lax.rsqrt(ms + NORM_EPS)) * g * (1.0 + sc) + sh


def _rope_apply(y, cos, sin_signed, lane_lo):
    sw = jnp.where(lane_lo, pltpu.roll(y, 96, 1), pltpu.roll(y, 32, 1))
    return y * cos + sw * sin_signed


def _pack_bf16_pairs(x):
    n = x.shape[1] // 2
    lo = lax.bitcast_convert_type(x[:, :n].astype(BF16).astype(F32), U32)
    hi = lax.bitcast_convert_type(x[:, n:].astype(BF16).astype(F32), U32)
    return (hi & jnp.uint32(0xFFFF0000)) | (lo >> 16)


def _unpack_bf16_pairs(p):
    lo = lax.bitcast_convert_type(p << 16, F32)
    hi = lax.bitcast_convert_type(p & jnp.uint32(0xFFFF0000), F32)
    return jnp.concatenate([lo, hi], axis=1)


def _dup_head(x, first_head, which):
    rolled = pltpu.roll(x, HEAD_DIM, 1)
    return jnp.where(first_head, x, rolled) if which == 0 else jnp.where(first_head, rolled, x)


def _mod_kernel(c_ref, w_ref, b_ref, o_ref):
    c = c_ref[...]
    s = c * jax.nn.sigmoid(c)
    o_ref[0] = jnp.dot(s, w_ref[0], preferred_element_type=F32,
                       precision=lax.Precision.HIGHEST) + b_ref[0]


def _modulation(c, ada_w, ada_b):
    depth, d, n = ada_w.shape
    bsz = c.shape[0]
    cp = jnp.zeros((8, d), F32).at[:bsz].set(c)
    tn = 1536
    out = pl.pallas_call(
        _mod_kernel,
        grid=(depth, n // tn),
        in_specs=[pl.BlockSpec((8, d), lambda l, j: (0, 0)),
                  pl.BlockSpec((1, d, tn), lambda l, j: (l, 0, j)),
                  pl.BlockSpec((1, 1, tn), lambda l, j: (l, 0, j))],
        out_specs=pl.BlockSpec((1, 8, tn), lambda l, j: (l, 0, j)),
        out_shape=jax.ShapeDtypeStruct((depth, 8, n), F32),
        compiler_params=_cparams("arbitrary", "arbitrary"),
        name="modulation",
    )(cp, ada_w, ada_b.reshape(depth, 1, n))
    return out[:, :bsz]


def _rope_kernel(pos_ref, invf_ref, sign_ref, cos_ref, sin_ref):
    ang = pos_ref[0].astype(F32) * invf_ref[...]
    cos_ref[0] = jnp.cos(ang)
    sin_ref[0] = jnp.sin(ang) * sign_ref[...]


def _rope_tables(positions):
    bsz, s = positions.shape
    inv_freq = ROPE_THETA ** (-jnp.arange(0, HEAD_DIM, 2, dtype=F32) / HEAD_DIM)
    invf = jnp.tile(inv_freq, 4).reshape(1, 128)
    half = HEAD_DIM // 2
    sign = jnp.tile(jnp.concatenate([-jnp.ones((half,), F32), jnp.ones((half,), F32)]), 2).reshape(1, 128)
    ts = TM
    return pl.pallas_call(
        _rope_kernel,
        grid=(bsz, s // ts),
        in_specs=[pl.BlockSpec((1, ts, 1), lambda b, i: (b, i, 0)),
                  pl.BlockSpec((1, 128), lambda b, i: (0, 0)),
                  pl.BlockSpec((1, 128), lambda b, i: (0, 0))],
        out_specs=[pl.BlockSpec((1, ts, 128), lambda b, i: (b, i, 0))] * 2,
        out_shape=[jax.ShapeDtypeStruct((bsz, s, 128), F32)] * 2,
        compiler_params=_cparams("parallel", "parallel"),
        name="rope_tables",
    )(positions.reshape(bsz, s, 1), invf, sign)


def _qkv0_kernel(x_ref, g_ref, sc_ref, sh_ref, w_ref, cos_ref, sin_ref,
                 q_ref, k_ref, v_ref, hb_scr):
    hb_scr[...] = _rms_mod(x_ref[0], g_ref[...], sc_ref[0], sh_ref[0]).astype(BF16)
    cos = cos_ref[0]
    sin = sin_ref[0]
    lane_lo = (lax.broadcasted_iota(I32, cos.shape, 1) % HEAD_DIM) < (HEAD_DIM // 2)
    nq = A_Q_HEADS * HEAD_DIM
    nkv = A_KV_HEADS * HEAD_DIM
    for c in range(nq // 256):
        y = jnp.dot(hb_scr[...], w_ref[:, 256 * c:256 * (c + 1)], preferred_element_type=F32)
        for cc in range(2):
            r = _rope_apply(y[:, 128 * cc:128 * (cc + 1)], cos, sin, lane_lo) * (HEAD_DIM ** -0.5)
            q_ref[0, :, 256 * c + 128 * cc:256 * c + 128 * (cc + 1)] = r.astype(BF16)
    first_head = lax.broadcasted_iota(I32, cos.shape, 1) < HEAD_DIM
    y = jnp.dot(hb_scr[...], w_ref[:, nq:nq + nkv], preferred_element_type=F32)
    for cc in range(nkv // 128):
        r = _rope_apply(y[:, 128 * cc:128 * (cc + 1)], cos, sin, lane_lo)
        for which in range(2):
            g = 2 * cc + which
            k_ref[0, :, 128 * g:128 * (g + 1)] = _dup_head(r, first_head, which).astype(BF16)
    y = jnp.dot(hb_scr[...], w_ref[:, nq + nkv:nq + 2 * nkv], preferred_element_type=F32)
    for cc in range(nkv // 128):
        r = y[:, 128 * cc:128 * (cc + 1)]
        for which in range(2):
            g = 2 * cc + which
            v_ref[0, :, 128 * g:128 * (g + 1)] = _dup_head(r, first_head, which).astype(BF16)


def _qkv0(x, g, sc, sh, w, cos, sin):
    bsz, s, d = x.shape
    nq = A_Q_HEADS * HEAD_DIM
    nkv = A_KV_HEADS * HEAD_DIM
    tm = TM
    tok = lambda w_: pl.BlockSpec((1, tm, w_), lambda b, i: (b, i, 0))
    per_b = pl.BlockSpec((1, 1, d), lambda b, i: (b, 0, 0))
    return pl.pallas_call(
        _qkv0_kernel,
        grid=(bsz, s // tm),
        in_specs=[tok(d), pl.BlockSpec((1, d), lambda b, i: (0, 0)), per_b, per_b,
                  pl.BlockSpec(w.shape, lambda b, i: (0, 0)), tok(128), tok(128)],
        out_specs=[tok(nq), tok(2 * nkv), tok(2 * nkv)],
        out_shape=[jax.ShapeDtypeStruct((bsz, s, nq), BF16),
                   jax.ShapeDtypeStruct((bsz, s, 2 * nkv), BF16),
                   jax.ShapeDtypeStruct((bsz, s, 2 * nkv), BF16)],
        scratch_shapes=[pltpu.VMEM((tm, d), BF16)],
        compiler_params=_cparams("parallel", "parallel"),
        name="qkv0",
    )(x, g.reshape(1, d), sc, sh, w, cos, sin)


def _att0_kernel(sink_ref, q_ref, kc_ref, kp_ref, kn_ref, vc_ref, vp_ref, vn_ref,
                 o_ref, kf, vf, *, tq, seq):
    i = pl.program_id(1)
    r = A_RADIUS
    ones = jnp.ones((tq + 2 * r, 128), BF16)
    for g in range(A_KV_HEADS):
        lanes = slice(128 * g, 128 * (g + 1))
        kf[g, 0:r] = kp_ref[0, :, lanes]
        kf[g, r:r + tq] = kc_ref[0, :, lanes]
        kf[g, r + tq:r + tq + r] = kn_ref[0, :, lanes]
        vf[g, 0:r, 0:128] = vp_ref[0, :, lanes]
        vf[g, r:r + tq, 0:128] = vc_ref[0, :, lanes]
        vf[g, r + tq:r + tq + r, 0:128] = vn_ref[0, :, lanes]
        vf[g, :, 128:256] = ones

    ii = lax.broadcasted_iota(I32, (r, r), 0)
    jj = lax.broadcasted_iota(I32, (r, r), 1)
    first_head = jj < HEAD_DIM
    zero = jnp.zeros((r, 128), BF16)

    def block(j, carry):
        q0 = pl.multiple_of(j * r, r)
        base = i * tq + q0
        bias_lo = jnp.where((jj >= ii) & (base - r + jj >= 0), 0.0, NEG)
        bias_hi = jnp.where((jj <= ii) & (base + r + jj < seq), 0.0, NEG)
        for g in range(A_KV_HEADS):
            kb = kf[g, pl.ds(q0, 3 * r), :]
            vb = vf[g, pl.ds(q0, 3 * r), :]
            qs = []
            for cc in range(2):
                c = 2 * g + cc
                qc = q_ref[0, pl.ds(q0, r), 128 * c:128 * (c + 1)]
                qs += [jnp.where(first_head, qc, zero), jnp.where(first_head, zero, qc)]
            s4 = lax.dot_general(jnp.concatenate(qs, axis=0), kb, (((1,), (1,)), ((), ())),
                                 preferred_element_type=F32)
            ps, ms = [], []
            for t in range(A_GROUP):
                s = s4[r * t:r * (t + 1)]
                a0 = s[:, 0:r] + bias_lo
                a1 = s[:, r:2 * r]
                a2 = s[:, 2 * r:3 * r] + bias_hi
                sink = sink_ref[A_GROUP * g + t]
                m = jnp.max(jnp.maximum(jnp.maximum(a0, a1), a2), axis=-1, keepdims=True)
                m = jnp.maximum(m, sink)
                ps.append(jnp.concatenate([jnp.exp(a0 - m), jnp.exp(a1 - m), jnp.exp(a2 - m)],
                                          axis=1).astype(BF16))
                ms.append(m)
            pv = jnp.dot(jnp.concatenate(ps, axis=0), vb, preferred_element_type=F32)
            os_ = []
            for t in range(A_GROUP):
                sink = sink_ref[A_GROUP * g + t]
                den = pv[r * t:r * (t + 1), 128:256] + jnp.exp(sink - ms[t])
                os_.append(pv[r * t:r * (t + 1), 0:128] / den)
            for cc in range(2):
                c = 2 * g + cc
                o_ref[0, pl.ds(q0, r), 128 * c:128 * (c + 1)] = jnp.where(
                    first_head, os_[2 * cc], os_[2 * cc + 1]).astype(BF16)
        return carry

    lax.fori_loop(0, tq // r, block, 0)


def _att0(q, k2, v2, sink):
    bsz, s, nq = q.shape
    nkv2 = k2.shape[-1]
    tq = TM
    r = A_RADIUS
    per = tq // r
    last = s // r - 1
    cur = lambda w_: pl.BlockSpec((1, tq, w_), lambda b, i: (b, i, 0))
    prev = pl.BlockSpec((1, r, nkv2), lambda b, i: (b, jnp.maximum(i * per - 1, 0), 0))
    nxt = pl.BlockSpec((1, r, nkv2), lambda b, i: (b, jnp.minimum(i * per + per, last), 0))
    return pl.pallas_call(
        functools.partial(_att0_kernel, tq=tq, seq=s),
        grid=(bsz, s // tq),
        in_specs=[pl.BlockSpec(memory_space=pltpu.SMEM),
                  cur(nq), cur(nkv2), prev, nxt, cur(nkv2), prev, nxt],
        out_specs=cur(nq),
        out_shape=jax.ShapeDtypeStruct((bsz, s, nq), BF16),
        scratch_shapes=[pltpu.VMEM((A_KV_HEADS, tq + 2 * r, 128), BF16),
                        pltpu.VMEM((A_KV_HEADS, tq + 2 * r, 256), BF16)],
        compiler_params=_cparams("parallel", "parallel"),
        name="att0",
    )(sink, q, k2, k2, k2, v2, v2, v2)


def _post_kernel(o_ref, wo_ref, x_ref, gm_ref, gf_ref, scf_ref, shf_ref, rw_ref, rb_ref,
                 x1_ref, h2_ref, cls_ref):
    mix = jnp.dot(o_ref[0], wo_ref[...], preferred_element_type=F32)
    x1 = x_ref[0] + gm_ref[0] * mix
    x1_ref[0] = x1
    h2 = _rms_mod(x1, gf_ref[...], scf_ref[0], shf_ref[0]).astype(BF16)
    h2_ref[0] = _pack_bf16_pairs(h2.astype(F32))
    tq = h2.shape[0]
    logits = lax.dot_general(rw_ref[...], h2, (((1,), (1,)), ((), ())), preferred_element_type=F32)
    scores = jax.nn.sigmoid(logits)
    biased = scores + rb_ref[...]
    pj = [biased[8 * j:8 * (j + 1)] for j in range(PER_GROUP)]
    sel = []
    for j in range(PER_GROUP):
        beaten = jnp.zeros((8, tq), F32)
        for j2 in range(PER_GROUP):
            if j2 == j:
                continue
            beats = (pj[j2] > pj[j]) | (pj[j2] == pj[j]) if j2 < j else (pj[j2] > pj[j])
            beaten = beaten + jnp.where(beats, 1.0, 0.0)
        sel.append(beaten < 2.0)
    gscore = jnp.zeros((8, tq), F32)
    for j in range(PER_GROUP):
        gscore = gscore + jnp.where(sel[j], pj[j], 0.0)
    gi = lax.broadcasted_iota(I32, (8, tq), 0).astype(F32)
    gmax = jnp.max(gscore, axis=0, keepdims=True)
    gidx = jnp.min(jnp.where(gscore == gmax, gi, 8.0), axis=0, keepdims=True)
    onehot = gi == gidx
    f = [jnp.max(jnp.where(onehot & sel[j], 1.0, 0.0), axis=0, keepdims=True) > 0.5
         for j in range(PER_GROUP)]
    pair = jnp.where(f[0], jnp.where(f[1], 0.0, jnp.where(f[2], 1.0, 2.0)),
                     jnp.where(f[1], jnp.where(f[2], 3.0, 4.0), 5.0))
    cls = (gidx * float(len(PAIRS)) + pair).astype(I32)
    cls_ref[0] = jnp.broadcast_to(cls, (8, tq))


def _post(o, wo, x, gm, gf, scf, shf, rw32, rb32):
    bsz, s, d = x.shape
    do = o.shape[-1]
    tq = TM
    nt = s // tq
    tok = lambda w_: pl.BlockSpec((1, tq, w_), lambda b, i: (b, i, 0))
    per_b = pl.BlockSpec((1, 1, d), lambda b, i: (b, 0, 0))
    full = lambda a: pl.BlockSpec(a.shape, lambda b, i: (0,) * a.ndim)
    gf2 = gf.reshape(1, d)
    return pl.pallas_call(
        _post_kernel,
        grid=(bsz, nt),
        in_specs=[tok(do), full(wo), tok(d), per_b, full(gf2), per_b, per_b, full(rw32), full(rb32)],
        out_specs=[tok(d), tok(d // 2), pl.BlockSpec((1, 8, tq), lambda b, i: (b * nt + i, 0, 0))],
        out_shape=[jax.ShapeDtypeStruct((bsz, s, d), F32),
                   jax.ShapeDtypeStruct((bsz, s, d // 2), U32),
                   jax.ShapeDtypeStruct((bsz * nt, 8, tq), I32)],
        compiler_params=_cparams("parallel", "parallel"),
        name="post_attention",
    )(o, wo, x, gm, gf2, scf, shf, rw32, rb32)


def _rank_kernel(cls_ref, rank_ref, cnt_ref, carry):
    i = pl.program_id(0)

    @pl.when(i == 0)
    def _():
        carry[...] = jnp.zeros_like(carry)

    tq = cls_ref.shape[-1]
    cls = cls_ref[0][0:1, :]
    oh = lax.broadcasted_iota(I32, (32, tq), 0) == cls
    ohf = jnp.where(oh, 1.0, 0.0)
    before = lax.broadcasted_iota(I32, (tq, tq), 0) < lax.broadcasted_iota(I32, (tq, tq), 1)
    upper = jnp.where(before, 1.0, 0.0).astype(BF16)
    within = jnp.dot(ohf.astype(BF16), upper, preferred_element_type=F32)
    c = carry[...]
    tot = jnp.sum(jnp.where(oh, within + c[:, 0:1], 0.0), axis=0, keepdims=True)
    rank_ref[0] = jnp.broadcast_to(tot.astype(I32), (8, tq))
    newc = c + jnp.sum(ohf, axis=1, keepdims=True)
    carry[...] = newc
    cnt_ref[...] = newc


def _class_ranks(cls3):
    nt, _, tq = cls3.shape
    blk = pl.BlockSpec((1, 8, tq), lambda i: (i, 0, 0))
    return pl.pallas_call(
        _rank_kernel,
        grid=(nt,),
        in_specs=[blk],
        out_specs=[blk, pl.BlockSpec((32, 128), lambda i: (0, 0))],
        out_shape=[jax.ShapeDtypeStruct((nt, 8, tq), I32), jax.ShapeDtypeStruct((32, 128), F32)],
        scratch_shapes=[pltpu.VMEM((32, 128), F32)],
        compiler_params=_cparams("arbitrary"),
        name="class_ranks",
    )(cls3)


def _ffn_kernel(e1_ref, e2_ref, nused_ref, hs_ref, wg1, wu1, wd1, rw1, wg2, wu2, wd2, rw2, ys_ref):
    j = pl.program_id(0)

    @pl.when(j < nused_ref[0])
    def _():
        xf = _unpack_bf16_pairs(hs_ref[...])
        x = xf.astype(BF16)
        s1 = jax.nn.sigmoid(jnp.sum(xf * rw1[0].astype(F32), axis=-1, keepdims=True))
        s2 = jax.nn.sigmoid(jnp.sum(xf * rw2[0].astype(F32), axis=-1, keepdims=True))
        den = s1 + s2

        def expert(wg, wu, wd):
            gt = jnp.dot(x, wg[0], preferred_element_type=F32)
            up = jnp.dot(x, wu[0], preferred_element_type=F32)
            a = (gt * jax.nn.sigmoid(gt)) * up
            return jnp.dot(a.astype(BF16), wd[0], preferred_element_type=F32)

        y = (s1 / den) * expert(wg1, wu1, wd1) + (s2 / den) * expert(wg2, wu2, wd2)
        ys_ref[...] = _pack_bf16_pairs(y)

    @pl.when(j >= nused_ref[0])
    def _():
        ys_ref[...] = jnp.zeros_like(ys_ref)


def _ffn(hs, e1, e2, nused, wg, wu, wd, rw3):
    npad, dh = hs.shape
    d = 2 * dh
    tm = FFN_TM
    de = wg.shape[-1]
    row = pl.BlockSpec((tm, dh), lambda j, e1, e2, nu: (jnp.minimum(j, nu[0] - 1), 0))
    first = lambda shape: pl.BlockSpec((1,) + shape, lambda j, e1, e2, nu: (e1[j], 0, 0))
    second = lambda shape: pl.BlockSpec((1,) + shape, lambda j, e1, e2, nu: (e2[j], 0, 0))
    grid_spec = pltpu.PrefetchScalarGridSpec(
        num_scalar_prefetch=3,
        grid=(npad // tm,),
        in_specs=[row,
                  first((d, de)), first((d, de)), first((de, d)), first((1, d)),
                  second((d, de)), second((d, de)), second((de, d)), second((1, d))],
        out_specs=pl.BlockSpec((tm, dh), lambda j, e1, e2, nu: (j, 0)),
    )
    return pl.pallas_call(
        _ffn_kernel,
        grid_spec=grid_spec,
        out_shape=jax.ShapeDtypeStruct((npad, dh), U32),
        compiler_params=_cparams("arbitrary"),
        name="expert_ffn",
    )(e1, e2, nused, hs, wg, wu, wd, rw3, wg, wu, wd, rw3)


def _sc_mesh():
    return plsc.VectorSubcoreMesh(core_axis_name="c", subcore_axis_name="s",
                                  num_cores=SC_CORES_V7X, num_subcores=SC_SUBCORES_V7X)


def _sc_rows_per_step(per_worker):
    return min(SC_MAX_INDEX_VECTOR, per_worker)


def _sc_scatter_rows(src, idx, n_out):
    n_in, d = src.shape
    workers = SC_CORES_V7X * SC_SUBCORES_V7X
    per_w = n_in // workers
    ch = _sc_rows_per_step(per_w)
    assert per_w * workers == n_in and per_w % ch == 0 and ch % 8 == 0

    def body(src_hbm, idx_hbm, out_hbm, idx_v, rows_v, sem):
        base = (lax.axis_index("s") * SC_CORES_V7X + lax.axis_index("c")) * per_w

        @pl.loop(0, per_w // ch)
        def _(i):
            off = base + i * ch
            pltpu.sync_copy(idx_hbm.at[pl.ds(off, ch)], idx_v)
            pltpu.sync_copy(src_hbm.at[pl.ds(off, ch)], rows_v)
            pltpu.async_copy(rows_v, out_hbm.at[idx_v], sem).wait()

    return pl.kernel(
        body, mesh=_sc_mesh(),
        out_type=jax.ShapeDtypeStruct((n_out, d), src.dtype),
        scratch_types=[pltpu.VMEM((ch,), I32), pltpu.VMEM((ch, d), src.dtype), pltpu.SemaphoreType.DMA],
        name="sc_scatter_rows",
    )(src, idx)


def _sc_gather_rows(table, idx):
    n_out = idx.shape[0]
    d = table.shape[1]
    workers = SC_CORES_V7X * SC_SUBCORES_V7X
    per_w = n_out // workers
    ch = _sc_rows_per_step(per_w)
    assert per_w * workers == n_out and per_w % ch == 0 and ch % 8 == 0

    def body(table_hbm, idx_hbm, out_hbm, idx_v, rows_v, sem):
        base = (lax.axis_index("s") * SC_CORES_V7X + lax.axis_index("c")) * per_w

        @pl.loop(0, per_w // ch)
        def _(i):
            off = base + i * ch
            pltpu.sync_copy(idx_hbm.at[pl.ds(off, ch)], idx_v)
            pltpu.async_copy(table_hbm.at[idx_v], rows_v, sem).wait()
            pltpu.sync_copy(rows_v, out_hbm.at[pl.ds(off, ch)])

    return pl.kernel(
        body, mesh=_sc_mesh(),
        out_type=jax.ShapeDtypeStruct((n_out, d), table.dtype),
        scratch_types=[pltpu.VMEM((ch,), I32), pltpu.VMEM((ch, d), table.dtype), pltpu.SemaphoreType.DMA],
        name="sc_gather_rows",
    )(table, idx)


def _moe(h2p, cls3, wg, wu, wd, rw3):
    bsz, s, dh = h2p.shape
    t = bsz * s
    tm = FFN_TM
    rank3, cnt = _class_ranks(cls3)
    cls = cls3[:, 0, :].reshape(t)
    rank = rank3[:, 0, :].reshape(t)
    counts = cnt[:N_CLASSES, 0].astype(I32)
    padded = ((counts + tm - 1) // tm) * tm
    upto = jnp.arange(N_CLASSES)[:, None] >= jnp.arange(N_CLASSES)[None, :]
    ends = jnp.sum(jnp.where(upto, padded[None, :], 0), axis=1)
    base = ends - padded
    pos = jnp.take(base, cls, mode="clip") + rank
    ntiles = t // tm + N_CLASSES
    tile_start = jnp.arange(ntiles, dtype=I32) * tm
    tcls = jnp.minimum(jnp.sum((ends[None, :] <= tile_start[:, None]).astype(I32), axis=1), N_CLASSES - 1)
    nused = (ends[-1:] // tm).astype(I32)
    grp = tcls // len(PAIRS)
    pr = tcls % len(PAIRS)
    first = jnp.array([p[0] for p in PAIRS], I32)
    second = jnp.array([p[1] for p in PAIRS], I32)
    e1 = (grp * PER_GROUP + jnp.take(first, pr, mode="clip")).astype(I32)
    e2 = (grp * PER_GROUP + jnp.take(second, pr, mode="clip")).astype(I32)
    hs = _sc_scatter_rows(h2p.reshape(t, dh), pos, ntiles * tm)
    ys = _ffn(hs, e1, e2, nused, wg, wu, wd, rw3)
    return _sc_gather_rows(ys, pos).reshape(bsz, s, dh)


def _qkv1_kernel(x_ref, y_ref, gfp_ref, g_ref, sc_ref, sh_ref, w_ref, cos_ref, sin_ref,
                 x2_ref, *rest):
    outs = rest[:9]
    hb_scr, ysc = rest[9:]
    tm = x_ref.shape[1]
    x2 = x_ref[0] + gfp_ref[0] * _unpack_bf16_pairs(y_ref[0])
    x2_ref[0] = x2
    hb_scr[...] = _rms_mod(x2, g_ref[...], sc_ref[0], sh_ref[0]).astype(BF16)
    cos = cos_ref[0]
    sin = sin_ref[0]
    lane_lo = (lax.broadcasted_iota(I32, cos.shape, 1) % HEAD_DIM) < (HEAD_DIM // 2)
    for gi, (_, dil) in enumerate(B_GROUPS):
        for j in range(3):
            c = gi * 3 + j
            y = jnp.dot(hb_scr[...], w_ref[:, B_WIDTH * c:B_WIDTH * (c + 1)], preferred_element_type=F32)
            out = outs[c]
            for cc in range(B_WIDTH // 128):
                lanes = slice(128 * cc, 128 * (cc + 1))
                r = y[:, lanes]
                if j < 2:
                    r = _rope_apply(r, cos, sin, lane_lo)
                if j == 0:
                    r = r * (HEAD_DIM ** -0.5)
                if dil == 1:
                    out[0, 0, :, lanes] = r.astype(BF16)
                else:
                    ysc[cc] = r
                    for rr in range(dil):
                        out[0, rr, :, lanes] = ysc[cc, pl.ds(rr, tm // dil, stride=dil), :].astype(BF16)


def _qkv1(x, y, gfp, g, sc, sh, w, cos, sin):
    bsz, s, d = x.shape
    tm = TM
    tok = lambda w_: pl.BlockSpec((1, tm, w_), lambda b, i: (b, i, 0))
    per_b = pl.BlockSpec((1, 1, d), lambda b, i: (b, 0, 0))
    out_specs = [tok(d)]
    out_shape = [jax.ShapeDtypeStruct((bsz, s, d), F32)]
    for _, dil in B_GROUPS:
        for _ in range(3):
            out_specs.append(pl.BlockSpec((1, dil, tm // dil, B_WIDTH), lambda b, i: (b, 0, i, 0)))
            out_shape.append(jax.ShapeDtypeStruct((bsz, dil, s // dil, B_WIDTH), BF16))
    return pl.pallas_call(
        _qkv1_kernel,
        grid=(bsz, s // tm),
        in_specs=[tok(d), tok(d // 2), per_b, pl.BlockSpec((1, d), lambda b, i: (0, 0)), per_b, per_b,
                  pl.BlockSpec(w.shape, lambda b, i: (0, 0)), tok(128), tok(128)],
        out_specs=out_specs,
        out_shape=out_shape,
        scratch_shapes=[pltpu.VMEM((tm, d), BF16), pltpu.VMEM((B_WIDTH // 128, tm, 128), F32)],
        compiler_params=_cparams("parallel", "parallel"),
        name="qkv1",
    )(x, y, gfp, g.reshape(1, d), sc, sh, w, cos, sin)


def _att1_kernel(*refs, chunk, seq):
    ins = refs[:21]
    o_ref = refs[21]
    scr = refs[22:]
    kfs, vfs, nat_o, nat_l = scr[0:3], scr[3:6], scr[6:9], scr[9:12]
    n = pl.program_id(1)
    rad = B_RADIUS
    npair = B_WIDTH // 128
    for gi, (_, dil) in enumerate(B_GROUPS):
        _, kc, kp, kn, vc, vp, vn = ins[7 * gi:7 * gi + 7]
        lseg = chunk // dil
        for full, (p_, c_, n_) in ((kfs[gi], (kp, kc, kn)), (vfs[gi], (vp, vc, vn))):
            full[:, 0:rad] = p_[0]
            full[:, rad:rad + lseg] = c_[0]
            full[:, rad + lseg:rad + lseg + rad] = n_[0]

    for gi, (_, dil) in enumerate(B_GROUPS):
        q_ref = ins[7 * gi]
        kf, vf = kfs[gi], vfs[gi]
        lseg = chunk // dil
        qb = min(2 * rad, lseg)
        kb_n = qb + 2 * rad
        nblk = lseg // qb
        lstr = seq // dil
        ii = lax.broadcasted_iota(I32, (qb, kb_n), 0)
        jj = lax.broadcasted_iota(I32, (qb, kb_n), 1)
        rel = jj - ii
        band = (rel >= 0) & (rel <= 2 * rad)
        first_head = lax.broadcasted_iota(I32, (qb, 128), 1) < HEAD_DIM
        zero = jnp.zeros((qb, 128), BF16)
        ones = jnp.ones((kb_n, 128), BF16)

        def body(t, carry, gi=gi, q_ref=q_ref, kf=kf, vf=vf, lseg=lseg, qb=qb, kb_n=kb_n, nblk=nblk,
                 lstr=lstr, jj=jj, band=band, dil=dil, first_head=first_head, zero=zero, ones=ones):
            res = t // nblk
            q0 = pl.multiple_of((t % nblk) * qb, qb)
            apos = n * lseg + q0 - rad + jj
            bias = jnp.where(band & (apos >= 0) & (apos < lstr), 0.0, NEG)
            if dil == 1:
                rows = pl.ds(q0, qb)
            else:
                rows = pl.ds(res + dil * q0, qb, stride=dil)
            for c in range(npair):
                lanes = slice(128 * c, 128 * (c + 1))
                qc = q_ref[0, res, pl.ds(q0, qb), lanes]
                kb = kf[res, pl.ds(q0, kb_n), lanes]
                vb = jnp.concatenate([vf[res, pl.ds(q0, kb_n), lanes], ones], axis=1)
                qs = jnp.concatenate([jnp.where(first_head, qc, zero), jnp.where(first_head, zero, qc)],
                                     axis=0)
                s2 = lax.dot_general(qs, kb, (((1,), (1,)), ((), ())), preferred_element_type=F32)
                os_, ls_ = [], []
                for t2 in range(2):
                    s = s2[qb * t2:qb * (t2 + 1)] + bias
                    m = jnp.max(s, axis=-1, keepdims=True)
                    p = jnp.exp(s - m).astype(BF16)
                    pv = jnp.dot(p, vb, preferred_element_type=F32)
                    den = pv[:, 128:256]
                    os_.append(pv[:, 0:128] / den)
                    ls_.append(m + jnp.log(den))
                nat_o[gi][c, rows, :] = jnp.where(first_head, os_[0], os_[1])
                nat_l[gi][c, rows, :] = jnp.where(first_head, ls_[0], ls_[1])
            return carry

        lax.fori_loop(0, dil * nblk, body, 0)

    for c in range(npair):
        a = [nat_o[gi][c] for gi in range(3)]
        ls = [nat_l[gi][c] for gi in range(3)]
        mx = jnp.maximum(jnp.maximum(ls[0], ls[1]), ls[2])
        w = [jnp.exp(x - mx) for x in ls]
        num = w[0] * a[0] + w[1] * a[1] + w[2] * a[2]
        o_ref[0, :, 128 * c:128 * (c + 1)] = (num / (w[0] + w[1] + w[2])).astype(BF16)


def _att1(qkv):
    bsz = qkv[0].shape[0]
    seq = qkv[0].shape[1] * qkv[0].shape[2]
    chunk = ATT1_CHUNK
    rad = B_RADIUS
    in_specs, args, scratch = [], [], []
    for gi, (_, dil) in enumerate(B_GROUPS):
        q, k, v = qkv[3 * gi:3 * gi + 3]
        lseg = chunk // dil
        per = lseg // rad
        last = seq // dil // rad - 1
        cur = pl.BlockSpec((1, dil, lseg, B_WIDTH), lambda b, n: (b, 0, n, 0))
        prev = pl.BlockSpec((1, dil, rad, B_WIDTH),
                            lambda b, n, per=per: (b, 0, jnp.maximum(n * per - 1, 0), 0))
        nxt = pl.BlockSpec((1, dil, rad, B_WIDTH),
                           lambda b, n, per=per, last=last: (b, 0, jnp.minimum(n * per + per, last), 0))
        in_specs += [cur, cur, prev, nxt, cur, prev, nxt]
        args += [q, k, k, k, v, v, v]
    for _ in range(2):
        for _, dil in B_GROUPS:
            scratch.append(pltpu.VMEM((dil, chunk // dil + 2 * rad, B_WIDTH), BF16))
    for _ in range(2):
        for _ in B_GROUPS:
            scratch.append(pltpu.VMEM((B_WIDTH // 128, chunk, 128), F32))
    return pl.pallas_call(
        functools.partial(_att1_kernel, chunk=chunk, seq=seq),
        grid=(bsz, seq // chunk),
        in_specs=in_specs,
        out_specs=pl.BlockSpec((1, chunk, B_WIDTH), lambda b, n: (b, n, 0)),
        out_shape=jax.ShapeDtypeStruct((bsz, seq, B_WIDTH), BF16),
        scratch_shapes=scratch,
        compiler_params=_cparams("parallel", "parallel"),
        name="att1",
    )(*args)


def _final_kernel(x_ref, y_ref, gfp_ref, g_ref, o_ref):
    x = x_ref[0] + gfp_ref[0] * _unpack_bf16_pairs(y_ref[0])
    ms = jnp.mean(x * x, axis=-1, keepdims=True)
    o_ref[0] = (x * lax.rsqrt(ms + NORM_EPS)) * g_ref[...]


def _final(x, y, gfp, g):
    bsz, s, d = x.shape
    tm = TM
    tok = pl.BlockSpec((1, tm, d), lambda b, i: (b, i, 0))
    return pl.pallas_call(
        _final_kernel,
        grid=(bsz, s // tm),
        in_specs=[tok, pl.BlockSpec((1, tm, d // 2), lambda b, i: (b, i, 0)),
                  pl.BlockSpec((1, 1, d), lambda b, i: (b, 0, 0)),
                  pl.BlockSpec((1, d), lambda b, i: (0, 0))],
        out_specs=tok,
        out_shape=jax.ShapeDtypeStruct((bsz, s, d), F32),
        compiler_params=_cparams("parallel", "parallel"),
        name="final_norm",
    )(x, y, gfp, g.reshape(1, d))


def kernel(x, c, positions, ada_w, ada_b, norm_mix_g, norm_ffn_g, a_w_qkv, a_w_o, a_sink, b_w_qkv, b_w_o,
           router_w, router_bias, exp_w_gate, exp_w_up, exp_w_down, final_norm_g):
    bsz, s, d = x.shape
    assert d == D_MODEL and ada_w.shape[0] == 2 and s % ATT1_CHUNK == 0
    assert all(w_ // (2 * dil) == B_RADIUS for w_, dil in B_GROUPS)

    mod = _modulation(c, ada_w, ada_b)
    mods = [[mod[l][:, k * d:(k + 1) * d].reshape(bsz, 1, d) for k in range(6)] for l in range(2)]
    cos, sin = _rope_tables(positions)

    rw_t = router_w.T.astype(BF16)
    rows = jnp.array([PER_GROUP * g + j if g < N_EXPERT_GROUPS else 0
                      for j in range(PER_GROUP) for g in range(8)], I32)
    live = jnp.array([1.0 if g < N_EXPERT_GROUPS else 0.0
                      for j in range(PER_GROUP) for g in range(8)], F32)
    rw32 = (rw_t[rows].astype(F32) * live[:, None]).astype(BF16)
    rb32 = jnp.where(live > 0, router_bias.astype(F32)[rows], NEG).reshape(32, 1)
    rw3 = rw_t.reshape(N_EXPERTS, 1, d)
    wg, wu, wd = exp_w_gate.astype(BF16), exp_w_up.astype(BF16), exp_w_down.astype(BF16)

    sh_m, sc_m, g_m, sh_f, sc_f, g_f = mods[0]
    q, k, v = _qkv0(x, norm_mix_g[0], sc_m, sh_m, a_w_qkv[0].astype(BF16), cos, sin)
    o = _att0(q, k, v, a_sink[0].astype(F32))
    x1, h2, cls3 = _post(o, a_w_o[0].astype(BF16), x, g_m, norm_ffn_g[0], sc_f, sh_f, rw32, rb32)
    y = _moe(h2, cls3, wg[0], wu[0], wd[0], rw3)
    g_f_prev = g_f

    sh_m, sc_m, g_m, sh_f, sc_f, g_f = mods[1]
    outs = _qkv1(x1, y, g_f_prev, norm_mix_g[1], sc_m, sh_m, b_w_qkv[0].astype(BF16), cos, sin)
    x2, qkv = outs[0], outs[1:]
    o = _att1(qkv)
    x3, h2, cls3 = _post(o, b_w_o[0].astype(BF16), x2, g_m, norm_ffn_g[1], sc_f, sh_f, rw32, rb32)
    y = _moe(h2, cls3, wg[1], wu[1], wd[1], rw3)

    return _final(x3, y, g_f, final_norm_g)
```

```python
import functools

import jax
import jax.numpy as jnp
from jax import lax
from jax.experimental import pallas as pl
from jax.experimental.pallas import tpu as pltpu
from jax.experimental.pallas import tpu_sc as plsc

F32, BF16, I32, U32 = jnp.float32, jnp.bfloat16, jnp.int32, jnp.uint32

D_MODEL = 1024
HEAD_DIM = 64
ROPE_THETA = 10000.0
NORM_EPS = 1e-6
A_Q_HEADS = 16
A_KV_HEADS = 4
A_GROUP = A_Q_HEADS // A_KV_HEADS
A_RADIUS = 128
B_GROUPS = ((128, 1), (512, 4), (2048, 16))
B_HEADS = 8
B_RADIUS = 64
B_WIDTH = B_HEADS * HEAD_DIM
N_EXPERTS = 16
N_EXPERT_GROUPS = 4
PER_GROUP = N_EXPERTS // N_EXPERT_GROUPS
D_EXPERT = D_MODEL // 2
PAIRS = ((0, 1), (0, 2), (0, 3), (1, 2), (1, 3), (2, 3))
N_CLASSES = N_EXPERT_GROUPS * len(PAIRS)
NEG = -1e30

VMEM_LIMIT = 56 * 1024 * 1024
TM = 512
ATT1_CHUNK = 1024
FFN_TM = 256
SC_CORES_V7X = 2
SC_SUBCORES_V7X = 16
SC_MAX_INDEX_VECTOR = 128


def _cparams(*sem):
    return pltpu.CompilerParams(dimension_semantics=sem, vmem_limit_bytes=VMEM_LIMIT)


def _rms_mod(x, g, sc, sh):
    ms = jnp.mean(x * x, axis=-1, keepdims=True)
    return (x * lax.rsqrt(ms + NORM_EPS)) * g * (1.0 + sc) + sh


def _rope_apply(y, cos, sin_signed, lane_lo):
    sw = jnp.where(lane_lo, pltpu.roll(y, 96, 1), pltpu.roll(y, 32, 1))
    return y * cos + sw * sin_signed


def _pack_bf16_pairs(x):
    n = x.shape[1] // 2
    lo = lax.bitcast_convert_type(x[:, :n].astype(BF16).astype(F32), U32)
    hi = lax.bitcast_convert_type(x[:, n:].astype(BF16).astype(F32), U32)
    return (hi & jnp.uint32(0xFFFF0000)) | (lo >> 16)


def _unpack_bf16_pairs(p):
    lo = lax.bitcast_convert_type(p << 16, F32)
    hi = lax.bitcast_convert_type(p & jnp.uint32(0xFFFF0000), F32)
    return jnp.concatenate([lo, hi], axis=1)


def _dup_head(x, first_head, which):
    rolled = pltpu.roll(x, HEAD_DIM, 1)
    return jnp.where(first_head, x, rolled) if which == 0 else jnp.where(first_head, rolled, x)


def _mod_kernel(c_ref, w_ref, b_ref, o_ref):
    c = c_ref[...]
    s = c * jax.nn.sigmoid(c)
    o_ref[0] = jnp.dot(s, w_ref[0], preferred_element_type=F32,
                       precision=lax.Precision.HIGHEST) + b_ref[0]


def _modulation(c, ada_w, ada_b):
    depth, d, n = ada_w.shape
    bsz = c.shape[0]
    cp = jnp.zeros((8, d), F32).at[:bsz].set(c)
    tn = 1536
    out = pl.pallas_call(
        _mod_kernel,
        grid=(depth, n // tn),
        in_specs=[pl.BlockSpec((8, d), lambda l, j: (0, 0)),
                  pl.BlockSpec((1, d, tn), lambda l, j: (l, 0, j)),
                  pl.BlockSpec((1, 1, tn), lambda l, j: (l, 0, j))],
        out_specs=pl.BlockSpec((1, 8, tn), lambda l, j: (l, 0, j)),
        out_shape=jax.ShapeDtypeStruct((depth, 8, n), F32),
        compiler_params=_cparams("arbitrary", "arbitrary"),
        name="modulation",
    )(cp, ada_w, ada_b.reshape(depth, 1, n))
    return out[:, :bsz]


def _rope_kernel(pos_ref, invf_ref, sign_ref, cos_ref, sin_ref):
    ang = pos_ref[0].astype(F32) * invf_ref[...]
    cos_ref[0] = jnp.cos(ang)
    sin_ref[0] = jnp.sin(ang) * sign_ref[...]


def _rope_tables(positions):
    bsz, s = positions.shape
    inv_freq = ROPE_THETA ** (-jnp.arange(0, HEAD_DIM, 2, dtype=F32) / HEAD_DIM)
    invf = jnp.tile(inv_freq, 4).reshape(1, 128)
    half = HEAD_DIM // 2
    sign = jnp.tile(jnp.concatenate([-jnp.ones((half,), F32), jnp.ones((half,), F32)]), 2).reshape(1, 128)
    ts = TM
    return pl.pallas_call(
        _rope_kernel,
        grid=(bsz, s // ts),
        in_specs=[pl.BlockSpec((1, ts, 1), lambda b, i: (b, i, 0)),
                  pl.BlockSpec((1, 128), lambda b, i: (0, 0)),
                  pl.BlockSpec((1, 128), lambda b, i: (0, 0))],
        out_specs=[pl.BlockSpec((1, ts, 128), lambda b, i: (b, i, 0))] * 2,
        out_shape=[jax.ShapeDtypeStruct((bsz, s, 128), F32)] * 2,
        compiler_params=_cparams("parallel", "parallel"),
        name="rope_tables",
    )(positions.reshape(bsz, s, 1), invf, sign)


def _qkv0_kernel(x_ref, g_ref, sc_ref, sh_ref, w_ref, cos_ref, sin_ref,
                 q_ref, k_ref, v_ref, hb_scr):
    hb_scr[...] = _rms_mod(x_ref[0], g_ref[...], sc_ref[0], sh_ref[0]).astype(BF16)
    cos = cos_ref[0]
    sin = sin_ref[0]
    lane_lo = (lax.broadcasted_iota(I32, cos.shape, 1) % HEAD_DIM) < (HEAD_DIM // 2)
    nq = A_Q_HEADS * HEAD_DIM
    nkv = A_KV_HEADS * HEAD_DIM
    for c in range(nq // 256):
        y = jnp.dot(hb_scr[...], w_ref[:, 256 * c:256 * (c + 1)], preferred_element_type=F32)
        for cc in range(2):
            r = _rope_apply(y[:, 128 * cc:128 * (cc + 1)], cos, sin, lane_lo) * (HEAD_DIM ** -0.5)
            q_ref[0, :, 256 * c + 128 * cc:256 * c + 128 * (cc + 1)] = r.astype(BF16)
    first_head = lax.broadcasted_iota(I32, cos.shape, 1) < HEAD_DIM
    y = jnp.dot(hb_scr[...], w_ref[:, nq:nq + nkv], preferred_element_type=F32)
    for cc in range(nkv // 128):
        r = _rope_apply(y[:, 128 * cc:128 * (cc + 1)], cos, sin, lane_lo)
        for which in range(2):
            g = 2 * cc + which
            k_ref[0, :, 128 * g:128 * (g + 1)] = _dup_head(r, first_head, which).astype(BF16)
    y = jnp.dot(hb_scr[...], w_ref[:, nq + nkv:nq + 2 * nkv], preferred_element_type=F32)
    for cc in range(nkv // 128):
        r = y[:, 128 * cc:128 * (cc + 1)]
        for which in range(2):
            g = 2 * cc + which
            v_ref[0, :, 128 * g:128 * (g + 1)] = _dup_head(r, first_head, which).astype(BF16)


def _qkv0(x, g, sc, sh, w, cos, sin):
    bsz, s, d = x.shape
    nq = A_Q_HEADS * HEAD_DIM
    nkv = A_KV_HEADS * HEAD_DIM
    tm = TM
    tok = lambda w_: pl.BlockSpec((1, tm, w_), lambda b, i: (b, i, 0))
    per_b = pl.BlockSpec((1, 1, d), lambda b, i: (b, 0, 0))
    return pl.pallas_call(
        _qkv0_kernel,
        grid=(bsz, s // tm),
        in_specs=[tok(d), pl.BlockSpec((1, d), lambda b, i: (0, 0)), per_b, per_b,
                  pl.BlockSpec(w.shape, lambda b, i: (0, 0)), tok(128), tok(128)],
        out_specs=[tok(nq), tok(2 * nkv), tok(2 * nkv)],
        out_shape=[jax.ShapeDtypeStruct((bsz, s, nq), BF16),
                   jax.ShapeDtypeStruct((bsz, s, 2 * nkv), BF16),
                   jax.ShapeDtypeStruct((bsz, s, 2 * nkv), BF16)],
        scratch_shapes=[pltpu.VMEM((tm, d), BF16)],
        compiler_params=_cparams("parallel", "parallel"),
        name="qkv0",
    )(x, g.reshape(1, d), sc, sh, w, cos, sin)


def _att0_kernel(sink_ref, q_ref, kc_ref, kp_ref, kn_ref, vc_ref, vp_ref, vn_ref,
                 o_ref, kf, vf, *, tq, seq):
    i = pl.program_id(1)
    r = A_RADIUS
    ones = jnp.ones((tq + 2 * r, 128), BF16)
    for g in range(A_KV_HEADS):
        lanes = slice(128 * g, 128 * (g + 1))
        kf[g, 0:r] = kp_ref[0, :, lanes]
        kf[g, r:r + tq] = kc_ref[0, :, lanes]
        kf[g, r + tq:r + tq + r] = kn_ref[0, :, lanes]
        vf[g, 0:r, 0:128] = vp_ref[0, :, lanes]
        vf[g, r:r + tq, 0:128] = vc_ref[0, :, lanes]
        vf[g, r + tq:r + tq + r, 0:128] = vn_ref[0, :, lanes]
        vf[g, :, 128:256] = ones

    ii = lax.broadcasted_iota(I32, (r, r), 0)
    jj = lax.broadcasted_iota(I32, (r, r), 1)
    first_head = jj < HEAD_DIM
    zero = jnp.zeros((r, 128), BF16)

    def block(j, carry):
        q0 = pl.multiple_of(j * r, r)
        base = i * tq + q0
        bias_lo = jnp.where((jj >= ii) & (base - r + jj >= 0), 0.0, NEG)
        bias_hi = jnp.where((jj <= ii) & (base + r + jj < seq), 0.0, NEG)
        for g in range(A_KV_HEADS):
            kb = kf[g, pl.ds(q0, 3 * r), :]
            vb = vf[g, pl.ds(q0, 3 * r), :]
            qs = []
            for cc in range(2):
                c = 2 * g + cc
                qc = q_ref[0, pl.ds(q0, r), 128 * c:128 * (c + 1)]
                qs += [jnp.where(first_head, qc, zero), jnp.where(first_head, zero, qc)]
            s4 = lax.dot_general(jnp.concatenate(qs, axis=0), kb, (((1,), (1,)), ((), ())),
                                 preferred_element_type=F32)
            ps, ms = [], []
            for t in range(A_GROUP):
                s = s4[r * t:r * (t + 1)]
                a0 = s[:, 0:r] + bias_lo
                a1 = s[:, r:2 * r]
                a2 = s[:, 2 * r:3 * r] + bias_hi
                sink = sink_ref[A_GROUP * g + t]
                m = jnp.max(jnp.maximum(jnp.maximum(a0, a1), a2), axis=-1, keepdims=True)
                m = jnp.maximum(m, sink)
                ps.append(jnp.concatenate([jnp.exp(a0 - m), jnp.exp(a1 - m), jnp.exp(a2 - m)],
                                          axis=1).astype(BF16))
                ms.append(m)
            pv = jnp.dot(jnp.concatenate(ps, axis=0), vb, preferred_element_type=F32)
            os_ = []
            for t in range(A_GROUP):
                sink = sink_ref[A_GROUP * g + t]
                den = pv[r * t:r * (t + 1), 128:256] + jnp.exp(sink - ms[t])
                os_.append(pv[r * t:r * (t + 1), 0:128] / den)
            for cc in range(2):
                c = 2 * g + cc
                o_ref[0, pl.ds(q0, r), 128 * c:128 * (c + 1)] = jnp.where(
                    first_head, os_[2 * cc], os_[2 * cc + 1]).astype(BF16)
        return carry

    lax.fori_loop(0, tq // r, block, 0)


def _att0(q, k2, v2, sink):
    bsz, s, nq = q.shape
    nkv2 = k2.shape[-1]
    tq = TM
    r = A_RADIUS
    per = tq // r
    last = s // r - 1
    cur = lambda w_: pl.BlockSpec((1, tq, w_), lambda b, i: (b, i, 0))
    prev = pl.BlockSpec((1, r, nkv2), lambda b, i: (b, jnp.maximum(i * per - 1, 0), 0))
    nxt = pl.BlockSpec((1, r, nkv2), lambda b, i: (b, jnp.minimum(i * per + per, last), 0))
    return pl.pallas_call(
        functools.partial(_att0_kernel, tq=tq, seq=s),
        grid=(bsz, s // tq),
        in_specs=[pl.BlockSpec(memory_space=pltpu.SMEM),
                  cur(nq), cur(nkv2), prev, nxt, cur(nkv2), prev, nxt],
        out_specs=cur(nq),
        out_shape=jax.ShapeDtypeStruct((bsz, s, nq), BF16),
        scratch_shapes=[pltpu.VMEM((A_KV_HEADS, tq + 2 * r, 128), BF16),
                        pltpu.VMEM((A_KV_HEADS, tq + 2 * r, 256), BF16)],
        compiler_params=_cparams("parallel", "parallel"),
        name="att0",
    )(sink, q, k2, k2, k2, v2, v2, v2)


def _post_kernel(o_ref, wo_ref, x_ref, gm_ref, gf_ref, scf_ref, shf_ref, rw_ref, rb_ref,
                 x1_ref, h2_ref, cls_ref, rank_ref, cnt_ref, carry):
    mix = jnp.dot(o_ref[0], wo_ref[...], preferred_element_type=F32)
    x1 = x_ref[0] + gm_ref[0] * mix
    x1_ref[0] = x1
    h2 = _rms_mod(x1, gf_ref[...], scf_ref[0], shf_ref[0]).astype(BF16)
    h2_ref[0] = _pack_bf16_pairs(h2.astype(F32))
    tq = h2.shape[0]
    logits = lax.dot_general(rw_ref[...], h2, (((1,), (1,)), ((), ())), preferred_element_type=F32)
    scores = jax.nn.sigmoid(logits)
    biased = scores + rb_ref[...]
    pj = [biased[8 * j:8 * (j + 1)] for j in range(PER_GROUP)]
    sel = []
    for j in range(PER_GROUP):
        beaten = jnp.zeros((8, tq), F32)
        for j2 in range(PER_GROUP):
            if j2 == j:
                continue
            beats = (pj[j2] > pj[j]) | (pj[j2] == pj[j]) if j2 < j else (pj[j2] > pj[j])
            beaten = beaten + jnp.where(beats, 1.0, 0.0)
        sel.append(beaten < 2.0)
    gscore = jnp.zeros((8, tq), F32)
    for j in range(PER_GROUP):
        gscore = gscore + jnp.where(sel[j], pj[j], 0.0)
    gi = lax.broadcasted_iota(I32, (8, tq), 0).astype(F32)
    gmax = jnp.max(gscore, axis=0, keepdims=True)
    gidx = jnp.min(jnp.where(gscore == gmax, gi, 8.0), axis=0, keepdims=True)
    onehot = gi == gidx
    f = [jnp.max(jnp.where(onehot & sel[j], 1.0, 0.0), axis=0, keepdims=True) > 0.5
         for j in range(PER_GROUP)]
    pair = jnp.where(f[0], jnp.where(f[1], 0.0, jnp.where(f[2], 1.0, 2.0)),
                     jnp.where(f[1], jnp.where(f[2], 3.0, 4.0), 5.0))
    cls = (gidx * float(len(PAIRS)) + pair).astype(I32)
    cls_ref[0] = jnp.broadcast_to(cls, (8, tq))

    @pl.when((pl.program_id(0) == 0) & (pl.program_id(1) == 0))
    def _():
        carry[...] = jnp.zeros_like(carry)

    oh = lax.broadcasted_iota(I32, (32, tq), 0) == cls
    ohf = jnp.where(oh, 1.0, 0.0)
    before = lax.broadcasted_iota(I32, (tq, tq), 0) < lax.broadcasted_iota(I32, (tq, tq), 1)
    upper = jnp.where(before, 1.0, 0.0).astype(BF16)
    within = jnp.dot(ohf.astype(BF16), upper, preferred_element_type=F32)
    c = carry[...]
    tot = jnp.sum(jnp.where(oh, within + c[:, 0:1], 0.0), axis=0, keepdims=True)
    rank_ref[0] = jnp.broadcast_to(tot.astype(I32), (8, tq))
    newc = c + jnp.sum(ohf, axis=1, keepdims=True)
    carry[...] = newc
    cnt_ref[...] = newc


def _post(o, wo, x, gm, gf, scf, shf, rw32, rb32):
    bsz, s, d = x.shape
    do = o.shape[-1]
    tq = TM
    nt = s // tq
    tok = lambda w_: pl.BlockSpec((1, tq, w_), lambda b, i: (b, i, 0))
    per_b = pl.BlockSpec((1, 1, d), lambda b, i: (b, 0, 0))
    full = lambda a: pl.BlockSpec(a.shape, lambda b, i: (0,) * a.ndim)
    per_tile = pl.BlockSpec((1, 8, tq), lambda b, i: (b * nt + i, 0, 0))
    gf2 = gf.reshape(1, d)
    return pl.pallas_call(
        _post_kernel,
        grid=(bsz, nt),
        in_specs=[tok(do), full(wo), tok(d), per_b, full(gf2), per_b, per_b, full(rw32), full(rb32)],
        out_specs=[tok(d), tok(d // 2), per_tile, per_tile, pl.BlockSpec((32, 128), lambda b, i: (0, 0))],
        out_shape=[jax.ShapeDtypeStruct((bsz, s, d), F32),
                   jax.ShapeDtypeStruct((bsz, s, d // 2), U32),
                   jax.ShapeDtypeStruct((bsz * nt, 8, tq), I32),
                   jax.ShapeDtypeStruct((bsz * nt, 8, tq), I32),
                   jax.ShapeDtypeStruct((32, 128), F32)],
        scratch_shapes=[pltpu.VMEM((32, 128), F32)],
        compiler_params=_cparams("arbitrary", "arbitrary"),
        name="post_attention",
    )(o, wo, x, gm, gf2, scf, shf, rw32, rb32)


def _ffn_kernel(e1_ref, e2_ref, nused_ref, hs_ref, wg1, wu1, wd1, rw1, wg2, wu2, wd2, rw2, ys_ref):
    j = pl.program_id(0)

    @pl.when(j < nused_ref[0])
    def _():
        xf = _unpack_bf16_pairs(hs_ref[...])
        x = xf.astype(BF16)
        s1 = jax.nn.sigmoid(jnp.sum(xf * rw1[0].astype(F32), axis=-1, keepdims=True))
        s2 = jax.nn.sigmoid(jnp.sum(xf * rw2[0].astype(F32), axis=-1, keepdims=True))
        den = s1 + s2

        def expert(wg, wu, wd):
            gt = jnp.dot(x, wg[0], preferred_element_type=F32)
            up = jnp.dot(x, wu[0], preferred_element_type=F32)
            a = (gt * jax.nn.sigmoid(gt)) * up
            return jnp.dot(a.astype(BF16), wd[0], preferred_element_type=F32)

        y = (s1 / den) * expert(wg1, wu1, wd1) + (s2 / den) * expert(wg2, wu2, wd2)
        ys_ref[...] = _pack_bf16_pairs(y)

    @pl.when(j >= nused_ref[0])
    def _():
        ys_ref[...] = jnp.zeros_like(ys_ref)


def _ffn(hs, e1, e2, nused, wg, wu, wd, rw3):
    npad, dh = hs.shape
    d = 2 * dh
    tm = FFN_TM
    de = wg.shape[-1]
    row = pl.BlockSpec((tm, dh), lambda j, e1, e2, nu: (jnp.minimum(j, nu[0] - 1), 0))
    first = lambda shape: pl.BlockSpec((1,) + shape, lambda j, e1, e2, nu: (e1[j], 0, 0))
    second = lambda shape: pl.BlockSpec((1,) + shape, lambda j, e1, e2, nu: (e2[j], 0, 0))
    grid_spec = pltpu.PrefetchScalarGridSpec(
        num_scalar_prefetch=3,
        grid=(npad // tm,),
        in_specs=[row,
                  first((d, de)), first((d, de)), first((de, d)), first((1, d)),
                  second((d, de)), second((d, de)), second((de, d)), second((1, d))],
        out_specs=pl.BlockSpec((tm, dh), lambda j, e1, e2, nu: (j, 0)),
    )
    return pl.pallas_call(
        _ffn_kernel,
        grid_spec=grid_spec,
        out_shape=jax.ShapeDtypeStruct((npad, dh), U32),
        compiler_params=_cparams("arbitrary"),
        name="expert_ffn",
    )(e1, e2, nused, hs, wg, wu, wd, rw3, wg, wu, wd, rw3)


def _sc_mesh():
    return plsc.VectorSubcoreMesh(core_axis_name="c", subcore_axis_name="s",
                                  num_cores=SC_CORES_V7X, num_subcores=SC_SUBCORES_V7X)


def _sc_rows_per_step(per_worker):
    return min(SC_MAX_INDEX_VECTOR, per_worker)


def _sc_scatter_rows(src, idx, n_out):
    n_in, d = src.shape
    workers = SC_CORES_V7X * SC_SUBCORES_V7X
    per_w = n_in // workers
    ch = _sc_rows_per_step(per_w)
    assert per_w * workers == n_in and per_w % ch == 0 and ch % 8 == 0

    def body(src_hbm, idx_hbm, out_hbm, idx_v, rows_v, sem):
        base = (lax.axis_index("s") * SC_CORES_V7X + lax.axis_index("c")) * per_w

        @pl.loop(0, per_w // ch)
        def _(i):
            off = base + i * ch
            pltpu.sync_copy(idx_hbm.at[pl.ds(off, ch)], idx_v)
            pltpu.sync_copy(src_hbm.at[pl.ds(off, ch)], rows_v)
            pltpu.async_copy(rows_v, out_hbm.at[idx_v], sem).wait()

    return pl.kernel(
        body, mesh=_sc_mesh(),
        out_type=jax.ShapeDtypeStruct((n_out, d), src.dtype),
        scratch_types=[pltpu.VMEM((ch,), I32), pltpu.VMEM((ch, d), src.dtype), pltpu.SemaphoreType.DMA],
        name="sc_scatter_rows",
    )(src, idx)


def _sc_gather_rows(table, idx):
    n_out = idx.shape[0]
    d = table.shape[1]
    workers = SC_CORES_V7X * SC_SUBCORES_V7X
    per_w = n_out // workers
    ch = _sc_rows_per_step(per_w)
    assert per_w * workers == n_out and per_w % ch == 0 and ch % 8 == 0

    def body(table_hbm, idx_hbm, out_hbm, idx_v, rows_v, sem):
        base = (lax.axis_index("s") * SC_CORES_V7X + lax.axis_index("c")) * per_w

        @pl.loop(0, per_w // ch)
        def _(i):
            off = base + i * ch
            pltpu.sync_copy(idx_hbm.at[pl.ds(off, ch)], idx_v)
            pltpu.async_copy(table_hbm.at[idx_v], rows_v, sem).wait()
            pltpu.sync_copy(rows_v, out_hbm.at[pl.ds(off, ch)])

    return pl.kernel(
        body, mesh=_sc_mesh(),
        out_type=jax.ShapeDtypeStruct((n_out, d), table.dtype),
        scratch_types=[pltpu.VMEM((ch,), I32), pltpu.VMEM((ch, d), table.dtype), pltpu.SemaphoreType.DMA],
        name="sc_gather_rows",
    )(table, idx)


def _moe(h2p, cls3, rank3, cnt, wg, wu, wd, rw3):
    bsz, s, dh = h2p.shape
    t = bsz * s
    tm = FFN_TM
    cls = cls3[:, 0, :].reshape(t)
    rank = rank3[:, 0, :].reshape(t)
    counts = cnt[:N_CLASSES, 0].astype(I32)
    padded = ((counts + tm - 1) // tm) * tm
    upto = jnp.arange(N_CLASSES)[:, None] >= jnp.arange(N_CLASSES)[None, :]
    ends = jnp.sum(jnp.where(upto, padded[None, :], 0), axis=1)
    base = ends - padded
    pos = jnp.take(base, cls, mode="clip") + rank
    ntiles = t // tm + N_CLASSES
    tile_start = jnp.arange(ntiles, dtype=I32) * tm
    tcls = jnp.minimum(jnp.sum((ends[None, :] <= tile_start[:, None]).astype(I32), axis=1), N_CLASSES - 1)
    nused = (ends[-1:] // tm).astype(I32)
    grp = tcls // len(PAIRS)
    pr = tcls % len(PAIRS)
    first = jnp.array([p[0] for p in PAIRS], I32)
    second = jnp.array([p[1] for p in PAIRS], I32)
    e1 = (grp * PER_GROUP + jnp.take(first, pr, mode="clip")).astype(I32)
    e2 = (grp * PER_GROUP + jnp.take(second, pr, mode="clip")).astype(I32)
    hs = _sc_scatter_rows(h2p.reshape(t, dh), pos, ntiles * tm)
    ys = _ffn(hs, e1, e2, nused, wg, wu, wd, rw3)
    return _sc_gather_rows(ys, pos).reshape(bsz, s, dh)


def _qkv1_kernel(x_ref, y_ref, gfp_ref, g_ref, sc_ref, sh_ref, w_ref, cos_ref, sin_ref,
                 x2_ref, *rest):
    outs = rest[:9]
    hb_scr, ysc = rest[9:]
    tm = x_ref.shape[1]
    x2 = x_ref[0] + gfp_ref[0] * _unpack_bf16_pairs(y_ref[0])
    x2_ref[0] = x2
    hb_scr[...] = _rms_mod(x2, g_ref[...], sc_ref[0], sh_ref[0]).astype(BF16)
    cos = cos_ref[0]
    sin = sin_ref[0]
    lane_lo = (lax.broadcasted_iota(I32, cos.shape, 1) % HEAD_DIM) < (HEAD_DIM // 2)
    for gi, (_, dil) in enumerate(B_GROUPS):
        for j in range(3):
            c = gi * 3 + j
            y = jnp.dot(hb_scr[...], w_ref[:, B_WIDTH * c:B_WIDTH * (c + 1)], preferred_element_type=F32)
            out = outs[c]
            for cc in range(B_WIDTH // 128):
                lanes = slice(128 * cc, 128 * (cc + 1))
                r = y[:, lanes]
                if j < 2:
                    r = _rope_apply(r, cos, sin, lane_lo)
                if j == 0:
                    r = r * (HEAD_DIM ** -0.5)
                if dil == 1:
                    out[0, 0, :, lanes] = r.astype(BF16)
                else:
                    ysc[cc] = r
                    for rr in range(dil):
                        out[0, rr, :, lanes] = ysc[cc, pl.ds(rr, tm // dil, stride=dil), :].astype(BF16)


def _qkv1(x, y, gfp, g, sc, sh, w, cos, sin):
    bsz, s, d = x.shape
    tm = TM
    tok = lambda w_: pl.BlockSpec((1, tm, w_), lambda b, i: (b, i, 0))
    per_b = pl.BlockSpec((1, 1, d), lambda b, i: (b, 0, 0))
    out_specs = [tok(d)]
    out_shape = [jax.ShapeDtypeStruct((bsz, s, d), F32)]
    for _, dil in B_GROUPS:
        for _ in range(3):
            out_specs.append(pl.BlockSpec((1, dil, tm // dil, B_WIDTH), lambda b, i: (b, 0, i, 0)))
            out_shape.append(jax.ShapeDtypeStruct((bsz, dil, s // dil, B_WIDTH), BF16))
    return pl.pallas_call(
        _qkv1_kernel,
        grid=(bsz, s // tm),
        in_specs=[tok(d), tok(d // 2), per_b, pl.BlockSpec((1, d), lambda b, i: (0, 0)), per_b, per_b,
                  pl.BlockSpec(w.shape, lambda b, i: (0, 0)), tok(128), tok(128)],
        out_specs=out_specs,
        out_shape=out_shape,
        scratch_shapes=[pltpu.VMEM((tm, d), BF16), pltpu.VMEM((B_WIDTH // 128, tm, 128), F32)],
        compiler_params=_cparams("parallel", "parallel"),
        name="qkv1",
    )(x, y, gfp, g.reshape(1, d), sc, sh, w, cos, sin)


def _att1_kernel(*refs, chunk, seq):
    ins = refs[:21]
    o_ref = refs[21]
    scr = refs[22:]
    kfs, vfs, nat_o, nat_l = scr[0:3], scr[3:6], scr[6:9], scr[9:12]
    n = pl.program_id(1)
    rad = B_RADIUS
    npair = B_WIDTH // 128
    for gi, (_, dil) in enumerate(B_GROUPS):
        _, kc, kp, kn, vc, vp, vn = ins[7 * gi:7 * gi + 7]
        lseg = chunk // dil
        for full, (p_, c_, n_) in ((kfs[gi], (kp, kc, kn)), (vfs[gi], (vp, vc, vn))):
            full[:, 0:rad] = p_[0]
            full[:, rad:rad + lseg] = c_[0]
            full[:, rad + lseg:rad + lseg + rad] = n_[0]

    for gi, (_, dil) in enumerate(B_GROUPS):
        q_ref = ins[7 * gi]
        kf, vf = kfs[gi], vfs[gi]
        lseg = chunk // dil
        qb = min(2 * rad, lseg)
        kb_n = qb + 2 * rad
        nblk = lseg // qb
        lstr = seq // dil
        ii = lax.broadcasted_iota(I32, (qb, kb_n), 0)
        jj = lax.broadcasted_iota(I32, (qb, kb_n), 1)
        rel = jj - ii
        band = (rel >= 0) & (rel <= 2 * rad)
        first_head = lax.broadcasted_iota(I32, (qb, 128), 1) < HEAD_DIM
        zero = jnp.zeros((qb, 128), BF16)
        ones = jnp.ones((kb_n, 128), BF16)

        per_iter = 2 if qb < 2 * rad else 1

        def tile(t, gi=gi, q_ref=q_ref, kf=kf, vf=vf, lseg=lseg, qb=qb, kb_n=kb_n, nblk=nblk,
                 lstr=lstr, jj=jj, band=band, dil=dil, first_head=first_head, zero=zero, ones=ones):
            res = t // nblk
            q0 = pl.multiple_of((t % nblk) * qb, qb)
            apos = n * lseg + q0 - rad + jj
            bias = jnp.where(band & (apos >= 0) & (apos < lstr), 0.0, NEG)
            if dil == 1:
                rows = pl.ds(q0, qb)
            else:
                rows = pl.ds(res + dil * q0, qb, stride=dil)
            for c in range(npair):
                lanes = slice(128 * c, 128 * (c + 1))
                qc = q_ref[0, res, pl.ds(q0, qb), lanes]
                kb = kf[res, pl.ds(q0, kb_n), lanes]
                vb = jnp.concatenate([vf[res, pl.ds(q0, kb_n), lanes], ones], axis=1)
                qs = jnp.concatenate([jnp.where(first_head, qc, zero), jnp.where(first_head, zero, qc)],
                                     axis=0)
                s2 = lax.dot_general(qs, kb, (((1,), (1,)), ((), ())), preferred_element_type=F32)
                os_, ls_ = [], []
                for t2 in range(2):
                    s = s2[qb * t2:qb * (t2 + 1)] + bias
                    m = jnp.max(s, axis=-1, keepdims=True)
                    p = jnp.exp(s - m).astype(BF16)
                    pv = jnp.dot(p, vb, preferred_element_type=F32)
                    den = pv[:, 128:256]
                    os_.append(pv[:, 0:128] / den)
                    ls_.append(m + jnp.log(den))
                nat_o[gi][c, rows, :] = jnp.where(first_head, os_[0], os_[1])
                nat_l[gi][c, rows, :] = jnp.where(first_head, ls_[0], ls_[1])

        def body(t, carry, tile=tile, per_iter=per_iter):
            for u in range(per_iter):
                tile(t * per_iter + u)
            return carry

        lax.fori_loop(0, dil * nblk // per_iter, body, 0)

    for c in range(npair):
        a = [nat_o[gi][c] for gi in range(3)]
        ls = [nat_l[gi][c] for gi in range(3)]
        mx = jnp.maximum(jnp.maximum(ls[0], ls[1]), ls[2])
        w = [jnp.exp(x - mx) for x in ls]
        num = w[0] * a[0] + w[1] * a[1] + w[2] * a[2]
        o_ref[0, :, 128 * c:128 * (c + 1)] = (num / (w[0] + w[1] + w[2])).astype(BF16)


def _att1(qkv):
    bsz = qkv[0].shape[0]
    seq = qkv[0].shape[1] * qkv[0].shape[2]
    chunk = ATT1_CHUNK
    rad = B_RADIUS
    in_specs, args, scratch = [], [], []
    for gi, (_, dil) in enumerate(B_GROUPS):
        q, k, v = qkv[3 * gi:3 * gi + 3]
        lseg = chunk // dil
        per = lseg // rad
        last = seq // dil // rad - 1
        cur = pl.BlockSpec((1, dil, lseg, B_WIDTH), lambda b, n: (b, 0, n, 0))
        prev = pl.BlockSpec((1, dil, rad, B_WIDTH),
                            lambda b, n, per=per: (b, 0, jnp.maximum(n * per - 1, 0), 0))
        nxt = pl.BlockSpec((1, dil, rad, B_WIDTH),
                           lambda b, n, per=per, last=last: (b, 0, jnp.minimum(n * per + per, last), 0))
        in_specs += [cur, cur, prev, nxt, cur, prev, nxt]
        args += [q, k, k, k, v, v, v]
    for _ in range(2):
        for _, dil in B_GROUPS:
            scratch.append(pltpu.VMEM((dil, chunk // dil + 2 * rad, B_WIDTH), BF16))
    for _ in range(2):
        for _ in B_GROUPS:
            scratch.append(pltpu.VMEM((B_WIDTH // 128, chunk, 128), F32))
    return pl.pallas_call(
        functools.partial(_att1_kernel, chunk=chunk, seq=seq),
        grid=(bsz, seq // chunk),
        in_specs=in_specs,
        out_specs=pl.BlockSpec((1, chunk, B_WIDTH), lambda b, n: (b, n, 0)),
        out_shape=jax.ShapeDtypeStruct((bsz, seq, B_WIDTH), BF16),
        scratch_shapes=scratch,
        compiler_params=_cparams("parallel", "parallel"),
        name="att1",
    )(*args)


def _final_kernel(x_ref, y_ref, gfp_ref, g_ref, o_ref):
    x = x_ref[0] + gfp_ref[0] * _unpack_bf16_pairs(y_ref[0])
    ms = jnp.mean(x * x, axis=-1, keepdims=True)
    o_ref[0] = (x * lax.rsqrt(ms + NORM_EPS)) * g_ref[...]


def _final(x, y, gfp, g):
    bsz, s, d = x.shape
    tm = TM
    tok = pl.BlockSpec((1, tm, d), lambda b, i: (b, i, 0))
    return pl.pallas_call(
        _final_kernel,
        grid=(bsz, s // tm),
        in_specs=[tok, pl.BlockSpec((1, tm, d // 2), lambda b, i: (b, i, 0)),
                  pl.BlockSpec((1, 1, d), lambda b, i: (b, 0, 0)),
                  pl.BlockSpec((1, d), lambda b, i: (0, 0))],
        out_specs=tok,
        out_shape=jax.ShapeDtypeStruct((bsz, s, d), F32),
        compiler_params=_cparams("parallel", "parallel"),
        name="final_norm",
    )(x, y, gfp, g.reshape(1, d))


def kernel(x, c, positions, ada_w, ada_b, norm_mix_g, norm_ffn_g, a_w_qkv, a_w_o, a_sink, b_w_qkv, b_w_o,
           router_w, router_bias, exp_w_gate, exp_w_up, exp_w_down, final_norm_g):
    bsz, s, d = x.shape
    assert d == D_MODEL and ada_w.shape[0] == 2 and s % ATT1_CHUNK == 0
    assert all(w_ // (2 * dil) == B_RADIUS for w_, dil in B_GROUPS)

    mod = _modulation(c, ada_w, ada_b)
    mods = [[mod[l][:, k * d:(k + 1) * d].reshape(bsz, 1, d) for k in range(6)] for l in range(2)]
    cos, sin = _rope_tables(positions)

    rw_t = router_w.T.astype(BF16)
    rows = jnp.array([PER_GROUP * g + j if g < N_EXPERT_GROUPS else 0
                      for j in range(PER_GROUP) for g in range(8)], I32)
    live = jnp.array([1.0 if g < N_EXPERT_GROUPS else 0.0
                      for j in range(PER_GROUP) for g in range(8)], F32)
    rw32 = (rw_t[rows].astype(F32) * live[:, None]).astype(BF16)
    rb32 = jnp.where(live > 0, router_bias.astype(F32)[rows], NEG).reshape(32, 1)
    rw3 = rw_t.reshape(N_EXPERTS, 1, d)
    experts = [tuple(w[l].astype(BF16) for w in (exp_w_gate, exp_w_up, exp_w_down)) for l in range(2)]

    sh_m, sc_m, g_m, sh_f, sc_f, g_f = mods[0]
    q, k, v = _qkv0(x, norm_mix_g[0], sc_m, sh_m, a_w_qkv[0].astype(BF16), cos, sin)
    o = _att0(q, k, v, a_sink[0].astype(F32))
    x1, *routed = _post(o, a_w_o[0].astype(BF16), x, g_m, norm_ffn_g[0], sc_f, sh_f, rw32, rb32)
    y = _moe(*routed, *experts[0], rw3)
    g_f_prev = g_f

    sh_m, sc_m, g_m, sh_f, sc_f, g_f = mods[1]
    outs = _qkv1(x1, y, g_f_prev, norm_mix_g[1], sc_m, sh_m, b_w_qkv[0].astype(BF16), cos, sin)
    x2, qkv = outs[0], outs[1:]
    o = _att1(qkv)
    x3, *routed = _post(o, b_w_o[0].astype(BF16), x2, g_m, norm_ffn_g[1], sc_f, sh_f, rw32, rb32)
    y = _moe(*routed, *experts[1], rw3)

    return _final(x3, y, g_f, final_norm_g)
```

```python
import functools

import jax
import jax.numpy as jnp
from jax import lax
from jax.experimental import pallas as pl
from jax.experimental.pallas import tpu as pltpu
from jax.experimental.pallas import tpu_sc as plsc

F32, BF16, I32, U32 = jnp.float32, jnp.bfloat16, jnp.int32, jnp.uint32

D_MODEL = 1024
HEAD_DIM = 64
ROPE_THETA = 10000.0
NORM_EPS = 1e-6
A_Q_HEADS = 16
A_KV_HEADS = 4
A_GROUP = A_Q_HEADS // A_KV_HEADS
A_RADIUS = 128
B_GROUPS = ((128, 1), (512, 4), (2048, 16))
B_HEADS = 8
B_RADIUS = 64
B_WIDTH = B_HEADS * HEAD_DIM
N_EXPERTS = 16
N_EXPERT_GROUPS = 4
PER_GROUP = N_EXPERTS // N_EXPERT_GROUPS
D_EXPERT = D_MODEL // 2
PAIRS = ((0, 1), (0, 2), (0, 3), (1, 2), (1, 3), (2, 3))
N_CLASSES = N_EXPERT_GROUPS * len(PAIRS)
NEG = -1e30
LOG2E = 1.4426950408889634
Q_SCALE = HEAD_DIM ** -0.5 * LOG2E

VMEM_LIMIT = 56 * 1024 * 1024
TM = 512
ATT1_CHUNK = 1024
ATT1_TILE_ROWS_PER_ITER = 1024
FFN_TM = 256
SC_CORES_V7X = 2
SC_SUBCORES_V7X = 16
SC_MAX_INDEX_VECTOR = 128


def _cparams(*sem):
    return pltpu.CompilerParams(dimension_semantics=sem, vmem_limit_bytes=VMEM_LIMIT)


def _rms_mod(x, g, sc, sh):
    ms = jnp.mean(x * x, axis=-1, keepdims=True)
    return (x * lax.rsqrt(ms + NORM_EPS)) * g * (1.0 + sc) + sh


def _rope_apply(y, cos, sin_signed, lane_lo):
    sw = jnp.where(lane_lo, pltpu.roll(y, 96, 1), pltpu.roll(y, 32, 1))
    return y * cos + sw * sin_signed


def _pack_bf16_pairs(x):
    n = x.shape[1] // 2
    lo = lax.bitcast_convert_type(x[:, :n].astype(BF16).astype(F32), U32)
    hi = lax.bitcast_convert_type(x[:, n:].astype(BF16).astype(F32), U32)
    return (hi & jnp.uint32(0xFFFF0000)) | (lo >> 16)


def _unpack_bf16_pairs(p):
    lo = lax.bitcast_convert_type(p << 16, F32)
    hi = lax.bitcast_convert_type(p & jnp.uint32(0xFFFF0000), F32)
    return jnp.concatenate([lo, hi], axis=1)


def _dup_head(x, first_head, which):
    rolled = pltpu.roll(x, HEAD_DIM, 1)
    return jnp.where(first_head, x, rolled) if which == 0 else jnp.where(first_head, rolled, x)


def _mod_kernel(c_ref, w_ref, b_ref, o_ref):
    c = c_ref[...]
    s = c * jax.nn.sigmoid(c)
    o_ref[0] = jnp.dot(s, w_ref[0], preferred_element_type=F32,
                       precision=lax.Precision.HIGHEST) + b_ref[0]


def _modulation(c, ada_w, ada_b):
    depth, d, n = ada_w.shape
    bsz = c.shape[0]
    cp = jnp.zeros((8, d), F32).at[:bsz].set(c)
    tn = 1536
    out = pl.pallas_call(
        _mod_kernel,
        grid=(depth, n // tn),
        in_specs=[pl.BlockSpec((8, d), lambda l, j: (0, 0)),
                  pl.BlockSpec((1, d, tn), lambda l, j: (l, 0, j)),
                  pl.BlockSpec((1, 1, tn), lambda l, j: (l, 0, j))],
        out_specs=pl.BlockSpec((1, 8, tn), lambda l, j: (l, 0, j)),
        out_shape=jax.ShapeDtypeStruct((depth, 8, n), F32),
        compiler_params=_cparams("arbitrary", "arbitrary"),
        name="modulation",
    )(cp, ada_w, ada_b.reshape(depth, 1, n))
    return out[:, :bsz]


def _rope_kernel(pos_ref, invf_ref, sign_ref, cos_ref, sin_ref):
    ang = pos_ref[0].astype(F32) * invf_ref[...]
    cos_ref[0] = jnp.cos(ang)
    sin_ref[0] = jnp.sin(ang) * sign_ref[...]


def _rope_tables(positions):
    bsz, s = positions.shape
    inv_freq = ROPE_THETA ** (-jnp.arange(0, HEAD_DIM, 2, dtype=F32) / HEAD_DIM)
    invf = jnp.tile(inv_freq, 4).reshape(1, 128)
    half = HEAD_DIM // 2
    sign = jnp.tile(jnp.concatenate([-jnp.ones((half,), F32), jnp.ones((half,), F32)]), 2).reshape(1, 128)
    ts = TM
    return pl.pallas_call(
        _rope_kernel,
        grid=(bsz, s // ts),
        in_specs=[pl.BlockSpec((1, ts, 1), lambda b, i: (b, i, 0)),
                  pl.BlockSpec((1, 128), lambda b, i: (0, 0)),
                  pl.BlockSpec((1, 128), lambda b, i: (0, 0))],
        out_specs=[pl.BlockSpec((1, ts, 128), lambda b, i: (b, i, 0))] * 2,
        out_shape=[jax.ShapeDtypeStruct((bsz, s, 128), F32)] * 2,
        compiler_params=_cparams("parallel", "parallel"),
        name="rope_tables",
    )(positions.reshape(bsz, s, 1), invf, sign)


def _qkv0_kernel(x_ref, g_ref, sc_ref, sh_ref, w_ref, cos_ref, sin_ref,
                 q_ref, k_ref, v_ref, hb_scr):
    hb_scr[...] = _rms_mod(x_ref[0], g_ref[...], sc_ref[0], sh_ref[0]).astype(BF16)
    cos = cos_ref[0]
    sin = sin_ref[0]
    lane_lo = (lax.broadcasted_iota(I32, cos.shape, 1) % HEAD_DIM) < (HEAD_DIM // 2)
    nq = A_Q_HEADS * HEAD_DIM
    nkv = A_KV_HEADS * HEAD_DIM
    for c in range(nq // 256):
        y = jnp.dot(hb_scr[...], w_ref[:, 256 * c:256 * (c + 1)], preferred_element_type=F32)
        for cc in range(2):
            r = _rope_apply(y[:, 128 * cc:128 * (cc + 1)], cos, sin, lane_lo) * Q_SCALE
            q_ref[0, :, 256 * c + 128 * cc:256 * c + 128 * (cc + 1)] = r.astype(BF16)
    first_head = lax.broadcasted_iota(I32, cos.shape, 1) < HEAD_DIM
    y = jnp.dot(hb_scr[...], w_ref[:, nq:nq + nkv], preferred_element_type=F32)
    for cc in range(nkv // 128):
        r = _rope_apply(y[:, 128 * cc:128 * (cc + 1)], cos, sin, lane_lo)
        for which in range(2):
            g = 2 * cc + which
            k_ref[0, :, 128 * g:128 * (g + 1)] = _dup_head(r, first_head, which).astype(BF16)
    y = jnp.dot(hb_scr[...], w_ref[:, nq + nkv:nq + 2 * nkv], preferred_element_type=F32)
    for cc in range(nkv // 128):
        r = y[:, 128 * cc:128 * (cc + 1)]
        for which in range(2):
            g = 2 * cc + which
            v_ref[0, :, 128 * g:128 * (g + 1)] = _dup_head(r, first_head, which).astype(BF16)


def _qkv0(x, g, sc, sh, w, cos, sin):
    bsz, s, d = x.shape
    nq = A_Q_HEADS * HEAD_DIM
    nkv = A_KV_HEADS * HEAD_DIM
    tm = TM
    tok = lambda w_: pl.BlockSpec((1, tm, w_), lambda b, i: (b, i, 0))
    per_b = pl.BlockSpec((1, 1, d), lambda b, i: (b, 0, 0))
    return pl.pallas_call(
        _qkv0_kernel,
        grid=(bsz, s // tm),
        in_specs=[tok(d), pl.BlockSpec((1, d), lambda b, i: (0, 0)), per_b, per_b,
                  pl.BlockSpec(w.shape, lambda b, i: (0, 0)), tok(128), tok(128)],
        out_specs=[tok(nq), tok(2 * nkv), tok(2 * nkv)],
        out_shape=[jax.ShapeDtypeStruct((bsz, s, nq), BF16),
                   jax.ShapeDtypeStruct((bsz, s, 2 * nkv), BF16),
                   jax.ShapeDtypeStruct((bsz, s, 2 * nkv), BF16)],
        scratch_shapes=[pltpu.VMEM((tm, d), BF16)],
        compiler_params=_cparams("parallel", "parallel"),
        name="qkv0",
    )(x, g.reshape(1, d), sc, sh, w, cos, sin)


def _att0_kernel(sink_ref, q_ref, kc_ref, kp_ref, kn_ref, vc_ref, vp_ref, vn_ref,
                 o_ref, kf, vf, *, tq, seq):
    i = pl.program_id(1)
    r = A_RADIUS
    ones = jnp.ones((tq + 2 * r, 128), BF16)
    for g in range(A_KV_HEADS):
        lanes = slice(128 * g, 128 * (g + 1))
        kf[g, 0:r] = kp_ref[0, :, lanes]
        kf[g, r:r + tq] = kc_ref[0, :, lanes]
        kf[g, r + tq:r + tq + r] = kn_ref[0, :, lanes]
        vf[g, 0:r, 0:128] = vp_ref[0, :, lanes]
        vf[g, r:r + tq, 0:128] = vc_ref[0, :, lanes]
        vf[g, r + tq:r + tq + r, 0:128] = vn_ref[0, :, lanes]
        vf[g, :, 128:256] = ones

    ii = lax.broadcasted_iota(I32, (r, r), 0)
    jj = lax.broadcasted_iota(I32, (r, r), 1)
    first_head = jj < HEAD_DIM
    zero = jnp.zeros((r, 128), BF16)

    for j in range(tq // r):
        q0 = j * r
        base = i * tq + q0
        bias_lo = jnp.where((jj >= ii) & (base - r + jj >= 0), 0.0, NEG)
        bias_hi = jnp.where((jj <= ii) & (base + r + jj < seq), 0.0, NEG)
        for g in range(A_KV_HEADS):
            kb = kf[g, pl.ds(q0, 3 * r), :]
            vb = vf[g, pl.ds(q0, 3 * r), :]
            qs = []
            for cc in range(2):
                c = 2 * g + cc
                qc = q_ref[0, pl.ds(q0, r), 128 * c:128 * (c + 1)]
                qs += [jnp.where(first_head, qc, zero), jnp.where(first_head, zero, qc)]
            s4 = lax.dot_general(jnp.concatenate(qs, axis=0), kb, (((1,), (1,)), ((), ())),
                                 preferred_element_type=F32)
            ps, ms = [], []
            for t in range(A_GROUP):
                s = s4[r * t:r * (t + 1)]
                a0 = s[:, 0:r] + bias_lo
                a1 = s[:, r:2 * r]
                a2 = s[:, 2 * r:3 * r] + bias_hi
                sink = sink_ref[A_GROUP * g + t] * LOG2E
                m = jnp.max(jnp.maximum(jnp.maximum(a0, a1), a2), axis=-1, keepdims=True)
                m = jnp.maximum(m, sink)
                ps.append(jnp.concatenate([jnp.exp2(a0 - m), jnp.exp2(a1 - m), jnp.exp2(a2 - m)],
                                          axis=1).astype(BF16))
                ms.append(m)
            pv = jnp.dot(jnp.concatenate(ps, axis=0), vb, preferred_element_type=F32)
            os_ = []
            for t in range(A_GROUP):
                sink = sink_ref[A_GROUP * g + t] * LOG2E
                den = pv[r * t:r * (t + 1), 128:256] + jnp.exp2(sink - ms[t])
                os_.append(pv[r * t:r * (t + 1), 0:128] / den)
            for cc in range(2):
                c = 2 * g + cc
                o_ref[0, pl.ds(q0, r), 128 * c:128 * (c + 1)] = jnp.where(
                    first_head, os_[2 * cc], os_[2 * cc + 1]).astype(BF16)


def _att0(q, k2, v2, sink):
    bsz, s, nq = q.shape
    nkv2 = k2.shape[-1]
    tq = TM
    r = A_RADIUS
    per = tq // r
    last = s // r - 1
    cur = lambda w_: pl.BlockSpec((1, tq, w_), lambda b, i: (b, i, 0))
    prev = pl.BlockSpec((1, r, nkv2), lambda b, i: (b, jnp.maximum(i * per - 1, 0), 0))
    nxt = pl.BlockSpec((1, r, nkv2), lambda b, i: (b, jnp.minimum(i * per + per, last), 0))
    return pl.pallas_call(
        functools.partial(_att0_kernel, tq=tq, seq=s),
        grid=(bsz, s // tq),
        in_specs=[pl.BlockSpec(memory_space=pltpu.SMEM),
                  cur(nq), cur(nkv2), prev, nxt, cur(nkv2), prev, nxt],
        out_specs=cur(nq),
        out_shape=jax.ShapeDtypeStruct((bsz, s, nq), BF16),
        scratch_shapes=[pltpu.VMEM((A_KV_HEADS, tq + 2 * r, 128), BF16),
                        pltpu.VMEM((A_KV_HEADS, tq + 2 * r, 256), BF16)],
        compiler_params=_cparams("parallel", "parallel"),
        name="att0",
    )(sink, q, k2, k2, k2, v2, v2, v2)


def _post_kernel(o_ref, wo_ref, x_ref, gm_ref, gf_ref, scf_ref, shf_ref, rw_ref, rb_ref,
                 x1_ref, h2_ref, cls_ref, rank_ref, cnt_ref, carry):
    mix = jnp.dot(o_ref[0], wo_ref[...], preferred_element_type=F32)
    x1 = x_ref[0] + gm_ref[0] * mix
    x1_ref[0] = x1
    h2 = _rms_mod(x1, gf_ref[...], scf_ref[0], shf_ref[0]).astype(BF16)
    h2_ref[0] = _pack_bf16_pairs(h2.astype(F32))
    tq = h2.shape[0]
    logits = lax.dot_general(rw_ref[...], h2, (((1,), (1,)), ((), ())), preferred_element_type=F32)
    scores = jax.nn.sigmoid(logits)
    biased = scores + rb_ref[...]
    pj = [biased[8 * j:8 * (j + 1)] for j in range(PER_GROUP)]
    sel = []
    for j in range(PER_GROUP):
        beaten = jnp.zeros((8, tq), F32)
        for j2 in range(PER_GROUP):
            if j2 == j:
                continue
            beats = (pj[j2] > pj[j]) | (pj[j2] == pj[j]) if j2 < j else (pj[j2] > pj[j])
            beaten = beaten + jnp.where(beats, 1.0, 0.0)
        sel.append(beaten < 2.0)
    gscore = jnp.zeros((8, tq), F32)
    for j in range(PER_GROUP):
        gscore = gscore + jnp.where(sel[j], pj[j], 0.0)
    gi = lax.broadcasted_iota(I32, (8, tq), 0).astype(F32)
    gmax = jnp.max(gscore, axis=0, keepdims=True)
    gidx = jnp.min(jnp.where(gscore == gmax, gi, 8.0), axis=0, keepdims=True)
    onehot = gi == gidx
    f = [jnp.max(jnp.where(onehot & sel[j], 1.0, 0.0), axis=0, keepdims=True) > 0.5
         for j in range(PER_GROUP)]
    pair = jnp.where(f[0], jnp.where(f[1], 0.0, jnp.where(f[2], 1.0, 2.0)),
                     jnp.where(f[1], jnp.where(f[2], 3.0, 4.0), 5.0))
    cls = (gidx * float(len(PAIRS)) + pair).astype(I32)
    cls_ref[0] = jnp.broadcast_to(cls, (8, tq))

    @pl.when((pl.program_id(0) == 0) & (pl.program_id(1) == 0))
    def _():
        carry[...] = jnp.zeros_like(carry)

    oh = lax.broadcasted_iota(I32, (32, tq), 0) == cls
    ohf = jnp.where(oh, 1.0, 0.0)
    before = lax.broadcasted_iota(I32, (tq, tq), 0) < lax.broadcasted_iota(I32, (tq, tq), 1)
    upper = jnp.where(before, 1.0, 0.0).astype(BF16)
    within = jnp.dot(ohf.astype(BF16), upper, preferred_element_type=F32)
    c = carry[...]
    tot = jnp.sum(jnp.where(oh, within + c[:, 0:1], 0.0), axis=0, keepdims=True)
    rank_ref[0] = jnp.broadcast_to(tot.astype(I32), (8, tq))
    newc = c + jnp.sum(ohf, axis=1, keepdims=True)
    carry[...] = newc
    cnt_ref[...] = newc


def _post(o, wo, x, gm, gf, scf, shf, rw32, rb32):
    bsz, s, d = x.shape
    do = o.shape[-1]
    tq = TM
    nt = s // tq
    tok = lambda w_: pl.BlockSpec((1, tq, w_), lambda b, i: (b, i, 0))
    per_b = pl.BlockSpec((1, 1, d), lambda b, i: (b, 0, 0))
    full = lambda a: pl.BlockSpec(a.shape, lambda b, i: (0,) * a.ndim)
    per_tile = pl.BlockSpec((1, 8, tq), lambda b, i: (b * nt + i, 0, 0))
    gf2 = gf.reshape(1, d)
    return pl.pallas_call(
        _post_kernel,
        grid=(bsz, nt),
        in_specs=[tok(do), full(wo), tok(d), per_b, full(gf2), per_b, per_b, full(rw32), full(rb32)],
        out_specs=[tok(d), tok(d // 2), per_tile, per_tile, pl.BlockSpec((32, 128), lambda b, i: (0, 0))],
        out_shape=[jax.ShapeDtypeStruct((bsz, s, d), F32),
                   jax.ShapeDtypeStruct((bsz, s, d // 2), U32),
                   jax.ShapeDtypeStruct((bsz * nt, 8, tq), I32),
                   jax.ShapeDtypeStruct((bsz * nt, 8, tq), I32),
                   jax.ShapeDtypeStruct((32, 128), F32)],
        scratch_shapes=[pltpu.VMEM((32, 128), F32)],
        compiler_params=_cparams("arbitrary", "arbitrary"),
        name="post_attention",
    )(o, wo, x, gm, gf2, scf, shf, rw32, rb32)


def _ffn_kernel(e1_ref, e2_ref, nused_ref, hs_ref, wg1, wu1, wd1, rw1, wg2, wu2, wd2, rw2, ys_ref):
    j = pl.program_id(0)

    @pl.when(j < nused_ref[0])
    def _():
        xf = _unpack_bf16_pairs(hs_ref[...])
        x = xf.astype(BF16)
        s1 = jax.nn.sigmoid(jnp.sum(xf * rw1[0].astype(F32), axis=-1, keepdims=True))
        s2 = jax.nn.sigmoid(jnp.sum(xf * rw2[0].astype(F32), axis=-1, keepdims=True))
        den = s1 + s2

        def expert(wg, wu, wd):
            gt = jnp.dot(x, wg[0], preferred_element_type=F32)
            up = jnp.dot(x, wu[0], preferred_element_type=F32)
            a = (gt * jax.nn.sigmoid(gt)) * up
            return jnp.dot(a.astype(BF16), wd[0], preferred_element_type=F32)

        y = (s1 / den) * expert(wg1, wu1, wd1) + (s2 / den) * expert(wg2, wu2, wd2)
        ys_ref[...] = _pack_bf16_pairs(y)

    @pl.when(j >= nused_ref[0])
    def _():
        ys_ref[...] = jnp.zeros_like(ys_ref)


def _cast_kernel(*refs):
    n = len(refs) // 2
    for src, dst in zip(refs[:n], refs[n:]):
        dst[...] = src[...].astype(BF16)


def _cast_expert_weights(ws):
    flat = [w.reshape((w.shape[0] * w.shape[1],) + w.shape[2:]) for w in ws]
    spec = lambda w: pl.BlockSpec((1,) + w.shape[1:], lambda i: (i, 0, 0))
    return pl.pallas_call(
        _cast_kernel,
        grid=(flat[0].shape[0],),
        in_specs=[spec(w) for w in flat],
        out_specs=[spec(w) for w in flat],
        out_shape=[jax.ShapeDtypeStruct(w.shape, BF16) for w in flat],
        compiler_params=_cparams("parallel"),
        name="cast_expert_weights",
    )(*flat)


def _ffn(hs, e1, e2, nused, wg, wu, wd, rw3, layer):
    npad, dh = hs.shape
    d = 2 * dh
    tm = FFN_TM
    de = wg.shape[-1]
    off = layer * N_EXPERTS
    row = pl.BlockSpec((tm, dh), lambda j, e1, e2, nu: (jnp.minimum(j, nu[0] - 1), 0))
    first = lambda shape: pl.BlockSpec((1,) + shape, lambda j, e1, e2, nu: (e1[j] + off, 0, 0))
    second = lambda shape: pl.BlockSpec((1,) + shape, lambda j, e1, e2, nu: (e2[j] + off, 0, 0))
    router1 = pl.BlockSpec((1, 1, d), lambda j, e1, e2, nu: (e1[j], 0, 0))
    router2 = pl.BlockSpec((1, 1, d), lambda j, e1, e2, nu: (e2[j], 0, 0))
    grid_spec = pltpu.PrefetchScalarGridSpec(
        num_scalar_prefetch=3,
        grid=(npad // tm,),
        in_specs=[row,
                  first((d, de)), first((d, de)), first((de, d)), router1,
                  second((d, de)), second((d, de)), second((de, d)), router2],
        out_specs=pl.BlockSpec((tm, dh), lambda j, e1, e2, nu: (j, 0)),
    )
    return pl.pallas_call(
        _ffn_kernel,
        grid_spec=grid_spec,
        out_shape=jax.ShapeDtypeStruct((npad, dh), U32),
        compiler_params=_cparams("arbitrary"),
        name="expert_ffn",
    )(e1, e2, nused, hs, wg, wu, wd, rw3, wg, wu, wd, rw3)


def _sc_mesh():
    return plsc.VectorSubcoreMesh(core_axis_name="c", subcore_axis_name="s",
                                  num_cores=SC_CORES_V7X, num_subcores=SC_SUBCORES_V7X)


def _sc_rows_per_step(per_worker):
    return min(SC_MAX_INDEX_VECTOR, per_worker)


def _sc_scatter_rows(src, idx, n_out):
    n_in, d = src.shape
    workers = SC_CORES_V7X * SC_SUBCORES_V7X
    per_w = n_in // workers
    ch = _sc_rows_per_step(per_w)
    assert per_w * workers == n_in and per_w % ch == 0 and ch % 8 == 0

    def body(src_hbm, idx_hbm, out_hbm, idx_v, rows_v, sem):
        base = (lax.axis_index("s") * SC_CORES_V7X + lax.axis_index("c")) * per_w

        @pl.loop(0, per_w // ch)
        def _(i):
            off = base + i * ch
            pltpu.sync_copy(idx_hbm.at[pl.ds(off, ch)], idx_v)
            pltpu.sync_copy(src_hbm.at[pl.ds(off, ch)], rows_v)
            pltpu.async_copy(rows_v, out_hbm.at[idx_v], sem).wait()

    return pl.kernel(
        body, mesh=_sc_mesh(),
        out_type=jax.ShapeDtypeStruct((n_out, d), src.dtype),
        scratch_types=[pltpu.VMEM((ch,), I32), pltpu.VMEM((ch, d), src.dtype), pltpu.SemaphoreType.DMA],
        name="sc_scatter_rows",
    )(src, idx)


def _sc_gather_rows(table, idx):
    n_out = idx.shape[0]
    d = table.shape[1]
    workers = SC_CORES_V7X * SC_SUBCORES_V7X
    per_w = n_out // workers
    ch = _sc_rows_per_step(per_w)
    assert per_w * workers == n_out and per_w % ch == 0 and ch % 8 == 0

    def body(table_hbm, idx_hbm, out_hbm, idx_v, rows_v, sem):
        base = (lax.axis_index("s") * SC_CORES_V7X + lax.axis_index("c")) * per_w

        @pl.loop(0, per_w // ch)
        def _(i):
            off = base + i * ch
            pltpu.sync_copy(idx_hbm.at[pl.ds(off, ch)], idx_v)
            pltpu.async_copy(table_hbm.at[idx_v], rows_v, sem).wait()
            pltpu.sync_copy(rows_v, out_hbm.at[pl.ds(off, ch)])

    return pl.kernel(
        body, mesh=_sc_mesh(),
        out_type=jax.ShapeDtypeStruct((n_out, d), table.dtype),
        scratch_types=[pltpu.VMEM((ch,), I32), pltpu.VMEM((ch, d), table.dtype), pltpu.SemaphoreType.DMA],
        name="sc_gather_rows",
    )(table, idx)


def _moe(h2p, cls3, rank3, cnt, wg, wu, wd, rw3, layer):
    bsz, s, dh = h2p.shape
    t = bsz * s
    tm = FFN_TM
    cls = cls3[:, 0, :].reshape(t)
    rank = rank3[:, 0, :].reshape(t)
    counts = cnt[:N_CLASSES, 0].astype(I32)
    padded = ((counts + tm - 1) // tm) * tm
    upto = jnp.arange(N_CLASSES)[:, None] >= jnp.arange(N_CLASSES)[None, :]
    ends = jnp.sum(jnp.where(upto, padded[None, :], 0), axis=1)
    base = ends - padded
    pos = jnp.take(base, cls, mode="clip") + rank
    ntiles = t // tm + N_CLASSES
    tile_start = jnp.arange(ntiles, dtype=I32) * tm
    tcls = jnp.minimum(jnp.sum((ends[None, :] <= tile_start[:, None]).astype(I32), axis=1), N_CLASSES - 1)
    nused = (ends[-1:] // tm).astype(I32)
    grp = tcls // len(PAIRS)
    pr = tcls % len(PAIRS)
    first = jnp.array([p[0] for p in PAIRS], I32)
    second = jnp.array([p[1] for p in PAIRS], I32)
    e1 = (grp * PER_GROUP + jnp.take(first, pr, mode="clip")).astype(I32)
    e2 = (grp * PER_GROUP + jnp.take(second, pr, mode="clip")).astype(I32)
    hs = _sc_scatter_rows(h2p.reshape(t, dh), pos, ntiles * tm)
    ys = _ffn(hs, e1, e2, nused, wg, wu, wd, rw3, layer)
    return _sc_gather_rows(ys, pos).reshape(bsz, s, dh)


def _qkv1_kernel(x_ref, y_ref, gfp_ref, g_ref, sc_ref, sh_ref, w_ref, cos_ref, sin_ref,
                 x2_ref, *rest):
    outs = rest[:9]
    hb_scr, ysc = rest[9:]
    tm = x_ref.shape[1]
    x2 = x_ref[0] + gfp_ref[0] * _unpack_bf16_pairs(y_ref[0])
    x2_ref[0] = x2
    hb_scr[...] = _rms_mod(x2, g_ref[...], sc_ref[0], sh_ref[0]).astype(BF16)
    cos = cos_ref[0]
    sin = sin_ref[0]
    lane_lo = (lax.broadcasted_iota(I32, cos.shape, 1) % HEAD_DIM) < (HEAD_DIM // 2)
    for gi, (_, dil) in enumerate(B_GROUPS):
        for j in range(3):
            c = gi * 3 + j
            y = jnp.dot(hb_scr[...], w_ref[:, B_WIDTH * c:B_WIDTH * (c + 1)], preferred_element_type=F32)
            out = outs[c]
            for cc in range(B_WIDTH // 128):
                lanes = slice(128 * cc, 128 * (cc + 1))
                r = y[:, lanes]
                if j < 2:
                    r = _rope_apply(r, cos, sin, lane_lo)
                if j == 0:
                    r = r * Q_SCALE
                if dil == 1:
                    out[0, 0, :, lanes] = r.astype(BF16)
                else:
                    ysc[cc] = r
                    for rr in range(dil):
                        out[0, rr, :, lanes] = ysc[cc, pl.ds(rr, tm // dil, stride=dil), :].astype(BF16)


def _qkv1(x, y, gfp, g, sc, sh, w, cos, sin):
    bsz, s, d = x.shape
    tm = TM
    tok = lambda w_: pl.BlockSpec((1, tm, w_), lambda b, i: (b, i, 0))
    per_b = pl.BlockSpec((1, 1, d), lambda b, i: (b, 0, 0))
    out_specs = [tok(d)]
    out_shape = [jax.ShapeDtypeStruct((bsz, s, d), F32)]
    for _, dil in B_GROUPS:
        for _ in range(3):
            out_specs.append(pl.BlockSpec((1, dil, tm // dil, B_WIDTH), lambda b, i: (b, 0, i, 0)))
            out_shape.append(jax.ShapeDtypeStruct((bsz, dil, s // dil, B_WIDTH), BF16))
    return pl.pallas_call(
        _qkv1_kernel,
        grid=(bsz, s // tm),
        in_specs=[tok(d), tok(d // 2), per_b, pl.BlockSpec((1, d), lambda b, i: (0, 0)), per_b, per_b,
                  pl.BlockSpec(w.shape, lambda b, i: (0, 0)), tok(128), tok(128)],
        out_specs=out_specs,
        out_shape=out_shape,
        scratch_shapes=[pltpu.VMEM((tm, d), BF16), pltpu.VMEM((B_WIDTH // 128, tm, 128), F32)],
        compiler_params=_cparams("parallel", "parallel"),
        name="qkv1",
    )(x, y, gfp, g.reshape(1, d), sc, sh, w, cos, sin)


def _att1_window_pad(lseg):
    qb = min(2 * B_RADIUS, lseg)
    return -(qb + 2 * B_RADIUS) % 128


def _att1_kernel(*refs, chunk, seq):
    ins = refs[:21]
    o_ref = refs[21]
    scr = refs[22:]
    kfs, vfs, nat_o, nat_d, nat_m = scr[0:3], scr[3:6], scr[6:9], scr[9:12], scr[12:15]
    n = pl.program_id(1)
    rad = B_RADIUS
    npair = B_WIDTH // 128
    for gi, (_, dil) in enumerate(B_GROUPS):
        _, kc, kp, kn, vc, vp, vn = ins[7 * gi:7 * gi + 7]
        lseg = chunk // dil
        for full, (p_, c_, n_) in ((kfs[gi], (kp, kc, kn)), (vfs[gi], (vp, vc, vn))):
            full[:, 0:rad] = p_[0]
            full[:, rad:rad + lseg] = c_[0]
            full[:, rad + lseg:rad + lseg + rad] = n_[0]
            if _att1_window_pad(lseg):
                full[:, rad + lseg + rad:rad + lseg + rad + _att1_window_pad(lseg)] = n_[0]

    for c in range(npair):
        lanes = slice(128 * c, 128 * (c + 1))
        for gi, (_, dil) in enumerate(B_GROUPS):
            q_ref = ins[7 * gi]
            kf, vf = kfs[gi], vfs[gi]
            lseg = chunk // dil
            qb = min(2 * rad, lseg)
            kb_n = qb + 2 * rad + _att1_window_pad(lseg)
            nblk = lseg // qb
            lstr = seq // dil
            rel = lax.broadcasted_iota(I32, (qb, kb_n), 1) - lax.broadcasted_iota(I32, (qb, kb_n), 0)
            band_bias = jnp.where((rel >= 0) & (rel <= 2 * rad), 0.0, NEG)
            jrow = lax.broadcasted_iota(I32, (1, kb_n), 1)
            first_head = lax.broadcasted_iota(I32, (qb, 128), 1) < HEAD_DIM
            zero = jnp.zeros((qb, 128), BF16)
            ones = jnp.ones((kb_n, 128), BF16)
            per_iter = ATT1_TILE_ROWS_PER_ITER // qb

            def tile(t, gi=gi, q_ref=q_ref, kf=kf, vf=vf, lseg=lseg, qb=qb, kb_n=kb_n, nblk=nblk, lstr=lstr,
                     jrow=jrow, band_bias=band_bias, dil=dil, first_head=first_head, zero=zero, ones=ones,
                     lanes=lanes):
                res = t // nblk
                q0 = pl.multiple_of((t % nblk) * qb, qb)
                apos = n * lseg + q0 - rad + jrow
                bias = band_bias + jnp.where((apos >= 0) & (apos < lstr), 0.0, NEG)
                if dil == 1:
                    rows = pl.ds(q0, qb)
                else:
                    rows = pl.ds(res + dil * q0, qb, stride=dil)
                qc = q_ref[0, res, pl.ds(q0, qb), lanes]
                kb = kf[res, pl.ds(q0, kb_n), lanes]
                vb = jnp.concatenate([vf[res, pl.ds(q0, kb_n), lanes], ones], axis=1)
                qs = jnp.concatenate([jnp.where(first_head, qc, zero), jnp.where(first_head, zero, qc)],
                                     axis=0)
                s2 = lax.dot_general(qs, kb, (((1,), (1,)), ((), ())), preferred_element_type=F32)
                ps, ms = [], []
                for t2 in range(2):
                    s = s2[qb * t2:qb * (t2 + 1)] + bias
                    m = jnp.max(s, axis=-1, keepdims=True)
                    ps.append(jnp.exp2(s - m).astype(BF16))
                    ms.append(m)
                pv = jnp.dot(jnp.concatenate(ps, axis=0), vb, preferred_element_type=F32)
                nat_o[gi][rows, :] = jnp.where(first_head, pv[0:qb, 0:128], pv[qb:2 * qb, 0:128])
                nat_d[gi][rows, :] = jnp.where(first_head, pv[0:qb, 128:256], pv[qb:2 * qb, 128:256])
                nat_m[gi][rows, :] = jnp.where(first_head, ms[0], ms[1])

            def body(t, carry, tile=tile, per_iter=per_iter):
                for u in range(per_iter):
                    tile(t * per_iter + u)
                return carry

            lax.fori_loop(0, dil * nblk // per_iter, body, 0)

        ms = [nat_m[gi][...] for gi in range(3)]
        mx = jnp.maximum(jnp.maximum(ms[0], ms[1]), ms[2])
        w = [jnp.exp2(x - mx) for x in ms]
        num = w[0] * nat_o[0][...] + w[1] * nat_o[1][...] + w[2] * nat_o[2][...]
        den = w[0] * nat_d[0][...] + w[1] * nat_d[1][...] + w[2] * nat_d[2][...]
        o_ref[0, :, lanes] = (num / den).astype(BF16)


def _att1(qkv):
    bsz = qkv[0].shape[0]
    seq = qkv[0].shape[1] * qkv[0].shape[2]
    chunk = ATT1_CHUNK
    rad = B_RADIUS
    in_specs, args, scratch = [], [], []
    for gi, (_, dil) in enumerate(B_GROUPS):
        q, k, v = qkv[3 * gi:3 * gi + 3]
        lseg = chunk // dil
        per = lseg // rad
        last = seq // dil // rad - 1
        cur = pl.BlockSpec((1, dil, lseg, B_WIDTH), lambda b, n: (b, 0, n, 0))
        prev = pl.BlockSpec((1, dil, rad, B_WIDTH),
                            lambda b, n, per=per: (b, 0, jnp.maximum(n * per - 1, 0), 0))
        nxt = pl.BlockSpec((1, dil, rad, B_WIDTH),
                           lambda b, n, per=per, last=last: (b, 0, jnp.minimum(n * per + per, last), 0))
        in_specs += [cur, cur, prev, nxt, cur, prev, nxt]
        args += [q, k, k, k, v, v, v]
    for _ in range(2):
        for _, dil in B_GROUPS:
            lseg = chunk // dil
            scratch.append(pltpu.VMEM((dil, lseg + 2 * rad + _att1_window_pad(lseg), B_WIDTH), BF16))
    for _ in range(3):
        for _ in B_GROUPS:
            scratch.append(pltpu.VMEM((chunk, 128), F32))
    return pl.pallas_call(
        functools.partial(_att1_kernel, chunk=chunk, seq=seq),
        grid=(bsz, seq // chunk),
        in_specs=in_specs,
        out_specs=pl.BlockSpec((1, chunk, B_WIDTH), lambda b, n: (b, n, 0)),
        out_shape=jax.ShapeDtypeStruct((bsz, seq, B_WIDTH), BF16),
        scratch_shapes=scratch,
        compiler_params=_cparams("parallel", "parallel"),
        name="att1",
    )(*args)


def _final_kernel(x_ref, y_ref, gfp_ref, g_ref, o_ref):
    x = x_ref[0] + gfp_ref[0] * _unpack_bf16_pairs(y_ref[0])
    ms = jnp.mean(x * x, axis=-1, keepdims=True)
    o_ref[0] = (x * lax.rsqrt(ms + NORM_EPS)) * g_ref[...]


def _final(x, y, gfp, g):
    bsz, s, d = x.shape
    tm = TM
    tok = pl.BlockSpec((1, tm, d), lambda b, i: (b, i, 0))
    return pl.pallas_call(
        _final_kernel,
        grid=(bsz, s // tm),
        in_specs=[tok, pl.BlockSpec((1, tm, d // 2), lambda b, i: (b, i, 0)),
                  pl.BlockSpec((1, 1, d), lambda b, i: (b, 0, 0)),
                  pl.BlockSpec((1, d), lambda b, i: (0, 0))],
        out_specs=tok,
        out_shape=jax.ShapeDtypeStruct((bsz, s, d), F32),
        compiler_params=_cparams("parallel", "parallel"),
        name="final_norm",
    )(x, y, gfp, g.reshape(1, d))


def kernel(x, c, positions, ada_w, ada_b, norm_mix_g, norm_ffn_g, a_w_qkv, a_w_o, a_sink, b_w_qkv, b_w_o,
           router_w, router_bias, exp_w_gate, exp_w_up, exp_w_down, final_norm_g):
    bsz, s, d = x.shape
    assert d == D_MODEL and ada_w.shape[0] == 2 and s % ATT1_CHUNK == 0
    assert all(w_ // (2 * dil) == B_RADIUS for w_, dil in B_GROUPS)

    mod = _modulation(c, ada_w, ada_b)
    mods = [[mod[l][:, k * d:(k + 1) * d].reshape(bsz, 1, d) for k in range(6)] for l in range(2)]
    cos, sin = _rope_tables(positions)

    rw_t = router_w.T.astype(BF16)
    rows = jnp.array([PER_GROUP * g + j if g < N_EXPERT_GROUPS else 0
                      for j in range(PER_GROUP) for g in range(8)], I32)
    live = jnp.array([1.0 if g < N_EXPERT_GROUPS else 0.0
                      for j in range(PER_GROUP) for g in range(8)], F32)
    rw32 = (rw_t[rows].astype(F32) * live[:, None]).astype(BF16)
    rb32 = jnp.where(live > 0, router_bias.astype(F32)[rows], NEG).reshape(32, 1)
    rw3 = rw_t.reshape(N_EXPERTS, 1, d)
    experts = _cast_expert_weights((exp_w_gate, exp_w_up, exp_w_down))

    sh_m, sc_m, g_m, sh_f, sc_f, g_f = mods[0]
    q, k, v = _qkv0(x, norm_mix_g[0], sc_m, sh_m, a_w_qkv[0].astype(BF16), cos, sin)
    o = _att0(q, k, v, a_sink[0].astype(F32))
    x1, *routed = _post(o, a_w_o[0].astype(BF16), x, g_m, norm_ffn_g[0], sc_f, sh_f, rw32, rb32)
    y = _moe(*routed, *experts, rw3, 0)
    g_f_prev = g_f

    sh_m, sc_m, g_m, sh_f, sc_f, g_f = mods[1]
    outs = _qkv1(x1, y, g_f_prev, norm_mix_g[1], sc_m, sh_m, b_w_qkv[0].astype(BF16), cos, sin)
    x2, qkv = outs[0], outs[1:]
    o = _att1(qkv)
    x3, *routed = _post(o, b_w_o[0].astype(BF16), x2, g_m, norm_ffn_g[1], sc_f, sh_f, rw32, rb32)
    y = _moe(*routed, *experts, rw3, 1)

    return _final(x3, y, g_f, final_norm_g)
```

```python
import functools

import jax
import jax.numpy as jnp
from jax import lax
from jax.experimental import pallas as pl
from jax.experimental.pallas import tpu as pltpu
from jax.experimental.pallas import tpu_sc as plsc

F32, BF16, I32, U32 = jnp.float32, jnp.bfloat16, jnp.int32, jnp.uint32

D_MODEL = 1024
HEAD_DIM = 64
ROPE_THETA = 10000.0
NORM_EPS = 1e-6
A_Q_HEADS = 16
A_KV_HEADS = 4
A_GROUP = A_Q_HEADS // A_KV_HEADS
A_RADIUS = 128
B_GROUPS = ((128, 1), (512, 4), (2048, 16))
B_HEADS = 8
B_RADIUS = 64
B_WIDTH = B_HEADS * HEAD_DIM
N_EXPERTS = 16
N_EXPERT_GROUPS = 4
PER_GROUP = N_EXPERTS // N_EXPERT_GROUPS
D_EXPERT = D_MODEL // 2
PAIRS = ((0, 1), (0, 2), (0, 3), (1, 2), (1, 3), (2, 3))
N_CLASSES = N_EXPERT_GROUPS * len(PAIRS)
NEG = -1e30
LOG2E = 1.4426950408889634
Q_SCALE = HEAD_DIM ** -0.5 * LOG2E

VMEM_LIMIT = 56 * 1024 * 1024
TM = 512
ATT1_CHUNK = 1024
ATT1_TILE_ROWS_PER_ITER = 1024
FFN_TM = 256
SC_CORES_V7X = 2
SC_SUBCORES_V7X = 16
SC_ROWS_PER_STEP = 64


def _cparams(*sem):
    return pltpu.CompilerParams(dimension_semantics=sem, vmem_limit_bytes=VMEM_LIMIT)


def _rms_mod(x, g, sc, sh):
    ms = jnp.mean(x * x, axis=-1, keepdims=True)
    return (x * lax.rsqrt(ms + NORM_EPS)) * g * (1.0 + sc) + sh


def _rope_apply(y, cos, sin_signed, lane_lo):
    sw = jnp.where(lane_lo, pltpu.roll(y, 96, 1), pltpu.roll(y, 32, 1))
    return y * cos + sw * sin_signed


def _pack_bf16_pairs(x):
    n = x.shape[1] // 2
    lo = lax.bitcast_convert_type(x[:, :n].astype(BF16).astype(F32), U32)
    hi = lax.bitcast_convert_type(x[:, n:].astype(BF16).astype(F32), U32)
    return (hi & jnp.uint32(0xFFFF0000)) | (lo >> 16)


def _unpack_bf16_pairs(p):
    lo = lax.bitcast_convert_type(p << 16, F32)
    hi = lax.bitcast_convert_type(p & jnp.uint32(0xFFFF0000), F32)
    return jnp.concatenate([lo, hi], axis=1)


def _dup_head(x, first_head, which):
    rolled = pltpu.roll(x, HEAD_DIM, 1)
    return jnp.where(first_head, x, rolled) if which == 0 else jnp.where(first_head, rolled, x)


def _mod_kernel(c_ref, w_ref, b_ref, o_ref):
    c = c_ref[...]
    s = c * jax.nn.sigmoid(c)
    o_ref[0] = jnp.dot(s, w_ref[0], preferred_element_type=F32,
                       precision=lax.Precision.HIGHEST) + b_ref[0]


def _modulation(c, ada_w, ada_b):
    depth, d, n = ada_w.shape
    bsz = c.shape[0]
    cp = jnp.zeros((8, d), F32).at[:bsz].set(c)
    tn = 1536
    out = pl.pallas_call(
        _mod_kernel,
        grid=(depth, n // tn),
        in_specs=[pl.BlockSpec((8, d), lambda l, j: (0, 0)),
                  pl.BlockSpec((1, d, tn), lambda l, j: (l, 0, j)),
                  pl.BlockSpec((1, 1, tn), lambda l, j: (l, 0, j))],
        out_specs=pl.BlockSpec((1, 8, tn), lambda l, j: (l, 0, j)),
        out_shape=jax.ShapeDtypeStruct((depth, 8, n), F32),
        compiler_params=_cparams("arbitrary", "arbitrary"),
        name="modulation",
    )(cp, ada_w, ada_b.reshape(depth, 1, n))
    return out[:, :bsz]


def _rope_kernel(pos_ref, invf_ref, sign_ref, cos_ref, sin_ref):
    ang = pos_ref[0].astype(F32) * invf_ref[...]
    cos_ref[0] = jnp.cos(ang)
    sin_ref[0] = jnp.sin(ang) * sign_ref[...]


def _rope_tables(positions):
    bsz, s = positions.shape
    inv_freq = ROPE_THETA ** (-jnp.arange(0, HEAD_DIM, 2, dtype=F32) / HEAD_DIM)
    invf = jnp.tile(inv_freq, 4).reshape(1, 128)
    half = HEAD_DIM // 2
    sign = jnp.tile(jnp.concatenate([-jnp.ones((half,), F32), jnp.ones((half,), F32)]), 2).reshape(1, 128)
    ts = TM
    return pl.pallas_call(
        _rope_kernel,
        grid=(bsz, s // ts),
        in_specs=[pl.BlockSpec((1, ts, 1), lambda b, i: (b, i, 0)),
                  pl.BlockSpec((1, 128), lambda b, i: (0, 0)),
                  pl.BlockSpec((1, 128), lambda b, i: (0, 0))],
        out_specs=[pl.BlockSpec((1, ts, 128), lambda b, i: (b, i, 0))] * 2,
        out_shape=[jax.ShapeDtypeStruct((bsz, s, 128), F32)] * 2,
        compiler_params=_cparams("parallel", "parallel"),
        name="rope_tables",
    )(positions.reshape(bsz, s, 1), invf, sign)


def _qkv0_kernel(x_ref, g_ref, sc_ref, sh_ref, w_ref, cos_ref, sin_ref,
                 q_ref, k_ref, v_ref, hb_scr):
    hb_scr[...] = _rms_mod(x_ref[0], g_ref[...], sc_ref[0], sh_ref[0]).astype(BF16)
    cos = cos_ref[0]
    sin = sin_ref[0]
    lane_lo = (lax.broadcasted_iota(I32, cos.shape, 1) % HEAD_DIM) < (HEAD_DIM // 2)
    nq = A_Q_HEADS * HEAD_DIM
    nkv = A_KV_HEADS * HEAD_DIM
    for c in range(nq // 256):
        y = jnp.dot(hb_scr[...], w_ref[:, 256 * c:256 * (c + 1)], preferred_element_type=F32)
        for cc in range(2):
            r = _rope_apply(y[:, 128 * cc:128 * (cc + 1)], cos, sin, lane_lo) * Q_SCALE
            q_ref[0, :, 256 * c + 128 * cc:256 * c + 128 * (cc + 1)] = r.astype(BF16)
    first_head = lax.broadcasted_iota(I32, cos.shape, 1) < HEAD_DIM
    y = jnp.dot(hb_scr[...], w_ref[:, nq:nq + nkv], preferred_element_type=F32)
    for cc in range(nkv // 128):
        r = _rope_apply(y[:, 128 * cc:128 * (cc + 1)], cos, sin, lane_lo)
        for which in range(2):
            g = 2 * cc + which
            k_ref[0, :, 128 * g:128 * (g + 1)] = _dup_head(r, first_head, which).astype(BF16)
    y = jnp.dot(hb_scr[...], w_ref[:, nq + nkv:nq + 2 * nkv], preferred_element_type=F32)
    for cc in range(nkv // 128):
        r = y[:, 128 * cc:128 * (cc + 1)]
        for which in range(2):
            g = 2 * cc + which
            v_ref[0, :, 128 * g:128 * (g + 1)] = _dup_head(r, first_head, which).astype(BF16)


def _qkv0(x, g, sc, sh, w, cos, sin):
    bsz, s, d = x.shape
    nq = A_Q_HEADS * HEAD_DIM
    nkv = A_KV_HEADS * HEAD_DIM
    tm = TM
    tok = lambda w_: pl.BlockSpec((1, tm, w_), lambda b, i: (b, i, 0))
    per_b = pl.BlockSpec((1, 1, d), lambda b, i: (b, 0, 0))
    return pl.pallas_call(
        _qkv0_kernel,
        grid=(bsz, s // tm),
        in_specs=[tok(d), pl.BlockSpec((1, d), lambda b, i: (0, 0)), per_b, per_b,
                  pl.BlockSpec(w.shape, lambda b, i: (0, 0)), tok(128), tok(128)],
        out_specs=[tok(nq), tok(2 * nkv), tok(2 * nkv)],
        out_shape=[jax.ShapeDtypeStruct((bsz, s, nq), BF16),
                   jax.ShapeDtypeStruct((bsz, s, 2 * nkv), BF16),
                   jax.ShapeDtypeStruct((bsz, s, 2 * nkv), BF16)],
        scratch_shapes=[pltpu.VMEM((tm, d), BF16)],
        compiler_params=_cparams("parallel", "parallel"),
        name="qkv0",
    )(x, g.reshape(1, d), sc, sh, w, cos, sin)


def _att0_kernel(sink_ref, q_ref, kc_ref, kp_ref, kn_ref, vc_ref, vp_ref, vn_ref,
                 o_ref, kf, vf, *, tq, seq):
    i = pl.program_id(1)
    r = A_RADIUS
    ones = jnp.ones((tq + 2 * r, 128), BF16)
    for g in range(A_KV_HEADS):
        lanes = slice(128 * g, 128 * (g + 1))
        kf[g, 0:r] = kp_ref[0, :, lanes]
        kf[g, r:r + tq] = kc_ref[0, :, lanes]
        kf[g, r + tq:r + tq + r] = kn_ref[0, :, lanes]
        vf[g, 0:r, 0:128] = vp_ref[0, :, lanes]
        vf[g, r:r + tq, 0:128] = vc_ref[0, :, lanes]
        vf[g, r + tq:r + tq + r, 0:128] = vn_ref[0, :, lanes]
        vf[g, :, 128:256] = ones

    ii = lax.broadcasted_iota(I32, (r, r), 0)
    jj = lax.broadcasted_iota(I32, (r, r), 1)
    first_head = jj < HEAD_DIM
    zero = jnp.zeros((r, 128), BF16)

    for j in range(tq // r):
        q0 = j * r
        base = i * tq + q0
        bias_lo = jnp.where((jj >= ii) & (base - r + jj >= 0), 0.0, NEG)
        bias_hi = jnp.where((jj <= ii) & (base + r + jj < seq), 0.0, NEG)
        for g in range(A_KV_HEADS):
            kb = kf[g, pl.ds(q0, 3 * r), :]
            vb = vf[g, pl.ds(q0, 3 * r), :]
            qs = []
            for cc in range(2):
                c = 2 * g + cc
                qc = q_ref[0, pl.ds(q0, r), 128 * c:128 * (c + 1)]
                qs += [jnp.where(first_head, qc, zero), jnp.where(first_head, zero, qc)]
            s4 = lax.dot_general(jnp.concatenate(qs, axis=0), kb, (((1,), (1,)), ((), ())),
                                 preferred_element_type=F32)
            ps, ms = [], []
            for t in range(A_GROUP):
                s = s4[r * t:r * (t + 1)]
                a0 = s[:, 0:r] + bias_lo
                a1 = s[:, r:2 * r]
                a2 = s[:, 2 * r:3 * r] + bias_hi
                sink = sink_ref[A_GROUP * g + t] * LOG2E
                m = jnp.max(jnp.maximum(jnp.maximum(a0, a1), a2), axis=-1, keepdims=True)
                m = jnp.maximum(m, sink)
                ps.append(jnp.concatenate([jnp.exp2(a0 - m), jnp.exp2(a1 - m), jnp.exp2(a2 - m)],
                                          axis=1).astype(BF16))
                ms.append(m)
            pv = jnp.dot(jnp.concatenate(ps, axis=0), vb, preferred_element_type=F32)
            os_ = []
            for t in range(A_GROUP):
                sink = sink_ref[A_GROUP * g + t] * LOG2E
                den = pv[r * t:r * (t + 1), 128:256] + jnp.exp2(sink - ms[t])
                os_.append(pv[r * t:r * (t + 1), 0:128] / den)
            for cc in range(2):
                c = 2 * g + cc
                o_ref[0, pl.ds(q0, r), 128 * c:128 * (c + 1)] = jnp.where(
                    first_head, os_[2 * cc], os_[2 * cc + 1]).astype(BF16)


def _att0(q, k2, v2, sink):
    bsz, s, nq = q.shape
    nkv2 = k2.shape[-1]
    tq = TM
    r = A_RADIUS
    per = tq // r
    last = s // r - 1
    cur = lambda w_: pl.BlockSpec((1, tq, w_), lambda b, i: (b, i, 0))
    prev = pl.BlockSpec((1, r, nkv2), lambda b, i: (b, jnp.maximum(i * per - 1, 0), 0))
    nxt = pl.BlockSpec((1, r, nkv2), lambda b, i: (b, jnp.minimum(i * per + per, last), 0))
    return pl.pallas_call(
        functools.partial(_att0_kernel, tq=tq, seq=s),
        grid=(bsz, s // tq),
        in_specs=[pl.BlockSpec(memory_space=pltpu.SMEM),
                  cur(nq), cur(nkv2), prev, nxt, cur(nkv2), prev, nxt],
        out_specs=cur(nq),
        out_shape=jax.ShapeDtypeStruct((bsz, s, nq), BF16),
        scratch_shapes=[pltpu.VMEM((A_KV_HEADS, tq + 2 * r, 128), BF16),
                        pltpu.VMEM((A_KV_HEADS, tq + 2 * r, 256), BF16)],
        compiler_params=_cparams("parallel", "parallel"),
        name="att0",
    )(sink, q, k2, k2, k2, v2, v2, v2)


def _post_kernel(o_ref, wo_ref, x_ref, gm_ref, gf_ref, scf_ref, shf_ref, rw_ref, rb_ref, upper_ref,
                 x1_ref, h2_ref, cls_ref, rank_ref, cnt_ref, carry):
    mix = jnp.dot(o_ref[0], wo_ref[...], preferred_element_type=F32)
    x1 = x_ref[0] + gm_ref[0] * mix
    x1_ref[0] = x1
    h2 = _rms_mod(x1, gf_ref[...], scf_ref[0], shf_ref[0]).astype(BF16)
    h2_ref[0] = _pack_bf16_pairs(h2.astype(F32))
    tq = h2.shape[0]
    logits = lax.dot_general(rw_ref[...], h2, (((1,), (1,)), ((), ())), preferred_element_type=F32)
    scores = jax.nn.sigmoid(logits)
    biased = scores + rb_ref[...]
    pj = [biased[8 * j:8 * (j + 1)] for j in range(PER_GROUP)]
    sel = []
    for j in range(PER_GROUP):
        beaten = jnp.zeros((8, tq), F32)
        for j2 in range(PER_GROUP):
            if j2 == j:
                continue
            beats = (pj[j2] > pj[j]) | (pj[j2] == pj[j]) if j2 < j else (pj[j2] > pj[j])
            beaten = beaten + jnp.where(beats, 1.0, 0.0)
        sel.append(beaten < 2.0)
    gscore = jnp.zeros((8, tq), F32)
    for j in range(PER_GROUP):
        gscore = gscore + jnp.where(sel[j], pj[j], 0.0)
    gi = lax.broadcasted_iota(I32, (8, tq), 0).astype(F32)
    gmax = jnp.max(gscore, axis=0, keepdims=True)
    gidx = jnp.min(jnp.where(gscore == gmax, gi, 8.0), axis=0, keepdims=True)
    onehot = gi == gidx
    f = [jnp.max(jnp.where(onehot & sel[j], 1.0, 0.0), axis=0, keepdims=True) > 0.5
         for j in range(PER_GROUP)]
    pair = jnp.where(f[0], jnp.where(f[1], 0.0, jnp.where(f[2], 1.0, 2.0)),
                     jnp.where(f[1], jnp.where(f[2], 3.0, 4.0), 5.0))
    cls = (gidx * float(len(PAIRS)) + pair).astype(I32)
    cls_ref[0] = jnp.broadcast_to(cls, (8, tq))

    @pl.when((pl.program_id(0) == 0) & (pl.program_id(1) == 0))
    def _():
        carry[...] = jnp.zeros_like(carry)

    oh = lax.broadcasted_iota(I32, (32, tq), 0) == cls
    ohf = jnp.where(oh, 1.0, 0.0)
    within = jnp.dot(ohf.astype(BF16), upper_ref[...], preferred_element_type=F32)
    c = carry[...]
    tot = jnp.sum(jnp.where(oh, within + c[:, 0:1], 0.0), axis=0, keepdims=True)
    rank_ref[0] = jnp.broadcast_to(tot.astype(I32), (8, tq))
    newc = c + jnp.sum(ohf, axis=1, keepdims=True)
    carry[...] = newc
    cnt_ref[...] = newc


def _post(o, wo, x, gm, gf, scf, shf, rw32, rb32):
    bsz, s, d = x.shape
    do = o.shape[-1]
    tq = TM
    nt = s // tq
    tok = lambda w_: pl.BlockSpec((1, tq, w_), lambda b, i: (b, i, 0))
    per_b = pl.BlockSpec((1, 1, d), lambda b, i: (b, 0, 0))
    full = lambda a: pl.BlockSpec(a.shape, lambda b, i: (0,) * a.ndim)
    per_tile = pl.BlockSpec((1, 8, tq), lambda b, i: (b * nt + i, 0, 0))
    gf2 = gf.reshape(1, d)
    upper = (jnp.arange(tq)[:, None] < jnp.arange(tq)[None, :]).astype(BF16)
    return pl.pallas_call(
        _post_kernel,
        grid=(bsz, nt),
        in_specs=[tok(do), full(wo), tok(d), per_b, full(gf2), per_b, per_b, full(rw32), full(rb32),
                  full(upper)],
        out_specs=[tok(d), tok(d // 2), per_tile, per_tile, pl.BlockSpec((32, 128), lambda b, i: (0, 0))],
        out_shape=[jax.ShapeDtypeStruct((bsz, s, d), F32),
                   jax.ShapeDtypeStruct((bsz, s, d // 2), U32),
                   jax.ShapeDtypeStruct((bsz * nt, 8, tq), I32),
                   jax.ShapeDtypeStruct((bsz * nt, 8, tq), I32),
                   jax.ShapeDtypeStruct((32, 128), F32)],
        scratch_shapes=[pltpu.VMEM((32, 128), F32)],
        compiler_params=_cparams("arbitrary", "arbitrary"),
        name="post_attention",
    )(o, wo, x, gm, gf2, scf, shf, rw32, rb32, upper)


def _ffn_kernel(e1_ref, e2_ref, nused_ref, hs_ref, wg1, wu1, wd1, rw1, wg2, wu2, wd2, rw2, ys_ref):
    j = pl.program_id(0)

    @pl.when(j < nused_ref[0])
    def _():
        xf = _unpack_bf16_pairs(hs_ref[...])
        x = xf.astype(BF16)
        s1 = jax.nn.sigmoid(jnp.sum(xf * rw1[0].astype(F32), axis=-1, keepdims=True))
        s2 = jax.nn.sigmoid(jnp.sum(xf * rw2[0].astype(F32), axis=-1, keepdims=True))
        den = s1 + s2

        def expert(wg, wu, wd):
            gt = jnp.dot(x, wg[0], preferred_element_type=F32)
            up = jnp.dot(x, wu[0], preferred_element_type=F32)
            a = (gt * jax.nn.sigmoid(gt)) * up
            return jnp.dot(a.astype(BF16), wd[0], preferred_element_type=F32)

        y = (s1 / den) * expert(wg1, wu1, wd1) + (s2 / den) * expert(wg2, wu2, wd2)
        ys_ref[...] = _pack_bf16_pairs(y)

    @pl.when(j >= nused_ref[0])
    def _():
        ys_ref[...] = jnp.zeros_like(ys_ref)


def _cast_kernel(*refs):
    n = len(refs) // 2
    for src, dst in zip(refs[:n], refs[n:]):
        dst[...] = src[...].astype(BF16)


def _cast_expert_weights(ws, layer):
    n_exp = ws[0].shape[1]
    flat = [w.reshape((w.shape[0] * w.shape[1],) + w.shape[2:]) for w in ws]
    src = lambda w: pl.BlockSpec((1,) + w.shape[1:], lambda i: (i + layer * n_exp, 0, 0))
    dst = lambda w: pl.BlockSpec((1,) + w.shape[1:], lambda i: (i, 0, 0))
    return pl.pallas_call(
        _cast_kernel,
        grid=(n_exp,),
        in_specs=[src(w) for w in flat],
        out_specs=[dst(w) for w in flat],
        out_shape=[jax.ShapeDtypeStruct((n_exp,) + w.shape[1:], BF16) for w in flat],
        compiler_params=_cparams("parallel"),
        name="cast_expert_weights",
    )(*flat)


def _ffn(hs, e1, e2, nused, wg, wu, wd, rw3):
    npad, dh = hs.shape
    d = 2 * dh
    tm = FFN_TM
    de = wg.shape[-1]
    row = pl.BlockSpec((tm, dh), lambda j, e1, e2, nu: (jnp.minimum(j, nu[0] - 1), 0))
    first = lambda shape: pl.BlockSpec((1,) + shape, lambda j, e1, e2, nu: (e1[j], 0, 0))
    second = lambda shape: pl.BlockSpec((1,) + shape, lambda j, e1, e2, nu: (e2[j], 0, 0))
    router1 = pl.BlockSpec((1, 1, d), lambda j, e1, e2, nu: (e1[j], 0, 0))
    router2 = pl.BlockSpec((1, 1, d), lambda j, e1, e2, nu: (e2[j], 0, 0))
    grid_spec = pltpu.PrefetchScalarGridSpec(
        num_scalar_prefetch=3,
        grid=(npad // tm,),
        in_specs=[row,
                  first((d, de)), first((d, de)), first((de, d)), router1,
                  second((d, de)), second((d, de)), second((de, d)), router2],
        out_specs=pl.BlockSpec((tm, dh), lambda j, e1, e2, nu: (j, 0)),
    )
    return pl.pallas_call(
        _ffn_kernel,
        grid_spec=grid_spec,
        out_shape=jax.ShapeDtypeStruct((npad, dh), U32),
        compiler_params=_cparams("arbitrary"),
        name="expert_ffn",
    )(e1, e2, nused, hs, wg, wu, wd, rw3, wg, wu, wd, rw3)


def _sc_mesh():
    return plsc.VectorSubcoreMesh(core_axis_name="c", subcore_axis_name="s",
                                  num_cores=SC_CORES_V7X, num_subcores=SC_SUBCORES_V7X)


def _sc_steps(n_rows):
    workers = SC_CORES_V7X * SC_SUBCORES_V7X
    per_w = n_rows // workers
    ch = min(SC_ROWS_PER_STEP, per_w // 2)
    n = per_w // ch
    assert per_w * workers == n_rows and n * ch == per_w and n % 2 == 0 and ch % 8 == 0
    return per_w, ch, n


def _sc_scratch(ch, d, dtype):
    return [pltpu.VMEM((2, ch), I32), pltpu.VMEM((2, ch, d), dtype),
            pltpu.SemaphoreType.DMA((2,)), pltpu.SemaphoreType.DMA((2,))]


def _sc_scatter_rows(src, idx, n_out):
    n_in, d = src.shape
    per_w, ch, n = _sc_steps(n_in)

    def body(src_hbm, idx_hbm, out_hbm, idx_v, rows_v, rsem, ssem):
        base = (lax.axis_index("s") * SC_CORES_V7X + lax.axis_index("c")) * per_w

        def read(i, b):
            return pltpu.make_async_copy(src_hbm.at[pl.ds(base + i * ch, ch)], rows_v.at[b], rsem.at[b])

        def scatter(b):
            return pltpu.make_async_copy(rows_v.at[b], out_hbm.at[idx_v.at[b]], ssem.at[b])

        def start_read(i, b):
            pltpu.sync_copy(idx_hbm.at[pl.ds(base + i * ch, ch)], idx_v.at[b])
            read(i, b).start()

        start_read(0, 0)

        @pl.loop(0, n, step=2)
        def _(i):
            for b in (0, 1):
                cur = i + b

                @pl.when(cur + 1 < n)
                def _():
                    @pl.when(cur >= 1)
                    def _():
                        scatter(1 - b).wait()
                    start_read(cur + 1, 1 - b)

                read(cur, b).wait()
                scatter(b).start()

        scatter(0).wait()
        scatter(1).wait()

    return pl.kernel(
        body, mesh=_sc_mesh(),
        out_type=jax.ShapeDtypeStruct((n_out, d), src.dtype),
        scratch_types=_sc_scratch(ch, d, src.dtype),
        name="sc_scatter_rows",
    )(src, idx)


def _sc_gather_rows(table, idx):
    n_out = idx.shape[0]
    d = table.shape[1]
    per_w, ch, n = _sc_steps(n_out)

    def body(table_hbm, idx_hbm, out_hbm, idx_v, rows_v, gsem, wsem):
        base = (lax.axis_index("s") * SC_CORES_V7X + lax.axis_index("c")) * per_w

        def gather(b):
            return pltpu.make_async_copy(table_hbm.at[idx_v.at[b]], rows_v.at[b], gsem.at[b])

        def write(i, b):
            return pltpu.make_async_copy(rows_v.at[b], out_hbm.at[pl.ds(base + i * ch, ch)], wsem.at[b])

        def start_gather(i, b):
            pltpu.sync_copy(idx_hbm.at[pl.ds(base + i * ch, ch)], idx_v.at[b])
            gather(b).start()

        start_gather(0, 0)

        @pl.loop(0, n, step=2)
        def _(i):
            for b in (0, 1):
                cur = i + b

                @pl.when(cur + 1 < n)
                def _():
                    @pl.when(cur >= 1)
                    def _():
                        write(cur - 1, 1 - b).wait()
                    start_gather(cur + 1, 1 - b)

                gather(b).wait()
                write(cur, b).start()

        write(n - 2, 0).wait()
        write(n - 1, 1).wait()

    return pl.kernel(
        body, mesh=_sc_mesh(),
        out_type=jax.ShapeDtypeStruct((n_out, d), table.dtype),
        scratch_types=_sc_scratch(ch, d, table.dtype),
        name="sc_gather_rows",
    )(table, idx)


def _moe(h2p, cls3, rank3, cnt, wg, wu, wd, rw3):
    bsz, s, dh = h2p.shape
    t = bsz * s
    tm = FFN_TM
    cls = cls3[:, 0, :].reshape(t)
    rank = rank3[:, 0, :].reshape(t)
    counts = cnt[:N_CLASSES, 0].astype(I32)
    padded = ((counts + tm - 1) // tm) * tm
    upto = jnp.arange(N_CLASSES)[:, None] >= jnp.arange(N_CLASSES)[None, :]
    ends = jnp.sum(jnp.where(upto, padded[None, :], 0), axis=1)
    base = ends - padded
    pos = jnp.take(base, cls, mode="clip") + rank
    ntiles = t // tm + N_CLASSES
    tile_start = jnp.arange(ntiles, dtype=I32) * tm
    tcls = jnp.minimum(jnp.sum((ends[None, :] <= tile_start[:, None]).astype(I32), axis=1), N_CLASSES - 1)
    nused = (ends[-1:] // tm).astype(I32)
    grp = tcls // len(PAIRS)
    pr = tcls % len(PAIRS)
    first = jnp.array([p[0] for p in PAIRS], I32)
    second = jnp.array([p[1] for p in PAIRS], I32)
    e1 = (grp * PER_GROUP + jnp.take(first, pr, mode="clip")).astype(I32)
    e2 = (grp * PER_GROUP + jnp.take(second, pr, mode="clip")).astype(I32)
    hs = _sc_scatter_rows(h2p.reshape(t, dh), pos, ntiles * tm)
    ys = _ffn(hs, e1, e2, nused, wg, wu, wd, rw3)
    return _sc_gather_rows(ys, pos).reshape(bsz, s, dh)


def _qkv1_kernel(x_ref, y_ref, gfp_ref, g_ref, sc_ref, sh_ref, w_ref, cos_ref, sin_ref,
                 x2_ref, *rest):
    outs = rest[:9]
    hb_scr, ysc, ysc2 = rest[9:]
    tm = x_ref.shape[1]
    x2 = x_ref[0] + gfp_ref[0] * _unpack_bf16_pairs(y_ref[0])
    x2_ref[0] = x2
    hb_scr[...] = _rms_mod(x2, g_ref[...], sc_ref[0], sh_ref[0]).astype(BF16)
    cos = cos_ref[0]
    sin = sin_ref[0]
    lane_lo = (lax.broadcasted_iota(I32, cos.shape, 1) % HEAD_DIM) < (HEAD_DIM // 2)
    for gi, (_, dil) in reversed(list(enumerate(B_GROUPS))):
        for j in range(3):
            c = gi * 3 + j
            y = jnp.dot(hb_scr[...], w_ref[:, B_WIDTH * c:B_WIDTH * (c + 1)], preferred_element_type=F32)
            out = outs[c]
            for cc in range(B_WIDTH // 128):
                lanes = slice(128 * cc, 128 * (cc + 1))
                r = y[:, lanes]
                if j < 2:
                    r = _rope_apply(r, cos, sin, lane_lo)
                if j == 0:
                    r = r * Q_SCALE
                if dil == 1:
                    out[0, 0, :, lanes] = r.astype(BF16)
                    continue
                ysc[cc] = r
                if dil % 16:
                    for rr in range(dil):
                        out[0, rr, :, lanes] = ysc[cc, pl.ds(rr, tm // dil, stride=dil), :].astype(BF16)
                    continue
                quarter = tm // 4
                inner = dil // 4
                for r1 in range(4):
                    ysc2[cc, r1 * quarter:(r1 + 1) * quarter] = ysc[cc, pl.ds(r1, quarter, stride=4), :]
                for r1 in range(4):
                    for r2 in range(inner):
                        out[0, r1 + 4 * r2, :, lanes] = ysc2[
                            cc, pl.ds(r1 * quarter + r2, tm // dil, stride=inner), :].astype(BF16)


def _qkv1(x, y, gfp, g, sc, sh, w, cos, sin):
    bsz, s, d = x.shape
    tm = TM
    tok = lambda w_: pl.BlockSpec((1, tm, w_), lambda b, i: (b, i, 0))
    per_b = pl.BlockSpec((1, 1, d), lambda b, i: (b, 0, 0))
    out_specs = [tok(d)]
    out_shape = [jax.ShapeDtypeStruct((bsz, s, d), F32)]
    for _, dil in B_GROUPS:
        for _ in range(3):
            out_specs.append(pl.BlockSpec((1, dil, tm // dil, B_WIDTH), lambda b, i: (b, 0, i, 0)))
            out_shape.append(jax.ShapeDtypeStruct((bsz, dil, s // dil, B_WIDTH), BF16))
    return pl.pallas_call(
        _qkv1_kernel,
        grid=(bsz, s // tm),
        in_specs=[tok(d), tok(d // 2), per_b, pl.BlockSpec((1, d), lambda b, i: (0, 0)), per_b, per_b,
                  pl.BlockSpec(w.shape, lambda b, i: (0, 0)), tok(128), tok(128)],
        out_specs=out_specs,
        out_shape=out_shape,
        scratch_shapes=[pltpu.VMEM((tm, d), BF16), pltpu.VMEM((B_WIDTH // 128, tm, 128), F32),
                        pltpu.VMEM((B_WIDTH // 128, tm, 128), F32)],
        compiler_params=_cparams("parallel", "parallel"),
        name="qkv1",
    )(x, y, gfp, g.reshape(1, d), sc, sh, w, cos, sin)


def _att1_window_pad(lseg):
    qb = min(2 * B_RADIUS, lseg)
    return -(qb + 2 * B_RADIUS) % 128


def _att1_kernel(*refs, chunk, seq):
    ins = refs[:21]
    o_ref = refs[21]
    scr = refs[22:]
    kfs, vfs, nat_o, nat_d, nat_m = scr[0:3], scr[3:6], scr[6:9], scr[9:12], scr[12:15]
    n = pl.program_id(1)
    rad = B_RADIUS
    npair = B_WIDTH // 128
    for gi, (_, dil) in enumerate(B_GROUPS):
        _, kc, kp, kn, vc, vp, vn = ins[7 * gi:7 * gi + 7]
        lseg = chunk // dil
        for full, (p_, c_, n_) in ((kfs[gi], (kp, kc, kn)), (vfs[gi], (vp, vc, vn))):
            full[:, 0:rad] = p_[0]
            full[:, rad:rad + lseg] = c_[0]
            full[:, rad + lseg:rad + lseg + rad] = n_[0]
            if _att1_window_pad(lseg):
                full[:, rad + lseg + rad:rad + lseg + rad + _att1_window_pad(lseg)] = n_[0]

    for c in range(npair):
        lanes = slice(128 * c, 128 * (c + 1))
        for gi, (_, dil) in enumerate(B_GROUPS):
            q_ref = ins[7 * gi]
            kf, vf = kfs[gi], vfs[gi]
            lseg = chunk // dil
            qb = min(2 * rad, lseg)
            kb_n = qb + 2 * rad + _att1_window_pad(lseg)
            nblk = lseg // qb
            lstr = seq // dil
            rel = lax.broadcasted_iota(I32, (qb, kb_n), 1) - lax.broadcasted_iota(I32, (qb, kb_n), 0)
            band_bias = jnp.where((rel >= 0) & (rel <= 2 * rad), 0.0, NEG)
            jrow = lax.broadcasted_iota(I32, (1, kb_n), 1)
            first_head = lax.broadcasted_iota(I32, (qb, 128), 1) < HEAD_DIM
            zero = jnp.zeros((qb, 128), BF16)
            ones = jnp.ones((kb_n, 128), BF16)
            per_iter = ATT1_TILE_ROWS_PER_ITER // qb

            def tile(t, gi=gi, q_ref=q_ref, kf=kf, vf=vf, lseg=lseg, qb=qb, kb_n=kb_n, nblk=nblk, lstr=lstr,
                     jrow=jrow, band_bias=band_bias, dil=dil, first_head=first_head, zero=zero, ones=ones,
                     lanes=lanes):
                res = t // nblk
                q0 = pl.multiple_of((t % nblk) * qb, qb)
                apos = n * lseg + q0 - rad + jrow
                bias = band_bias + jnp.where((apos >= 0) & (apos < lstr), 0.0, NEG)
                if dil == 1:
                    rows = pl.ds(q0, qb)
                else:
                    rows = pl.ds(res + dil * q0, qb, stride=dil)
                qc = q_ref[0, res, pl.ds(q0, qb), lanes]
                kb = kf[res, pl.ds(q0, kb_n), lanes]
                vb = jnp.concatenate([vf[res, pl.ds(q0, kb_n), lanes], ones], axis=1)
                qs = jnp.concatenate([jnp.where(first_head, qc, zero), jnp.where(first_head, zero, qc)],
                                     axis=0)
                s2 = lax.dot_general(qs, kb, (((1,), (1,)), ((), ())), preferred_element_type=F32)
                ps, ms = [], []
                for t2 in range(2):
                    s = s2[qb * t2:qb * (t2 + 1)] + bias
                    m = jnp.max(s, axis=-1, keepdims=True)
                    ps.append(jnp.exp2(s - m).astype(BF16))
                    ms.append(m)
                pv = jnp.dot(jnp.concatenate(ps, axis=0), vb, preferred_element_type=F32)
                nat_o[gi][rows, :] = jnp.where(first_head, pv[0:qb, 0:128], pv[qb:2 * qb, 0:128])
                nat_d[gi][rows, :] = jnp.where(first_head, pv[0:qb, 128:256], pv[qb:2 * qb, 128:256])
                nat_m[gi][rows, :] = jnp.where(first_head, ms[0], ms[1])

            def body(t, carry, tile=tile, per_iter=per_iter):
                for u in range(per_iter):
                    tile(t * per_iter + u)
                return carry

            lax.fori_loop(0, dil * nblk // per_iter, body, 0)

        ms = [nat_m[gi][...] for gi in range(3)]
        mx = jnp.maximum(jnp.maximum(ms[0], ms[1]), ms[2])
        w = [jnp.exp2(x - mx) for x in ms]
        num = w[0] * nat_o[0][...] + w[1] * nat_o[1][...] + w[2] * nat_o[2][...]
        den = w[0] * nat_d[0][...] + w[1] * nat_d[1][...] + w[2] * nat_d[2][...]
        o_ref[0, :, lanes] = (num / den).astype(BF16)


def _att1(qkv):
    bsz = qkv[0].shape[0]
    seq = qkv[0].shape[1] * qkv[0].shape[2]
    chunk = ATT1_CHUNK
    rad = B_RADIUS
    in_specs, args, scratch = [], [], []
    for gi, (_, dil) in enumerate(B_GROUPS):
        q, k, v = qkv[3 * gi:3 * gi + 3]
        lseg = chunk // dil
        per = lseg // rad
        last = seq // dil // rad - 1
        cur = pl.BlockSpec((1, dil, lseg, B_WIDTH), lambda b, n: (b, 0, n, 0))
        prev = pl.BlockSpec((1, dil, rad, B_WIDTH),
                            lambda b, n, per=per: (b, 0, jnp.maximum(n * per - 1, 0), 0))
        nxt = pl.BlockSpec((1, dil, rad, B_WIDTH),
                           lambda b, n, per=per, last=last: (b, 0, jnp.minimum(n * per + per, last), 0))
        in_specs += [cur, cur, prev, nxt, cur, prev, nxt]
        args += [q, k, k, k, v, v, v]
    for _ in range(2):
        for _, dil in B_GROUPS:
            lseg = chunk // dil
            scratch.append(pltpu.VMEM((dil, lseg + 2 * rad + _att1_window_pad(lseg), B_WIDTH), BF16))
    for _ in range(3):
        for _ in B_GROUPS:
            scratch.append(pltpu.VMEM((chunk, 128), F32))
    return pl.pallas_call(
        functools.partial(_att1_kernel, chunk=chunk, seq=seq),
        grid=(bsz, seq // chunk),
        in_specs=in_specs,
        out_specs=pl.BlockSpec((1, chunk, B_WIDTH), lambda b, n: (b, n, 0)),
        out_shape=jax.ShapeDtypeStruct((bsz, seq, B_WIDTH), BF16),
        scratch_shapes=scratch,
        compiler_params=_cparams("parallel", "parallel"),
        name="att1",
    )(*args)


def _final_kernel(x_ref, y_ref, gfp_ref, g_ref, o_ref):
    x = x_ref[0] + gfp_ref[0] * _unpack_bf16_pairs(y_ref[0])
    ms = jnp.mean(x * x, axis=-1, keepdims=True)
    o_ref[0] = (x * lax.rsqrt(ms + NORM_EPS)) * g_ref[...]


def _final(x, y, gfp, g):
    bsz, s, d = x.shape
    tm = TM
    tok = pl.BlockSpec((1, tm, d), lambda b, i: (b, i, 0))
    return pl.pallas_call(
        _final_kernel,
        grid=(bsz, s // tm),
        in_specs=[tok, pl.BlockSpec((1, tm, d // 2), lambda b, i: (b, i, 0)),
                  pl.BlockSpec((1, 1, d), lambda b, i: (b, 0, 0)),
                  pl.BlockSpec((1, d), lambda b, i: (0, 0))],
        out_specs=tok,
        out_shape=jax.ShapeDtypeStruct((bsz, s, d), F32),
        compiler_params=_cparams("parallel", "parallel"),
        name="final_norm",
    )(x, y, gfp, g.reshape(1, d))


def kernel(x, c, positions, ada_w, ada_b, norm_mix_g, norm_ffn_g, a_w_qkv, a_w_o, a_sink, b_w_qkv, b_w_o,
           router_w, router_bias, exp_w_gate, exp_w_up, exp_w_down, final_norm_g):
    bsz, s, d = x.shape
    assert d == D_MODEL and ada_w.shape[0] == 2 and s % ATT1_CHUNK == 0
    assert all(w_ // (2 * dil) == B_RADIUS for w_, dil in B_GROUPS)

    mod = _modulation(c, ada_w, ada_b)
    mods = [[mod[l][:, k * d:(k + 1) * d].reshape(bsz, 1, d) for k in range(6)] for l in range(2)]
    cos, sin = _rope_tables(positions)

    rw_t = router_w.T.astype(BF16)
    rows = jnp.array([PER_GROUP * g + j if g < N_EXPERT_GROUPS else 0
                      for j in range(PER_GROUP) for g in range(8)], I32)
    live = jnp.array([1.0 if g < N_EXPERT_GROUPS else 0.0
                      for j in range(PER_GROUP) for g in range(8)], F32)
    rw32 = (rw_t[rows].astype(F32) * live[:, None]).astype(BF16)
    rb32 = jnp.where(live > 0, router_bias.astype(F32)[rows], NEG).reshape(32, 1)
    rw3 = rw_t.reshape(N_EXPERTS, 1, d)
    experts = [_cast_expert_weights((exp_w_gate, exp_w_up, exp_w_down), l) for l in range(2)]

    sh_m, sc_m, g_m, sh_f, sc_f, g_f = mods[0]
    q, k, v = _qkv0(x, norm_mix_g[0], sc_m, sh_m, a_w_qkv[0].astype(BF16), cos, sin)
    o = _att0(q, k, v, a_sink[0].astype(F32))
    x1, *routed = _post(o, a_w_o[0].astype(BF16), x, g_m, norm_ffn_g[0], sc_f, sh_f, rw32, rb32)
    y = _moe(*routed, *experts[0], rw3)
    g_f_prev = g_f

    sh_m, sc_m, g_m, sh_f, sc_f, g_f = mods[1]
    outs = _qkv1(x1, y, g_f_prev, norm_mix_g[1], sc_m, sh_m, b_w_qkv[0].astype(BF16), cos, sin)
    x2, qkv = outs[0], outs[1:]
    o = _att1(qkv)
    x3, *routed = _post(o, b_w_o[0].astype(BF16), x2, g_m, norm_ffn_g[1], sc_f, sh_f, rw32, rb32)
    y = _moe(*routed, *experts[1], rw3)

    return _final(x3, y, g_f, final_norm_g)
```

```python
import functools

import jax
import jax.numpy as jnp
from jax import lax
from jax.experimental import pallas as pl
from jax.experimental.pallas import tpu as pltpu
from jax.experimental.pallas import tpu_sc as plsc

F32, BF16, I32, U32 = jnp.float32, jnp.bfloat16, jnp.int32, jnp.uint32

D_MODEL = 1024
HEAD_DIM = 64
ROPE_THETA = 10000.0
NORM_EPS = 1e-6
A_Q_HEADS = 16
A_KV_HEADS = 4
A_GROUP = A_Q_HEADS // A_KV_HEADS
A_RADIUS = 128
B_GROUPS = ((128, 1), (512, 4), (2048, 16))
B_HEADS = 8
B_RADIUS = 64
B_WIDTH = B_HEADS * HEAD_DIM
N_EXPERTS = 16
N_EXPERT_GROUPS = 4
PER_GROUP = N_EXPERTS // N_EXPERT_GROUPS
D_EXPERT = D_MODEL // 2
PAIRS = ((0, 1), (0, 2), (0, 3), (1, 2), (1, 3), (2, 3))
N_CLASSES = N_EXPERT_GROUPS * len(PAIRS)
NEG = -1e30
LOG2E = 1.4426950408889634
Q_SCALE = HEAD_DIM ** -0.5 * LOG2E

VMEM_LIMIT = 56 * 1024 * 1024
TM = 512
ATT1_CHUNK = 1024
ATT1_TILE_ROWS_PER_ITER = 1024
FFN_TM = 256
SC_CORES_V7X = 2
SC_SUBCORES_V7X = 16
SC_ROWS_PER_STEP = 64


def _cparams(*sem):
    return pltpu.CompilerParams(dimension_semantics=sem, vmem_limit_bytes=VMEM_LIMIT)


def _rms_mod(x, g, sc, sh):
    ms = jnp.mean(x * x, axis=-1, keepdims=True)
    return (x * lax.rsqrt(ms + NORM_EPS)) * g * (1.0 + sc) + sh


def _rope_apply(y, cos, sin_signed, lane_lo):
    sw = jnp.where(lane_lo, pltpu.roll(y, 96, 1), pltpu.roll(y, 32, 1))
    return y * cos + sw * sin_signed


def _pack_bf16_pairs(x):
    n = x.shape[1] // 2
    lo = lax.bitcast_convert_type(x[:, :n].astype(BF16).astype(F32), U32)
    hi = lax.bitcast_convert_type(x[:, n:].astype(BF16).astype(F32), U32)
    return (hi & jnp.uint32(0xFFFF0000)) | (lo >> 16)


def _unpack_bf16_pairs(p):
    lo = lax.bitcast_convert_type(p << 16, F32)
    hi = lax.bitcast_convert_type(p & jnp.uint32(0xFFFF0000), F32)
    return jnp.concatenate([lo, hi], axis=1)


def _dup_head(x, first_head, which):
    rolled = pltpu.roll(x, HEAD_DIM, 1)
    return jnp.where(first_head, x, rolled) if which == 0 else jnp.where(first_head, rolled, x)


def _mod_kernel(c_ref, w_ref, b_ref, o_ref):
    c = c_ref[...]
    s = c * jax.nn.sigmoid(c)
    o_ref[0] = jnp.dot(s, w_ref[0], preferred_element_type=F32,
                       precision=lax.Precision.HIGHEST) + b_ref[0]


def _modulation(c, ada_w, ada_b):
    depth, d, n = ada_w.shape
    bsz = c.shape[0]
    cp = jnp.zeros((8, d), F32).at[:bsz].set(c)
    tn = 1536
    out = pl.pallas_call(
        _mod_kernel,
        grid=(depth, n // tn),
        in_specs=[pl.BlockSpec((8, d), lambda l, j: (0, 0)),
                  pl.BlockSpec((1, d, tn), lambda l, j: (l, 0, j)),
                  pl.BlockSpec((1, 1, tn), lambda l, j: (l, 0, j))],
        out_specs=pl.BlockSpec((1, 8, tn), lambda l, j: (l, 0, j)),
        out_shape=jax.ShapeDtypeStruct((depth, 8, n), F32),
        compiler_params=_cparams("arbitrary", "arbitrary"),
        name="modulation",
    )(cp, ada_w, ada_b.reshape(depth, 1, n))
    return out[:, :bsz]


def _rope_kernel(pos_ref, invf_ref, sign_ref, cos_ref, sin_ref):
    ang = pos_ref[0].astype(F32) * invf_ref[...]
    cos_ref[0] = jnp.cos(ang)
    sin_ref[0] = jnp.sin(ang) * sign_ref[...]


def _rope_tables(positions):
    bsz, s = positions.shape
    inv_freq = ROPE_THETA ** (-jnp.arange(0, HEAD_DIM, 2, dtype=F32) / HEAD_DIM)
    invf = jnp.tile(inv_freq, 4).reshape(1, 128)
    half = HEAD_DIM // 2
    sign = jnp.tile(jnp.concatenate([-jnp.ones((half,), F32), jnp.ones((half,), F32)]), 2).reshape(1, 128)
    ts = TM
    return pl.pallas_call(
        _rope_kernel,
        grid=(bsz, s // ts),
        in_specs=[pl.BlockSpec((1, ts, 1), lambda b, i: (b, i, 0)),
                  pl.BlockSpec((1, 128), lambda b, i: (0, 0)),
                  pl.BlockSpec((1, 128), lambda b, i: (0, 0))],
        out_specs=[pl.BlockSpec((1, ts, 128), lambda b, i: (b, i, 0))] * 2,
        out_shape=[jax.ShapeDtypeStruct((bsz, s, 128), F32)] * 2,
        compiler_params=_cparams("parallel", "parallel"),
        name="rope_tables",
    )(positions.reshape(bsz, s, 1), invf, sign)


def _qkv0_kernel(x_ref, g_ref, sc_ref, sh_ref, w_ref, cos_ref, sin_ref,
                 q_ref, k_ref, v_ref, hb_scr):
    hb_scr[...] = _rms_mod(x_ref[0], g_ref[...], sc_ref[0], sh_ref[0]).astype(BF16)
    cos = cos_ref[0]
    sin = sin_ref[0]
    lane_lo = (lax.broadcasted_iota(I32, cos.shape, 1) % HEAD_DIM) < (HEAD_DIM // 2)
    nq = A_Q_HEADS * HEAD_DIM
    nkv = A_KV_HEADS * HEAD_DIM
    for c in range(nq // 256):
        y = jnp.dot(hb_scr[...], w_ref[:, 256 * c:256 * (c + 1)], preferred_element_type=F32)
        for cc in range(2):
            r = _rope_apply(y[:, 128 * cc:128 * (cc + 1)], cos, sin, lane_lo) * Q_SCALE
            q_ref[0, :, 256 * c + 128 * cc:256 * c + 128 * (cc + 1)] = r.astype(BF16)
    first_head = lax.broadcasted_iota(I32, cos.shape, 1) < HEAD_DIM
    y = jnp.dot(hb_scr[...], w_ref[:, nq:nq + nkv], preferred_element_type=F32)
    for cc in range(nkv // 128):
        r = _rope_apply(y[:, 128 * cc:128 * (cc + 1)], cos, sin, lane_lo)
        for which in range(2):
            g = 2 * cc + which
            k_ref[0, :, 128 * g:128 * (g + 1)] = _dup_head(r, first_head, which).astype(BF16)
    y = jnp.dot(hb_scr[...], w_ref[:, nq + nkv:nq + 2 * nkv], preferred_element_type=F32)
    for cc in range(nkv // 128):
        r = y[:, 128 * cc:128 * (cc + 1)]
        for which in range(2):
            g = 2 * cc + which
            v_ref[0, :, 128 * g:128 * (g + 1)] = _dup_head(r, first_head, which).astype(BF16)


def _qkv0(x, g, sc, sh, w, cos, sin):
    bsz, s, d = x.shape
    nq = A_Q_HEADS * HEAD_DIM
    nkv = A_KV_HEADS * HEAD_DIM
    tm = TM
    tok = lambda w_: pl.BlockSpec((1, tm, w_), lambda b, i: (b, i, 0))
    per_b = pl.BlockSpec((1, 1, d), lambda b, i: (b, 0, 0))
    return pl.pallas_call(
        _qkv0_kernel,
        grid=(bsz, s // tm),
        in_specs=[tok(d), pl.BlockSpec((1, d), lambda b, i: (0, 0)), per_b, per_b,
                  pl.BlockSpec(w.shape, lambda b, i: (0, 0)), tok(128), tok(128)],
        out_specs=[tok(nq), tok(2 * nkv), tok(2 * nkv)],
        out_shape=[jax.ShapeDtypeStruct((bsz, s, nq), BF16),
                   jax.ShapeDtypeStruct((bsz, s, 2 * nkv), BF16),
                   jax.ShapeDtypeStruct((bsz, s, 2 * nkv), BF16)],
        scratch_shapes=[pltpu.VMEM((tm, d), BF16)],
        compiler_params=_cparams("parallel", "parallel"),
        name="qkv0",
    )(x, g.reshape(1, d), sc, sh, w, cos, sin)


def _side_cast_specs(ws, layer, grid):
    nsteps = grid[0] * grid[1]
    in_specs, out_specs, out_shape, args = [], [], [], []
    for w in ws:
        n_exp, k, n = w.shape[1:]
        rows = n_exp * k // nsteps
        assert rows * nsteps == n_exp * k and rows % 16 == 0
        in_specs.append(pl.BlockSpec((rows, n), lambda b, i: (layer * nsteps + b * grid[1] + i, 0)))
        out_specs.append(pl.BlockSpec((rows, n), lambda b, i: (b * grid[1] + i, 0)))
        out_shape.append(jax.ShapeDtypeStruct((n_exp * k, n), BF16))
        args.append(w.reshape(w.shape[0] * n_exp * k, n))
    return in_specs, out_specs, out_shape, args


def _side_cast(srcs, dsts):
    for src, dst in zip(srcs, dsts):
        dst[...] = src[...].astype(BF16)


def _att0_kernel(sink_ref, q_ref, kc_ref, kp_ref, kn_ref, vc_ref, vp_ref, vn_ref, wg_ref, wu_ref, wd_ref,
                 o_ref, wgo_ref, wuo_ref, wdo_ref, kf, vf, *, tq, seq):
    _side_cast((wg_ref, wu_ref, wd_ref), (wgo_ref, wuo_ref, wdo_ref))
    i = pl.program_id(1)
    r = A_RADIUS
    ones = jnp.ones((tq + 2 * r, 128), BF16)
    for g in range(A_KV_HEADS):
        lanes = slice(128 * g, 128 * (g + 1))
        kf[g, 0:r] = kp_ref[0, :, lanes]
        kf[g, r:r + tq] = kc_ref[0, :, lanes]
        kf[g, r + tq:r + tq + r] = kn_ref[0, :, lanes]
        vf[g, 0:r, 0:128] = vp_ref[0, :, lanes]
        vf[g, r:r + tq, 0:128] = vc_ref[0, :, lanes]
        vf[g, r + tq:r + tq + r, 0:128] = vn_ref[0, :, lanes]
        vf[g, :, 128:256] = ones

    ii = lax.broadcasted_iota(I32, (r, r), 0)
    jj = lax.broadcasted_iota(I32, (r, r), 1)
    first_head = jj < HEAD_DIM
    zero = jnp.zeros((r, 128), BF16)

    for j in range(tq // r):
        q0 = j * r
        base = i * tq + q0
        bias_lo = jnp.where((jj >= ii) & (base - r + jj >= 0), 0.0, NEG)
        bias_hi = jnp.where((jj <= ii) & (base + r + jj < seq), 0.0, NEG)
        for g in range(A_KV_HEADS):
            kb = kf[g, pl.ds(q0, 3 * r), :]
            vb = vf[g, pl.ds(q0, 3 * r), :]
            qs = []
            for cc in range(2):
                c = 2 * g + cc
                qc = q_ref[0, pl.ds(q0, r), 128 * c:128 * (c + 1)]
                qs += [jnp.where(first_head, qc, zero), jnp.where(first_head, zero, qc)]
            s4 = lax.dot_general(jnp.concatenate(qs, axis=0), kb, (((1,), (1,)), ((), ())),
                                 preferred_element_type=F32)
            ps, ms = [], []
            for t in range(A_GROUP):
                s = s4[r * t:r * (t + 1)]
                a0 = s[:, 0:r] + bias_lo
                a1 = s[:, r:2 * r]
                a2 = s[:, 2 * r:3 * r] + bias_hi
                sink = sink_ref[A_GROUP * g + t] * LOG2E
                m = jnp.max(jnp.maximum(jnp.maximum(a0, a1), a2), axis=-1, keepdims=True)
                m = jnp.maximum(m, sink)
                ps.append(jnp.concatenate([jnp.exp2(a0 - m), jnp.exp2(a1 - m), jnp.exp2(a2 - m)],
                                          axis=1).astype(BF16))
                ms.append(m)
            pv = jnp.dot(jnp.concatenate(ps, axis=0), vb, preferred_element_type=F32)
            os_ = []
            for t in range(A_GROUP):
                sink = sink_ref[A_GROUP * g + t] * LOG2E
                den = pv[r * t:r * (t + 1), 128:256] + jnp.exp2(sink - ms[t])
                os_.append(pv[r * t:r * (t + 1), 0:128] / den)
            for cc in range(2):
                c = 2 * g + cc
                o_ref[0, pl.ds(q0, r), 128 * c:128 * (c + 1)] = jnp.where(
                    first_head, os_[2 * cc], os_[2 * cc + 1]).astype(BF16)


def _att0(q, k2, v2, sink, expert_ws, layer):
    bsz, s, nq = q.shape
    nkv2 = k2.shape[-1]
    tq = TM
    r = A_RADIUS
    per = tq // r
    last = s // r - 1
    grid = (bsz, s // tq)
    cur = lambda w_: pl.BlockSpec((1, tq, w_), lambda b, i: (b, i, 0))
    prev = pl.BlockSpec((1, r, nkv2), lambda b, i: (b, jnp.maximum(i * per - 1, 0), 0))
    nxt = pl.BlockSpec((1, r, nkv2), lambda b, i: (b, jnp.minimum(i * per + per, last), 0))
    w_in, w_out, w_shape, w_args = _side_cast_specs(expert_ws, layer, grid)
    o, *cast = pl.pallas_call(
        functools.partial(_att0_kernel, tq=tq, seq=s),
        grid=grid,
        in_specs=[pl.BlockSpec(memory_space=pltpu.SMEM),
                  cur(nq), cur(nkv2), prev, nxt, cur(nkv2), prev, nxt] + w_in,
        out_specs=[cur(nq)] + w_out,
        out_shape=[jax.ShapeDtypeStruct((bsz, s, nq), BF16)] + w_shape,
        scratch_shapes=[pltpu.VMEM((A_KV_HEADS, tq + 2 * r, 128), BF16),
                        pltpu.VMEM((A_KV_HEADS, tq + 2 * r, 256), BF16)],
        compiler_params=_cparams("parallel", "parallel"),
        name="att0",
    )(sink, q, k2, k2, k2, v2, v2, v2, *w_args)
    return o, [c.reshape(w.shape[1:]) for c, w in zip(cast, expert_ws)]


def _post_kernel(o_ref, wo_ref, x_ref, gm_ref, gf_ref, scf_ref, shf_ref, rw_ref, rb_ref, upper_ref,
                 x1_ref, h2_ref, cls_ref, rank_ref, cnt_ref, carry):
    mix = jnp.dot(o_ref[0], wo_ref[...], preferred_element_type=F32)
    x1 = x_ref[0] + gm_ref[0] * mix
    x1_ref[0] = x1
    h2 = _rms_mod(x1, gf_ref[...], scf_ref[0], shf_ref[0]).astype(BF16)
    h2_ref[0] = _pack_bf16_pairs(h2.astype(F32))
    tq = h2.shape[0]
    logits = lax.dot_general(rw_ref[...], h2, (((1,), (1,)), ((), ())), preferred_element_type=F32)
    scores = jax.nn.sigmoid(logits)
    biased = scores + rb_ref[...]
    pj = [biased[8 * j:8 * (j + 1)] for j in range(PER_GROUP)]
    sel = []
    for j in range(PER_GROUP):
        beaten = jnp.zeros((8, tq), F32)
        for j2 in range(PER_GROUP):
            if j2 == j:
                continue
            beats = (pj[j2] > pj[j]) | (pj[j2] == pj[j]) if j2 < j else (pj[j2] > pj[j])
            beaten = beaten + jnp.where(beats, 1.0, 0.0)
        sel.append(beaten < 2.0)
    gscore = jnp.zeros((8, tq), F32)
    for j in range(PER_GROUP):
        gscore = gscore + jnp.where(sel[j], pj[j], 0.0)
    gi = lax.broadcasted_iota(I32, (8, tq), 0).astype(F32)
    gmax = jnp.max(gscore, axis=0, keepdims=True)
    gidx = jnp.min(jnp.where(gscore == gmax, gi, 8.0), axis=0, keepdims=True)
    onehot = gi == gidx
    f = [jnp.max(jnp.where(onehot & sel[j], 1.0, 0.0), axis=0, keepdims=True) > 0.5
         for j in range(PER_GROUP)]
    pair = jnp.where(f[0], jnp.where(f[1], 0.0, jnp.where(f[2], 1.0, 2.0)),
                     jnp.where(f[1], jnp.where(f[2], 3.0, 4.0), 5.0))
    cls = (gidx * float(len(PAIRS)) + pair).astype(I32)
    cls_ref[0] = jnp.broadcast_to(cls, (8, tq))

    @pl.when((pl.program_id(0) == 0) & (pl.program_id(1) == 0))
    def _():
        carry[...] = jnp.zeros_like(carry)

    oh = lax.broadcasted_iota(I32, (32, tq), 0) == cls
    ohf = jnp.where(oh, 1.0, 0.0)
    within = jnp.dot(ohf.astype(BF16), upper_ref[...], preferred_element_type=F32)
    c = carry[...]
    tot = jnp.sum(jnp.where(oh, within + c[:, 0:1], 0.0), axis=0, keepdims=True)
    rank_ref[0] = jnp.broadcast_to(tot.astype(I32), (8, tq))
    newc = c + jnp.sum(ohf, axis=1, keepdims=True)
    carry[...] = newc
    cnt_ref[...] = newc


def _post(o, wo, x, gm, gf, scf, shf, rw32, rb32):
    bsz, s, d = x.shape
    do = o.shape[-1]
    tq = TM
    nt = s // tq
    tok = lambda w_: pl.BlockSpec((1, tq, w_), lambda b, i: (b, i, 0))
    per_b = pl.BlockSpec((1, 1, d), lambda b, i: (b, 0, 0))
    full = lambda a: pl.BlockSpec(a.shape, lambda b, i: (0,) * a.ndim)
    per_tile = pl.BlockSpec((1, 8, tq), lambda b, i: (b * nt + i, 0, 0))
    gf2 = gf.reshape(1, d)
    upper = (jnp.arange(tq)[:, None] < jnp.arange(tq)[None, :]).astype(BF16)
    return pl.pallas_call(
        _post_kernel,
        grid=(bsz, nt),
        in_specs=[tok(do), full(wo), tok(d), per_b, full(gf2), per_b, per_b, full(rw32), full(rb32),
                  full(upper)],
        out_specs=[tok(d), tok(d // 2), per_tile, per_tile, pl.BlockSpec((32, 128), lambda b, i: (0, 0))],
        out_shape=[jax.ShapeDtypeStruct((bsz, s, d), F32),
                   jax.ShapeDtypeStruct((bsz, s, d // 2), U32),
                   jax.ShapeDtypeStruct((bsz * nt, 8, tq), I32),
                   jax.ShapeDtypeStruct((bsz * nt, 8, tq), I32),
                   jax.ShapeDtypeStruct((32, 128), F32)],
        scratch_shapes=[pltpu.VMEM((32, 128), F32)],
        compiler_params=_cparams("arbitrary", "arbitrary"),
        name="post_attention",
    )(o, wo, x, gm, gf2, scf, shf, rw32, rb32, upper)


def _ffn_kernel(e1_ref, e2_ref, nused_ref, hs_ref, wg1, wu1, wd1, rw1, wg2, wu2, wd2, rw2, ys_ref):
    j = pl.program_id(0)

    @pl.when(j < nused_ref[0])
    def _():
        xf = _unpack_bf16_pairs(hs_ref[...])
        x = xf.astype(BF16)
        s1 = jax.nn.sigmoid(jnp.sum(xf * rw1[0].astype(F32), axis=-1, keepdims=True))
        s2 = jax.nn.sigmoid(jnp.sum(xf * rw2[0].astype(F32), axis=-1, keepdims=True))
        den = s1 + s2

        def expert(wg, wu, wd):
            gt = jnp.dot(x, wg[0], preferred_element_type=F32)
            up = jnp.dot(x, wu[0], preferred_element_type=F32)
            a = (gt * jax.nn.sigmoid(gt)) * up
            return jnp.dot(a.astype(BF16), wd[0], preferred_element_type=F32)

        y = (s1 / den) * expert(wg1, wu1, wd1) + (s2 / den) * expert(wg2, wu2, wd2)
        ys_ref[...] = _pack_bf16_pairs(y)

    @pl.when(j >= nused_ref[0])
    def _():
        ys_ref[...] = jnp.zeros_like(ys_ref)


def _ffn(hs, e1, e2, nused, wg, wu, wd, rw3):
    npad, dh = hs.shape
    d = 2 * dh
    tm = FFN_TM
    de = wg.shape[-1]
    row = pl.BlockSpec((tm, dh), lambda j, e1, e2, nu: (jnp.minimum(j, nu[0] - 1), 0))
    first = lambda shape: pl.BlockSpec((1,) + shape, lambda j, e1, e2, nu: (e1[j], 0, 0))
    second = lambda shape: pl.BlockSpec((1,) + shape, lambda j, e1, e2, nu: (e2[j], 0, 0))
    router1 = pl.BlockSpec((1, 1, d), lambda j, e1, e2, nu: (e1[j], 0, 0))
    router2 = pl.BlockSpec((1, 1, d), lambda j, e1, e2, nu: (e2[j], 0, 0))
    grid_spec = pltpu.PrefetchScalarGridSpec(
        num_scalar_prefetch=3,
        grid=(npad // tm,),
        in_specs=[row,
                  first((d, de)), first((d, de)), first((de, d)), router1,
                  second((d, de)), second((d, de)), second((de, d)), router2],
        out_specs=pl.BlockSpec((tm, dh), lambda j, e1, e2, nu: (j, 0)),
    )
    return pl.pallas_call(
        _ffn_kernel,
        grid_spec=grid_spec,
        out_shape=jax.ShapeDtypeStruct((npad, dh), U32),
        compiler_params=_cparams("arbitrary"),
        name="expert_ffn",
    )(e1, e2, nused, hs, wg, wu, wd, rw3, wg, wu, wd, rw3)


def _sc_mesh():
    return plsc.VectorSubcoreMesh(core_axis_name="c", subcore_axis_name="s",
                                  num_cores=SC_CORES_V7X, num_subcores=SC_SUBCORES_V7X)


def _sc_steps(n_rows):
    workers = SC_CORES_V7X * SC_SUBCORES_V7X
    per_w = n_rows // workers
    ch = min(SC_ROWS_PER_STEP, per_w // 2)
    n = per_w // ch
    assert per_w * workers == n_rows and n * ch == per_w and n % 2 == 0 and ch % 8 == 0
    return per_w, ch, n


def _sc_scratch(ch, d, dtype):
    return [pltpu.VMEM((2, ch), I32), pltpu.VMEM((2, ch, d), dtype),
            pltpu.SemaphoreType.DMA((2,)), pltpu.SemaphoreType.DMA((2,))]


def _sc_scatter_rows(src, idx, n_out):
    n_in, d = src.shape
    per_w, ch, n = _sc_steps(n_in)

    def body(src_hbm, idx_hbm, out_hbm, idx_v, rows_v, rsem, ssem):
        base = (lax.axis_index("s") * SC_CORES_V7X + lax.axis_index("c")) * per_w

        def read(i, b):
            return pltpu.make_async_copy(src_hbm.at[pl.ds(base + i * ch, ch)], rows_v.at[b], rsem.at[b])

        def scatter(b):
            return pltpu.make_async_copy(rows_v.at[b], out_hbm.at[idx_v.at[b]], ssem.at[b])

        def start_read(i, b):
            pltpu.sync_copy(idx_hbm.at[pl.ds(base + i * ch, ch)], idx_v.at[b])
            read(i, b).start()

        start_read(0, 0)

        @pl.loop(0, n, step=2)
        def _(i):
            for b in (0, 1):
                cur = i + b

                @pl.when(cur + 1 < n)
                def _():
                    @pl.when(cur >= 1)
                    def _():
                        scatter(1 - b).wait()
                    start_read(cur + 1, 1 - b)

                read(cur, b).wait()
                scatter(b).start()

        scatter(0).wait()
        scatter(1).wait()

    return pl.kernel(
        body, mesh=_sc_mesh(),
        out_type=jax.ShapeDtypeStruct((n_out, d), src.dtype),
        scratch_types=_sc_scratch(ch, d, src.dtype),
        name="sc_scatter_rows",
    )(src, idx)


def _sc_gather_rows(table, idx):
    n_out = idx.shape[0]
    d = table.shape[1]
    per_w, ch, n = _sc_steps(n_out)

    def body(table_hbm, idx_hbm, out_hbm, idx_v, rows_v, gsem, wsem):
        base = (lax.axis_index("s") * SC_CORES_V7X + lax.axis_index("c")) * per_w

        def gather(b):
            return pltpu.make_async_copy(table_hbm.at[idx_v.at[b]], rows_v.at[b], gsem.at[b])

        def write(i, b):
            return pltpu.make_async_copy(rows_v.at[b], out_hbm.at[pl.ds(base + i * ch, ch)], wsem.at[b])

        def start_gather(i, b):
            pltpu.sync_copy(idx_hbm.at[pl.ds(base + i * ch, ch)], idx_v.at[b])
            gather(b).start()

        start_gather(0, 0)

        @pl.loop(0, n, step=2)
        def _(i):
            for b in (0, 1):
                cur = i + b

                @pl.when(cur + 1 < n)
                def _():
                    @pl.when(cur >= 1)
                    def _():
                        write(cur - 1, 1 - b).wait()
                    start_gather(cur + 1, 1 - b)

                gather(b).wait()
                write(cur, b).start()

        write(n - 2, 0).wait()
        write(n - 1, 1).wait()

    return pl.kernel(
        body, mesh=_sc_mesh(),
        out_type=jax.ShapeDtypeStruct((n_out, d), table.dtype),
        scratch_types=_sc_scratch(ch, d, table.dtype),
        name="sc_gather_rows",
    )(table, idx)


def _moe(h2p, cls3, rank3, cnt, wg, wu, wd, rw3):
    bsz, s, dh = h2p.shape
    t = bsz * s
    tm = FFN_TM
    cls = cls3[:, 0, :].reshape(t)
    rank = rank3[:, 0, :].reshape(t)
    counts = cnt[:N_CLASSES, 0].astype(I32)
    padded = ((counts + tm - 1) // tm) * tm
    upto = jnp.arange(N_CLASSES)[:, None] >= jnp.arange(N_CLASSES)[None, :]
    ends = jnp.sum(jnp.where(upto, padded[None, :], 0), axis=1)
    base = ends - padded
    pos = jnp.take(base, cls, mode="clip") + rank
    ntiles = t // tm + N_CLASSES
    tile_start = jnp.arange(ntiles, dtype=I32) * tm
    tcls = jnp.minimum(jnp.sum((ends[None, :] <= tile_start[:, None]).astype(I32), axis=1), N_CLASSES - 1)
    nused = (ends[-1:] // tm).astype(I32)
    grp = tcls // len(PAIRS)
    pr = tcls % len(PAIRS)
    first = jnp.array([p[0] for p in PAIRS], I32)
    second = jnp.array([p[1] for p in PAIRS], I32)
    e1 = (grp * PER_GROUP + jnp.take(first, pr, mode="clip")).astype(I32)
    e2 = (grp * PER_GROUP + jnp.take(second, pr, mode="clip")).astype(I32)
    hs = _sc_scatter_rows(h2p.reshape(t, dh), pos, ntiles * tm)
    ys = _ffn(hs, e1, e2, nused, wg, wu, wd, rw3)
    return _sc_gather_rows(ys, pos).reshape(bsz, s, dh)


def _qkv1_kernel(x_ref, y_ref, gfp_ref, g_ref, sc_ref, sh_ref, w_ref, cos_ref, sin_ref,
                 wg_ref, wu_ref, wd_ref, x2_ref, *rest):
    outs = rest[:9]
    _side_cast((wg_ref, wu_ref, wd_ref), rest[9:12])
    hb_scr, ysc, ysc2 = rest[12:]
    tm = x_ref.shape[1]
    x2 = x_ref[0] + gfp_ref[0] * _unpack_bf16_pairs(y_ref[0])
    x2_ref[0] = x2
    hb_scr[...] = _rms_mod(x2, g_ref[...], sc_ref[0], sh_ref[0]).astype(BF16)
    cos = cos_ref[0]
    sin = sin_ref[0]
    lane_lo = (lax.broadcasted_iota(I32, cos.shape, 1) % HEAD_DIM) < (HEAD_DIM // 2)
    for gi, (_, dil) in reversed(list(enumerate(B_GROUPS))):
        for j in range(3):
            c = gi * 3 + j
            y = jnp.dot(hb_scr[...], w_ref[:, B_WIDTH * c:B_WIDTH * (c + 1)], preferred_element_type=F32)
            out = outs[c]
            for cc in range(B_WIDTH // 128):
                lanes = slice(128 * cc, 128 * (cc + 1))
                r = y[:, lanes]
                if j < 2:
                    r = _rope_apply(r, cos, sin, lane_lo)
                if j == 0:
                    r = r * Q_SCALE
                if dil == 1:
                    out[0, 0, :, lanes] = r.astype(BF16)
                    continue
                ysc[cc] = r
                if dil % 16:
                    for rr in range(dil):
                        out[0, rr, :, lanes] = ysc[cc, pl.ds(rr, tm // dil, stride=dil), :].astype(BF16)
                    continue
                quarter = tm // 4
                inner = dil // 4
                for r1 in range(4):
                    ysc2[cc, r1 * quarter:(r1 + 1) * quarter] = ysc[cc, pl.ds(r1, quarter, stride=4), :]
                for r1 in range(4):
                    for r2 in range(inner):
                        out[0, r1 + 4 * r2, :, lanes] = ysc2[
                            cc, pl.ds(r1 * quarter + r2, tm // dil, stride=inner), :].astype(BF16)


def _qkv1(x, y, gfp, g, sc, sh, w, cos, sin, expert_ws, layer):
    bsz, s, d = x.shape
    tm = TM
    grid = (bsz, s // tm)
    w_in, w_out, w_shape, w_args = _side_cast_specs(expert_ws, layer, grid)
    tok = lambda w_: pl.BlockSpec((1, tm, w_), lambda b, i: (b, i, 0))
    per_b = pl.BlockSpec((1, 1, d), lambda b, i: (b, 0, 0))
    out_specs = [tok(d)]
    out_shape = [jax.ShapeDtypeStruct((bsz, s, d), F32)]
    for _, dil in B_GROUPS:
        for _ in range(3):
            out_specs.append(pl.BlockSpec((1, dil, tm // dil, B_WIDTH), lambda b, i: (b, 0, i, 0)))
            out_shape.append(jax.ShapeDtypeStruct((bsz, dil, s // dil, B_WIDTH), BF16))
    outs = pl.pallas_call(
        _qkv1_kernel,
        grid=grid,
        in_specs=[tok(d), tok(d // 2), per_b, pl.BlockSpec((1, d), lambda b, i: (0, 0)), per_b, per_b,
                  pl.BlockSpec(w.shape, lambda b, i: (0, 0)), tok(128), tok(128)] + w_in,
        out_specs=out_specs + w_out,
        out_shape=out_shape + w_shape,
        scratch_shapes=[pltpu.VMEM((tm, d), BF16), pltpu.VMEM((B_WIDTH // 128, tm, 128), F32),
                        pltpu.VMEM((B_WIDTH // 128, tm, 128), F32)],
        compiler_params=_cparams("parallel", "parallel"),
        name="qkv1",
    )(x, y, gfp, g.reshape(1, d), sc, sh, w, cos, sin, *w_args)
    return outs[:10], [c.reshape(w_.shape[1:]) for c, w_ in zip(outs[10:], expert_ws)]


def _att1_window_pad(lseg):
    qb = min(2 * B_RADIUS, lseg)
    return -(qb + 2 * B_RADIUS) % 128


def _att1_kernel(*refs, chunk, seq):
    ins = refs[:21]
    o_ref = refs[21]
    scr = refs[22:]
    kfs, vfs, nat_o, nat_d, nat_m = scr[0:3], scr[3:6], scr[6:9], scr[9:12], scr[12:15]
    n = pl.program_id(1)
    rad = B_RADIUS
    npair = B_WIDTH // 128
    for gi, (_, dil) in enumerate(B_GROUPS):
        _, kc, kp, kn, vc, vp, vn = ins[7 * gi:7 * gi + 7]
        lseg = chunk // dil
        for full, (p_, c_, n_) in ((kfs[gi], (kp, kc, kn)), (vfs[gi], (vp, vc, vn))):
            full[:, 0:rad] = p_[0]
            full[:, rad:rad + lseg] = c_[0]
            full[:, rad + lseg:rad + lseg + rad] = n_[0]
            if _att1_window_pad(lseg):
                full[:, rad + lseg + rad:rad + lseg + rad + _att1_window_pad(lseg)] = n_[0]

    for c in range(npair):
        lanes = slice(128 * c, 128 * (c + 1))
        for gi, (_, dil) in enumerate(B_GROUPS):
            q_ref = ins[7 * gi]
            kf, vf = kfs[gi], vfs[gi]
            lseg = chunk // dil
            qb = min(2 * rad, lseg)
            kb_n = qb + 2 * rad + _att1_window_pad(lseg)
            nblk = lseg // qb
            lstr = seq // dil
            rel = lax.broadcasted_iota(I32, (qb, kb_n), 1) - lax.broadcasted_iota(I32, (qb, kb_n), 0)
            band_bias = jnp.where((rel >= 0) & (rel <= 2 * rad), 0.0, NEG)
            jrow = lax.broadcasted_iota(I32, (1, kb_n), 1)
            first_head = lax.broadcasted_iota(I32, (qb, 128), 1) < HEAD_DIM
            zero = jnp.zeros((qb, 128), BF16)
            ones = jnp.ones((kb_n, 128), BF16)
            per_iter = ATT1_TILE_ROWS_PER_ITER // qb

            def tile(t, gi=gi, q_ref=q_ref, kf=kf, vf=vf, lseg=lseg, qb=qb, kb_n=kb_n, nblk=nblk, lstr=lstr,
                     jrow=jrow, band_bias=band_bias, dil=dil, first_head=first_head, zero=zero, ones=ones,
                     lanes=lanes):
                res = t // nblk
                q0 = pl.multiple_of((t % nblk) * qb, qb)
                apos = n * lseg + q0 - rad + jrow
                bias = band_bias + jnp.where((apos >= 0) & (apos < lstr), 0.0, NEG)
                if dil == 1:
                    rows = pl.ds(q0, qb)
                else:
                    rows = pl.ds(res + dil * q0, qb, stride=dil)
                qc = q_ref[0, res, pl.ds(q0, qb), lanes]
                kb = kf[res, pl.ds(q0, kb_n), lanes]
                vb = jnp.concatenate([vf[res, pl.ds(q0, kb_n), lanes], ones], axis=1)
                qs = jnp.concatenate([jnp.where(first_head, qc, zero), jnp.where(first_head, zero, qc)],
                                     axis=0)
                s2 = lax.dot_general(qs, kb, (((1,), (1,)), ((), ())), preferred_element_type=F32)
                ps, ms = [], []
                for t2 in range(2):
                    s = s2[qb * t2:qb * (t2 + 1)] + bias
                    m = jnp.max(s, axis=-1, keepdims=True)
                    ps.append(jnp.exp2(s - m).astype(BF16))
                    ms.append(m)
                pv = jnp.dot(jnp.concatenate(ps, axis=0), vb, preferred_element_type=F32)
                nat_o[gi][rows, :] = jnp.where(first_head, pv[0:qb, 0:128], pv[qb:2 * qb, 0:128])
                nat_d[gi][rows, :] = jnp.where(first_head, pv[0:qb, 128:256], pv[qb:2 * qb, 128:256])
                nat_m[gi][rows, :] = jnp.where(first_head, ms[0], ms[1])

            def body(t, carry, tile=tile, per_iter=per_iter):
                for u in range(per_iter):
                    tile(t * per_iter + u)
                return carry

            lax.fori_loop(0, dil * nblk // per_iter, body, 0)

        ms = [nat_m[gi][...] for gi in range(3)]
        mx = jnp.maximum(jnp.maximum(ms[0], ms[1]), ms[2])
        w = [jnp.exp2(x - mx) for x in ms]
        num = w[0] * nat_o[0][...] + w[1] * nat_o[1][...] + w[2] * nat_o[2][...]
        den = w[0] * nat_d[0][...] + w[1] * nat_d[1][...] + w[2] * nat_d[2][...]
        o_ref[0, :, lanes] = (num / den).astype(BF16)


def _att1(qkv):
    bsz = qkv[0].shape[0]
    seq = qkv[0].shape[1] * qkv[0].shape[2]
    chunk = ATT1_CHUNK
    rad = B_RADIUS
    in_specs, args, scratch = [], [], []
    for gi, (_, dil) in enumerate(B_GROUPS):
        q, k, v = qkv[3 * gi:3 * gi + 3]
        lseg = chunk // dil
        per = lseg // rad
        last = seq // dil // rad - 1
        cur = pl.BlockSpec((1, dil, lseg, B_WIDTH), lambda b, n: (b, 0, n, 0))
        prev = pl.BlockSpec((1, dil, rad, B_WIDTH),
                            lambda b, n, per=per: (b, 0, jnp.maximum(n * per - 1, 0), 0))
        nxt = pl.BlockSpec((1, dil, rad, B_WIDTH),
                           lambda b, n, per=per, last=last: (b, 0, jnp.minimum(n * per + per, last), 0))
        in_specs += [cur, cur, prev, nxt, cur, prev, nxt]
        args += [q, k, k, k, v, v, v]
    for _ in range(2):
        for _, dil in B_GROUPS:
            lseg = chunk // dil
            scratch.append(pltpu.VMEM((dil, lseg + 2 * rad + _att1_window_pad(lseg), B_WIDTH), BF16))
    for _ in range(3):
        for _ in B_GROUPS:
            scratch.append(pltpu.VMEM((chunk, 128), F32))
    return pl.pallas_call(
        functools.partial(_att1_kernel, chunk=chunk, seq=seq),
        grid=(bsz, seq // chunk),
        in_specs=in_specs,
        out_specs=pl.BlockSpec((1, chunk, B_WIDTH), lambda b, n: (b, n, 0)),
        out_shape=jax.ShapeDtypeStruct((bsz, seq, B_WIDTH), BF16),
        scratch_shapes=scratch,
        compiler_params=_cparams("parallel", "parallel"),
        name="att1",
    )(*args)


def _final_kernel(x_ref, y_ref, gfp_ref, g_ref, o_ref):
    x = x_ref[0] + gfp_ref[0] * _unpack_bf16_pairs(y_ref[0])
    ms = jnp.mean(x * x, axis=-1, keepdims=True)
    o_ref[0] = (x * lax.rsqrt(ms + NORM_EPS)) * g_ref[...]


def _final(x, y, gfp, g):
    bsz, s, d = x.shape
    tm = TM
    tok = pl.BlockSpec((1, tm, d), lambda b, i: (b, i, 0))
    return pl.pallas_call(
        _final_kernel,
        grid=(bsz, s // tm),
        in_specs=[tok, pl.BlockSpec((1, tm, d // 2), lambda b, i: (b, i, 0)),
                  pl.BlockSpec((1, 1, d), lambda b, i: (b, 0, 0)),
                  pl.BlockSpec((1, d), lambda b, i: (0, 0))],
        out_specs=tok,
        out_shape=jax.ShapeDtypeStruct((bsz, s, d), F32),
        compiler_params=_cparams("parallel", "parallel"),
        name="final_norm",
    )(x, y, gfp, g.reshape(1, d))


def kernel(x, c, positions, ada_w, ada_b, norm_mix_g, norm_ffn_g, a_w_qkv, a_w_o, a_sink, b_w_qkv, b_w_o,
           router_w, router_bias, exp_w_gate, exp_w_up, exp_w_down, final_norm_g):
    bsz, s, d = x.shape
    assert d == D_MODEL and ada_w.shape[0] == 2 and s % ATT1_CHUNK == 0
    assert all(w_ // (2 * dil) == B_RADIUS for w_, dil in B_GROUPS)

    mod = _modulation(c, ada_w, ada_b)
    mods = [[mod[l][:, k * d:(k + 1) * d].reshape(bsz, 1, d) for k in range(6)] for l in range(2)]
    cos, sin = _rope_tables(positions)

    rw_t = router_w.T.astype(BF16)
    rows = jnp.array([PER_GROUP * g + j if g < N_EXPERT_GROUPS else 0
                      for j in range(PER_GROUP) for g in range(8)], I32)
    live = jnp.array([1.0 if g < N_EXPERT_GROUPS else 0.0
                      for j in range(PER_GROUP) for g in range(8)], F32)
    rw32 = (rw_t[rows].astype(F32) * live[:, None]).astype(BF16)
    rb32 = jnp.where(live > 0, router_bias.astype(F32)[rows], NEG).reshape(32, 1)
    rw3 = rw_t.reshape(N_EXPERTS, 1, d)
    expert_ws = (exp_w_gate, exp_w_up, exp_w_down)

    sh_m, sc_m, g_m, sh_f, sc_f, g_f = mods[0]
    q, k, v = _qkv0(x, norm_mix_g[0], sc_m, sh_m, a_w_qkv[0].astype(BF16), cos, sin)
    o, experts = _att0(q, k, v, a_sink[0].astype(F32), expert_ws, 0)
    x1, *routed = _post(o, a_w_o[0].astype(BF16), x, g_m, norm_ffn_g[0], sc_f, sh_f, rw32, rb32)
    y = _moe(*routed, *experts, rw3)
    g_f_prev = g_f

    sh_m, sc_m, g_m, sh_f, sc_f, g_f = mods[1]
    outs, experts = _qkv1(x1, y, g_f_prev, norm_mix_g[1], sc_m, sh_m, b_w_qkv[0].astype(BF16), cos, sin,
                          expert_ws, 1)
    x2, qkv = outs[0], outs[1:]
    o = _att1(qkv)
    x3, *routed = _post(o, b_w_o[0].astype(BF16), x2, g_m, norm_ffn_g[1], sc_f, sh_f, rw32, rb32)
    y = _moe(*routed, *experts, rw3)

    return _final(x3, y, g_f, final_norm_g)
```

```python
import functools

import jax
import jax.numpy as jnp
from jax import lax
from jax.experimental import pallas as pl
from jax.experimental.pallas import tpu as pltpu
from jax.experimental.pallas import tpu_sc as plsc

F32, BF16, I32, U32 = jnp.float32, jnp.bfloat16, jnp.int32, jnp.uint32

D_MODEL = 1024
HEAD_DIM = 64
ROPE_THETA = 10000.0
NORM_EPS = 1e-6
A_Q_HEADS = 16
A_KV_HEADS = 4
A_GROUP = A_Q_HEADS // A_KV_HEADS
A_RADIUS = 128
B_GROUPS = ((128, 1), (512, 4), (2048, 16))
B_HEADS = 8
B_RADIUS = 64
B_WIDTH = B_HEADS * HEAD_DIM
N_EXPERTS = 16
N_EXPERT_GROUPS = 4
PER_GROUP = N_EXPERTS // N_EXPERT_GROUPS
D_EXPERT = D_MODEL // 2
PAIRS = ((0, 1), (0, 2), (0, 3), (1, 2), (1, 3), (2, 3))
N_CLASSES = N_EXPERT_GROUPS * len(PAIRS)
NEG = -1e30
LOG2E = 1.4426950408889634
Q_SCALE = HEAD_DIM ** -0.5 * LOG2E

VMEM_LIMIT = 56 * 1024 * 1024
TM = 512
ATT1_CHUNK = 1024
FFN_TM = 256
FFN_TILES_PER_STEP = 2
SC_CORES_V7X = 2
SC_SUBCORES_V7X = 16
SC_ROWS_PER_STEP = 64


def _cparams(*sem):
    return pltpu.CompilerParams(dimension_semantics=sem, vmem_limit_bytes=VMEM_LIMIT)


def _rms_mod(x, g, sc, sh):
    ms = jnp.mean(x * x, axis=-1, keepdims=True)
    return (x * lax.rsqrt(ms + NORM_EPS)) * g * (1.0 + sc) + sh


def _rope_apply(y, cos, sin_signed, lane_lo):
    sw = jnp.where(lane_lo, pltpu.roll(y, 96, 1), pltpu.roll(y, 32, 1))
    return y * cos + sw * sin_signed


def _pack_bf16_pairs(x):
    n = x.shape[1] // 2
    lo = lax.bitcast_convert_type(x[:, :n].astype(BF16).astype(F32), U32)
    hi = lax.bitcast_convert_type(x[:, n:].astype(BF16).astype(F32), U32)
    return (hi & jnp.uint32(0xFFFF0000)) | (lo >> 16)


def _unpack_bf16_pairs(p):
    lo = lax.bitcast_convert_type(p << 16, F32)
    hi = lax.bitcast_convert_type(p & jnp.uint32(0xFFFF0000), F32)
    return jnp.concatenate([lo, hi], axis=1)


def _dup_head(x, first_head, which):
    rolled = pltpu.roll(x, HEAD_DIM, 1)
    return jnp.where(first_head, x, rolled) if which == 0 else jnp.where(first_head, rolled, x)


def _mod_kernel(c_ref, w_ref, b_ref, o_ref):
    c = c_ref[...]
    s = c * jax.nn.sigmoid(c)
    o_ref[0] = jnp.dot(s, w_ref[0], preferred_element_type=F32,
                       precision=lax.Precision.HIGHEST) + b_ref[0]


def _modulation(c, ada_w, ada_b):
    depth, d, n = ada_w.shape
    bsz = c.shape[0]
    cp = jnp.zeros((8, d), F32).at[:bsz].set(c)
    tn = 1536
    out = pl.pallas_call(
        _mod_kernel,
        grid=(depth, n // tn),
        in_specs=[pl.BlockSpec((8, d), lambda l, j: (0, 0)),
                  pl.BlockSpec((1, d, tn), lambda l, j: (l, 0, j)),
                  pl.BlockSpec((1, 1, tn), lambda l, j: (l, 0, j))],
        out_specs=pl.BlockSpec((1, 8, tn), lambda l, j: (l, 0, j)),
        out_shape=jax.ShapeDtypeStruct((depth, 8, n), F32),
        compiler_params=_cparams("arbitrary", "arbitrary"),
        name="modulation",
    )(cp, ada_w, ada_b.reshape(depth, 1, n))
    return out[:, :bsz]


def _rope_kernel(pos_ref, invf_ref, sign_ref, cos_ref, sin_ref):
    ang = pos_ref[0].astype(F32) * invf_ref[...]
    cos_ref[0] = jnp.cos(ang)
    sin_ref[0] = jnp.sin(ang) * sign_ref[...]


def _rope_tables(positions):
    bsz, s = positions.shape
    inv_freq = ROPE_THETA ** (-jnp.arange(0, HEAD_DIM, 2, dtype=F32) / HEAD_DIM)
    invf = jnp.tile(inv_freq, 4).reshape(1, 128)
    half = HEAD_DIM // 2
    sign = jnp.tile(jnp.concatenate([-jnp.ones((half,), F32), jnp.ones((half,), F32)]), 2).reshape(1, 128)
    ts = TM
    return pl.pallas_call(
        _rope_kernel,
        grid=(bsz, s // ts),
        in_specs=[pl.BlockSpec((1, ts, 1), lambda b, i: (b, i, 0)),
                  pl.BlockSpec((1, 128), lambda b, i: (0, 0)),
                  pl.BlockSpec((1, 128), lambda b, i: (0, 0))],
        out_specs=[pl.BlockSpec((1, ts, 128), lambda b, i: (b, i, 0))] * 2,
        out_shape=[jax.ShapeDtypeStruct((bsz, s, 128), F32)] * 2,
        compiler_params=_cparams("parallel", "parallel"),
        name="rope_tables",
    )(positions.reshape(bsz, s, 1), invf, sign)


def _qkv0_kernel(x_ref, g_ref, sc_ref, sh_ref, w_ref, cos_ref, sin_ref,
                 q_ref, k_ref, v_ref, hb_scr):
    hb_scr[...] = _rms_mod(x_ref[0], g_ref[...], sc_ref[0], sh_ref[0]).astype(BF16)
    cos = cos_ref[0]
    sin = sin_ref[0]
    lane_lo = (lax.broadcasted_iota(I32, cos.shape, 1) % HEAD_DIM) < (HEAD_DIM // 2)
    nq = A_Q_HEADS * HEAD_DIM
    nkv = A_KV_HEADS * HEAD_DIM
    for c in range(nq // 256):
        y = jnp.dot(hb_scr[...], w_ref[:, 256 * c:256 * (c + 1)], preferred_element_type=F32)
        for cc in range(2):
            r = _rope_apply(y[:, 128 * cc:128 * (cc + 1)], cos, sin, lane_lo) * Q_SCALE
            q_ref[0, :, 256 * c + 128 * cc:256 * c + 128 * (cc + 1)] = r.astype(BF16)
    first_head = lax.broadcasted_iota(I32, cos.shape, 1) < HEAD_DIM
    y = jnp.dot(hb_scr[...], w_ref[:, nq:nq + nkv], preferred_element_type=F32)
    for cc in range(nkv // 128):
        r = _rope_apply(y[:, 128 * cc:128 * (cc + 1)], cos, sin, lane_lo)
        for which in range(2):
            g = 2 * cc + which
            k_ref[0, :, 128 * g:128 * (g + 1)] = _dup_head(r, first_head, which).astype(BF16)
    y = jnp.dot(hb_scr[...], w_ref[:, nq + nkv:nq + 2 * nkv], preferred_element_type=F32)
    for cc in range(nkv // 128):
        r = y[:, 128 * cc:128 * (cc + 1)]
        for which in range(2):
            g = 2 * cc + which
            v_ref[0, :, 128 * g:128 * (g + 1)] = _dup_head(r, first_head, which).astype(BF16)


def _qkv0(x, g, sc, sh, w, cos, sin):
    bsz, s, d = x.shape
    nq = A_Q_HEADS * HEAD_DIM
    nkv = A_KV_HEADS * HEAD_DIM
    tm = TM
    tok = lambda w_: pl.BlockSpec((1, tm, w_), lambda b, i: (b, i, 0))
    per_b = pl.BlockSpec((1, 1, d), lambda b, i: (b, 0, 0))
    return pl.pallas_call(
        _qkv0_kernel,
        grid=(bsz, s // tm),
        in_specs=[tok(d), pl.BlockSpec((1, d), lambda b, i: (0, 0)), per_b, per_b,
                  pl.BlockSpec(w.shape, lambda b, i: (0, 0)), tok(128), tok(128)],
        out_specs=[tok(nq), tok(2 * nkv), tok(2 * nkv)],
        out_shape=[jax.ShapeDtypeStruct((bsz, s, nq), BF16),
                   jax.ShapeDtypeStruct((bsz, s, 2 * nkv), BF16),
                   jax.ShapeDtypeStruct((bsz, s, 2 * nkv), BF16)],
        scratch_shapes=[pltpu.VMEM((tm, d), BF16)],
        compiler_params=_cparams("parallel", "parallel"),
        name="qkv0",
    )(x, g.reshape(1, d), sc, sh, w, cos, sin)


def _side_cast_specs(ws, layer, grid):
    nsteps = grid[0] * grid[1]
    in_specs, out_specs, out_shape, args = [], [], [], []
    for w in ws:
        n_exp, k, n = w.shape[1:]
        rows = n_exp * k // nsteps
        assert rows * nsteps == n_exp * k and rows % 16 == 0
        in_specs.append(pl.BlockSpec((rows, n), lambda b, i: (layer * nsteps + b * grid[1] + i, 0)))
        out_specs.append(pl.BlockSpec((rows, n), lambda b, i: (b * grid[1] + i, 0)))
        out_shape.append(jax.ShapeDtypeStruct((n_exp * k, n), BF16))
        args.append(w.reshape(w.shape[0] * n_exp * k, n))
    return in_specs, out_specs, out_shape, args


def _side_cast(srcs, dsts):
    for src, dst in zip(srcs, dsts):
        dst[...] = src[...].astype(BF16)


def _att0_kernel(sink_ref, q_ref, kc_ref, kp_ref, kn_ref, vc_ref, vp_ref, vn_ref, wg_ref, wu_ref, wd_ref,
                 o_ref, wgo_ref, wuo_ref, wdo_ref, kf, vf, *, tq, seq):
    _side_cast((wg_ref, wu_ref, wd_ref), (wgo_ref, wuo_ref, wdo_ref))
    i = pl.program_id(1)
    r = A_RADIUS
    ones = jnp.ones((tq + 2 * r, 128), BF16)
    for g in range(A_KV_HEADS):
        lanes = slice(128 * g, 128 * (g + 1))
        kf[g, 0:r] = kp_ref[0, :, lanes]
        kf[g, r:r + tq] = kc_ref[0, :, lanes]
        kf[g, r + tq:r + tq + r] = kn_ref[0, :, lanes]
        vf[g, 0:r, 0:128] = vp_ref[0, :, lanes]
        vf[g, r:r + tq, 0:128] = vc_ref[0, :, lanes]
        vf[g, r + tq:r + tq + r, 0:128] = vn_ref[0, :, lanes]
        vf[g, :, 128:256] = ones

    ii = lax.broadcasted_iota(I32, (r, r), 0)
    jj = lax.broadcasted_iota(I32, (r, r), 1)
    first_head = jj < HEAD_DIM
    zero = jnp.zeros((r, 128), BF16)

    for j in range(tq // r):
        q0 = j * r
        base = i * tq + q0
        bias_lo = jnp.where((jj >= ii) & (base - r + jj >= 0), 0.0, NEG)
        bias_hi = jnp.where((jj <= ii) & (base + r + jj < seq), 0.0, NEG)
        for g in range(A_KV_HEADS):
            kb = kf[g, pl.ds(q0, 3 * r), :]
            vb = vf[g, pl.ds(q0, 3 * r), :]
            qs = []
            for cc in range(2):
                c = 2 * g + cc
                qc = q_ref[0, pl.ds(q0, r), 128 * c:128 * (c + 1)]
                qs += [jnp.where(first_head, qc, zero), jnp.where(first_head, zero, qc)]
            s4 = lax.dot_general(jnp.concatenate(qs, axis=0), kb, (((1,), (1,)), ((), ())),
                                 preferred_element_type=F32)
            ps, ms = [], []
            for t in range(A_GROUP):
                s = s4[r * t:r * (t + 1)]
                a0 = s[:, 0:r] + bias_lo
                a1 = s[:, r:2 * r]
                a2 = s[:, 2 * r:3 * r] + bias_hi
                sink = sink_ref[A_GROUP * g + t] * LOG2E
                m = jnp.max(jnp.maximum(jnp.maximum(a0, a1), a2), axis=-1, keepdims=True)
                m = jnp.maximum(m, sink)
                ps.append(jnp.concatenate([jnp.exp2(a0 - m), jnp.exp2(a1 - m), jnp.exp2(a2 - m)],
                                          axis=1).astype(BF16))
                ms.append(m)
            pv = jnp.dot(jnp.concatenate(ps, axis=0), vb, preferred_element_type=F32)
            os_ = []
            for t in range(A_GROUP):
                sink = sink_ref[A_GROUP * g + t] * LOG2E
                den = pv[r * t:r * (t + 1), 128:256] + jnp.exp2(sink - ms[t])
                os_.append(pv[r * t:r * (t + 1), 0:128] / den)
            for cc in range(2):
                c = 2 * g + cc
                o_ref[0, pl.ds(q0, r), 128 * c:128 * (c + 1)] = jnp.where(
                    first_head, os_[2 * cc], os_[2 * cc + 1]).astype(BF16)


def _att0(q, k2, v2, sink, expert_ws, layer):
    bsz, s, nq = q.shape
    nkv2 = k2.shape[-1]
    tq = TM
    r = A_RADIUS
    per = tq // r
    last = s // r - 1
    grid = (bsz, s // tq)
    cur = lambda w_: pl.BlockSpec((1, tq, w_), lambda b, i: (b, i, 0))
    prev = pl.BlockSpec((1, r, nkv2), lambda b, i: (b, jnp.maximum(i * per - 1, 0), 0))
    nxt = pl.BlockSpec((1, r, nkv2), lambda b, i: (b, jnp.minimum(i * per + per, last), 0))
    w_in, w_out, w_shape, w_args = _side_cast_specs(expert_ws, layer, grid)
    o, *cast = pl.pallas_call(
        functools.partial(_att0_kernel, tq=tq, seq=s),
        grid=grid,
        in_specs=[pl.BlockSpec(memory_space=pltpu.SMEM),
                  cur(nq), cur(nkv2), prev, nxt, cur(nkv2), prev, nxt] + w_in,
        out_specs=[cur(nq)] + w_out,
        out_shape=[jax.ShapeDtypeStruct((bsz, s, nq), BF16)] + w_shape,
        scratch_shapes=[pltpu.VMEM((A_KV_HEADS, tq + 2 * r, 128), BF16),
                        pltpu.VMEM((A_KV_HEADS, tq + 2 * r, 256), BF16)],
        compiler_params=_cparams("parallel", "parallel"),
        name="att0",
    )(sink, q, k2, k2, k2, v2, v2, v2, *w_args)
    return o, [c.reshape(w.shape[1:]) for c, w in zip(cast, expert_ws)]


def _post_kernel(o_ref, wo_ref, x_ref, gm_ref, gf_ref, scf_ref, shf_ref, rw_ref, rb_ref, upper_ref,
                 x1_ref, h2_ref, cls_ref, rank_ref, cnt_ref, carry):
    mix = jnp.dot(o_ref[0], wo_ref[...], preferred_element_type=F32)
    x1 = x_ref[0] + gm_ref[0] * mix
    x1_ref[0] = x1
    h2 = _rms_mod(x1, gf_ref[...], scf_ref[0], shf_ref[0]).astype(BF16)
    h2_ref[0] = _pack_bf16_pairs(h2.astype(F32))
    tq = h2.shape[0]
    logits = lax.dot_general(rw_ref[...], h2, (((1,), (1,)), ((), ())), preferred_element_type=F32)
    scores = jax.nn.sigmoid(logits)
    biased = scores + rb_ref[...]
    pj = [biased[8 * j:8 * (j + 1)] for j in range(PER_GROUP)]
    sel = []
    for j in range(PER_GROUP):
        beaten = jnp.zeros((8, tq), F32)
        for j2 in range(PER_GROUP):
            if j2 == j:
                continue
            beats = (pj[j2] > pj[j]) | (pj[j2] == pj[j]) if j2 < j else (pj[j2] > pj[j])
            beaten = beaten + jnp.where(beats, 1.0, 0.0)
        sel.append(beaten < 2.0)
    gscore = jnp.zeros((8, tq), F32)
    for j in range(PER_GROUP):
        gscore = gscore + jnp.where(sel[j], pj[j], 0.0)
    gi = lax.broadcasted_iota(I32, (8, tq), 0).astype(F32)
    gmax = jnp.max(gscore, axis=0, keepdims=True)
    gidx = jnp.min(jnp.where(gscore == gmax, gi, 8.0), axis=0, keepdims=True)
    onehot = gi == gidx
    f = [jnp.max(jnp.where(onehot & sel[j], 1.0, 0.0), axis=0, keepdims=True) > 0.5
         for j in range(PER_GROUP)]
    pair = jnp.where(f[0], jnp.where(f[1], 0.0, jnp.where(f[2], 1.0, 2.0)),
                     jnp.where(f[1], jnp.where(f[2], 3.0, 4.0), 5.0))
    cls = (gidx * float(len(PAIRS)) + pair).astype(I32)
    cls_ref[0] = jnp.broadcast_to(cls, (8, tq))

    @pl.when((pl.program_id(0) == 0) & (pl.program_id(1) == 0))
    def _():
        carry[...] = jnp.zeros_like(carry)

    oh = lax.broadcasted_iota(I32, (32, tq), 0) == cls
    ohf = jnp.where(oh, 1.0, 0.0)
    within = jnp.dot(ohf.astype(BF16), upper_ref[...], preferred_element_type=F32)
    c = carry[...]
    tot = jnp.sum(jnp.where(oh, within + c[:, 0:1], 0.0), axis=0, keepdims=True)
    rank_ref[0] = jnp.broadcast_to(tot.astype(I32), (8, tq))
    newc = c + jnp.sum(ohf, axis=1, keepdims=True)
    carry[...] = newc
    cnt_ref[...] = newc


def _post(o, wo, x, gm, gf, scf, shf, rw32, rb32):
    bsz, s, d = x.shape
    do = o.shape[-1]
    tq = TM
    nt = s // tq
    tok = lambda w_: pl.BlockSpec((1, tq, w_), lambda b, i: (b, i, 0))
    per_b = pl.BlockSpec((1, 1, d), lambda b, i: (b, 0, 0))
    full = lambda a: pl.BlockSpec(a.shape, lambda b, i: (0,) * a.ndim)
    per_tile = pl.BlockSpec((1, 8, tq), lambda b, i: (b * nt + i, 0, 0))
    gf2 = gf.reshape(1, d)
    upper = (jnp.arange(tq)[:, None] < jnp.arange(tq)[None, :]).astype(BF16)
    return pl.pallas_call(
        _post_kernel,
        grid=(bsz, nt),
        in_specs=[tok(do), full(wo), tok(d), per_b, full(gf2), per_b, per_b, full(rw32), full(rb32),
                  full(upper)],
        out_specs=[tok(d), tok(d // 2), per_tile, per_tile, pl.BlockSpec((32, 128), lambda b, i: (0, 0))],
        out_shape=[jax.ShapeDtypeStruct((bsz, s, d), F32),
                   jax.ShapeDtypeStruct((bsz, s, d // 2), U32),
                   jax.ShapeDtypeStruct((bsz * nt, 8, tq), I32),
                   jax.ShapeDtypeStruct((bsz * nt, 8, tq), I32),
                   jax.ShapeDtypeStruct((32, 128), F32)],
        scratch_shapes=[pltpu.VMEM((32, 128), F32)],
        compiler_params=_cparams("arbitrary", "arbitrary"),
        name="post_attention",
    )(o, wo, x, gm, gf2, scf, shf, rw32, rb32, upper)


def _ffn_kernel(e1_ref, e2_ref, nused_ref, hs_ref, *refs):
    ys_ref = refs[-1]
    j = pl.program_id(0)
    tm = FFN_TM

    @pl.when(j * FFN_TILES_PER_STEP < nused_ref[0])
    def _():
        for sub in range(FFN_TILES_PER_STEP):
            wg1, wu1, wd1, rw1, wg2, wu2, wd2, rw2 = refs[8 * sub:8 * (sub + 1)]
            rows = slice(tm * sub, tm * (sub + 1))
            xf = _unpack_bf16_pairs(hs_ref[rows])
            x = xf.astype(BF16)
            gt1 = jnp.dot(x, wg1[0], preferred_element_type=F32)
            up1 = jnp.dot(x, wu1[0], preferred_element_type=F32)
            gt2 = jnp.dot(x, wg2[0], preferred_element_type=F32)
            up2 = jnp.dot(x, wu2[0], preferred_element_type=F32)
            a1 = ((gt1 * jax.nn.sigmoid(gt1)) * up1).astype(BF16)
            y1 = jnp.dot(a1, wd1[0], preferred_element_type=F32)
            a2 = ((gt2 * jax.nn.sigmoid(gt2)) * up2).astype(BF16)
            y2 = jnp.dot(a2, wd2[0], preferred_element_type=F32)
            s1 = jax.nn.sigmoid(jnp.sum(xf * rw1[0].astype(F32), axis=-1, keepdims=True))
            s2 = jax.nn.sigmoid(jnp.sum(xf * rw2[0].astype(F32), axis=-1, keepdims=True))
            den = s1 + s2
            ys_ref[rows] = _pack_bf16_pairs((s1 / den) * y1 + (s2 / den) * y2)

    @pl.when(j * FFN_TILES_PER_STEP >= nused_ref[0])
    def _():
        ys_ref[...] = jnp.zeros_like(ys_ref)


def _ffn(hs, e1, e2, nused, wg, wu, wd, rw3):
    npad, dh = hs.shape
    d = 2 * dh
    per = FFN_TILES_PER_STEP
    rows = per * FFN_TM
    de = wg.shape[-1]
    assert npad % rows == 0
    row = pl.BlockSpec((rows, dh), lambda j, e1, e2, nu: (jnp.minimum(j, (nu[0] - 1) // per), 0))
    w_specs = []
    for sub in range(per):
        for which in range(2):
            pick = lambda j, e1, e2, nu, sub=sub, which=which: ((e2 if which else e1)[per * j + sub], 0, 0)
            w_specs += [pl.BlockSpec((1, d, de), pick), pl.BlockSpec((1, d, de), pick),
                        pl.BlockSpec((1, de, d), pick), pl.BlockSpec((1, 1, d), pick)]
    grid_spec = pltpu.PrefetchScalarGridSpec(
        num_scalar_prefetch=3,
        grid=(npad // rows,),
        in_specs=[row] + w_specs,
        out_specs=pl.BlockSpec((rows, dh), lambda j, e1, e2, nu: (j, 0)),
    )
    return pl.pallas_call(
        _ffn_kernel,
        grid_spec=grid_spec,
        out_shape=jax.ShapeDtypeStruct((npad, dh), U32),
        compiler_params=_cparams("arbitrary"),
        name="expert_ffn",
    )(e1, e2, nused, hs, *([wg, wu, wd, rw3] * (2 * per)))


def _sc_mesh():
    return plsc.VectorSubcoreMesh(core_axis_name="c", subcore_axis_name="s",
                                  num_cores=SC_CORES_V7X, num_subcores=SC_SUBCORES_V7X)


def _sc_steps(n_rows):
    workers = SC_CORES_V7X * SC_SUBCORES_V7X
    per_w = n_rows // workers
    ch = min(SC_ROWS_PER_STEP, per_w // 2)
    n = per_w // ch
    assert per_w * workers == n_rows and n * ch == per_w and n % 2 == 0 and ch % 8 == 0
    return per_w, ch, n


def _sc_scratch(ch, d, dtype):
    return [pltpu.VMEM((2, ch), I32), pltpu.VMEM((2, ch, d), dtype),
            pltpu.SemaphoreType.DMA((2,)), pltpu.SemaphoreType.DMA((2,))]


def _sc_scatter_rows(src, idx, n_out):
    n_in, d = src.shape
    per_w, ch, n = _sc_steps(n_in)

    def body(src_hbm, idx_hbm, out_hbm, idx_v, rows_v, rsem, ssem):
        base = (lax.axis_index("s") * SC_CORES_V7X + lax.axis_index("c")) * per_w

        def read(i, b):
            return pltpu.make_async_copy(src_hbm.at[pl.ds(base + i * ch, ch)], rows_v.at[b], rsem.at[b])

        def scatter(b):
            return pltpu.make_async_copy(rows_v.at[b], out_hbm.at[idx_v.at[b]], ssem.at[b])

        def start_read(i, b):
            pltpu.sync_copy(idx_hbm.at[pl.ds(base + i * ch, ch)], idx_v.at[b])
            read(i, b).start()

        start_read(0, 0)

        @pl.loop(0, n, step=2)
        def _(i):
            for b in (0, 1):
                cur = i + b

                @pl.when(cur + 1 < n)
                def _():
                    @pl.when(cur >= 1)
                    def _():
                        scatter(1 - b).wait()
                    start_read(cur + 1, 1 - b)

                read(cur, b).wait()
                scatter(b).start()

        scatter(0).wait()
        scatter(1).wait()

    return pl.kernel(
        body, mesh=_sc_mesh(),
        out_type=jax.ShapeDtypeStruct((n_out, d), src.dtype),
        scratch_types=_sc_scratch(ch, d, src.dtype),
        name="sc_scatter_rows",
    )(src, idx)


def _sc_gather_rows(table, idx):
    n_out = idx.shape[0]
    d = table.shape[1]
    per_w, ch, n = _sc_steps(n_out)

    def body(table_hbm, idx_hbm, out_hbm, idx_v, rows_v, gsem, wsem):
        base = (lax.axis_index("s") * SC_CORES_V7X + lax.axis_index("c")) * per_w

        def gather(b):
            return pltpu.make_async_copy(table_hbm.at[idx_v.at[b]], rows_v.at[b], gsem.at[b])

        def write(i, b):
            return pltpu.make_async_copy(rows_v.at[b], out_hbm.at[pl.ds(base + i * ch, ch)], wsem.at[b])

        def start_gather(i, b):
            pltpu.sync_copy(idx_hbm.at[pl.ds(base + i * ch, ch)], idx_v.at[b])
            gather(b).start()

        start_gather(0, 0)

        @pl.loop(0, n, step=2)
        def _(i):
            for b in (0, 1):
                cur = i + b

                @pl.when(cur + 1 < n)
                def _():
                    @pl.when(cur >= 1)
                    def _():
                        write(cur - 1, 1 - b).wait()
                    start_gather(cur + 1, 1 - b)

                gather(b).wait()
                write(cur, b).start()

        write(n - 2, 0).wait()
        write(n - 1, 1).wait()

    return pl.kernel(
        body, mesh=_sc_mesh(),
        out_type=jax.ShapeDtypeStruct((n_out, d), table.dtype),
        scratch_types=_sc_scratch(ch, d, table.dtype),
        name="sc_gather_rows",
    )(table, idx)


def _moe(h2p, cls3, rank3, cnt, wg, wu, wd, rw3):
    bsz, s, dh = h2p.shape
    t = bsz * s
    tm = FFN_TM
    cls = cls3[:, 0, :].reshape(t)
    rank = rank3[:, 0, :].reshape(t)
    counts = cnt[:N_CLASSES, 0].astype(I32)
    padded = ((counts + tm - 1) // tm) * tm
    upto = jnp.arange(N_CLASSES)[:, None] >= jnp.arange(N_CLASSES)[None, :]
    ends = jnp.sum(jnp.where(upto, padded[None, :], 0), axis=1)
    base = ends - padded
    pos = jnp.take(base, cls, mode="clip") + rank
    ntiles = t // tm + N_CLASSES
    tile_start = jnp.arange(ntiles, dtype=I32) * tm
    tcls = jnp.minimum(jnp.sum((ends[None, :] <= tile_start[:, None]).astype(I32), axis=1), N_CLASSES - 1)
    nused = (ends[-1:] // tm).astype(I32)
    grp = tcls // len(PAIRS)
    pr = tcls % len(PAIRS)
    first = jnp.array([p[0] for p in PAIRS], I32)
    second = jnp.array([p[1] for p in PAIRS], I32)
    e1 = (grp * PER_GROUP + jnp.take(first, pr, mode="clip")).astype(I32)
    e2 = (grp * PER_GROUP + jnp.take(second, pr, mode="clip")).astype(I32)
    hs = _sc_scatter_rows(h2p.reshape(t, dh), pos, ntiles * tm)
    ys = _ffn(hs, e1, e2, nused, wg, wu, wd, rw3)
    return _sc_gather_rows(ys, pos).reshape(bsz, s, dh)


def _qkv1_kernel(x_ref, y_ref, gfp_ref, g_ref, sc_ref, sh_ref, w_ref, cos_ref, sin_ref,
                 wg_ref, wu_ref, wd_ref, x2_ref, *rest):
    outs = rest[:9]
    _side_cast((wg_ref, wu_ref, wd_ref), rest[9:12])
    hb_scr, ysc, ysc2 = rest[12:]
    tm = x_ref.shape[1]
    x2 = x_ref[0] + gfp_ref[0] * _unpack_bf16_pairs(y_ref[0])
    x2_ref[0] = x2
    hb_scr[...] = _rms_mod(x2, g_ref[...], sc_ref[0], sh_ref[0]).astype(BF16)
    cos = cos_ref[0]
    sin = sin_ref[0]
    lane_lo = (lax.broadcasted_iota(I32, cos.shape, 1) % HEAD_DIM) < (HEAD_DIM // 2)
    for gi, (_, dil) in reversed(list(enumerate(B_GROUPS))):
        for j in range(3):
            c = gi * 3 + j
            y = jnp.dot(hb_scr[...], w_ref[:, B_WIDTH * c:B_WIDTH * (c + 1)], preferred_element_type=F32)
            out = outs[c]
            for cc in range(B_WIDTH // 128):
                lanes = slice(128 * cc, 128 * (cc + 1))
                r = y[:, lanes]
                if j < 2:
                    r = _rope_apply(r, cos, sin, lane_lo)
                if j == 0:
                    r = r * Q_SCALE
                if dil == 1:
                    out[0, 0, :, lanes] = r.astype(BF16)
                    continue
                ysc[cc] = r
                if dil % 16:
                    for rr in range(dil):
                        out[0, rr, :, lanes] = ysc[cc, pl.ds(rr, tm // dil, stride=dil), :].astype(BF16)
                    continue
                quarter = tm // 4
                inner = dil // 4
                for r1 in range(4):
                    ysc2[cc, r1 * quarter:(r1 + 1) * quarter] = ysc[cc, pl.ds(r1, quarter, stride=4), :]
                for r1 in range(4):
                    for r2 in range(inner):
                        out[0, r1 + 4 * r2, :, lanes] = ysc2[
                            cc, pl.ds(r1 * quarter + r2, tm // dil, stride=inner), :].astype(BF16)


def _qkv1(x, y, gfp, g, sc, sh, w, cos, sin, expert_ws, layer):
    bsz, s, d = x.shape
    tm = TM
    grid = (bsz, s // tm)
    w_in, w_out, w_shape, w_args = _side_cast_specs(expert_ws, layer, grid)
    tok = lambda w_: pl.BlockSpec((1, tm, w_), lambda b, i: (b, i, 0))
    per_b = pl.BlockSpec((1, 1, d), lambda b, i: (b, 0, 0))
    out_specs = [tok(d)]
    out_shape = [jax.ShapeDtypeStruct((bsz, s, d), F32)]
    for _, dil in B_GROUPS:
        for _ in range(3):
            out_specs.append(pl.BlockSpec((1, dil, tm // dil, B_WIDTH), lambda b, i: (b, 0, i, 0)))
            out_shape.append(jax.ShapeDtypeStruct((bsz, dil, s // dil, B_WIDTH), BF16))
    outs = pl.pallas_call(
        _qkv1_kernel,
        grid=grid,
        in_specs=[tok(d), tok(d // 2), per_b, pl.BlockSpec((1, d), lambda b, i: (0, 0)), per_b, per_b,
                  pl.BlockSpec(w.shape, lambda b, i: (0, 0)), tok(128), tok(128)] + w_in,
        out_specs=out_specs + w_out,
        out_shape=out_shape + w_shape,
        scratch_shapes=[pltpu.VMEM((tm, d), BF16), pltpu.VMEM((B_WIDTH // 128, tm, 128), F32),
                        pltpu.VMEM((B_WIDTH // 128, tm, 128), F32)],
        compiler_params=_cparams("parallel", "parallel"),
        name="qkv1",
    )(x, y, gfp, g.reshape(1, d), sc, sh, w, cos, sin, *w_args)
    return outs[:10], [c.reshape(w_.shape[1:]) for c, w_ in zip(outs[10:], expert_ws)]


def _att1_window_pad(lseg):
    qb = min(2 * B_RADIUS, lseg)
    return -(qb + 2 * B_RADIUS) % 128


def _att1_kernel(*refs, chunk, seq):
    ins = refs[:21]
    o_ref = refs[21]
    scr = refs[22:]
    nat_o, nat_d, nat_m = scr[0:3], scr[3:6], scr[6:9]
    n = pl.program_id(1)
    rad = B_RADIUS
    npair = B_WIDTH // 128

    def window(cur, prev, nxt, res, lo, hi, lseg, lanes):
        parts = []
        if lo < 0:
            parts.append(prev[0, res, :, lanes])
        parts.append(cur[0, res, max(lo, 0):min(hi, lseg), lanes])
        if hi > lseg:
            parts.append(nxt[0, res, :, lanes])
        if hi > lseg + rad:
            parts.append(nxt[0, res, :, lanes])
        return jnp.concatenate(parts, axis=0) if len(parts) > 1 else parts[0]

    for c in range(npair):
        lanes = slice(128 * c, 128 * (c + 1))
        for gi, (_, dil) in enumerate(B_GROUPS):
            q_ref, kc, kp, kn, vc, vp, vn = ins[7 * gi:7 * gi + 7]
            lseg = chunk // dil
            qb = min(2 * rad, lseg)
            kb_n = qb + 2 * rad + _att1_window_pad(lseg)
            lstr = seq // dil
            rel = lax.broadcasted_iota(I32, (qb, kb_n), 1) - lax.broadcasted_iota(I32, (qb, kb_n), 0)
            band_bias = jnp.where((rel >= 0) & (rel <= 2 * rad), 0.0, NEG)
            jrow = lax.broadcasted_iota(I32, (1, kb_n), 1)
            first_head = lax.broadcasted_iota(I32, (qb, 128), 1) < HEAD_DIM
            zero = jnp.zeros((qb, 128), BF16)
            ones = jnp.ones((kb_n, 128), BF16)
            for res in range(dil):
                for q0 in range(0, lseg, qb):
                    apos = n * lseg + q0 - rad + jrow
                    bias = band_bias + jnp.where((apos >= 0) & (apos < lstr), 0.0, NEG)
                    rows = pl.ds(q0, qb) if dil == 1 else pl.ds(res + dil * q0, qb, stride=dil)
                    qc = q_ref[0, res, q0:q0 + qb, lanes]
                    kb = window(kc, kp, kn, res, q0 - rad, q0 - rad + kb_n, lseg, lanes)
                    vb = jnp.concatenate(
                        [window(vc, vp, vn, res, q0 - rad, q0 - rad + kb_n, lseg, lanes), ones], axis=1)
                    qs = jnp.concatenate(
                        [jnp.where(first_head, qc, zero), jnp.where(first_head, zero, qc)], axis=0)
                    s2 = lax.dot_general(qs, kb, (((1,), (1,)), ((), ())), preferred_element_type=F32)
                    ps, ms = [], []
                    for t2 in range(2):
                        s = s2[qb * t2:qb * (t2 + 1)] + bias
                        m = jnp.max(s, axis=-1, keepdims=True)
                        ps.append(jnp.exp2(s - m).astype(BF16))
                        ms.append(m)
                    pv = jnp.dot(jnp.concatenate(ps, axis=0), vb, preferred_element_type=F32)
                    nat_o[gi][rows, :] = jnp.where(first_head, pv[0:qb, 0:128], pv[qb:2 * qb, 0:128])
                    nat_d[gi][rows, :] = jnp.where(first_head, pv[0:qb, 128:256], pv[qb:2 * qb, 128:256])
                    nat_m[gi][rows, :] = jnp.where(first_head, ms[0], ms[1])

        ms = [nat_m[gi][...] for gi in range(3)]
        mx = jnp.maximum(jnp.maximum(ms[0], ms[1]), ms[2])
        w = [jnp.exp2(x - mx) for x in ms]
        num = w[0] * nat_o[0][...] + w[1] * nat_o[1][...] + w[2] * nat_o[2][...]
        den = w[0] * nat_d[0][...] + w[1] * nat_d[1][...] + w[2] * nat_d[2][...]
        o_ref[0, :, lanes] = (num / den).astype(BF16)


def _att1(qkv):
    bsz = qkv[0].shape[0]
    seq = qkv[0].shape[1] * qkv[0].shape[2]
    chunk = ATT1_CHUNK
    rad = B_RADIUS
    in_specs, args, scratch = [], [], []
    for gi, (_, dil) in enumerate(B_GROUPS):
        q, k, v = qkv[3 * gi:3 * gi + 3]
        lseg = chunk // dil
        per = lseg // rad
        last = seq // dil // rad - 1
        cur = pl.BlockSpec((1, dil, lseg, B_WIDTH), lambda b, n: (b, 0, n, 0))
        prev = pl.BlockSpec((1, dil, rad, B_WIDTH),
                            lambda b, n, per=per: (b, 0, jnp.maximum(n * per - 1, 0), 0))
        nxt = pl.BlockSpec((1, dil, rad, B_WIDTH),
                           lambda b, n, per=per, last=last: (b, 0, jnp.minimum(n * per + per, last), 0))
        in_specs += [cur, cur, prev, nxt, cur, prev, nxt]
        args += [q, k, k, k, v, v, v]
    for _ in range(3):
        for _ in B_GROUPS:
            scratch.append(pltpu.VMEM((chunk, 128), F32))
    return pl.pallas_call(
        functools.partial(_att1_kernel, chunk=chunk, seq=seq),
        grid=(bsz, seq // chunk),
        in_specs=in_specs,
        out_specs=pl.BlockSpec((1, chunk, B_WIDTH), lambda b, n: (b, n, 0)),
        out_shape=jax.ShapeDtypeStruct((bsz, seq, B_WIDTH), BF16),
        scratch_shapes=scratch,
        compiler_params=_cparams("parallel", "parallel"),
        name="att1",
    )(*args)


def _final_kernel(x_ref, y_ref, gfp_ref, g_ref, o_ref):
    x = x_ref[0] + gfp_ref[0] * _unpack_bf16_pairs(y_ref[0])
    ms = jnp.mean(x * x, axis=-1, keepdims=True)
    o_ref[0] = (x * lax.rsqrt(ms + NORM_EPS)) * g_ref[...]


def _final(x, y, gfp, g):
    bsz, s, d = x.shape
    tm = TM
    tok = pl.BlockSpec((1, tm, d), lambda b, i: (b, i, 0))
    return pl.pallas_call(
        _final_kernel,
        grid=(bsz, s // tm),
        in_specs=[tok, pl.BlockSpec((1, tm, d // 2), lambda b, i: (b, i, 0)),
                  pl.BlockSpec((1, 1, d), lambda b, i: (b, 0, 0)),
                  pl.BlockSpec((1, d), lambda b, i: (0, 0))],
        out_specs=tok,
        out_shape=jax.ShapeDtypeStruct((bsz, s, d), F32),
        compiler_params=_cparams("parallel", "parallel"),
        name="final_norm",
    )(x, y, gfp, g.reshape(1, d))


def kernel(x, c, positions, ada_w, ada_b, norm_mix_g, norm_ffn_g, a_w_qkv, a_w_o, a_sink, b_w_qkv, b_w_o,
           router_w, router_bias, exp_w_gate, exp_w_up, exp_w_down, final_norm_g):
    bsz, s, d = x.shape
    assert d == D_MODEL and ada_w.shape[0] == 2 and s % ATT1_CHUNK == 0
    assert all(w_ // (2 * dil) == B_RADIUS for w_, dil in B_GROUPS)

    mod = _modulation(c, ada_w, ada_b)
    mods = [[mod[l][:, k * d:(k + 1) * d].reshape(bsz, 1, d) for k in range(6)] for l in range(2)]
    cos, sin = _rope_tables(positions)

    rw_t = router_w.T.astype(BF16)
    rows = jnp.array([PER_GROUP * g + j if g < N_EXPERT_GROUPS else 0
                      for j in range(PER_GROUP) for g in range(8)], I32)
    live = jnp.array([1.0 if g < N_EXPERT_GROUPS else 0.0
                      for j in range(PER_GROUP) for g in range(8)], F32)
    rw32 = (rw_t[rows].astype(F32) * live[:, None]).astype(BF16)
    rb32 = jnp.where(live > 0, router_bias.astype(F32)[rows], NEG).reshape(32, 1)
    rw3 = rw_t.reshape(N_EXPERTS, 1, d)
    expert_ws = (exp_w_gate, exp_w_up, exp_w_down)

    sh_m, sc_m, g_m, sh_f, sc_f, g_f = mods[0]
    q, k, v = _qkv0(x, norm_mix_g[0], sc_m, sh_m, a_w_qkv[0].astype(BF16), cos, sin)
    o, experts = _att0(q, k, v, a_sink[0].astype(F32), expert_ws, 0)
    x1, *routed = _post(o, a_w_o[0].astype(BF16), x, g_m, norm_ffn_g[0], sc_f, sh_f, rw32, rb32)
    y = _moe(*routed, *experts, rw3)
    g_f_prev = g_f

    sh_m, sc_m, g_m, sh_f, sc_f, g_f = mods[1]
    outs, experts = _qkv1(x1, y, g_f_prev, norm_mix_g[1], sc_m, sh_m, b_w_qkv[0].astype(BF16), cos, sin,
                          expert_ws, 1)
    x2, qkv = outs[0], outs[1:]
    o = _att1(qkv)
    x3, *routed = _post(o, b_w_o[0].astype(BF16), x2, g_m, norm_ffn_g[1], sc_f, sh_f, rw32, rb32)
    y = _moe(*routed, *experts, rw3)

    return _final(x3, y, g_f, final_norm_g)
```

```python
import functools

import jax
import jax.numpy as jnp
from jax import lax
from jax.experimental import pallas as pl
from jax.experimental.pallas import tpu as pltpu
from jax.experimental.pallas import tpu_sc as plsc

F32, BF16, I32, U32 = jnp.float32, jnp.bfloat16, jnp.int32, jnp.uint32

D_MODEL = 1024
HEAD_DIM = 64
ROPE_THETA = 10000.0
NORM_EPS = 1e-6
A_Q_HEADS = 16
A_KV_HEADS = 4
A_GROUP = A_Q_HEADS // A_KV_HEADS
A_RADIUS = 128
B_GROUPS = ((128, 1), (512, 4), (2048, 16))
B_HEADS = 8
B_RADIUS = 64
B_WIDTH = B_HEADS * HEAD_DIM
N_EXPERTS = 16
N_EXPERT_GROUPS = 4
PER_GROUP = N_EXPERTS // N_EXPERT_GROUPS
D_EXPERT = D_MODEL // 2
PAIRS = ((0, 1), (0, 2), (0, 3), (1, 2), (1, 3), (2, 3))
N_CLASSES = N_EXPERT_GROUPS * len(PAIRS)
NEG = -1e30
LOG2E = 1.4426950408889634
Q_SCALE = HEAD_DIM ** -0.5 * LOG2E

VMEM_LIMIT = 56 * 1024 * 1024
TM = 512
TM_WIDE = 1024
RANK_BLOCK = 512
ATT1_CHUNK = 1024
FFN_TM = 256
FFN_TILES_PER_STEP = 2
SC_CORES_V7X = 2
SC_SUBCORES_V7X = 16
SC_ROWS_PER_STEP = 64


def _cparams(*sem):
    return pltpu.CompilerParams(dimension_semantics=sem, vmem_limit_bytes=VMEM_LIMIT)


def _rms_mod(x, g, sc, sh):
    ms = jnp.mean(x * x, axis=-1, keepdims=True)
    return (x * lax.rsqrt(ms + NORM_EPS)) * g * (1.0 + sc) + sh


def _rope_apply(y, cos, sin_signed, lane_lo):
    sw = jnp.where(lane_lo, pltpu.roll(y, 96, 1), pltpu.roll(y, 32, 1))
    return y * cos + sw * sin_signed


def _pack_bf16_pairs(x):
    n = x.shape[1] // 2
    lo = lax.bitcast_convert_type(x[:, :n].astype(BF16).astype(F32), U32)
    hi = lax.bitcast_convert_type(x[:, n:].astype(BF16).astype(F32), U32)
    return (hi & jnp.uint32(0xFFFF0000)) | (lo >> 16)


def _unpack_bf16_pairs(p):
    lo = lax.bitcast_convert_type(p << 16, F32)
    hi = lax.bitcast_convert_type(p & jnp.uint32(0xFFFF0000), F32)
    return jnp.concatenate([lo, hi], axis=1)


def _dup_head(x, first_head, which):
    rolled = pltpu.roll(x, HEAD_DIM, 1)
    return jnp.where(first_head, x, rolled) if which == 0 else jnp.where(first_head, rolled, x)


def _mod_kernel(c_ref, w_ref, b_ref, o_ref):
    c = c_ref[...]
    s = c * jax.nn.sigmoid(c)
    o_ref[0] = jnp.dot(s, w_ref[0], preferred_element_type=F32,
                       precision=lax.Precision.HIGHEST) + b_ref[0]


def _modulation(c, ada_w, ada_b):
    depth, d, n = ada_w.shape
    bsz = c.shape[0]
    cp = jnp.zeros((8, d), F32).at[:bsz].set(c)
    tn = 1536
    out = pl.pallas_call(
        _mod_kernel,
        grid=(depth, n // tn),
        in_specs=[pl.BlockSpec((8, d), lambda l, j: (0, 0)),
                  pl.BlockSpec((1, d, tn), lambda l, j: (l, 0, j)),
                  pl.BlockSpec((1, 1, tn), lambda l, j: (l, 0, j))],
        out_specs=pl.BlockSpec((1, 8, tn), lambda l, j: (l, 0, j)),
        out_shape=jax.ShapeDtypeStruct((depth, 8, n), F32),
        compiler_params=_cparams("arbitrary", "arbitrary"),
        name="modulation",
    )(cp, ada_w, ada_b.reshape(depth, 1, n))
    return out[:, :bsz]


def _rope_kernel(pos_ref, invf_ref, sign_ref, cos_ref, sin_ref):
    ang = pos_ref[0].astype(F32) * invf_ref[...]
    cos_ref[0] = jnp.cos(ang)
    sin_ref[0] = jnp.sin(ang) * sign_ref[...]


def _rope_tables(positions):
    bsz, s = positions.shape
    inv_freq = ROPE_THETA ** (-jnp.arange(0, HEAD_DIM, 2, dtype=F32) / HEAD_DIM)
    invf = jnp.tile(inv_freq, 4).reshape(1, 128)
    half = HEAD_DIM // 2
    sign = jnp.tile(jnp.concatenate([-jnp.ones((half,), F32), jnp.ones((half,), F32)]), 2).reshape(1, 128)
    ts = TM
    return pl.pallas_call(
        _rope_kernel,
        grid=(bsz, s // ts),
        in_specs=[pl.BlockSpec((1, ts, 1), lambda b, i: (b, i, 0)),
                  pl.BlockSpec((1, 128), lambda b, i: (0, 0)),
                  pl.BlockSpec((1, 128), lambda b, i: (0, 0))],
        out_specs=[pl.BlockSpec((1, ts, 128), lambda b, i: (b, i, 0))] * 2,
        out_shape=[jax.ShapeDtypeStruct((bsz, s, 128), F32)] * 2,
        compiler_params=_cparams("parallel", "parallel"),
        name="rope_tables",
    )(positions.reshape(bsz, s, 1), invf, sign)


def _qkv0_kernel(x_ref, g_ref, sc_ref, sh_ref, w_ref, cos_ref, sin_ref,
                 q_ref, k_ref, v_ref, hb_scr):
    hb_scr[...] = _rms_mod(x_ref[0], g_ref[...], sc_ref[0], sh_ref[0]).astype(BF16)
    cos = cos_ref[0]
    sin = sin_ref[0]
    lane_lo = (lax.broadcasted_iota(I32, cos.shape, 1) % HEAD_DIM) < (HEAD_DIM // 2)
    nq = A_Q_HEADS * HEAD_DIM
    nkv = A_KV_HEADS * HEAD_DIM
    for c in range(nq // 256):
        y = jnp.dot(hb_scr[...], w_ref[:, 256 * c:256 * (c + 1)], preferred_element_type=F32)
        for cc in range(2):
            r = _rope_apply(y[:, 128 * cc:128 * (cc + 1)], cos, sin, lane_lo) * Q_SCALE
            q_ref[0, :, 256 * c + 128 * cc:256 * c + 128 * (cc + 1)] = r.astype(BF16)
    first_head = lax.broadcasted_iota(I32, cos.shape, 1) < HEAD_DIM
    y = jnp.dot(hb_scr[...], w_ref[:, nq:nq + nkv], preferred_element_type=F32)
    for cc in range(nkv // 128):
        r = _rope_apply(y[:, 128 * cc:128 * (cc + 1)], cos, sin, lane_lo)
        for which in range(2):
            g = 2 * cc + which
            k_ref[0, :, 128 * g:128 * (g + 1)] = _dup_head(r, first_head, which).astype(BF16)
    y = jnp.dot(hb_scr[...], w_ref[:, nq + nkv:nq + 2 * nkv], preferred_element_type=F32)
    for cc in range(nkv // 128):
        r = y[:, 128 * cc:128 * (cc + 1)]
        for which in range(2):
            g = 2 * cc + which
            v_ref[0, :, 128 * g:128 * (g + 1)] = _dup_head(r, first_head, which).astype(BF16)


def _qkv0(x, g, sc, sh, w, cos, sin):
    bsz, s, d = x.shape
    nq = A_Q_HEADS * HEAD_DIM
    nkv = A_KV_HEADS * HEAD_DIM
    tm = TM_WIDE
    tok = lambda w_: pl.BlockSpec((1, tm, w_), lambda b, i: (b, i, 0))
    per_b = pl.BlockSpec((1, 1, d), lambda b, i: (b, 0, 0))
    return pl.pallas_call(
        _qkv0_kernel,
        grid=(bsz, s // tm),
        in_specs=[tok(d), pl.BlockSpec((1, d), lambda b, i: (0, 0)), per_b, per_b,
                  pl.BlockSpec(w.shape, lambda b, i: (0, 0)), tok(128), tok(128)],
        out_specs=[tok(nq), tok(2 * nkv), tok(2 * nkv)],
        out_shape=[jax.ShapeDtypeStruct((bsz, s, nq), BF16),
                   jax.ShapeDtypeStruct((bsz, s, 2 * nkv), BF16),
                   jax.ShapeDtypeStruct((bsz, s, 2 * nkv), BF16)],
        scratch_shapes=[pltpu.VMEM((tm, d), BF16)],
        compiler_params=_cparams("parallel", "parallel"),
        name="qkv0",
    )(x, g.reshape(1, d), sc, sh, w, cos, sin)


def _side_cast_specs(ws, layer, grid):
    nsteps = grid[0] * grid[1]
    in_specs, out_specs, out_shape, args = [], [], [], []
    for w in ws:
        n_exp, k, n = w.shape[1:]
        rows = n_exp * k // nsteps
        assert rows * nsteps == n_exp * k and rows % 16 == 0
        in_specs.append(pl.BlockSpec((rows, n), lambda b, i: (layer * nsteps + b * grid[1] + i, 0)))
        out_specs.append(pl.BlockSpec((rows, n), lambda b, i: (b * grid[1] + i, 0)))
        out_shape.append(jax.ShapeDtypeStruct((n_exp * k, n), BF16))
        args.append(w.reshape(w.shape[0] * n_exp * k, n))
    return in_specs, out_specs, out_shape, args


def _side_cast(srcs, dsts):
    for src, dst in zip(srcs, dsts):
        dst[...] = src[...].astype(BF16)


def _att0_kernel(sink_ref, q_ref, kc_ref, kp_ref, kn_ref, vc_ref, vp_ref, vn_ref, wg_ref, wu_ref, wd_ref,
                 o_ref, wgo_ref, wuo_ref, wdo_ref, kf, vf, *, tq, seq):
    _side_cast((wg_ref, wu_ref, wd_ref), (wgo_ref, wuo_ref, wdo_ref))
    i = pl.program_id(1)
    r = A_RADIUS
    ones = jnp.ones((tq + 2 * r, 128), BF16)
    for g in range(A_KV_HEADS):
        lanes = slice(128 * g, 128 * (g + 1))
        kf[g, 0:r] = kp_ref[0, :, lanes]
        kf[g, r:r + tq] = kc_ref[0, :, lanes]
        kf[g, r + tq:r + tq + r] = kn_ref[0, :, lanes]
        vf[g, 0:r, 0:128] = vp_ref[0, :, lanes]
        vf[g, r:r + tq, 0:128] = vc_ref[0, :, lanes]
        vf[g, r + tq:r + tq + r, 0:128] = vn_ref[0, :, lanes]
        vf[g, :, 128:256] = ones

    ii = lax.broadcasted_iota(I32, (r, r), 0)
    jj = lax.broadcasted_iota(I32, (r, r), 1)
    first_head = jj < HEAD_DIM
    zero = jnp.zeros((r, 128), BF16)

    for j in range(tq // r):
        q0 = j * r
        base = i * tq + q0
        bias_lo = jnp.where((jj >= ii) & (base - r + jj >= 0), 0.0, NEG)
        bias_hi = jnp.where((jj <= ii) & (base + r + jj < seq), 0.0, NEG)
        for g in range(A_KV_HEADS):
            kb = kf[g, pl.ds(q0, 3 * r), :]
            vb = vf[g, pl.ds(q0, 3 * r), :]
            qs = []
            for cc in range(2):
                c = 2 * g + cc
                qc = q_ref[0, pl.ds(q0, r), 128 * c:128 * (c + 1)]
                qs += [jnp.where(first_head, qc, zero), jnp.where(first_head, zero, qc)]
            s4 = lax.dot_general(jnp.concatenate(qs, axis=0), kb, (((1,), (1,)), ((), ())),
                                 preferred_element_type=F32)
            ps, ms = [], []
            for t in range(A_GROUP):
                s = s4[r * t:r * (t + 1)]
                a0 = s[:, 0:r] + bias_lo
                a1 = s[:, r:2 * r]
                a2 = s[:, 2 * r:3 * r] + bias_hi
                sink = sink_ref[A_GROUP * g + t] * LOG2E
                m = jnp.max(jnp.maximum(jnp.maximum(a0, a1), a2), axis=-1, keepdims=True)
                m = jnp.maximum(m, sink)
                ps.append(jnp.concatenate([jnp.exp2(a0 - m), jnp.exp2(a1 - m), jnp.exp2(a2 - m)],
                                          axis=1).astype(BF16))
                ms.append(m)
            pv = jnp.dot(jnp.concatenate(ps, axis=0), vb, preferred_element_type=F32)
            os_ = []
            for t in range(A_GROUP):
                sink = sink_ref[A_GROUP * g + t] * LOG2E
                den = pv[r * t:r * (t + 1), 128:256] + jnp.exp2(sink - ms[t])
                os_.append(pv[r * t:r * (t + 1), 0:128] / den)
            for cc in range(2):
                c = 2 * g + cc
                o_ref[0, pl.ds(q0, r), 128 * c:128 * (c + 1)] = jnp.where(
                    first_head, os_[2 * cc], os_[2 * cc + 1]).astype(BF16)


def _att0(q, k2, v2, sink, expert_ws, layer):
    bsz, s, nq = q.shape
    nkv2 = k2.shape[-1]
    tq = TM_WIDE
    r = A_RADIUS
    per = tq // r
    last = s // r - 1
    grid = (bsz, s // tq)
    cur = lambda w_: pl.BlockSpec((1, tq, w_), lambda b, i: (b, i, 0))
    prev = pl.BlockSpec((1, r, nkv2), lambda b, i: (b, jnp.maximum(i * per - 1, 0), 0))
    nxt = pl.BlockSpec((1, r, nkv2), lambda b, i: (b, jnp.minimum(i * per + per, last), 0))
    w_in, w_out, w_shape, w_args = _side_cast_specs(expert_ws, layer, grid)
    o, *cast = pl.pallas_call(
        functools.partial(_att0_kernel, tq=tq, seq=s),
        grid=grid,
        in_specs=[pl.BlockSpec(memory_space=pltpu.SMEM),
                  cur(nq), cur(nkv2), prev, nxt, cur(nkv2), prev, nxt] + w_in,
        out_specs=[cur(nq)] + w_out,
        out_shape=[jax.ShapeDtypeStruct((bsz, s, nq), BF16)] + w_shape,
        scratch_shapes=[pltpu.VMEM((A_KV_HEADS, tq + 2 * r, 128), BF16),
                        pltpu.VMEM((A_KV_HEADS, tq + 2 * r, 256), BF16)],
        compiler_params=_cparams("parallel", "parallel"),
        name="att0",
    )(sink, q, k2, k2, k2, v2, v2, v2, *w_args)
    return o, [c.reshape(w.shape[1:]) for c, w in zip(cast, expert_ws)]


def _post_kernel(o_ref, wo_ref, x_ref, gm_ref, gf_ref, scf_ref, shf_ref, rw_ref, rb_ref, upper_ref,
                 x1_ref, h2_ref, cls_ref, rank_ref, cnt_ref, carry):
    mix = jnp.dot(o_ref[0], wo_ref[...], preferred_element_type=F32)
    x1 = x_ref[0] + gm_ref[0] * mix
    x1_ref[0] = x1
    h2 = _rms_mod(x1, gf_ref[...], scf_ref[0], shf_ref[0]).astype(BF16)
    h2_ref[0] = _pack_bf16_pairs(h2.astype(F32))
    tq = h2.shape[0]
    logits = lax.dot_general(rw_ref[...], h2, (((1,), (1,)), ((), ())), preferred_element_type=F32)
    scores = jax.nn.sigmoid(logits)
    biased = scores + rb_ref[...]
    pj = [biased[8 * j:8 * (j + 1)] for j in range(PER_GROUP)]
    sel = []
    for j in range(PER_GROUP):
        beaten = jnp.zeros((8, tq), F32)
        for j2 in range(PER_GROUP):
            if j2 == j:
                continue
            beats = (pj[j2] > pj[j]) | (pj[j2] == pj[j]) if j2 < j else (pj[j2] > pj[j])
            beaten = beaten + jnp.where(beats, 1.0, 0.0)
        sel.append(beaten < 2.0)
    gscore = jnp.zeros((8, tq), F32)
    for j in range(PER_GROUP):
        gscore = gscore + jnp.where(sel[j], pj[j], 0.0)
    gi = lax.broadcasted_iota(I32, (8, tq), 0).astype(F32)
    gmax = jnp.max(gscore, axis=0, keepdims=True)
    gidx = jnp.min(jnp.where(gscore == gmax, gi, 8.0), axis=0, keepdims=True)
    onehot = gi == gidx
    f = [jnp.max(jnp.where(onehot & sel[j], 1.0, 0.0), axis=0, keepdims=True) > 0.5
         for j in range(PER_GROUP)]
    pair = jnp.where(f[0], jnp.where(f[1], 0.0, jnp.where(f[2], 1.0, 2.0)),
                     jnp.where(f[1], jnp.where(f[2], 3.0, 4.0), 5.0))
    cls = (gidx * float(len(PAIRS)) + pair).astype(I32)
    cls_ref[0] = jnp.broadcast_to(cls, (8, tq))

    @pl.when((pl.program_id(0) == 0) & (pl.program_id(1) == 0))
    def _():
        carry[...] = jnp.zeros_like(carry)

    c = carry[...]
    ranks = []
    for blk in range(tq // RANK_BLOCK):
        cls_b = cls[:, RANK_BLOCK * blk:RANK_BLOCK * (blk + 1)]
        oh = lax.broadcasted_iota(I32, (32, RANK_BLOCK), 0) == cls_b
        ohf = jnp.where(oh, 1.0, 0.0)
        within = jnp.dot(ohf.astype(BF16), upper_ref[...], preferred_element_type=F32)
        ranks.append(jnp.sum(jnp.where(oh, within + c[:, 0:1], 0.0), axis=0, keepdims=True))
        c = c + jnp.sum(ohf, axis=1, keepdims=True)
    rank_ref[0] = jnp.broadcast_to(jnp.concatenate(ranks, axis=1).astype(I32), (8, tq))
    carry[...] = c
    cnt_ref[...] = c


def _post(o, wo, x, gm, gf, scf, shf, rw32, rb32):
    bsz, s, d = x.shape
    do = o.shape[-1]
    tq = TM_WIDE
    nt = s // tq
    tok = lambda w_: pl.BlockSpec((1, tq, w_), lambda b, i: (b, i, 0))
    per_b = pl.BlockSpec((1, 1, d), lambda b, i: (b, 0, 0))
    full = lambda a: pl.BlockSpec(a.shape, lambda b, i: (0,) * a.ndim)
    per_tile = pl.BlockSpec((1, 8, tq), lambda b, i: (b * nt + i, 0, 0))
    gf2 = gf.reshape(1, d)
    upper = (jnp.arange(RANK_BLOCK)[:, None] < jnp.arange(RANK_BLOCK)[None, :]).astype(BF16)
    return pl.pallas_call(
        _post_kernel,
        grid=(bsz, nt),
        in_specs=[tok(do), full(wo), tok(d), per_b, full(gf2), per_b, per_b, full(rw32), full(rb32),
                  full(upper)],
        out_specs=[tok(d), tok(d // 2), per_tile, per_tile, pl.BlockSpec((32, 128), lambda b, i: (0, 0))],
        out_shape=[jax.ShapeDtypeStruct((bsz, s, d), F32),
                   jax.ShapeDtypeStruct((bsz, s, d // 2), U32),
                   jax.ShapeDtypeStruct((bsz * nt, 8, tq), I32),
                   jax.ShapeDtypeStruct((bsz * nt, 8, tq), I32),
                   jax.ShapeDtypeStruct((32, 128), F32)],
        scratch_shapes=[pltpu.VMEM((32, 128), F32)],
        compiler_params=_cparams("arbitrary", "arbitrary"),
        name="post_attention",
    )(o, wo, x, gm, gf2, scf, shf, rw32, rb32, upper)


def _ffn_kernel(e1_ref, e2_ref, nused_ref, hs_ref, *refs):
    ys_ref = refs[-1]
    j = pl.program_id(0)
    tm = FFN_TM

    @pl.when(j * FFN_TILES_PER_STEP < nused_ref[0])
    def _():
        for sub in range(FFN_TILES_PER_STEP):
            wg1, wu1, wd1, rw1, wg2, wu2, wd2, rw2 = refs[8 * sub:8 * (sub + 1)]
            rows = slice(tm * sub, tm * (sub + 1))
            xf = _unpack_bf16_pairs(hs_ref[rows])
            x = xf.astype(BF16)
            gt1 = jnp.dot(x, wg1[0], preferred_element_type=F32)
            up1 = jnp.dot(x, wu1[0], preferred_element_type=F32)
            gt2 = jnp.dot(x, wg2[0], preferred_element_type=F32)
            up2 = jnp.dot(x, wu2[0], preferred_element_type=F32)
            a1 = ((gt1 * jax.nn.sigmoid(gt1)) * up1).astype(BF16)
            y1 = jnp.dot(a1, wd1[0], preferred_element_type=F32)
            a2 = ((gt2 * jax.nn.sigmoid(gt2)) * up2).astype(BF16)
            y2 = jnp.dot(a2, wd2[0], preferred_element_type=F32)
            s1 = jax.nn.sigmoid(jnp.sum(xf * rw1[0].astype(F32), axis=-1, keepdims=True))
            s2 = jax.nn.sigmoid(jnp.sum(xf * rw2[0].astype(F32), axis=-1, keepdims=True))
            den = s1 + s2
            ys_ref[rows] = _pack_bf16_pairs((s1 / den) * y1 + (s2 / den) * y2)

    @pl.when(j * FFN_TILES_PER_STEP >= nused_ref[0])
    def _():
        ys_ref[...] = jnp.zeros_like(ys_ref)


def _ffn(hs, e1, e2, nused, wg, wu, wd, rw3):
    npad, dh = hs.shape
    d = 2 * dh
    per = FFN_TILES_PER_STEP
    rows = per * FFN_TM
    de = wg.shape[-1]
    assert npad % rows == 0
    row = pl.BlockSpec((rows, dh), lambda j, e1, e2, nu: (jnp.minimum(j, (nu[0] - 1) // per), 0))
    w_specs = []
    for sub in range(per):
        for which in range(2):
            pick = lambda j, e1, e2, nu, sub=sub, which=which: ((e2 if which else e1)[per * j + sub], 0, 0)
            w_specs += [pl.BlockSpec((1, d, de), pick), pl.BlockSpec((1, d, de), pick),
                        pl.BlockSpec((1, de, d), pick), pl.BlockSpec((1, 1, d), pick)]
    grid_spec = pltpu.PrefetchScalarGridSpec(
        num_scalar_prefetch=3,
        grid=(npad // rows,),
        in_specs=[row] + w_specs,
        out_specs=pl.BlockSpec((rows, dh), lambda j, e1, e2, nu: (j, 0)),
    )
    return pl.pallas_call(
        _ffn_kernel,
        grid_spec=grid_spec,
        out_shape=jax.ShapeDtypeStruct((npad, dh), U32),
        compiler_params=_cparams("arbitrary"),
        name="expert_ffn",
    )(e1, e2, nused, hs, *([wg, wu, wd, rw3] * (2 * per)))


def _sc_mesh():
    return plsc.VectorSubcoreMesh(core_axis_name="c", subcore_axis_name="s",
                                  num_cores=SC_CORES_V7X, num_subcores=SC_SUBCORES_V7X)


def _sc_steps(n_rows):
    workers = SC_CORES_V7X * SC_SUBCORES_V7X
    per_w = n_rows // workers
    ch = min(SC_ROWS_PER_STEP, per_w // 2)
    n = per_w // ch
    assert per_w * workers == n_rows and n * ch == per_w and n % 2 == 0 and ch % 8 == 0
    return per_w, ch, n


def _sc_scratch(ch, d, dtype):
    return [pltpu.VMEM((2, ch), I32), pltpu.VMEM((2, ch, d), dtype),
            pltpu.SemaphoreType.DMA((2,)), pltpu.SemaphoreType.DMA((2,))]


def _sc_scatter_rows(src, idx, n_out):
    n_in, d = src.shape
    per_w, ch, n = _sc_steps(n_in)

    def body(src_hbm, idx_hbm, out_hbm, idx_v, rows_v, rsem, ssem):
        base = (lax.axis_index("s") * SC_CORES_V7X + lax.axis_index("c")) * per_w

        def read(i, b):
            return pltpu.make_async_copy(src_hbm.at[pl.ds(base + i * ch, ch)], rows_v.at[b], rsem.at[b])

        def scatter(b):
            return pltpu.make_async_copy(rows_v.at[b], out_hbm.at[idx_v.at[b]], ssem.at[b])

        def start_read(i, b):
            pltpu.sync_copy(idx_hbm.at[pl.ds(base + i * ch, ch)], idx_v.at[b])
            read(i, b).start()

        start_read(0, 0)

        @pl.loop(0, n, step=2)
        def _(i):
            for b in (0, 1):
                cur = i + b

                @pl.when(cur + 1 < n)
                def _():
                    @pl.when(cur >= 1)
                    def _():
                        scatter(1 - b).wait()
                    start_read(cur + 1, 1 - b)

                read(cur, b).wait()
                scatter(b).start()

        scatter(0).wait()
        scatter(1).wait()

    return pl.kernel(
        body, mesh=_sc_mesh(),
        out_type=jax.ShapeDtypeStruct((n_out, d), src.dtype),
        scratch_types=_sc_scratch(ch, d, src.dtype),
        name="sc_scatter_rows",
    )(src, idx)


def _sc_gather_rows(table, idx):
    n_out = idx.shape[0]
    d = table.shape[1]
    per_w, ch, n = _sc_steps(n_out)

    def body(table_hbm, idx_hbm, out_hbm, idx_v, rows_v, gsem, wsem):
        base = (lax.axis_index("s") * SC_CORES_V7X + lax.axis_index("c")) * per_w

        def gather(b):
            return pltpu.make_async_copy(table_hbm.at[idx_v.at[b]], rows_v.at[b], gsem.at[b])

        def write(i, b):
            return pltpu.make_async_copy(rows_v.at[b], out_hbm.at[pl.ds(base + i * ch, ch)], wsem.at[b])

        def start_gather(i, b):
            pltpu.sync_copy(idx_hbm.at[pl.ds(base + i * ch, ch)], idx_v.at[b])
            gather(b).start()

        start_gather(0, 0)

        @pl.loop(0, n, step=2)
        def _(i):
            for b in (0, 1):
                cur = i + b

                @pl.when(cur + 1 < n)
                def _():
                    @pl.when(cur >= 1)
                    def _():
                        write(cur - 1, 1 - b).wait()
                    start_gather(cur + 1, 1 - b)

                gather(b).wait()
                write(cur, b).start()

        write(n - 2, 0).wait()
        write(n - 1, 1).wait()

    return pl.kernel(
        body, mesh=_sc_mesh(),
        out_type=jax.ShapeDtypeStruct((n_out, d), table.dtype),
        scratch_types=_sc_scratch(ch, d, table.dtype),
        name="sc_gather_rows",
    )(table, idx)


def _moe(h2p, cls3, rank3, cnt, wg, wu, wd, rw3):
    bsz, s, dh = h2p.shape
    t = bsz * s
    tm = FFN_TM
    cls = cls3[:, 0, :].reshape(t)
    rank = rank3[:, 0, :].reshape(t)
    counts = cnt[:N_CLASSES, 0].astype(I32)
    padded = ((counts + tm - 1) // tm) * tm
    upto = jnp.arange(N_CLASSES)[:, None] >= jnp.arange(N_CLASSES)[None, :]
    ends = jnp.sum(jnp.where(upto, padded[None, :], 0), axis=1)
    base = ends - padded
    pos = jnp.take(base, cls, mode="clip") + rank
    ntiles = t // tm + N_CLASSES
    tile_start = jnp.arange(ntiles, dtype=I32) * tm
    tcls = jnp.minimum(jnp.sum((ends[None, :] <= tile_start[:, None]).astype(I32), axis=1), N_CLASSES - 1)
    nused = (ends[-1:] // tm).astype(I32)
    grp = tcls // len(PAIRS)
    pr = tcls % len(PAIRS)
    first = jnp.array([p[0] for p in PAIRS], I32)
    second = jnp.array([p[1] for p in PAIRS], I32)
    e1 = (grp * PER_GROUP + jnp.take(first, pr, mode="clip")).astype(I32)
    e2 = (grp * PER_GROUP + jnp.take(second, pr, mode="clip")).astype(I32)
    hs = _sc_scatter_rows(h2p.reshape(t, dh), pos, ntiles * tm)
    ys = _ffn(hs, e1, e2, nused, wg, wu, wd, rw3)
    return _sc_gather_rows(ys, pos).reshape(bsz, s, dh)


def _qkv1_kernel(x_ref, y_ref, gfp_ref, g_ref, sc_ref, sh_ref, w_ref, cos_ref, sin_ref,
                 wg_ref, wu_ref, wd_ref, x2_ref, *rest):
    outs = rest[:9]
    _side_cast((wg_ref, wu_ref, wd_ref), rest[9:12])
    hb_scr, ysc, ysc2 = rest[12:]
    tm = x_ref.shape[1]
    x2 = x_ref[0] + gfp_ref[0] * _unpack_bf16_pairs(y_ref[0])
    x2_ref[0] = x2
    hb_scr[...] = _rms_mod(x2, g_ref[...], sc_ref[0], sh_ref[0]).astype(BF16)
    cos = cos_ref[0]
    sin = sin_ref[0]
    lane_lo = (lax.broadcasted_iota(I32, cos.shape, 1) % HEAD_DIM) < (HEAD_DIM // 2)
    for gi, (_, dil) in reversed(list(enumerate(B_GROUPS))):
        for j in range(3):
            c = gi * 3 + j
            y = jnp.dot(hb_scr[...], w_ref[:, B_WIDTH * c:B_WIDTH * (c + 1)], preferred_element_type=F32)
            out = outs[c]
            for cc in range(B_WIDTH // 128):
                lanes = slice(128 * cc, 128 * (cc + 1))
                r = y[:, lanes]
                if j < 2:
                    r = _rope_apply(r, cos, sin, lane_lo)
                if j == 0:
                    r = r * Q_SCALE
                if dil == 1:
                    out[0, 0, :, lanes] = r.astype(BF16)
                    continue
                ysc[cc] = r
                if dil % 16:
                    for rr in range(dil):
                        out[0, rr, :, lanes] = ysc[cc, pl.ds(rr, tm // dil, stride=dil), :].astype(BF16)
                    continue
                quarter = tm // 4
                inner = dil // 4
                for r1 in range(4):
                    ysc2[cc, r1 * quarter:(r1 + 1) * quarter] = ysc[cc, pl.ds(r1, quarter, stride=4), :]
                for r1 in range(4):
                    for r2 in range(inner):
                        out[0, r1 + 4 * r2, :, lanes] = ysc2[
                            cc, pl.ds(r1 * quarter + r2, tm // dil, stride=inner), :].astype(BF16)


def _qkv1(x, y, gfp, g, sc, sh, w, cos, sin, expert_ws, layer):
    bsz, s, d = x.shape
    tm = TM
    grid = (bsz, s // tm)
    w_in, w_out, w_shape, w_args = _side_cast_specs(expert_ws, layer, grid)
    tok = lambda w_: pl.BlockSpec((1, tm, w_), lambda b, i: (b, i, 0))
    per_b = pl.BlockSpec((1, 1, d), lambda b, i: (b, 0, 0))
    out_specs = [tok(d)]
    out_shape = [jax.ShapeDtypeStruct((bsz, s, d), F32)]
    for _, dil in B_GROUPS:
        for _ in range(3):
            out_specs.append(pl.BlockSpec((1, dil, tm // dil, B_WIDTH), lambda b, i: (b, 0, i, 0)))
            out_shape.append(jax.ShapeDtypeStruct((bsz, dil, s // dil, B_WIDTH), BF16))
    outs = pl.pallas_call(
        _qkv1_kernel,
        grid=grid,
        in_specs=[tok(d), tok(d // 2), per_b, pl.BlockSpec((1, d), lambda b, i: (0, 0)), per_b, per_b,
                  pl.BlockSpec(w.shape, lambda b, i: (0, 0)), tok(128), tok(128)] + w_in,
        out_specs=out_specs + w_out,
        out_shape=out_shape + w_shape,
        scratch_shapes=[pltpu.VMEM((tm, d), BF16), pltpu.VMEM((B_WIDTH // 128, tm, 128), F32),
                        pltpu.VMEM((B_WIDTH // 128, tm, 128), F32)],
        compiler_params=_cparams("parallel", "parallel"),
        name="qkv1",
    )(x, y, gfp, g.reshape(1, d), sc, sh, w, cos, sin, *w_args)
    return outs[:10], [c.reshape(w_.shape[1:]) for c, w_ in zip(outs[10:], expert_ws)]


def _att1_window_pad(lseg):
    qb = min(2 * B_RADIUS, lseg)
    return -(qb + 2 * B_RADIUS) % 128


def _att1_kernel(*refs, chunk, seq):
    ins = refs[:21]
    o_ref = refs[21]
    scr = refs[22:]
    nat_o, nat_d, nat_m = scr[0:3], scr[3:6], scr[6:9]
    n = pl.program_id(1)
    rad = B_RADIUS
    npair = B_WIDTH // 128

    def window(cur, prev, nxt, res, lo, hi, lseg, lanes):
        parts = []
        if lo < 0:
            parts.append(prev[0, res, :, lanes])
        parts.append(cur[0, res, max(lo, 0):min(hi, lseg), lanes])
        if hi > lseg:
            parts.append(nxt[0, res, :, lanes])
        if hi > lseg + rad:
            parts.append(nxt[0, res, :, lanes])
        return jnp.concatenate(parts, axis=0) if len(parts) > 1 else parts[0]

    for c in range(npair):
        lanes = slice(128 * c, 128 * (c + 1))
        for gi, (_, dil) in enumerate(B_GROUPS):
            q_ref, kc, kp, kn, vc, vp, vn = ins[7 * gi:7 * gi + 7]
            lseg = chunk // dil
            qb = min(2 * rad, lseg)
            kb_n = qb + 2 * rad + _att1_window_pad(lseg)
            lstr = seq // dil
            rel = lax.broadcasted_iota(I32, (qb, kb_n), 1) - lax.broadcasted_iota(I32, (qb, kb_n), 0)
            band_bias = jnp.where((rel >= 0) & (rel <= 2 * rad), 0.0, NEG)
            jrow = lax.broadcasted_iota(I32, (1, kb_n), 1)
            first_head = lax.broadcasted_iota(I32, (qb, 128), 1) < HEAD_DIM
            zero = jnp.zeros((qb, 128), BF16)
            ones = jnp.ones((kb_n, 128), BF16)
            for res in range(dil):
                for q0 in range(0, lseg, qb):
                    apos = n * lseg + q0 - rad + jrow
                    bias = band_bias + jnp.where((apos >= 0) & (apos < lstr), 0.0, NEG)
                    rows = pl.ds(q0, qb) if dil == 1 else pl.ds(res + dil * q0, qb, stride=dil)
                    qc = q_ref[0, res, q0:q0 + qb, lanes]
                    kb = window(kc, kp, kn, res, q0 - rad, q0 - rad + kb_n, lseg, lanes)
                    vb = jnp.concatenate(
                        [window(vc, vp, vn, res, q0 - rad, q0 - rad + kb_n, lseg, lanes), ones], axis=1)
                    qs = jnp.concatenate(
                        [jnp.where(first_head, qc, zero), jnp.where(first_head, zero, qc)], axis=0)
                    s2 = lax.dot_general(qs, kb, (((1,), (1,)), ((), ())), preferred_element_type=F32)
                    ps, ms = [], []
                    for t2 in range(2):
                        s = s2[qb * t2:qb * (t2 + 1)] + bias
                        m = jnp.max(s, axis=-1, keepdims=True)
                        ps.append(jnp.exp2(s - m).astype(BF16))
                        ms.append(m)
                    pv = jnp.dot(jnp.concatenate(ps, axis=0), vb, preferred_element_type=F32)
                    nat_o[gi][rows, :] = jnp.where(first_head, pv[0:qb, 0:128], pv[qb:2 * qb, 0:128])
                    nat_d[gi][rows, :] = jnp.where(first_head, pv[0:qb, 128:256], pv[qb:2 * qb, 128:256])
                    nat_m[gi][rows, :] = jnp.where(first_head, ms[0], ms[1])

        ms = [nat_m[gi][...] for gi in range(3)]
        mx = jnp.maximum(jnp.maximum(ms[0], ms[1]), ms[2])
        w = [jnp.exp2(x - mx) for x in ms]
        num = w[0] * nat_o[0][...] + w[1] * nat_o[1][...] + w[2] * nat_o[2][...]
        den = w[0] * nat_d[0][...] + w[1] * nat_d[1][...] + w[2] * nat_d[2][...]
        o_ref[0, :, lanes] = (num / den).astype(BF16)


def _att1(qkv):
    bsz = qkv[0].shape[0]
    seq = qkv[0].shape[1] * qkv[0].shape[2]
    chunk = ATT1_CHUNK
    rad = B_RADIUS
    in_specs, args, scratch = [], [], []
    for gi, (_, dil) in enumerate(B_GROUPS):
        q, k, v = qkv[3 * gi:3 * gi + 3]
        lseg = chunk // dil
        per = lseg // rad
        last = seq // dil // rad - 1
        cur = pl.BlockSpec((1, dil, lseg, B_WIDTH), lambda b, n: (b, 0, n, 0))
        prev = pl.BlockSpec((1, dil, rad, B_WIDTH),
                            lambda b, n, per=per: (b, 0, jnp.maximum(n * per - 1, 0), 0))
        nxt = pl.BlockSpec((1, dil, rad, B_WIDTH),
                           lambda b, n, per=per, last=last: (b, 0, jnp.minimum(n * per + per, last), 0))
        in_specs += [cur, cur, prev, nxt, cur, prev, nxt]
        args += [q, k, k, k, v, v, v]
    for _ in range(3):
        for _ in B_GROUPS:
            scratch.append(pltpu.VMEM((chunk, 128), F32))
    return pl.pallas_call(
        functools.partial(_att1_kernel, chunk=chunk, seq=seq),
        grid=(bsz, seq // chunk),
        in_specs=in_specs,
        out_specs=pl.BlockSpec((1, chunk, B_WIDTH), lambda b, n: (b, n, 0)),
        out_shape=jax.ShapeDtypeStruct((bsz, seq, B_WIDTH), BF16),
        scratch_shapes=scratch,
        compiler_params=_cparams("parallel", "parallel"),
        name="att1",
    )(*args)


def _final_kernel(x_ref, y_ref, gfp_ref, g_ref, o_ref):
    x = x_ref[0] + gfp_ref[0] * _unpack_bf16_pairs(y_ref[0])
    ms = jnp.mean(x * x, axis=-1, keepdims=True)
    o_ref[0] = (x * lax.rsqrt(ms + NORM_EPS)) * g_ref[...]


def _final(x, y, gfp, g):
    bsz, s, d = x.shape
    tm = TM_WIDE
    tok = pl.BlockSpec((1, tm, d), lambda b, i: (b, i, 0))
    return pl.pallas_call(
        _final_kernel,
        grid=(bsz, s // tm),
        in_specs=[tok, pl.BlockSpec((1, tm, d // 2), lambda b, i: (b, i, 0)),
                  pl.BlockSpec((1, 1, d), lambda b, i: (b, 0, 0)),
                  pl.BlockSpec((1, d), lambda b, i: (0, 0))],
        out_specs=tok,
        out_shape=jax.ShapeDtypeStruct((bsz, s, d), F32),
        compiler_params=_cparams("parallel", "parallel"),
        name="final_norm",
    )(x, y, gfp, g.reshape(1, d))


def kernel(x, c, positions, ada_w, ada_b, norm_mix_g, norm_ffn_g, a_w_qkv, a_w_o, a_sink, b_w_qkv, b_w_o,
           router_w, router_bias, exp_w_gate, exp_w_up, exp_w_down, final_norm_g):
    bsz, s, d = x.shape
    assert d == D_MODEL and ada_w.shape[0] == 2 and s % ATT1_CHUNK == 0
    assert all(w_ // (2 * dil) == B_RADIUS for w_, dil in B_GROUPS)

    mod = _modulation(c, ada_w, ada_b)
    mods = [[mod[l][:, k * d:(k + 1) * d].reshape(bsz, 1, d) for k in range(6)] for l in range(2)]
    cos, sin = _rope_tables(positions)

    rw_t = router_w.T.astype(BF16)
    rows = jnp.array([PER_GROUP * g + j if g < N_EXPERT_GROUPS else 0
                      for j in range(PER_GROUP) for g in range(8)], I32)
    live = jnp.array([1.0 if g < N_EXPERT_GROUPS else 0.0
                      for j in range(PER_GROUP) for g in range(8)], F32)
    rw32 = (rw_t[rows].astype(F32) * live[:, None]).astype(BF16)
    rb32 = jnp.where(live > 0, router_bias.astype(F32)[rows], NEG).reshape(32, 1)
    rw3 = rw_t.reshape(N_EXPERTS, 1, d)
    expert_ws = (exp_w_gate, exp_w_up, exp_w_down)

    sh_m, sc_m, g_m, sh_f, sc_f, g_f = mods[0]
    q, k, v = _qkv0(x, norm_mix_g[0], sc_m, sh_m, a_w_qkv[0].astype(BF16), cos, sin)
    o, experts = _att0(q, k, v, a_sink[0].astype(F32), expert_ws, 0)
    x1, *routed = _post(o, a_w_o[0].astype(BF16), x, g_m, norm_ffn_g[0], sc_f, sh_f, rw32, rb32)
    y = _moe(*routed, *experts, rw3)
    g_f_prev = g_f

    sh_m, sc_m, g_m, sh_f, sc_f, g_f = mods[1]
    outs, experts = _qkv1(x1, y, g_f_prev, norm_mix_g[1], sc_m, sh_m, b_w_qkv[0].astype(BF16), cos, sin,
                          expert_ws, 1)
    x2, qkv = outs[0], outs[1:]
    o = _att1(qkv)
    x3, *routed = _post(o, b_w_o[0].astype(BF16), x2, g_m, norm_ffn_g[1], sc_f, sh_f, rw32, rb32)
    y = _moe(*routed, *experts, rw3)

    return _final(x3, y, g_f, final_norm_g)
```

```python
import functools

import jax
import jax.numpy as jnp
from jax import lax
from jax.experimental import pallas as pl
from jax.experimental.pallas import tpu as pltpu
from jax.experimental.pallas import tpu_sc as plsc

F32, BF16, I32, U32 = jnp.float32, jnp.bfloat16, jnp.int32, jnp.uint32

D_MODEL = 1024
HEAD_DIM = 64
ROPE_THETA = 10000.0
NORM_EPS = 1e-6
A_Q_HEADS = 16
A_KV_HEADS = 4
A_GROUP = A_Q_HEADS // A_KV_HEADS
A_RADIUS = 128
B_GROUPS = ((128, 1), (512, 4), (2048, 16))
B_HEADS = 8
B_RADIUS = 64
B_WIDTH = B_HEADS * HEAD_DIM
N_EXPERTS = 16
N_EXPERT_GROUPS = 4
PER_GROUP = N_EXPERTS // N_EXPERT_GROUPS
D_EXPERT = D_MODEL // 2
PAIRS = ((0, 1), (0, 2), (0, 3), (1, 2), (1, 3), (2, 3))
N_CLASSES = N_EXPERT_GROUPS * len(PAIRS)
NEG = -1e30
LOG2E = 1.4426950408889634
Q_SCALE = HEAD_DIM ** -0.5 * LOG2E

VMEM_LIMIT = 56 * 1024 * 1024
TM = 512
TM_WIDE = 1024
RANK_BLOCK = 512
ATT1_CHUNK = 1024
FFN_TM = 256
FFN_TILES_PER_STEP = 2
SC_CORES_V7X = 2
SC_SUBCORES_V7X = 16
SC_ROWS_PER_STEP = 64


def _cparams(*sem):
    return pltpu.CompilerParams(dimension_semantics=sem, vmem_limit_bytes=VMEM_LIMIT)


def _rms_mod(x, g, sc, sh):
    ms = jnp.mean(x * x, axis=-1, keepdims=True)
    return (x * lax.rsqrt(ms + NORM_EPS)) * (g * (1.0 + sc)) + sh


def _rope_apply(y, cos, sin_signed, lane_lo):
    sw = jnp.where(lane_lo, pltpu.roll(y, 96, 1), pltpu.roll(y, 32, 1))
    return y * cos + sw * sin_signed


def _pack_bf16_pairs(x):
    n = x.shape[1] // 2
    lo = lax.bitcast_convert_type(x[:, :n].astype(BF16).astype(F32), U32)
    hi = lax.bitcast_convert_type(x[:, n:].astype(BF16).astype(F32), U32)
    return (hi & jnp.uint32(0xFFFF0000)) | (lo >> 16)


def _unpack_bf16_pairs(p):
    lo = lax.bitcast_convert_type(p << 16, F32)
    hi = lax.bitcast_convert_type(p & jnp.uint32(0xFFFF0000), F32)
    return jnp.concatenate([lo, hi], axis=1)


def _dup_head(x, first_head, which):
    rolled = pltpu.roll(x, HEAD_DIM, 1)
    return jnp.where(first_head, x, rolled) if which == 0 else jnp.where(first_head, rolled, x)


def _mod_kernel(c_ref, w_ref, b_ref, o_ref):
    c = c_ref[...]
    s = c * jax.nn.sigmoid(c)
    o_ref[0] = jnp.dot(s, w_ref[0], preferred_element_type=F32,
                       precision=lax.Precision.HIGHEST) + b_ref[0]


def _modulation(c, ada_w, ada_b):
    depth, d, n = ada_w.shape
    bsz = c.shape[0]
    cp = jnp.zeros((8, d), F32).at[:bsz].set(c)
    tn = 1536
    out = pl.pallas_call(
        _mod_kernel,
        grid=(depth, n // tn),
        in_specs=[pl.BlockSpec((8, d), lambda l, j: (0, 0)),
                  pl.BlockSpec((1, d, tn), lambda l, j: (l, 0, j)),
                  pl.BlockSpec((1, 1, tn), lambda l, j: (l, 0, j))],
        out_specs=pl.BlockSpec((1, 8, tn), lambda l, j: (l, 0, j)),
        out_shape=jax.ShapeDtypeStruct((depth, 8, n), F32),
        compiler_params=_cparams("arbitrary", "arbitrary"),
        name="modulation",
    )(cp, ada_w, ada_b.reshape(depth, 1, n))
    return out[:, :bsz]


def _rope_kernel(pos_ref, invf_ref, sign_ref, cos_ref, sin_ref):
    ang = pos_ref[0].astype(F32) * invf_ref[...]
    cos_ref[0] = jnp.cos(ang)
    sin_ref[0] = jnp.sin(ang) * sign_ref[...]


def _rope_tables(positions):
    bsz, s = positions.shape
    inv_freq = ROPE_THETA ** (-jnp.arange(0, HEAD_DIM, 2, dtype=F32) / HEAD_DIM)
    invf = jnp.tile(inv_freq, 4).reshape(1, 128)
    half = HEAD_DIM // 2
    sign = jnp.tile(jnp.concatenate([-jnp.ones((half,), F32), jnp.ones((half,), F32)]), 2).reshape(1, 128)
    ts = TM
    return pl.pallas_call(
        _rope_kernel,
        grid=(bsz, s // ts),
        in_specs=[pl.BlockSpec((1, ts, 1), lambda b, i: (b, i, 0)),
                  pl.BlockSpec((1, 128), lambda b, i: (0, 0)),
                  pl.BlockSpec((1, 128), lambda b, i: (0, 0))],
        out_specs=[pl.BlockSpec((1, ts, 128), lambda b, i: (b, i, 0))] * 2,
        out_shape=[jax.ShapeDtypeStruct((bsz, s, 128), F32)] * 2,
        compiler_params=_cparams("parallel", "parallel"),
        name="rope_tables",
    )(positions.reshape(bsz, s, 1), invf, sign)


def _qkv0_kernel(x_ref, g_ref, sc_ref, sh_ref, w_ref, cos_ref, sin_ref,
                 q_ref, k_ref, v_ref, hb_scr):
    hb_scr[...] = _rms_mod(x_ref[0], g_ref[...], sc_ref[0], sh_ref[0]).astype(BF16)
    cos = cos_ref[0]
    sin = sin_ref[0]
    lane_lo = (lax.broadcasted_iota(I32, cos.shape, 1) % HEAD_DIM) < (HEAD_DIM // 2)
    nq = A_Q_HEADS * HEAD_DIM
    nkv = A_KV_HEADS * HEAD_DIM
    for c in range(nq // 256):
        y = jnp.dot(hb_scr[...], w_ref[:, 256 * c:256 * (c + 1)], preferred_element_type=F32)
        for cc in range(2):
            r = _rope_apply(y[:, 128 * cc:128 * (cc + 1)], cos, sin, lane_lo) * Q_SCALE
            q_ref[0, :, 256 * c + 128 * cc:256 * c + 128 * (cc + 1)] = r.astype(BF16)
    first_head = lax.broadcasted_iota(I32, cos.shape, 1) < HEAD_DIM
    y = jnp.dot(hb_scr[...], w_ref[:, nq:nq + nkv], preferred_element_type=F32)
    for cc in range(nkv // 128):
        r = _rope_apply(y[:, 128 * cc:128 * (cc + 1)], cos, sin, lane_lo)
        for which in range(2):
            g = 2 * cc + which
            k_ref[0, :, 128 * g:128 * (g + 1)] = _dup_head(r, first_head, which).astype(BF16)
    y = jnp.dot(hb_scr[...], w_ref[:, nq + nkv:nq + 2 * nkv], preferred_element_type=F32)
    for cc in range(nkv // 128):
        r = y[:, 128 * cc:128 * (cc + 1)]
        for which in range(2):
            g = 2 * cc + which
            v_ref[0, :, 128 * g:128 * (g + 1)] = _dup_head(r, first_head, which).astype(BF16)


def _qkv0(x, g, sc, sh, w, cos, sin):
    bsz, s, d = x.shape
    nq = A_Q_HEADS * HEAD_DIM
    nkv = A_KV_HEADS * HEAD_DIM
    tm = TM_WIDE
    tok = lambda w_: pl.BlockSpec((1, tm, w_), lambda b, i: (b, i, 0))
    per_b = pl.BlockSpec((1, 1, d), lambda b, i: (b, 0, 0))
    return pl.pallas_call(
        _qkv0_kernel,
        grid=(bsz, s // tm),
        in_specs=[tok(d), pl.BlockSpec((1, d), lambda b, i: (0, 0)), per_b, per_b,
                  pl.BlockSpec(w.shape, lambda b, i: (0, 0)), tok(128), tok(128)],
        out_specs=[tok(nq), tok(2 * nkv), tok(2 * nkv)],
        out_shape=[jax.ShapeDtypeStruct((bsz, s, nq), BF16),
                   jax.ShapeDtypeStruct((bsz, s, 2 * nkv), BF16),
                   jax.ShapeDtypeStruct((bsz, s, 2 * nkv), BF16)],
        scratch_shapes=[pltpu.VMEM((tm, d), BF16)],
        compiler_params=_cparams("parallel", "parallel"),
        name="qkv0",
    )(x, g.reshape(1, d), sc, sh, w, cos, sin)


def _side_cast_specs(ws, layer, grid):
    nsteps = grid[0] * grid[1]
    in_specs, out_specs, out_shape, args = [], [], [], []
    for w in ws:
        n_exp, k, n = w.shape[1:]
        rows = n_exp * k // nsteps
        assert rows * nsteps == n_exp * k and rows % 16 == 0
        in_specs.append(pl.BlockSpec((rows, n), lambda b, i: (layer * nsteps + b * grid[1] + i, 0)))
        out_specs.append(pl.BlockSpec((rows, n), lambda b, i: (b * grid[1] + i, 0)))
        out_shape.append(jax.ShapeDtypeStruct((n_exp * k, n), BF16))
        args.append(w.reshape(w.shape[0] * n_exp * k, n))
    return in_specs, out_specs, out_shape, args


def _side_cast(srcs, dsts):
    for src, dst in zip(srcs, dsts):
        dst[...] = src[...].astype(BF16)


def _att0_kernel(sink_ref, q_ref, kc_ref, kp_ref, kn_ref, vc_ref, vp_ref, vn_ref, wg_ref, wu_ref, wd_ref,
                 o_ref, wgo_ref, wuo_ref, wdo_ref, kf, vf, *, tq, seq):
    _side_cast((wg_ref, wu_ref, wd_ref), (wgo_ref, wuo_ref, wdo_ref))
    i = pl.program_id(1)
    r = A_RADIUS
    ones = jnp.ones((tq + 2 * r, 128), BF16)
    for g in range(A_KV_HEADS):
        lanes = slice(128 * g, 128 * (g + 1))
        kf[g, 0:r] = kp_ref[0, :, lanes]
        kf[g, r:r + tq] = kc_ref[0, :, lanes]
        kf[g, r + tq:r + tq + r] = kn_ref[0, :, lanes]
        vf[g, 0:r, 0:128] = vp_ref[0, :, lanes]
        vf[g, r:r + tq, 0:128] = vc_ref[0, :, lanes]
        vf[g, r + tq:r + tq + r, 0:128] = vn_ref[0, :, lanes]
        vf[g, :, 128:256] = ones

    ii = lax.broadcasted_iota(I32, (r, r), 0)
    jj = lax.broadcasted_iota(I32, (r, r), 1)
    first_head = jj < HEAD_DIM
    zero = jnp.zeros((r, 128), BF16)

    for j in range(tq // r):
        q0 = j * r
        base = i * tq + q0
        bias_lo = jnp.where((jj >= ii) & (base - r + jj >= 0), 0.0, NEG)
        bias_hi = jnp.where((jj <= ii) & (base + r + jj < seq), 0.0, NEG)
        for g in range(A_KV_HEADS):
            kb = kf[g, pl.ds(q0, 3 * r), :]
            vb = vf[g, pl.ds(q0, 3 * r), :]
            qs = []
            for cc in range(2):
                c = 2 * g + cc
                qc = q_ref[0, pl.ds(q0, r), 128 * c:128 * (c + 1)]
                qs += [jnp.where(first_head, qc, zero), jnp.where(first_head, zero, qc)]
            s4 = lax.dot_general(jnp.concatenate(qs, axis=0), kb, (((1,), (1,)), ((), ())),
                                 preferred_element_type=F32)
            ps, ms = [], []
            for t in range(A_GROUP):
                s = s4[r * t:r * (t + 1)]
                a0 = s[:, 0:r] + bias_lo
                a1 = s[:, r:2 * r]
                a2 = s[:, 2 * r:3 * r] + bias_hi
                sink = sink_ref[A_GROUP * g + t] * LOG2E
                m = jnp.max(jnp.maximum(jnp.maximum(a0, a1), a2), axis=-1, keepdims=True)
                m = jnp.maximum(m, sink)
                ps.append(jnp.exp2(jnp.concatenate([a0 - m, a1 - m, a2 - m], axis=1).astype(BF16)))
                ms.append(m)
            pv = jnp.dot(jnp.concatenate(ps, axis=0), vb, preferred_element_type=F32)
            os_ = []
            for t in range(A_GROUP):
                sink = sink_ref[A_GROUP * g + t] * LOG2E
                den = pv[r * t:r * (t + 1), 128:256] + jnp.exp2(sink - ms[t])
                os_.append(pv[r * t:r * (t + 1), 0:128] / den)
            for cc in range(2):
                c = 2 * g + cc
                o_ref[0, pl.ds(q0, r), 128 * c:128 * (c + 1)] = jnp.where(
                    first_head, os_[2 * cc], os_[2 * cc + 1]).astype(BF16)


def _att0(q, k2, v2, sink, expert_ws, layer):
    bsz, s, nq = q.shape
    nkv2 = k2.shape[-1]
    tq = TM_WIDE
    r = A_RADIUS
    per = tq // r
    last = s // r - 1
    grid = (bsz, s // tq)
    cur = lambda w_: pl.BlockSpec((1, tq, w_), lambda b, i: (b, i, 0))
    prev = pl.BlockSpec((1, r, nkv2), lambda b, i: (b, jnp.maximum(i * per - 1, 0), 0))
    nxt = pl.BlockSpec((1, r, nkv2), lambda b, i: (b, jnp.minimum(i * per + per, last), 0))
    w_in, w_out, w_shape, w_args = _side_cast_specs(expert_ws, layer, grid)
    o, *cast = pl.pallas_call(
        functools.partial(_att0_kernel, tq=tq, seq=s),
        grid=grid,
        in_specs=[pl.BlockSpec(memory_space=pltpu.SMEM),
                  cur(nq), cur(nkv2), prev, nxt, cur(nkv2), prev, nxt] + w_in,
        out_specs=[cur(nq)] + w_out,
        out_shape=[jax.ShapeDtypeStruct((bsz, s, nq), BF16)] + w_shape,
        scratch_shapes=[pltpu.VMEM((A_KV_HEADS, tq + 2 * r, 128), BF16),
                        pltpu.VMEM((A_KV_HEADS, tq + 2 * r, 256), BF16)],
        compiler_params=_cparams("parallel", "parallel"),
        name="att0",
    )(sink, q, k2, k2, k2, v2, v2, v2, *w_args)
    return o, [c.reshape(w.shape[1:]) for c, w in zip(cast, expert_ws)]


def _post_kernel(o_ref, wo_ref, x_ref, gm_ref, gf_ref, scf_ref, shf_ref, rw_ref, rb_ref, upper_ref,
                 x1_ref, h2_ref, cls_ref, rank_ref, cnt_ref, carry):
    mix = jnp.dot(o_ref[0], wo_ref[...], preferred_element_type=F32)
    x1 = x_ref[0] + gm_ref[0] * mix
    x1_ref[0] = x1
    h2 = _rms_mod(x1, gf_ref[...], scf_ref[0], shf_ref[0]).astype(BF16)
    h2_ref[0] = _pack_bf16_pairs(h2.astype(F32))
    tq = h2.shape[0]
    logits = lax.dot_general(rw_ref[...], h2, (((1,), (1,)), ((), ())), preferred_element_type=F32)
    scores = jax.nn.sigmoid(logits)
    biased = scores + rb_ref[...]
    pj = [biased[8 * j:8 * (j + 1)] for j in range(PER_GROUP)]
    sel = []
    for j in range(PER_GROUP):
        beaten = jnp.zeros((8, tq), F32)
        for j2 in range(PER_GROUP):
            if j2 == j:
                continue
            beats = (pj[j2] > pj[j]) | (pj[j2] == pj[j]) if j2 < j else (pj[j2] > pj[j])
            beaten = beaten + jnp.where(beats, 1.0, 0.0)
        sel.append(beaten < 2.0)
    gscore = jnp.zeros((8, tq), F32)
    for j in range(PER_GROUP):
        gscore = gscore + jnp.where(sel[j], pj[j], 0.0)
    gi = lax.broadcasted_iota(I32, (8, tq), 0).astype(F32)
    gmax = jnp.max(gscore, axis=0, keepdims=True)
    gidx = jnp.min(jnp.where(gscore == gmax, gi, 8.0), axis=0, keepdims=True)
    onehot = gi == gidx
    f = [jnp.max(jnp.where(onehot & sel[j], 1.0, 0.0), axis=0, keepdims=True) > 0.5
         for j in range(PER_GROUP)]
    pair = jnp.where(f[0], jnp.where(f[1], 0.0, jnp.where(f[2], 1.0, 2.0)),
                     jnp.where(f[1], jnp.where(f[2], 3.0, 4.0), 5.0))
    cls = (gidx * float(len(PAIRS)) + pair).astype(I32)
    cls_ref[0] = jnp.broadcast_to(cls, (8, tq))

    @pl.when((pl.program_id(0) == 0) & (pl.program_id(1) == 0))
    def _():
        carry[...] = jnp.zeros_like(carry)

    c = carry[...]
    ranks = []
    for blk in range(tq // RANK_BLOCK):
        cls_b = cls[:, RANK_BLOCK * blk:RANK_BLOCK * (blk + 1)]
        oh = lax.broadcasted_iota(I32, (32, RANK_BLOCK), 0) == cls_b
        ohf = jnp.where(oh, 1.0, 0.0)
        within = jnp.dot(ohf.astype(BF16), upper_ref[...], preferred_element_type=F32)
        ranks.append(jnp.sum(jnp.where(oh, within + c[:, 0:1], 0.0), axis=0, keepdims=True))
        c = c + jnp.sum(ohf, axis=1, keepdims=True)
    rank_ref[0] = jnp.broadcast_to(jnp.concatenate(ranks, axis=1).astype(I32), (8, tq))
    carry[...] = c
    cnt_ref[...] = c


def _post(o, wo, x, gm, gf, scf, shf, rw32, rb32):
    bsz, s, d = x.shape
    do = o.shape[-1]
    tq = TM_WIDE
    nt = s // tq
    tok = lambda w_: pl.BlockSpec((1, tq, w_), lambda b, i: (b, i, 0))
    per_b = pl.BlockSpec((1, 1, d), lambda b, i: (b, 0, 0))
    full = lambda a: pl.BlockSpec(a.shape, lambda b, i: (0,) * a.ndim)
    per_tile = pl.BlockSpec((1, 8, tq), lambda b, i: (b * nt + i, 0, 0))
    gf2 = gf.reshape(1, d)
    upper = (jnp.arange(RANK_BLOCK)[:, None] < jnp.arange(RANK_BLOCK)[None, :]).astype(BF16)
    return pl.pallas_call(
        _post_kernel,
        grid=(bsz, nt),
        in_specs=[tok(do), full(wo), tok(d), per_b, full(gf2), per_b, per_b, full(rw32), full(rb32),
                  full(upper)],
        out_specs=[tok(d), tok(d // 2), per_tile, per_tile, pl.BlockSpec((32, 128), lambda b, i: (0, 0))],
        out_shape=[jax.ShapeDtypeStruct((bsz, s, d), F32),
                   jax.ShapeDtypeStruct((bsz, s, d // 2), U32),
                   jax.ShapeDtypeStruct((bsz * nt, 8, tq), I32),
                   jax.ShapeDtypeStruct((bsz * nt, 8, tq), I32),
                   jax.ShapeDtypeStruct((32, 128), F32)],
        scratch_shapes=[pltpu.VMEM((32, 128), F32)],
        compiler_params=_cparams("arbitrary", "arbitrary"),
        name="post_attention",
    )(o, wo, x, gm, gf2, scf, shf, rw32, rb32, upper)


def _ffn_kernel(e1_ref, e2_ref, nused_ref, hs_ref, *refs):
    ys_ref = refs[-1]
    j = pl.program_id(0)
    tm = FFN_TM

    @pl.when(j * FFN_TILES_PER_STEP < nused_ref[0])
    def _():
        for sub in range(FFN_TILES_PER_STEP):
            wg1, wu1, wd1, rw1, wg2, wu2, wd2, rw2 = refs[8 * sub:8 * (sub + 1)]
            rows = slice(tm * sub, tm * (sub + 1))
            xf = _unpack_bf16_pairs(hs_ref[rows])
            x = xf.astype(BF16)
            gt1 = jnp.dot(x, wg1[0], preferred_element_type=F32)
            up1 = jnp.dot(x, wu1[0], preferred_element_type=F32)
            gt2 = jnp.dot(x, wg2[0], preferred_element_type=F32)
            up2 = jnp.dot(x, wu2[0], preferred_element_type=F32)
            a1 = ((gt1 * jax.nn.sigmoid(gt1)) * up1).astype(BF16)
            y1 = jnp.dot(a1, wd1[0], preferred_element_type=F32)
            a2 = ((gt2 * jax.nn.sigmoid(gt2)) * up2).astype(BF16)
            y2 = jnp.dot(a2, wd2[0], preferred_element_type=F32)
            s1 = jax.nn.sigmoid(jnp.sum(xf * rw1[0].astype(F32), axis=-1, keepdims=True))
            s2 = jax.nn.sigmoid(jnp.sum(xf * rw2[0].astype(F32), axis=-1, keepdims=True))
            den = s1 + s2
            ys_ref[rows] = _pack_bf16_pairs((s1 / den) * y1 + (s2 / den) * y2)

    @pl.when(j * FFN_TILES_PER_STEP >= nused_ref[0])
    def _():
        ys_ref[...] = jnp.zeros_like(ys_ref)


def _ffn(hs, e1, e2, nused, wg, wu, wd, rw3):
    npad, dh = hs.shape
    d = 2 * dh
    per = FFN_TILES_PER_STEP
    rows = per * FFN_TM
    de = wg.shape[-1]
    assert npad % rows == 0
    row = pl.BlockSpec((rows, dh), lambda j, e1, e2, nu: (jnp.minimum(j, (nu[0] - 1) // per), 0))
    w_specs = []
    for sub in range(per):
        for which in range(2):
            pick = lambda j, e1, e2, nu, sub=sub, which=which: ((e2 if which else e1)[per * j + sub], 0, 0)
            w_specs += [pl.BlockSpec((1, d, de), pick), pl.BlockSpec((1, d, de), pick),
                        pl.BlockSpec((1, de, d), pick), pl.BlockSpec((1, 1, d), pick)]
    grid_spec = pltpu.PrefetchScalarGridSpec(
        num_scalar_prefetch=3,
        grid=(npad // rows,),
        in_specs=[row] + w_specs,
        out_specs=pl.BlockSpec((rows, dh), lambda j, e1, e2, nu: (j, 0)),
    )
    return pl.pallas_call(
        _ffn_kernel,
        grid_spec=grid_spec,
        out_shape=jax.ShapeDtypeStruct((npad, dh), U32),
        compiler_params=_cparams("arbitrary"),
        name="expert_ffn",
    )(e1, e2, nused, hs, *([wg, wu, wd, rw3] * (2 * per)))


def _sc_mesh():
    return plsc.VectorSubcoreMesh(core_axis_name="c", subcore_axis_name="s",
                                  num_cores=SC_CORES_V7X, num_subcores=SC_SUBCORES_V7X)


def _sc_steps(n_rows):
    workers = SC_CORES_V7X * SC_SUBCORES_V7X
    per_w = n_rows // workers
    ch = min(SC_ROWS_PER_STEP, per_w // 2)
    n = per_w // ch
    assert per_w * workers == n_rows and n * ch == per_w and n % 2 == 0 and ch % 8 == 0
    return per_w, ch, n


def _sc_scratch(ch, d, dtype):
    return [pltpu.VMEM((2, ch), I32), pltpu.VMEM((2, ch, d), dtype),
            pltpu.SemaphoreType.DMA((2,)), pltpu.SemaphoreType.DMA((2,))]


def _sc_scatter_rows(src, idx, n_out):
    n_in, d = src.shape
    per_w, ch, n = _sc_steps(n_in)

    def body(src_hbm, idx_hbm, out_hbm, idx_v, rows_v, rsem, ssem):
        base = (lax.axis_index("s") * SC_CORES_V7X + lax.axis_index("c")) * per_w

        def read(i, b):
            return pltpu.make_async_copy(src_hbm.at[pl.ds(base + i * ch, ch)], rows_v.at[b], rsem.at[b])

        def scatter(b):
            return pltpu.make_async_copy(rows_v.at[b], out_hbm.at[idx_v.at[b]], ssem.at[b])

        def start_read(i, b):
            pltpu.sync_copy(idx_hbm.at[pl.ds(base + i * ch, ch)], idx_v.at[b])
            read(i, b).start()

        start_read(0, 0)

        @pl.loop(0, n, step=2)
        def _(i):
            for b in (0, 1):
                cur = i + b

                @pl.when(cur + 1 < n)
                def _():
                    @pl.when(cur >= 1)
                    def _():
                        scatter(1 - b).wait()
                    start_read(cur + 1, 1 - b)

                read(cur, b).wait()
                scatter(b).start()

        scatter(0).wait()
        scatter(1).wait()

    return pl.kernel(
        body, mesh=_sc_mesh(),
        out_type=jax.ShapeDtypeStruct((n_out, d), src.dtype),
        scratch_types=_sc_scratch(ch, d, src.dtype),
        name="sc_scatter_rows",
    )(src, idx)


def _sc_gather_rows(table, idx):
    n_out = idx.shape[0]
    d = table.shape[1]
    per_w, ch, n = _sc_steps(n_out)

    def body(table_hbm, idx_hbm, out_hbm, idx_v, rows_v, gsem, wsem):
        base = (lax.axis_index("s") * SC_CORES_V7X + lax.axis_index("c")) * per_w

        def gather(b):
            return pltpu.make_async_copy(table_hbm.at[idx_v.at[b]], rows_v.at[b], gsem.at[b])

        def write(i, b):
            return pltpu.make_async_copy(rows_v.at[b], out_hbm.at[pl.ds(base + i * ch, ch)], wsem.at[b])

        def start_gather(i, b):
            pltpu.sync_copy(idx_hbm.at[pl.ds(base + i * ch, ch)], idx_v.at[b])
            gather(b).start()

        start_gather(0, 0)

        @pl.loop(0, n, step=2)
        def _(i):
            for b in (0, 1):
                cur = i + b

                @pl.when(cur + 1 < n)
                def _():
                    @pl.when(cur >= 1)
                    def _():
                        write(cur - 1, 1 - b).wait()
                    start_gather(cur + 1, 1 - b)

                gather(b).wait()
                write(cur, b).start()

        write(n - 2, 0).wait()
        write(n - 1, 1).wait()

    return pl.kernel(
        body, mesh=_sc_mesh(),
        out_type=jax.ShapeDtypeStruct((n_out, d), table.dtype),
        scratch_types=_sc_scratch(ch, d, table.dtype),
        name="sc_gather_rows",
    )(table, idx)


def _moe(h2p, cls3, rank3, cnt, wg, wu, wd, rw3):
    bsz, s, dh = h2p.shape
    t = bsz * s
    tm = FFN_TM
    cls = cls3[:, 0, :].reshape(t)
    rank = rank3[:, 0, :].reshape(t)
    counts = cnt[:N_CLASSES, 0].astype(I32)
    padded = ((counts + tm - 1) // tm) * tm
    upto = jnp.arange(N_CLASSES)[:, None] >= jnp.arange(N_CLASSES)[None, :]
    ends = jnp.sum(jnp.where(upto, padded[None, :], 0), axis=1)
    base = ends - padded
    pos = jnp.take(base, cls, mode="clip") + rank
    ntiles = t // tm + N_CLASSES
    tile_start = jnp.arange(ntiles, dtype=I32) * tm
    tcls = jnp.minimum(jnp.sum((ends[None, :] <= tile_start[:, None]).astype(I32), axis=1), N_CLASSES - 1)
    nused = (ends[-1:] // tm).astype(I32)
    grp = tcls // len(PAIRS)
    pr = tcls % len(PAIRS)
    first = jnp.array([p[0] for p in PAIRS], I32)
    second = jnp.array([p[1] for p in PAIRS], I32)
    e1 = (grp * PER_GROUP + jnp.take(first, pr, mode="clip")).astype(I32)
    e2 = (grp * PER_GROUP + jnp.take(second, pr, mode="clip")).astype(I32)
    hs = _sc_scatter_rows(h2p.reshape(t, dh), pos, ntiles * tm)
    ys = _ffn(hs, e1, e2, nused, wg, wu, wd, rw3)
    return _sc_gather_rows(ys, pos).reshape(bsz, s, dh)


def _qkv1_kernel(x_ref, y_ref, gfp_ref, g_ref, sc_ref, sh_ref, w_ref, cos_ref, sin_ref,
                 wg_ref, wu_ref, wd_ref, x2_ref, *rest):
    outs = rest[:9]
    _side_cast((wg_ref, wu_ref, wd_ref), rest[9:12])
    hb_scr, ysc, ysc2 = rest[12:]
    tm = x_ref.shape[1]
    x2 = x_ref[0] + gfp_ref[0] * _unpack_bf16_pairs(y_ref[0])
    x2_ref[0] = x2
    hb_scr[...] = _rms_mod(x2, g_ref[...], sc_ref[0], sh_ref[0]).astype(BF16)
    cos = cos_ref[0]
    sin = sin_ref[0]
    lane_lo = (lax.broadcasted_iota(I32, cos.shape, 1) % HEAD_DIM) < (HEAD_DIM // 2)
    for gi, (_, dil) in reversed(list(enumerate(B_GROUPS))):
        for j in range(3):
            c = gi * 3 + j
            y = jnp.dot(hb_scr[...], w_ref[:, B_WIDTH * c:B_WIDTH * (c + 1)], preferred_element_type=F32)
            out = outs[c]
            for cc in range(B_WIDTH // 128):
                lanes = slice(128 * cc, 128 * (cc + 1))
                r = y[:, lanes]
                if j < 2:
                    r = _rope_apply(r, cos, sin, lane_lo)
                if j == 0:
                    r = r * Q_SCALE
                if dil == 1:
                    out[0, 0, :, lanes] = r.astype(BF16)
                    continue
                ysc[cc] = r
                if dil % 16:
                    for rr in range(dil):
                        out[0, rr, :, lanes] = ysc[cc, pl.ds(rr, tm // dil, stride=dil), :].astype(BF16)
                    continue
                quarter = tm // 4
                inner = dil // 4
                for r1 in range(4):
                    ysc2[cc, r1 * quarter:(r1 + 1) * quarter] = ysc[cc, pl.ds(r1, quarter, stride=4), :]
                for r1 in range(4):
                    for r2 in range(inner):
                        out[0, r1 + 4 * r2, :, lanes] = ysc2[
                            cc, pl.ds(r1 * quarter + r2, tm // dil, stride=inner), :].astype(BF16)


def _qkv1(x, y, gfp, g, sc, sh, w, cos, sin, expert_ws, layer):
    bsz, s, d = x.shape
    tm = TM
    grid = (bsz, s // tm)
    w_in, w_out, w_shape, w_args = _side_cast_specs(expert_ws, layer, grid)
    tok = lambda w_: pl.BlockSpec((1, tm, w_), lambda b, i: (b, i, 0))
    per_b = pl.BlockSpec((1, 1, d), lambda b, i: (b, 0, 0))
    out_specs = [tok(d)]
    out_shape = [jax.ShapeDtypeStruct((bsz, s, d), F32)]
    for _, dil in B_GROUPS:
        for _ in range(3):
            out_specs.append(pl.BlockSpec((1, dil, tm // dil, B_WIDTH), lambda b, i: (b, 0, i, 0)))
            out_shape.append(jax.ShapeDtypeStruct((bsz, dil, s // dil, B_WIDTH), BF16))
    outs = pl.pallas_call(
        _qkv1_kernel,
        grid=grid,
        in_specs=[tok(d), tok(d // 2), per_b, pl.BlockSpec((1, d), lambda b, i: (0, 0)), per_b, per_b,
                  pl.BlockSpec(w.shape, lambda b, i: (0, 0)), tok(128), tok(128)] + w_in,
        out_specs=out_specs + w_out,
        out_shape=out_shape + w_shape,
        scratch_shapes=[pltpu.VMEM((tm, d), BF16), pltpu.VMEM((B_WIDTH // 128, tm, 128), F32),
                        pltpu.VMEM((B_WIDTH // 128, tm, 128), F32)],
        compiler_params=_cparams("parallel", "parallel"),
        name="qkv1",
    )(x, y, gfp, g.reshape(1, d), sc, sh, w, cos, sin, *w_args)
    return outs[:10], [c.reshape(w_.shape[1:]) for c, w_ in zip(outs[10:], expert_ws)]


def _att1_window_pad(lseg):
    qb = min(2 * B_RADIUS, lseg)
    return -(qb + 2 * B_RADIUS) % 128


def _att1_kernel(*refs, chunk, seq):
    ins = refs[:21]
    o_ref = refs[21]
    scr = refs[22:]
    nat_o, nat_d, nat_m = scr[0:3], scr[3:6], scr[6:9]
    n = pl.program_id(1)
    rad = B_RADIUS
    npair = B_WIDTH // 128

    def window(cur, prev, nxt, res, lo, hi, lseg, lanes):
        parts = []
        if lo < 0:
            parts.append(prev[0, res, :, lanes])
        parts.append(cur[0, res, max(lo, 0):min(hi, lseg), lanes])
        if hi > lseg:
            parts.append(nxt[0, res, :, lanes])
        if hi > lseg + rad:
            parts.append(nxt[0, res, :, lanes])
        return jnp.concatenate(parts, axis=0) if len(parts) > 1 else parts[0]

    for c in range(npair):
        lanes = slice(128 * c, 128 * (c + 1))
        for gi, (_, dil) in enumerate(B_GROUPS):
            q_ref, kc, kp, kn, vc, vp, vn = ins[7 * gi:7 * gi + 7]
            lseg = chunk // dil
            qb = min(2 * rad, lseg)
            kb_n = qb + 2 * rad + _att1_window_pad(lseg)
            lstr = seq // dil
            rel = lax.broadcasted_iota(I32, (qb, kb_n), 1) - lax.broadcasted_iota(I32, (qb, kb_n), 0)
            band_bias = jnp.where((rel >= 0) & (rel <= 2 * rad), 0.0, NEG)
            jrow = lax.broadcasted_iota(I32, (1, kb_n), 1)
            first_head = lax.broadcasted_iota(I32, (qb, 128), 1) < HEAD_DIM
            zero = jnp.zeros((qb, 128), BF16)
            ones = jnp.ones((kb_n, 128), BF16)
            for res in range(dil):
                for q0 in range(0, lseg, qb):
                    apos = n * lseg + q0 - rad + jrow
                    bias = band_bias + jnp.where((apos >= 0) & (apos < lstr), 0.0, NEG)
                    rows = pl.ds(q0, qb) if dil == 1 else pl.ds(res + dil * q0, qb, stride=dil)
                    qc = q_ref[0, res, q0:q0 + qb, lanes]
                    kb = window(kc, kp, kn, res, q0 - rad, q0 - rad + kb_n, lseg, lanes)
                    vb = jnp.concatenate(
                        [window(vc, vp, vn, res, q0 - rad, q0 - rad + kb_n, lseg, lanes), ones], axis=1)
                    qs = jnp.concatenate(
                        [jnp.where(first_head, qc, zero), jnp.where(first_head, zero, qc)], axis=0)
                    s2 = lax.dot_general(qs, kb, (((1,), (1,)), ((), ())), preferred_element_type=F32)
                    ps, ms = [], []
                    for t2 in range(2):
                        s = s2[qb * t2:qb * (t2 + 1)] + bias
                        m = jnp.max(s, axis=-1, keepdims=True)
                        ps.append(jnp.exp2((s - m).astype(BF16)))
                        ms.append(m)
                    pv = jnp.dot(jnp.concatenate(ps, axis=0), vb, preferred_element_type=F32)
                    nat_o[gi][rows, :] = jnp.where(first_head, pv[0:qb, 0:128], pv[qb:2 * qb, 0:128])
                    nat_d[gi][rows, :] = jnp.where(first_head, pv[0:qb, 128:256], pv[qb:2 * qb, 128:256])
                    nat_m[gi][rows, :] = jnp.where(first_head, ms[0], ms[1])

        ms = [nat_m[gi][...] for gi in range(3)]
        mx = jnp.maximum(jnp.maximum(ms[0], ms[1]), ms[2])
        w = [jnp.exp2(x - mx) for x in ms]
        num = w[0] * nat_o[0][...] + w[1] * nat_o[1][...] + w[2] * nat_o[2][...]
        den = w[0] * nat_d[0][...] + w[1] * nat_d[1][...] + w[2] * nat_d[2][...]
        o_ref[0, :, lanes] = (num / den).astype(BF16)


def _att1(qkv):
    bsz = qkv[0].shape[0]
    seq = qkv[0].shape[1] * qkv[0].shape[2]
    chunk = ATT1_CHUNK
    rad = B_RADIUS
    in_specs, args, scratch = [], [], []
    for gi, (_, dil) in enumerate(B_GROUPS):
        q, k, v = qkv[3 * gi:3 * gi + 3]
        lseg = chunk // dil
        per = lseg // rad
        last = seq // dil // rad - 1
        cur = pl.BlockSpec((1, dil, lseg, B_WIDTH), lambda b, n: (b, 0, n, 0))
        prev = pl.BlockSpec((1, dil, rad, B_WIDTH),
                            lambda b, n, per=per: (b, 0, jnp.maximum(n * per - 1, 0), 0))
        nxt = pl.BlockSpec((1, dil, rad, B_WIDTH),
                           lambda b, n, per=per, last=last: (b, 0, jnp.minimum(n * per + per, last), 0))
        in_specs += [cur, cur, prev, nxt, cur, prev, nxt]
        args += [q, k, k, k, v, v, v]
    for _ in range(3):
        for _ in B_GROUPS:
            scratch.append(pltpu.VMEM((chunk, 128), F32))
    return pl.pallas_call(
        functools.partial(_att1_kernel, chunk=chunk, seq=seq),
        grid=(bsz, seq // chunk),
        in_specs=in_specs,
        out_specs=pl.BlockSpec((1, chunk, B_WIDTH), lambda b, n: (b, n, 0)),
        out_shape=jax.ShapeDtypeStruct((bsz, seq, B_WIDTH), BF16),
        scratch_shapes=scratch,
        compiler_params=_cparams("parallel", "parallel"),
        name="att1",
    )(*args)


def _final_kernel(x_ref, y_ref, gfp_ref, g_ref, o_ref):
    x = x_ref[0] + gfp_ref[0] * _unpack_bf16_pairs(y_ref[0])
    ms = jnp.mean(x * x, axis=-1, keepdims=True)
    o_ref[0] = (x * lax.rsqrt(ms + NORM_EPS)) * g_ref[...]


def _final(x, y, gfp, g):
    bsz, s, d = x.shape
    tm = TM_WIDE
    tok = pl.BlockSpec((1, tm, d), lambda b, i: (b, i, 0))
    return pl.pallas_call(
        _final_kernel,
        grid=(bsz, s // tm),
        in_specs=[tok, pl.BlockSpec((1, tm, d // 2), lambda b, i: (b, i, 0)),
                  pl.BlockSpec((1, 1, d), lambda b, i: (b, 0, 0)),
                  pl.BlockSpec((1, d), lambda b, i: (0, 0))],
        out_specs=tok,
        out_shape=jax.ShapeDtypeStruct((bsz, s, d), F32),
        compiler_params=_cparams("parallel", "parallel"),
        name="final_norm",
    )(x, y, gfp, g.reshape(1, d))


def kernel(x, c, positions, ada_w, ada_b, norm_mix_g, norm_ffn_g, a_w_qkv, a_w_o, a_sink, b_w_qkv, b_w_o,
           router_w, router_bias, exp_w_gate, exp_w_up, exp_w_down, final_norm_g):
    bsz, s, d = x.shape
    assert d == D_MODEL and ada_w.shape[0] == 2 and s % ATT1_CHUNK == 0
    assert all(w_ // (2 * dil) == B_RADIUS for w_, dil in B_GROUPS)

    mod = _modulation(c, ada_w, ada_b)
    mods = [[mod[l][:, k * d:(k + 1) * d].reshape(bsz, 1, d) for k in range(6)] for l in range(2)]
    cos, sin = _rope_tables(positions)

    rw_t = router_w.T.astype(BF16)
    rows = jnp.array([PER_GROUP * g + j if g < N_EXPERT_GROUPS else 0
                      for j in range(PER_GROUP) for g in range(8)], I32)
    live = jnp.array([1.0 if g < N_EXPERT_GROUPS else 0.0
                      for j in range(PER_GROUP) for g in range(8)], F32)
    rw32 = (rw_t[rows].astype(F32) * live[:, None]).astype(BF16)
    rb32 = jnp.where(live > 0, router_bias.astype(F32)[rows], NEG).reshape(32, 1)
    rw3 = rw_t.reshape(N_EXPERTS, 1, d)
    expert_ws = (exp_w_gate, exp_w_up, exp_w_down)

    sh_m, sc_m, g_m, sh_f, sc_f, g_f = mods[0]
    q, k, v = _qkv0(x, norm_mix_g[0], sc_m, sh_m, a_w_qkv[0].astype(BF16), cos, sin)
    o, experts = _att0(q, k, v, a_sink[0].astype(F32), expert_ws, 0)
    x1, *routed = _post(o, a_w_o[0].astype(BF16), x, g_m, norm_ffn_g[0], sc_f, sh_f, rw32, rb32)
    y = _moe(*routed, *experts, rw3)
    g_f_prev = g_f

    sh_m, sc_m, g_m, sh_f, sc_f, g_f = mods[1]
    outs, experts = _qkv1(x1, y, g_f_prev, norm_mix_g[1], sc_m, sh_m, b_w_qkv[0].astype(BF16), cos, sin,
                          expert_ws, 1)
    x2, qkv = outs[0], outs[1:]
    o = _att1(qkv)
    x3, *routed = _post(o, b_w_o[0].astype(BF16), x2, g_m, norm_ffn_g[1], sc_f, sh_f, rw32, rb32)
    y = _moe(*routed, *experts, rw3)

    return _final(x3, y, g_f, final_norm_g)
```

```python
import functools

import jax
import jax.numpy as jnp
from jax import lax
from jax.experimental import pallas as pl
from jax.experimental.pallas import tpu as pltpu
from jax.experimental.pallas import tpu_sc as plsc

F32, BF16, I32, U32 = jnp.float32, jnp.bfloat16, jnp.int32, jnp.uint32

D_MODEL = 1024
HEAD_DIM = 64
ROPE_THETA = 10000.0
NORM_EPS = 1e-6
A_Q_HEADS = 16
A_KV_HEADS = 4
A_GROUP = A_Q_HEADS // A_KV_HEADS
A_RADIUS = 128
B_GROUPS = ((128, 1), (512, 4), (2048, 16))
B_HEADS = 8
B_RADIUS = 64
B_WIDTH = B_HEADS * HEAD_DIM
N_EXPERTS = 16
N_EXPERT_GROUPS = 4
PER_GROUP = N_EXPERTS // N_EXPERT_GROUPS
D_EXPERT = D_MODEL // 2
PAIRS = ((0, 1), (0, 2), (0, 3), (1, 2), (1, 3), (2, 3))
N_CLASSES = N_EXPERT_GROUPS * len(PAIRS)
NEG = -1e30
LOG2E = 1.4426950408889634
Q_SCALE = HEAD_DIM ** -0.5 * LOG2E

VMEM_LIMIT = 56 * 1024 * 1024
TM = 512
ROPE_TS = 2048
TM_WIDE = 1024
RANK_BLOCK = 512
ATT1_CHUNK = 1024
FFN_TM = 256
FFN_TILES_PER_STEP = 2
SC_CORES_V7X = 2
SC_SUBCORES_V7X = 16
SC_ROWS_PER_STEP = 64


def _cparams(*sem):
    return pltpu.CompilerParams(dimension_semantics=sem, vmem_limit_bytes=VMEM_LIMIT)


def _rms_mod(x, g, sc, sh):
    ms = jnp.mean(x * x, axis=-1, keepdims=True)
    return (x * lax.rsqrt(ms + NORM_EPS)) * (g * (1.0 + sc)) + sh


def _rope_apply(y, cos, sin_signed, lane_lo):
    sw = jnp.where(lane_lo, pltpu.roll(y, 96, 1), pltpu.roll(y, 32, 1))
    return y * cos + sw * sin_signed


def _pack_bf16_pairs(x):
    n = x.shape[1] // 2
    lo = lax.bitcast_convert_type(x[:, :n].astype(BF16).astype(F32), U32)
    hi = lax.bitcast_convert_type(x[:, n:].astype(BF16).astype(F32), U32)
    return (hi & jnp.uint32(0xFFFF0000)) | (lo >> 16)


def _unpack_bf16_pairs(p):
    lo = lax.bitcast_convert_type(p << 16, F32)
    hi = lax.bitcast_convert_type(p & jnp.uint32(0xFFFF0000), F32)
    return jnp.concatenate([lo, hi], axis=1)


def _dup_head(x, first_head, which):
    rolled = pltpu.roll(x, HEAD_DIM, 1)
    return jnp.where(first_head, x, rolled) if which == 0 else jnp.where(first_head, rolled, x)


def _mod_kernel(c_ref, w_ref, b_ref, o_ref):
    c = c_ref[...]
    s = c * jax.nn.sigmoid(c)
    o_ref[0] = jnp.dot(s, w_ref[0], preferred_element_type=F32,
                       precision=lax.Precision.HIGHEST) + b_ref[0]


def _modulation(c, ada_w, ada_b):
    depth, d, n = ada_w.shape
    bsz = c.shape[0]
    cp = jnp.zeros((8, d), F32).at[:bsz].set(c)
    tn = 1536
    out = pl.pallas_call(
        _mod_kernel,
        grid=(depth, n // tn),
        in_specs=[pl.BlockSpec((8, d), lambda l, j: (0, 0)),
                  pl.BlockSpec((1, d, tn), lambda l, j: (l, 0, j)),
                  pl.BlockSpec((1, 1, tn), lambda l, j: (l, 0, j))],
        out_specs=pl.BlockSpec((1, 8, tn), lambda l, j: (l, 0, j)),
        out_shape=jax.ShapeDtypeStruct((depth, 8, n), F32),
        compiler_params=_cparams("arbitrary", "arbitrary"),
        name="modulation",
    )(cp, ada_w, ada_b.reshape(depth, 1, n))
    return out[:, :bsz]


def _rope_kernel(pos_ref, invf_ref, sign_ref, cos_ref, sin_ref):
    ang = pos_ref[0].astype(F32) * invf_ref[...]
    cos_ref[0] = jnp.cos(ang)
    sin_ref[0] = jnp.sin(ang) * sign_ref[...]


def _rope_tables(positions):
    bsz, s = positions.shape
    inv_freq = ROPE_THETA ** (-jnp.arange(0, HEAD_DIM, 2, dtype=F32) / HEAD_DIM)
    invf = jnp.tile(inv_freq, 4).reshape(1, 128)
    half = HEAD_DIM // 2
    sign = jnp.tile(jnp.concatenate([-jnp.ones((half,), F32), jnp.ones((half,), F32)]), 2).reshape(1, 128)
    ts = ROPE_TS
    return pl.pallas_call(
        _rope_kernel,
        grid=(bsz, s // ts),
        in_specs=[pl.BlockSpec((1, ts, 1), lambda b, i: (b, i, 0)),
                  pl.BlockSpec((1, 128), lambda b, i: (0, 0)),
                  pl.BlockSpec((1, 128), lambda b, i: (0, 0))],
        out_specs=[pl.BlockSpec((1, ts, 128), lambda b, i: (b, i, 0))] * 2,
        out_shape=[jax.ShapeDtypeStruct((bsz, s, 128), F32)] * 2,
        compiler_params=_cparams("parallel", "parallel"),
        name="rope_tables",
    )(positions.reshape(bsz, s, 1), invf, sign)


def _qkv0_kernel(x_ref, g_ref, sc_ref, sh_ref, w_ref, cos_ref, sin_ref,
                 q_ref, k_ref, v_ref, hb_scr):
    hb_scr[...] = _rms_mod(x_ref[0], g_ref[...], sc_ref[0], sh_ref[0]).astype(BF16)
    cos = cos_ref[0]
    sin = sin_ref[0]
    lane_lo = (lax.broadcasted_iota(I32, cos.shape, 1) % HEAD_DIM) < (HEAD_DIM // 2)
    nq = A_Q_HEADS * HEAD_DIM
    nkv = A_KV_HEADS * HEAD_DIM
    for c in range(nq // 256):
        y = jnp.dot(hb_scr[...], w_ref[:, 256 * c:256 * (c + 1)], preferred_element_type=F32)
        for cc in range(2):
            r = _rope_apply(y[:, 128 * cc:128 * (cc + 1)], cos, sin, lane_lo) * Q_SCALE
            q_ref[0, :, 256 * c + 128 * cc:256 * c + 128 * (cc + 1)] = r.astype(BF16)
    first_head = lax.broadcasted_iota(I32, cos.shape, 1) < HEAD_DIM
    y = jnp.dot(hb_scr[...], w_ref[:, nq:nq + nkv], preferred_element_type=F32)
    for cc in range(nkv // 128):
        r = _rope_apply(y[:, 128 * cc:128 * (cc + 1)], cos, sin, lane_lo)
        for which in range(2):
            g = 2 * cc + which
            k_ref[0, :, 128 * g:128 * (g + 1)] = _dup_head(r, first_head, which).astype(BF16)
    y = jnp.dot(hb_scr[...], w_ref[:, nq + nkv:nq + 2 * nkv], preferred_element_type=F32)
    for cc in range(nkv // 128):
        r = y[:, 128 * cc:128 * (cc + 1)]
        for which in range(2):
            g = 2 * cc + which
            v_ref[0, :, 128 * g:128 * (g + 1)] = _dup_head(r, first_head, which).astype(BF16)


def _qkv0(x, g, sc, sh, w, cos, sin):
    bsz, s, d = x.shape
    nq = A_Q_HEADS * HEAD_DIM
    nkv = A_KV_HEADS * HEAD_DIM
    tm = TM_WIDE
    tok = lambda w_: pl.BlockSpec((1, tm, w_), lambda b, i: (b, i, 0))
    per_b = pl.BlockSpec((1, 1, d), lambda b, i: (b, 0, 0))
    return pl.pallas_call(
        _qkv0_kernel,
        grid=(bsz, s // tm),
        in_specs=[tok(d), pl.BlockSpec((1, d), lambda b, i: (0, 0)), per_b, per_b,
                  pl.BlockSpec(w.shape, lambda b, i: (0, 0)), tok(128), tok(128)],
        out_specs=[tok(nq), tok(2 * nkv), tok(2 * nkv)],
        out_shape=[jax.ShapeDtypeStruct((bsz, s, nq), BF16),
                   jax.ShapeDtypeStruct((bsz, s, 2 * nkv), BF16),
                   jax.ShapeDtypeStruct((bsz, s, 2 * nkv), BF16)],
        scratch_shapes=[pltpu.VMEM((tm, d), BF16)],
        compiler_params=_cparams("parallel", "parallel"),
        name="qkv0",
    )(x, g.reshape(1, d), sc, sh, w, cos, sin)


def _side_cast_specs(items, grid):
    nsteps = grid[0] * grid[1]
    in_specs, out_specs, out_shape, args = [], [], [], []
    for w, layer in items:
        n = w.shape[-1]
        per_layer = w[0].size // n
        rows = per_layer // nsteps
        assert rows * nsteps == per_layer and rows % 16 == 0
        in_specs.append(pl.BlockSpec(
            (rows, n), lambda b, i, layer=layer: (layer * nsteps + b * grid[1] + i, 0)))
        out_specs.append(pl.BlockSpec((rows, n), lambda b, i: (b * grid[1] + i, 0)))
        out_shape.append(jax.ShapeDtypeStruct((per_layer, n), BF16))
        args.append(w.reshape(w.shape[0] * per_layer, n))
    return in_specs, out_specs, out_shape, args


def _side_cast(srcs, dsts):
    for src, dst in zip(srcs, dsts):
        dst[...] = src[...].astype(BF16)


def _att0_kernel(sink_ref, q_ref, kc_ref, kp_ref, kn_ref, vc_ref, vp_ref, vn_ref, *rest, tq, seq, ncast):
    o_ref = rest[ncast]
    kf, vf = rest[2 * ncast + 1:]
    _side_cast(rest[:ncast], rest[ncast + 1:2 * ncast + 1])
    i = pl.program_id(1)
    r = A_RADIUS
    ones = jnp.ones((tq + 2 * r, 128), BF16)
    for g in range(A_KV_HEADS):
        lanes = slice(128 * g, 128 * (g + 1))
        kf[g, 0:r] = kp_ref[0, :, lanes]
        kf[g, r:r + tq] = kc_ref[0, :, lanes]
        kf[g, r + tq:r + tq + r] = kn_ref[0, :, lanes]
        vf[g, 0:r, 0:128] = vp_ref[0, :, lanes]
        vf[g, r:r + tq, 0:128] = vc_ref[0, :, lanes]
        vf[g, r + tq:r + tq + r, 0:128] = vn_ref[0, :, lanes]
        vf[g, :, 128:256] = ones

    ii = lax.broadcasted_iota(I32, (r, r), 0)
    jj = lax.broadcasted_iota(I32, (r, r), 1)
    first_head = jj < HEAD_DIM
    zero = jnp.zeros((r, 128), BF16)

    for j in range(tq // r):
        q0 = j * r
        base = i * tq + q0
        bias_lo = jnp.where((jj >= ii) & (base - r + jj >= 0), 0.0, NEG)
        bias_hi = jnp.where((jj <= ii) & (base + r + jj < seq), 0.0, NEG)
        for g in range(A_KV_HEADS):
            kb = kf[g, pl.ds(q0, 3 * r), :]
            vb = vf[g, pl.ds(q0, 3 * r), :]
            qs = []
            for cc in range(2):
                c = 2 * g + cc
                qc = q_ref[0, pl.ds(q0, r), 128 * c:128 * (c + 1)]
                qs += [jnp.where(first_head, qc, zero), jnp.where(first_head, zero, qc)]
            s4 = lax.dot_general(jnp.concatenate(qs, axis=0), kb, (((1,), (1,)), ((), ())),
                                 preferred_element_type=F32)
            ps, ms = [], []
            for t in range(A_GROUP):
                s = s4[r * t:r * (t + 1)]
                a0 = s[:, 0:r] + bias_lo
                a1 = s[:, r:2 * r]
                a2 = s[:, 2 * r:3 * r] + bias_hi
                sink = sink_ref[A_GROUP * g + t] * LOG2E
                m = jnp.max(jnp.maximum(jnp.maximum(a0, a1), a2), axis=-1, keepdims=True)
                m = jnp.maximum(m, sink)
                ps.append(jnp.exp2(jnp.concatenate([a0 - m, a1 - m, a2 - m], axis=1).astype(BF16)))
                ms.append(m)
            pv = jnp.dot(jnp.concatenate(ps, axis=0), vb, preferred_element_type=F32)
            os_ = []
            for t in range(A_GROUP):
                sink = sink_ref[A_GROUP * g + t] * LOG2E
                den = pv[r * t:r * (t + 1), 128:256] + jnp.exp2(sink - ms[t])
                os_.append(pv[r * t:r * (t + 1), 0:128] / den)
            for cc in range(2):
                c = 2 * g + cc
                o_ref[0, pl.ds(q0, r), 128 * c:128 * (c + 1)] = jnp.where(
                    first_head, os_[2 * cc], os_[2 * cc + 1]).astype(BF16)


def _att0(q, k2, v2, sink, cast_items):
    bsz, s, nq = q.shape
    nkv2 = k2.shape[-1]
    tq = TM_WIDE
    r = A_RADIUS
    per = tq // r
    last = s // r - 1
    grid = (bsz, s // tq)
    cur = lambda w_: pl.BlockSpec((1, tq, w_), lambda b, i: (b, i, 0))
    prev = pl.BlockSpec((1, r, nkv2), lambda b, i: (b, jnp.maximum(i * per - 1, 0), 0))
    nxt = pl.BlockSpec((1, r, nkv2), lambda b, i: (b, jnp.minimum(i * per + per, last), 0))
    w_in, w_out, w_shape, w_args = _side_cast_specs(cast_items, grid)
    o, *cast = pl.pallas_call(
        functools.partial(_att0_kernel, tq=tq, seq=s, ncast=len(cast_items)),
        grid=grid,
        in_specs=[pl.BlockSpec(memory_space=pltpu.SMEM),
                  cur(nq), cur(nkv2), prev, nxt, cur(nkv2), prev, nxt] + w_in,
        out_specs=[cur(nq)] + w_out,
        out_shape=[jax.ShapeDtypeStruct((bsz, s, nq), BF16)] + w_shape,
        scratch_shapes=[pltpu.VMEM((A_KV_HEADS, tq + 2 * r, 128), BF16),
                        pltpu.VMEM((A_KV_HEADS, tq + 2 * r, 256), BF16)],
        compiler_params=_cparams("parallel", "parallel"),
        name="att0",
    )(sink, q, k2, k2, k2, v2, v2, v2, *w_args)
    return o, [c.reshape(w.shape[1:]) for c, (w, _) in zip(cast, cast_items)]


def _post_kernel(o_ref, wo_ref, x_ref, gm_ref, gf_ref, scf_ref, shf_ref, rw_ref, rb_ref, upper_ref,
                 x1_ref, h2_ref, cls_ref, rank_ref, cnt_ref, carry):
    mix = jnp.dot(o_ref[0], wo_ref[...], preferred_element_type=F32)
    x1 = x_ref[0] + gm_ref[0] * mix
    x1_ref[0] = x1
    h2 = _rms_mod(x1, gf_ref[...], scf_ref[0], shf_ref[0]).astype(BF16)
    h2_ref[0] = _pack_bf16_pairs(h2.astype(F32))
    tq = h2.shape[0]
    logits = lax.dot_general(rw_ref[...], h2, (((1,), (1,)), ((), ())), preferred_element_type=F32)
    scores = jax.nn.sigmoid(logits)
    biased = scores + rb_ref[...]
    pj = [biased[8 * j:8 * (j + 1)] for j in range(PER_GROUP)]
    sel = []
    for j in range(PER_GROUP):
        beaten = jnp.zeros((8, tq), F32)
        for j2 in range(PER_GROUP):
            if j2 == j:
                continue
            beats = (pj[j2] > pj[j]) | (pj[j2] == pj[j]) if j2 < j else (pj[j2] > pj[j])
            beaten = beaten + jnp.where(beats, 1.0, 0.0)
        sel.append(beaten < 2.0)
    gscore = jnp.zeros((8, tq), F32)
    for j in range(PER_GROUP):
        gscore = gscore + jnp.where(sel[j], pj[j], 0.0)
    gi = lax.broadcasted_iota(I32, (8, tq), 0).astype(F32)
    gmax = jnp.max(gscore, axis=0, keepdims=True)
    gidx = jnp.min(jnp.where(gscore == gmax, gi, 8.0), axis=0, keepdims=True)
    onehot = gi == gidx
    f = [jnp.max(jnp.where(onehot & sel[j], 1.0, 0.0), axis=0, keepdims=True) > 0.5
         for j in range(PER_GROUP)]
    pair = jnp.where(f[0], jnp.where(f[1], 0.0, jnp.where(f[2], 1.0, 2.0)),
                     jnp.where(f[1], jnp.where(f[2], 3.0, 4.0), 5.0))
    cls = (gidx * float(len(PAIRS)) + pair).astype(I32)
    cls_ref[0] = jnp.broadcast_to(cls, (8, tq))

    @pl.when((pl.program_id(0) == 0) & (pl.program_id(1) == 0))
    def _():
        carry[...] = jnp.zeros_like(carry)

    c = carry[...]
    ranks = []
    for blk in range(tq // RANK_BLOCK):
        cls_b = cls[:, RANK_BLOCK * blk:RANK_BLOCK * (blk + 1)]
        oh = lax.broadcasted_iota(I32, (32, RANK_BLOCK), 0) == cls_b
        ohf = jnp.where(oh, 1.0, 0.0)
        within = jnp.dot(ohf.astype(BF16), upper_ref[...], preferred_element_type=F32)
        ranks.append(jnp.sum(jnp.where(oh, within + c[:, 0:1], 0.0), axis=0, keepdims=True))
        c = c + jnp.sum(ohf, axis=1, keepdims=True)
    rank_ref[0] = jnp.broadcast_to(jnp.concatenate(ranks, axis=1).astype(I32), (8, tq))
    carry[...] = c
    cnt_ref[...] = c


def _post(o, wo, x, gm, gf, scf, shf, rw32, rb32):
    bsz, s, d = x.shape
    do = o.shape[-1]
    tq = TM_WIDE
    nt = s // tq
    tok = lambda w_: pl.BlockSpec((1, tq, w_), lambda b, i: (b, i, 0))
    per_b = pl.BlockSpec((1, 1, d), lambda b, i: (b, 0, 0))
    full = lambda a: pl.BlockSpec(a.shape, lambda b, i: (0,) * a.ndim)
    per_tile = pl.BlockSpec((1, 8, tq), lambda b, i: (b * nt + i, 0, 0))
    gf2 = gf.reshape(1, d)
    upper = (jnp.arange(RANK_BLOCK)[:, None] < jnp.arange(RANK_BLOCK)[None, :]).astype(BF16)
    return pl.pallas_call(
        _post_kernel,
        grid=(bsz, nt),
        in_specs=[tok(do), full(wo), tok(d), per_b, full(gf2), per_b, per_b, full(rw32), full(rb32),
                  full(upper)],
        out_specs=[tok(d), tok(d // 2), per_tile, per_tile, pl.BlockSpec((32, 128), lambda b, i: (0, 0))],
        out_shape=[jax.ShapeDtypeStruct((bsz, s, d), F32),
                   jax.ShapeDtypeStruct((bsz, s, d // 2), U32),
                   jax.ShapeDtypeStruct((bsz * nt, 8, tq), I32),
                   jax.ShapeDtypeStruct((bsz * nt, 8, tq), I32),
                   jax.ShapeDtypeStruct((32, 128), F32)],
        scratch_shapes=[pltpu.VMEM((32, 128), F32)],
        compiler_params=_cparams("arbitrary", "arbitrary"),
        name="post_attention",
    )(o, wo, x, gm, gf2, scf, shf, rw32, rb32, upper)


def _ffn_kernel(e1_ref, e2_ref, nused_ref, hs_ref, *refs):
    ys_ref = refs[-1]
    j = pl.program_id(0)
    tm = FFN_TM

    @pl.when(j * FFN_TILES_PER_STEP < nused_ref[0])
    def _():
        for sub in range(FFN_TILES_PER_STEP):
            wg1, wu1, wd1, rw1, wg2, wu2, wd2, rw2 = refs[8 * sub:8 * (sub + 1)]
            rows = slice(tm * sub, tm * (sub + 1))
            xf = _unpack_bf16_pairs(hs_ref[rows])
            x = xf.astype(BF16)
            gt1 = jnp.dot(x, wg1[0], preferred_element_type=F32)
            up1 = jnp.dot(x, wu1[0], preferred_element_type=F32)
            gt2 = jnp.dot(x, wg2[0], preferred_element_type=F32)
            up2 = jnp.dot(x, wu2[0], preferred_element_type=F32)
            a1 = ((gt1 * jax.nn.sigmoid(gt1)) * up1).astype(BF16)
            y1 = jnp.dot(a1, wd1[0], preferred_element_type=F32)
            a2 = ((gt2 * jax.nn.sigmoid(gt2)) * up2).astype(BF16)
            y2 = jnp.dot(a2, wd2[0], preferred_element_type=F32)
            s1 = jax.nn.sigmoid(jnp.sum(xf * rw1[0].astype(F32), axis=-1, keepdims=True))
            s2 = jax.nn.sigmoid(jnp.sum(xf * rw2[0].astype(F32), axis=-1, keepdims=True))
            den = s1 + s2
            ys_ref[rows] = _pack_bf16_pairs((s1 / den) * y1 + (s2 / den) * y2)

    @pl.when(j * FFN_TILES_PER_STEP >= nused_ref[0])
    def _():
        ys_ref[...] = jnp.zeros_like(ys_ref)


def _ffn(hs, e1, e2, nused, wg, wu, wd, rw3):
    npad, dh = hs.shape
    d = 2 * dh
    per = FFN_TILES_PER_STEP
    rows = per * FFN_TM
    de = wg.shape[-1]
    assert npad % rows == 0
    row = pl.BlockSpec((rows, dh), lambda j, e1, e2, nu: (jnp.minimum(j, (nu[0] - 1) // per), 0))
    w_specs = []
    for sub in range(per):
        for which in range(2):
            pick = lambda j, e1, e2, nu, sub=sub, which=which: ((e2 if which else e1)[per * j + sub], 0, 0)
            w_specs += [pl.BlockSpec((1, d, de), pick), pl.BlockSpec((1, d, de), pick),
                        pl.BlockSpec((1, de, d), pick), pl.BlockSpec((1, 1, d), pick)]
    grid_spec = pltpu.PrefetchScalarGridSpec(
        num_scalar_prefetch=3,
        grid=(npad // rows,),
        in_specs=[row] + w_specs,
        out_specs=pl.BlockSpec((rows, dh), lambda j, e1, e2, nu: (j, 0)),
    )
    return pl.pallas_call(
        _ffn_kernel,
        grid_spec=grid_spec,
        out_shape=jax.ShapeDtypeStruct((npad, dh), U32),
        compiler_params=_cparams("arbitrary"),
        name="expert_ffn",
    )(e1, e2, nused, hs, *([wg, wu, wd, rw3] * (2 * per)))


def _sc_mesh():
    return plsc.VectorSubcoreMesh(core_axis_name="c", subcore_axis_name="s",
                                  num_cores=SC_CORES_V7X, num_subcores=SC_SUBCORES_V7X)


def _sc_steps(n_rows):
    workers = SC_CORES_V7X * SC_SUBCORES_V7X
    per_w = n_rows // workers
    ch = min(SC_ROWS_PER_STEP, per_w // 2)
    n = per_w // ch
    assert per_w * workers == n_rows and n * ch == per_w and n % 2 == 0 and ch % 8 == 0
    return per_w, ch, n


def _sc_scratch(ch, d, dtype):
    return [pltpu.VMEM((2, ch), I32), pltpu.VMEM((2, ch, d), dtype),
            pltpu.SemaphoreType.DMA((2,)), pltpu.SemaphoreType.DMA((2,))]


def _sc_scatter_rows(src, idx, n_out):
    n_in, d = src.shape
    per_w, ch, n = _sc_steps(n_in)

    def body(src_hbm, idx_hbm, out_hbm, idx_v, rows_v, rsem, ssem):
        base = (lax.axis_index("s") * SC_CORES_V7X + lax.axis_index("c")) * per_w

        def read(i, b):
            return pltpu.make_async_copy(src_hbm.at[pl.ds(base + i * ch, ch)], rows_v.at[b], rsem.at[b])

        def scatter(b):
            return pltpu.make_async_copy(rows_v.at[b], out_hbm.at[idx_v.at[b]], ssem.at[b])

        def start_read(i, b):
            pltpu.sync_copy(idx_hbm.at[pl.ds(base + i * ch, ch)], idx_v.at[b])
            read(i, b).start()

        start_read(0, 0)

        @pl.loop(0, n, step=2)
        def _(i):
            for b in (0, 1):
                cur = i + b

                @pl.when(cur + 1 < n)
                def _():
                    @pl.when(cur >= 1)
                    def _():
                        scatter(1 - b).wait()
                    start_read(cur + 1, 1 - b)

                read(cur, b).wait()
                scatter(b).start()

        scatter(0).wait()
        scatter(1).wait()

    return pl.kernel(
        body, mesh=_sc_mesh(),
        out_type=jax.ShapeDtypeStruct((n_out, d), src.dtype),
        scratch_types=_sc_scratch(ch, d, src.dtype),
        name="sc_scatter_rows",
    )(src, idx)


def _sc_gather_rows(table, idx):
    n_out = idx.shape[0]
    d = table.shape[1]
    per_w, ch, n = _sc_steps(n_out)

    def body(table_hbm, idx_hbm, out_hbm, idx_v, rows_v, gsem, wsem):
        base = (lax.axis_index("s") * SC_CORES_V7X + lax.axis_index("c")) * per_w

        def gather(b):
            return pltpu.make_async_copy(table_hbm.at[idx_v.at[b]], rows_v.at[b], gsem.at[b])

        def write(i, b):
            return pltpu.make_async_copy(rows_v.at[b], out_hbm.at[pl.ds(base + i * ch, ch)], wsem.at[b])

        def start_gather(i, b):
            pltpu.sync_copy(idx_hbm.at[pl.ds(base + i * ch, ch)], idx_v.at[b])
            gather(b).start()

        start_gather(0, 0)

        @pl.loop(0, n, step=2)
        def _(i):
            for b in (0, 1):
                cur = i + b

                @pl.when(cur + 1 < n)
                def _():
                    @pl.when(cur >= 1)
                    def _():
                        write(cur - 1, 1 - b).wait()
                    start_gather(cur + 1, 1 - b)

                gather(b).wait()
                write(cur, b).start()

        write(n - 2, 0).wait()
        write(n - 1, 1).wait()

    return pl.kernel(
        body, mesh=_sc_mesh(),
        out_type=jax.ShapeDtypeStruct((n_out, d), table.dtype),
        scratch_types=_sc_scratch(ch, d, table.dtype),
        name="sc_gather_rows",
    )(table, idx)


def _moe(h2p, cls3, rank3, cnt, wg, wu, wd, rw3):
    bsz, s, dh = h2p.shape
    t = bsz * s
    tm = FFN_TM
    counts = cnt[:N_CLASSES, 0].astype(I32)
    padded = ((counts + tm - 1) // tm) * tm
    upto = jnp.arange(N_CLASSES)[:, None] >= jnp.arange(N_CLASSES)[None, :]
    ends = jnp.sum(jnp.where(upto, padded[None, :], 0), axis=1)
    base = ends - padded
    cls2 = cls3[:, 0, :]
    pos2 = rank3[:, 0, :]
    for c in range(N_CLASSES):
        pos2 = pos2 + jnp.where(cls2 == c, base[c], 0)
    pos = pos2.reshape(t)
    ntiles = t // tm + N_CLASSES
    tile_start = jnp.arange(ntiles, dtype=I32) * tm
    tcls = jnp.minimum(jnp.sum((ends[None, :] <= tile_start[:, None]).astype(I32), axis=1), N_CLASSES - 1)
    nused = (ends[-1:] // tm).astype(I32)
    grp = tcls // len(PAIRS)
    pr = tcls % len(PAIRS)
    first = sum(jnp.where(pr == i, p[0], 0) for i, p in enumerate(PAIRS))
    second = sum(jnp.where(pr == i, p[1], 0) for i, p in enumerate(PAIRS))
    e1 = (grp * PER_GROUP + first).astype(I32)
    e2 = (grp * PER_GROUP + second).astype(I32)
    hs = _sc_scatter_rows(h2p.reshape(t, dh), pos, ntiles * tm)
    ys = _ffn(hs, e1, e2, nused, wg, wu, wd, rw3)
    return _sc_gather_rows(ys, pos).reshape(bsz, s, dh)


def _qkv1_kernel(x_ref, y_ref, gfp_ref, g_ref, sc_ref, sh_ref, w_ref, cos_ref, sin_ref,
                 wg_ref, wu_ref, wd_ref, x2_ref, *rest):
    outs = rest[:9]
    _side_cast((wg_ref, wu_ref, wd_ref), rest[9:12])
    hb_scr, ysc, ysc2 = rest[12:]
    tm = x_ref.shape[1]
    x2 = x_ref[0] + gfp_ref[0] * _unpack_bf16_pairs(y_ref[0])
    x2_ref[0] = x2
    hb_scr[...] = _rms_mod(x2, g_ref[...], sc_ref[0], sh_ref[0]).astype(BF16)
    cos = cos_ref[0]
    sin = sin_ref[0]
    lane_lo = (lax.broadcasted_iota(I32, cos.shape, 1) % HEAD_DIM) < (HEAD_DIM // 2)
    for gi, (_, dil) in reversed(list(enumerate(B_GROUPS))):
        for j in range(3):
            c = gi * 3 + j
            y = jnp.dot(hb_scr[...], w_ref[:, B_WIDTH * c:B_WIDTH * (c + 1)], preferred_element_type=F32)
            out = outs[c]
            for cc in range(B_WIDTH // 128):
                lanes = slice(128 * cc, 128 * (cc + 1))
                r = y[:, lanes]
                if j < 2:
                    r = _rope_apply(r, cos, sin, lane_lo)
                if j == 0:
                    r = r * Q_SCALE
                if dil == 1:
                    out[0, 0, :, lanes] = r.astype(BF16)
                    continue
                ysc[cc] = r
                if dil % 16:
                    for rr in range(dil):
                        out[0, rr, :, lanes] = ysc[cc, pl.ds(rr, tm // dil, stride=dil), :].astype(BF16)
                    continue
                quarter = tm // 4
                inner = dil // 4
                for r1 in range(4):
                    ysc2[cc, r1 * quarter:(r1 + 1) * quarter] = ysc[cc, pl.ds(r1, quarter, stride=4), :]
                for r1 in range(4):
                    for r2 in range(inner):
                        out[0, r1 + 4 * r2, :, lanes] = ysc2[
                            cc, pl.ds(r1 * quarter + r2, tm // dil, stride=inner), :].astype(BF16)


def _qkv1(x, y, gfp, g, sc, sh, w, cos, sin, expert_ws, layer):
    bsz, s, d = x.shape
    tm = TM
    grid = (bsz, s // tm)
    w_in, w_out, w_shape, w_args = _side_cast_specs([(w_, layer) for w_ in expert_ws], grid)
    tok = lambda w_: pl.BlockSpec((1, tm, w_), lambda b, i: (b, i, 0))
    per_b = pl.BlockSpec((1, 1, d), lambda b, i: (b, 0, 0))
    out_specs = [tok(d)]
    out_shape = [jax.ShapeDtypeStruct((bsz, s, d), F32)]
    for _, dil in B_GROUPS:
        for _ in range(3):
            out_specs.append(pl.BlockSpec((1, dil, tm // dil, B_WIDTH), lambda b, i: (b, 0, i, 0)))
            out_shape.append(jax.ShapeDtypeStruct((bsz, dil, s // dil, B_WIDTH), BF16))
    outs = pl.pallas_call(
        _qkv1_kernel,
        grid=grid,
        in_specs=[tok(d), tok(d // 2), per_b, pl.BlockSpec((1, d), lambda b, i: (0, 0)), per_b, per_b,
                  pl.BlockSpec(w.shape, lambda b, i: (0, 0)), tok(128), tok(128)] + w_in,
        out_specs=out_specs + w_out,
        out_shape=out_shape + w_shape,
        scratch_shapes=[pltpu.VMEM((tm, d), BF16), pltpu.VMEM((B_WIDTH // 128, tm, 128), F32),
                        pltpu.VMEM((B_WIDTH // 128, tm, 128), F32)],
        compiler_params=_cparams("parallel", "parallel"),
        name="qkv1",
    )(x, y, gfp, g.reshape(1, d), sc, sh, w, cos, sin, *w_args)
    return outs[:10], [c.reshape(w_.shape[1:]) for c, w_ in zip(outs[10:], expert_ws)]


def _att1_window_pad(lseg):
    qb = min(2 * B_RADIUS, lseg)
    return -(qb + 2 * B_RADIUS) % 128


def _att1_kernel(*refs, chunk, seq):
    ins = refs[:21]
    o_ref = refs[21]
    scr = refs[22:]
    nat_o, nat_d, nat_m = scr[0:3], scr[3:6], scr[6:9]
    n = pl.program_id(1)
    rad = B_RADIUS
    npair = B_WIDTH // 128

    def window(cur, prev, nxt, res, lo, hi, lseg, lanes):
        parts = []
        if lo < 0:
            parts.append(prev[0, res, :, lanes])
        parts.append(cur[0, res, max(lo, 0):min(hi, lseg), lanes])
        if hi > lseg:
            parts.append(nxt[0, res, :, lanes])
        if hi > lseg + rad:
            parts.append(nxt[0, res, :, lanes])
        return jnp.concatenate(parts, axis=0) if len(parts) > 1 else parts[0]

    for c in range(npair):
        lanes = slice(128 * c, 128 * (c + 1))
        for gi, (_, dil) in enumerate(B_GROUPS):
            q_ref, kc, kp, kn, vc, vp, vn = ins[7 * gi:7 * gi + 7]
            lseg = chunk // dil
            qb = min(2 * rad, lseg)
            kb_n = qb + 2 * rad + _att1_window_pad(lseg)
            lstr = seq // dil
            rel = lax.broadcasted_iota(I32, (qb, kb_n), 1) - lax.broadcasted_iota(I32, (qb, kb_n), 0)
            band_bias = jnp.where((rel >= 0) & (rel <= 2 * rad), 0.0, NEG)
            jrow = lax.broadcasted_iota(I32, (1, kb_n), 1)
            first_head = lax.broadcasted_iota(I32, (qb, 128), 1) < HEAD_DIM
            zero = jnp.zeros((qb, 128), BF16)
            ones = jnp.ones((kb_n, 128), BF16)
            for res in range(dil):
                for q0 in range(0, lseg, qb):
                    apos = n * lseg + q0 - rad + jrow
                    bias = band_bias + jnp.where((apos >= 0) & (apos < lstr), 0.0, NEG)
                    rows = pl.ds(q0, qb) if dil == 1 else pl.ds(res + dil * q0, qb, stride=dil)
                    qc = q_ref[0, res, q0:q0 + qb, lanes]
                    kb = window(kc, kp, kn, res, q0 - rad, q0 - rad + kb_n, lseg, lanes)
                    vb = jnp.concatenate(
                        [window(vc, vp, vn, res, q0 - rad, q0 - rad + kb_n, lseg, lanes), ones], axis=1)
                    qs = jnp.concatenate(
                        [jnp.where(first_head, qc, zero), jnp.where(first_head, zero, qc)], axis=0)
                    s2 = lax.dot_general(qs, kb, (((1,), (1,)), ((), ())), preferred_element_type=F32)
                    ps, ms = [], []
                    for t2 in range(2):
                        s = s2[qb * t2:qb * (t2 + 1)] + bias
                        m = jnp.max(s, axis=-1, keepdims=True)
                        ps.append(jnp.exp2((s - m).astype(BF16)))
                        ms.append(m)
                    pv = jnp.dot(jnp.concatenate(ps, axis=0), vb, preferred_element_type=F32)
                    nat_o[gi][rows, :] = jnp.where(first_head, pv[0:qb, 0:128], pv[qb:2 * qb, 0:128])
                    nat_d[gi][rows, :] = jnp.where(first_head, pv[0:qb, 128:256], pv[qb:2 * qb, 128:256])
                    nat_m[gi][rows, :] = jnp.where(first_head, ms[0], ms[1])

        ms = [nat_m[gi][...] for gi in range(3)]
        mx = jnp.maximum(jnp.maximum(ms[0], ms[1]), ms[2])
        w = [jnp.exp2(x - mx) for x in ms]
        num = w[0] * nat_o[0][...] + w[1] * nat_o[1][...] + w[2] * nat_o[2][...]
        den = w[0] * nat_d[0][...] + w[1] * nat_d[1][...] + w[2] * nat_d[2][...]
        o_ref[0, :, lanes] = (num / den).astype(BF16)


def _att1(qkv):
    bsz = qkv[0].shape[0]
    seq = qkv[0].shape[1] * qkv[0].shape[2]
    chunk = ATT1_CHUNK
    rad = B_RADIUS
    in_specs, args, scratch = [], [], []
    for gi, (_, dil) in enumerate(B_GROUPS):
        q, k, v = qkv[3 * gi:3 * gi + 3]
        lseg = chunk // dil
        per = lseg // rad
        last = seq // dil // rad - 1
        cur = pl.BlockSpec((1, dil, lseg, B_WIDTH), lambda b, n: (b, 0, n, 0))
        prev = pl.BlockSpec((1, dil, rad, B_WIDTH),
                            lambda b, n, per=per: (b, 0, jnp.maximum(n * per - 1, 0), 0))
        nxt = pl.BlockSpec((1, dil, rad, B_WIDTH),
                           lambda b, n, per=per, last=last: (b, 0, jnp.minimum(n * per + per, last), 0))
        in_specs += [cur, cur, prev, nxt, cur, prev, nxt]
        args += [q, k, k, k, v, v, v]
    for _ in range(3):
        for _ in B_GROUPS:
            scratch.append(pltpu.VMEM((chunk, 128), F32))
    return pl.pallas_call(
        functools.partial(_att1_kernel, chunk=chunk, seq=seq),
        grid=(bsz, seq // chunk),
        in_specs=in_specs,
        out_specs=pl.BlockSpec((1, chunk, B_WIDTH), lambda b, n: (b, n, 0)),
        out_shape=jax.ShapeDtypeStruct((bsz, seq, B_WIDTH), BF16),
        scratch_shapes=scratch,
        compiler_params=_cparams("parallel", "parallel"),
        name="att1",
    )(*args)


def _final_kernel(x_ref, y_ref, gfp_ref, g_ref, o_ref):
    x = x_ref[0] + gfp_ref[0] * _unpack_bf16_pairs(y_ref[0])
    ms = jnp.mean(x * x, axis=-1, keepdims=True)
    o_ref[0] = (x * lax.rsqrt(ms + NORM_EPS)) * g_ref[...]


def _final(x, y, gfp, g):
    bsz, s, d = x.shape
    tm = TM_WIDE
    tok = pl.BlockSpec((1, tm, d), lambda b, i: (b, i, 0))
    return pl.pallas_call(
        _final_kernel,
        grid=(bsz, s // tm),
        in_specs=[tok, pl.BlockSpec((1, tm, d // 2), lambda b, i: (b, i, 0)),
                  pl.BlockSpec((1, 1, d), lambda b, i: (b, 0, 0)),
                  pl.BlockSpec((1, d), lambda b, i: (0, 0))],
        out_specs=tok,
        out_shape=jax.ShapeDtypeStruct((bsz, s, d), F32),
        compiler_params=_cparams("parallel", "parallel"),
        name="final_norm",
    )(x, y, gfp, g.reshape(1, d))


def kernel(x, c, positions, ada_w, ada_b, norm_mix_g, norm_ffn_g, a_w_qkv, a_w_o, a_sink, b_w_qkv, b_w_o,
           router_w, router_bias, exp_w_gate, exp_w_up, exp_w_down, final_norm_g):
    bsz, s, d = x.shape
    assert d == D_MODEL and ada_w.shape[0] == 2 and s % ATT1_CHUNK == 0
    assert all(w_ // (2 * dil) == B_RADIUS for w_, dil in B_GROUPS)

    mod = _modulation(c, ada_w, ada_b)
    mods = [[mod[l][:, k * d:(k + 1) * d].reshape(bsz, 1, d) for k in range(6)] for l in range(2)]
    cos, sin = _rope_tables(positions)

    rw_t = router_w.T.astype(BF16)
    rows = jnp.array([PER_GROUP * g + j if g < N_EXPERT_GROUPS else 0
                      for j in range(PER_GROUP) for g in range(8)], I32)
    live = jnp.array([1.0 if g < N_EXPERT_GROUPS else 0.0
                      for j in range(PER_GROUP) for g in range(8)], F32)
    rw32 = (rw_t[rows].astype(F32) * live[:, None]).astype(BF16)
    rb32 = jnp.where(live > 0, router_bias.astype(F32)[rows], NEG).reshape(32, 1)
    rw3 = rw_t.reshape(N_EXPERTS, 1, d)
    expert_ws = (exp_w_gate, exp_w_up, exp_w_down)

    sh_m, sc_m, g_m, sh_f, sc_f, g_f = mods[0]
    q, k, v = _qkv0(x, norm_mix_g[0], sc_m, sh_m, a_w_qkv[0].astype(BF16), cos, sin)
    o, cast = _att0(q, k, v, a_sink[0].astype(F32),
                    [(w_, 0) for w_ in expert_ws] + [(a_w_o, 0), (b_w_o, 0), (b_w_qkv, 0)])
    experts, (a_wo, b_wo, b_wqkv) = cast[:3], cast[3:]
    x1, *routed = _post(o, a_wo, x, g_m, norm_ffn_g[0], sc_f, sh_f, rw32, rb32)
    y = _moe(*routed, *experts, rw3)
    g_f_prev = g_f

    sh_m, sc_m, g_m, sh_f, sc_f, g_f = mods[1]
    outs, experts = _qkv1(x1, y, g_f_prev, norm_mix_g[1], sc_m, sh_m, b_wqkv, cos, sin, expert_ws, 1)
    x2, qkv = outs[0], outs[1:]
    o = _att1(qkv)
    x3, *routed = _post(o, b_wo, x2, g_m, norm_ffn_g[1], sc_f, sh_f, rw32, rb32)
    y = _moe(*routed, *experts, rw3)

    return _final(x3, y, g_f, final_norm_g)
```

```python
import functools

import numpy as np
import jax
import jax.numpy as jnp
from jax import lax
from jax.experimental import pallas as pl
from jax.experimental.pallas import tpu as pltpu
from jax.experimental.pallas import tpu_sc as plsc

F32, BF16, I32, U32 = jnp.float32, jnp.bfloat16, jnp.int32, jnp.uint32

D_MODEL = 1024
HEAD_DIM = 64
ROPE_THETA = 10000.0
NORM_EPS = 1e-6
A_Q_HEADS = 16
A_KV_HEADS = 4
A_GROUP = A_Q_HEADS // A_KV_HEADS
A_RADIUS = 128
B_GROUPS = ((128, 1), (512, 4), (2048, 16))
B_HEADS = 8
B_RADIUS = 64
B_WIDTH = B_HEADS * HEAD_DIM
N_EXPERTS = 16
N_EXPERT_GROUPS = 4
PER_GROUP = N_EXPERTS // N_EXPERT_GROUPS
D_EXPERT = D_MODEL // 2
PAIRS = ((0, 1), (0, 2), (0, 3), (1, 2), (1, 3), (2, 3))
N_CLASSES = N_EXPERT_GROUPS * len(PAIRS)
NEG = -1e30
LOG2E = 1.4426950408889634
Q_SCALE = HEAD_DIM ** -0.5 * LOG2E

VMEM_LIMIT = 56 * 1024 * 1024
TM = 512
ROPE_TS = 2048
TM_WIDE = 1024
RANK_BLOCK = 512
ATT1_CHUNK = 1024
FFN_TM = 256
FFN_TILES_PER_STEP = 2
SC_CORES_V7X = 2
SC_SUBCORES_V7X = 16
SC_ROWS_PER_STEP = 64


def _cparams(*sem):
    return pltpu.CompilerParams(dimension_semantics=sem, vmem_limit_bytes=VMEM_LIMIT)


def _rms_mod(x, g, sc, sh):
    ms = jnp.mean(x * x, axis=-1, keepdims=True)
    return (x * lax.rsqrt(ms + NORM_EPS)) * (g * (1.0 + sc)) + sh


def _rope_apply(y, cos, sin_signed, lane_lo):
    sw = jnp.where(lane_lo, pltpu.roll(y, 96, 1), pltpu.roll(y, 32, 1))
    return y * cos + sw * sin_signed


def _rope_apply_paired(y, cos, sin_paired):
    return y * cos + pltpu.roll(y, HEAD_DIM, 1) * sin_paired


def _paired_layout_columns(n_cols):
    j = np.arange(n_cols)
    part, r = (j % 128) // 32, j % 32
    return (j // 128) * 128 + (part % 2) * HEAD_DIM + (part // 2) * (HEAD_DIM // 2) + r


def _pack_bf16_pairs(x):
    n = x.shape[1] // 2
    lo = lax.bitcast_convert_type(x[:, :n].astype(BF16).astype(F32), U32)
    hi = lax.bitcast_convert_type(x[:, n:].astype(BF16).astype(F32), U32)
    return (hi & jnp.uint32(0xFFFF0000)) | (lo >> 16)


def _unpack_bf16_pairs(p):
    lo = lax.bitcast_convert_type(p << 16, F32)
    hi = lax.bitcast_convert_type(p & jnp.uint32(0xFFFF0000), F32)
    return jnp.concatenate([lo, hi], axis=1)


def _dup_head(x, first_head, which):
    rolled = pltpu.roll(x, HEAD_DIM, 1)
    return jnp.where(first_head, x, rolled) if which == 0 else jnp.where(first_head, rolled, x)


def _mod_kernel(ct_ref, w_ref, b_ref, o_ref, *, bsz):
    ct = ct_ref[...]
    s = ct * jax.nn.sigmoid(ct)
    w = w_ref[0]
    rows = [jnp.sum(w * s[:, b:b + 1], axis=0, keepdims=True) + b_ref[0] for b in range(bsz)]
    o_ref[0] = jnp.concatenate(rows + [jnp.zeros_like(rows[0])] * (8 - bsz), axis=0)


def _modulation(c, ada_w, ada_b):
    depth, d, n = ada_w.shape
    bsz = c.shape[0]
    assert bsz <= 8
    tn = 1536
    out = pl.pallas_call(
        functools.partial(_mod_kernel, bsz=bsz),
        grid=(depth, n // tn),
        in_specs=[pl.BlockSpec((d, bsz), lambda l, j: (0, 0)),
                  pl.BlockSpec((1, d, tn), lambda l, j: (l, 0, j)),
                  pl.BlockSpec((1, 1, tn), lambda l, j: (l, 0, j))],
        out_specs=pl.BlockSpec((1, 8, tn), lambda l, j: (l, 0, j)),
        out_shape=jax.ShapeDtypeStruct((depth, 8, n), F32),
        compiler_params=_cparams("arbitrary", "arbitrary"),
        name="modulation",
    )(c.T, ada_w, ada_b.reshape(depth, 1, n))
    return out[:, :bsz]


def _rope_kernel(pos_ref, invf_ref, sign_ref, cos_ref, sin_ref, sinp_ref):
    ang = pos_ref[0].astype(F32) * invf_ref[...]
    cos_ref[0] = jnp.cos(ang)
    s = jnp.sin(ang)
    sin_ref[0] = s * sign_ref[0:1]
    sinp_ref[0] = s * sign_ref[1:2]


def _rope_tables(positions):
    bsz, s = positions.shape
    inv_freq = ROPE_THETA ** (-jnp.arange(0, HEAD_DIM, 2, dtype=F32) / HEAD_DIM)
    invf = jnp.tile(inv_freq, 4).reshape(1, 128)
    half = HEAD_DIM // 2
    minus, plus = -jnp.ones((half,), F32), jnp.ones((half,), F32)
    sign = jnp.stack([jnp.concatenate([minus, plus, minus, plus]),
                      jnp.concatenate([minus, minus, plus, plus])])
    ts = ROPE_TS
    return pl.pallas_call(
        _rope_kernel,
        grid=(bsz, s // ts),
        in_specs=[pl.BlockSpec((1, ts, 1), lambda b, i: (b, i, 0)),
                  pl.BlockSpec((1, 128), lambda b, i: (0, 0)),
                  pl.BlockSpec((2, 128), lambda b, i: (0, 0))],
        out_specs=[pl.BlockSpec((1, ts, 128), lambda b, i: (b, i, 0))] * 3,
        out_shape=[jax.ShapeDtypeStruct((bsz, s, 128), F32)] * 3,
        compiler_params=_cparams("parallel", "parallel"),
        name="rope_tables",
    )(positions.reshape(bsz, s, 1), invf, sign)


def _qkv0_kernel(x_ref, g_ref, sc_ref, sh_ref, w_ref, cos_ref, sin_ref,
                 q_ref, k_ref, v_ref, hb_scr):
    hb_scr[...] = _rms_mod(x_ref[0], g_ref[...], sc_ref[0], sh_ref[0]).astype(BF16)
    cos = cos_ref[0]
    sin = sin_ref[0]
    lane = lax.broadcasted_iota(I32, cos.shape, 1)
    x_lo = (lane % HEAD_DIM) < (HEAD_DIM // 2)
    nq = A_Q_HEADS * HEAD_DIM
    nkv = A_KV_HEADS * HEAD_DIM
    for c in range(nq // 256):
        y = jnp.dot(hb_scr[...], w_ref[:, 256 * c:256 * (c + 1)], preferred_element_type=F32)
        for cc in range(2):
            r = _rope_apply_paired(y[:, 128 * cc:128 * (cc + 1)], cos, sin) * Q_SCALE
            q_ref[0, :, 256 * c + 128 * cc:256 * c + 128 * (cc + 1)] = r.astype(BF16)
    first_head = lane < HEAD_DIM
    y = jnp.dot(hb_scr[...], w_ref[:, nq:nq + nkv], preferred_element_type=F32)
    for cc in range(nkv // 128):
        r = _rope_apply_paired(y[:, 128 * cc:128 * (cc + 1)], cos, sin)
        k_ref[0, :, 256 * cc:256 * cc + 128] = jnp.where(
            x_lo, r, pltpu.roll(r, HEAD_DIM // 2, 1)).astype(BF16)
        k_ref[0, :, 256 * cc + 128:256 * cc + 256] = jnp.where(
            x_lo, pltpu.roll(r, 128 - HEAD_DIM // 2, 1), r).astype(BF16)
    y = jnp.dot(hb_scr[...], w_ref[:, nq + nkv:nq + 2 * nkv], preferred_element_type=F32)
    for cc in range(nkv // 128):
        r = y[:, 128 * cc:128 * (cc + 1)]
        for which in range(2):
            g = 2 * cc + which
            v_ref[0, :, 128 * g:128 * (g + 1)] = _dup_head(r, first_head, which).astype(BF16)


def _qkv0(x, g, sc, sh, w, cos, sin):
    bsz, s, d = x.shape
    nq = A_Q_HEADS * HEAD_DIM
    nkv = A_KV_HEADS * HEAD_DIM
    tm = TM_WIDE
    tok = lambda w_: pl.BlockSpec((1, tm, w_), lambda b, i: (b, i, 0))
    per_b = pl.BlockSpec((1, 1, d), lambda b, i: (b, 0, 0))
    return pl.pallas_call(
        _qkv0_kernel,
        grid=(bsz, s // tm),
        in_specs=[tok(d), pl.BlockSpec((1, d), lambda b, i: (0, 0)), per_b, per_b,
                  pl.BlockSpec(w.shape, lambda b, i: (0, 0)), tok(128), tok(128)],
        out_specs=[tok(nq), tok(2 * nkv), tok(2 * nkv)],
        out_shape=[jax.ShapeDtypeStruct((bsz, s, nq), BF16),
                   jax.ShapeDtypeStruct((bsz, s, 2 * nkv), BF16),
                   jax.ShapeDtypeStruct((bsz, s, 2 * nkv), BF16)],
        scratch_shapes=[pltpu.VMEM((tm, d), BF16)],
        compiler_params=_cparams("parallel", "parallel"),
        name="qkv0",
    )(x, g.reshape(1, d), sc, sh, w, cos, sin)


def _side_cast_specs(items, grid):
    nsteps = grid[0] * grid[1]
    in_specs, out_specs, out_shape, args = [], [], [], []
    for w, layer in items:
        n = w.shape[-1]
        per_layer = w[0].size // n
        rows = per_layer // nsteps
        assert rows * nsteps == per_layer and rows % 16 == 0
        in_specs.append(pl.BlockSpec(
            (rows, n), lambda b, i, layer=layer: (layer * nsteps + b * grid[1] + i, 0)))
        out_specs.append(pl.BlockSpec((rows, n), lambda b, i: (b * grid[1] + i, 0)))
        out_shape.append(jax.ShapeDtypeStruct((per_layer, n), BF16))
        args.append(w.reshape(w.shape[0] * per_layer, n))
    return in_specs, out_specs, out_shape, args


def _side_cast(srcs, dsts):
    for src, dst in zip(srcs, dsts):
        dst[...] = src[...].astype(BF16)


def _att0_kernel(sink_ref, q_ref, kc_ref, kp_ref, kn_ref, vc_ref, vp_ref, vn_ref, *rest, tq, seq, ncast):
    o_ref = rest[ncast]
    kf, vf = rest[2 * ncast + 1:]
    _side_cast(rest[:ncast], rest[ncast + 1:2 * ncast + 1])
    i = pl.program_id(1)
    r = A_RADIUS
    ones = jnp.ones((tq + 2 * r, 128), BF16)
    for g in range(A_KV_HEADS):
        lanes = slice(128 * g, 128 * (g + 1))
        kf[g, 0:r] = kp_ref[0, :, lanes]
        kf[g, r:r + tq] = kc_ref[0, :, lanes]
        kf[g, r + tq:r + tq + r] = kn_ref[0, :, lanes]
        vf[g, 0:r, 0:128] = vp_ref[0, :, lanes]
        vf[g, r:r + tq, 0:128] = vc_ref[0, :, lanes]
        vf[g, r + tq:r + tq + r, 0:128] = vn_ref[0, :, lanes]
        vf[g, :, 128:256] = ones

    ii = lax.broadcasted_iota(I32, (r, r), 0)
    jj = lax.broadcasted_iota(I32, (r, r), 1)
    first_head = jj < HEAD_DIM
    q_first = (jj % HEAD_DIM) < (HEAD_DIM // 2)
    zero = jnp.zeros((r, 128), BF16)

    for j in range(tq // r):
        q0 = j * r
        base = i * tq + q0
        bias_lo = jnp.where((jj >= ii) & (base - r + jj >= 0), 0.0, NEG)
        bias_hi = jnp.where((jj <= ii) & (base + r + jj < seq), 0.0, NEG)
        for g in range(A_KV_HEADS):
            kb = kf[g, pl.ds(q0, 3 * r), :]
            vb = vf[g, pl.ds(q0, 3 * r), :]
            qs = []
            for cc in range(2):
                c = 2 * g + cc
                qc = q_ref[0, pl.ds(q0, r), 128 * c:128 * (c + 1)]
                qs += [jnp.where(q_first, qc, zero), jnp.where(q_first, zero, qc)]
            s4 = lax.dot_general(jnp.concatenate(qs, axis=0), kb, (((1,), (1,)), ((), ())),
                                 preferred_element_type=F32)
            ps, ms = [], []
            for t in range(A_GROUP):
                s = s4[r * t:r * (t + 1)]
                a0 = s[:, 0:r] + bias_lo
                a1 = s[:, r:2 * r]
                a2 = s[:, 2 * r:3 * r] + bias_hi
                sink = sink_ref[A_GROUP * g + t] * LOG2E
                m = jnp.max(jnp.maximum(jnp.maximum(a0, a1), a2), axis=-1, keepdims=True)
                m = jnp.maximum(m, sink)
                ps.append(jnp.exp2(jnp.concatenate([a0 - m, a1 - m, a2 - m], axis=1).astype(BF16)))
                ms.append(m)
            pv = jnp.dot(jnp.concatenate(ps, axis=0), vb, preferred_element_type=F32)
            os_ = []
            for t in range(A_GROUP):
                sink = sink_ref[A_GROUP * g + t] * LOG2E
                den = pv[r * t:r * (t + 1), 128:256] + jnp.exp2(sink - ms[t])
                os_.append(pv[r * t:r * (t + 1), 0:128] / den)
            for cc in range(2):
                c = 2 * g + cc
                o_ref[0, pl.ds(q0, r), 128 * c:128 * (c + 1)] = jnp.where(
                    first_head, os_[2 * cc], os_[2 * cc + 1]).astype(BF16)


def _att0(q, k2, v2, sink, cast_items):
    bsz, s, nq = q.shape
    nkv2 = k2.shape[-1]
    tq = TM_WIDE
    r = A_RADIUS
    per = tq // r
    last = s // r - 1
    grid = (bsz, s // tq)
    cur = lambda w_: pl.BlockSpec((1, tq, w_), lambda b, i: (b, i, 0))
    prev = pl.BlockSpec((1, r, nkv2), lambda b, i: (b, jnp.maximum(i * per - 1, 0), 0))
    nxt = pl.BlockSpec((1, r, nkv2), lambda b, i: (b, jnp.minimum(i * per + per, last), 0))
    w_in, w_out, w_shape, w_args = _side_cast_specs(cast_items, grid)
    o, *cast = pl.pallas_call(
        functools.partial(_att0_kernel, tq=tq, seq=s, ncast=len(cast_items)),
        grid=grid,
        in_specs=[pl.BlockSpec(memory_space=pltpu.SMEM),
                  cur(nq), cur(nkv2), prev, nxt, cur(nkv2), prev, nxt] + w_in,
        out_specs=[cur(nq)] + w_out,
        out_shape=[jax.ShapeDtypeStruct((bsz, s, nq), BF16)] + w_shape,
        scratch_shapes=[pltpu.VMEM((A_KV_HEADS, tq + 2 * r, 128), BF16),
                        pltpu.VMEM((A_KV_HEADS, tq + 2 * r, 256), BF16)],
        compiler_params=_cparams("parallel", "parallel"),
        name="att0",
    )(sink, q, k2, k2, k2, v2, v2, v2, *w_args)
    return o, [c.reshape(w.shape[1:]) for c, (w, _) in zip(cast, cast_items)]


def _post_kernel(o_ref, wo_ref, x_ref, gm_ref, gf_ref, scf_ref, shf_ref, rw_ref, rb_ref, upper_ref,
                 x1_ref, h2_ref, cls_ref, rank_ref, cnt_ref, carry):
    mix = jnp.dot(o_ref[0], wo_ref[...], preferred_element_type=F32)
    x1 = x_ref[0] + gm_ref[0] * mix
    x1_ref[0] = x1
    h2 = _rms_mod(x1, gf_ref[...], scf_ref[0], shf_ref[0]).astype(BF16)
    h2_ref[0] = _pack_bf16_pairs(h2.astype(F32))
    tq = h2.shape[0]
    logits = lax.dot_general(rw_ref[...], h2, (((1,), (1,)), ((), ())), preferred_element_type=F32)
    scores = jax.nn.sigmoid(logits)
    biased = scores + rb_ref[...]
    pj = [biased[8 * j:8 * (j + 1)] for j in range(PER_GROUP)]
    sel = []
    for j in range(PER_GROUP):
        beaten = jnp.zeros((8, tq), F32)
        for j2 in range(PER_GROUP):
            if j2 == j:
                continue
            beats = (pj[j2] > pj[j]) | (pj[j2] == pj[j]) if j2 < j else (pj[j2] > pj[j])
            beaten = beaten + jnp.where(beats, 1.0, 0.0)
        sel.append(beaten < 2.0)
    gscore = jnp.zeros((8, tq), F32)
    for j in range(PER_GROUP):
        gscore = gscore + jnp.where(sel[j], pj[j], 0.0)
    gi = lax.broadcasted_iota(I32, (8, tq), 0).astype(F32)
    gmax = jnp.max(gscore, axis=0, keepdims=True)
    gidx = jnp.min(jnp.where(gscore == gmax, gi, 8.0), axis=0, keepdims=True)
    onehot = gi == gidx
    f = [jnp.max(jnp.where(onehot & sel[j], 1.0, 0.0), axis=0, keepdims=True) > 0.5
         for j in range(PER_GROUP)]
    pair = jnp.where(f[0], jnp.where(f[1], 0.0, jnp.where(f[2], 1.0, 2.0)),
                     jnp.where(f[1], jnp.where(f[2], 3.0, 4.0), 5.0))
    cls = (gidx * float(len(PAIRS)) + pair).astype(I32)
    cls_ref[0] = jnp.broadcast_to(cls, (8, tq))

    @pl.when((pl.program_id(0) == 0) & (pl.program_id(1) == 0))
    def _():
        carry[...] = jnp.zeros_like(carry)

    c = carry[...]
    ranks = []
    for blk in range(tq // RANK_BLOCK):
        cls_b = cls[:, RANK_BLOCK * blk:RANK_BLOCK * (blk + 1)]
        oh = lax.broadcasted_iota(I32, (32, RANK_BLOCK), 0) == cls_b
        ohf = jnp.where(oh, 1.0, 0.0)
        within = jnp.dot(ohf.astype(BF16), upper_ref[...], preferred_element_type=F32)
        ranks.append(jnp.sum(jnp.where(oh, within + c[:, 0:1], 0.0), axis=0, keepdims=True))
        c = c + jnp.sum(ohf, axis=1, keepdims=True)
    rank_ref[0] = jnp.broadcast_to(jnp.concatenate(ranks, axis=1).astype(I32), (8, tq))
    carry[...] = c
    cnt_ref[...] = c


def _post(o, wo, x, gm, gf, scf, shf, rw32, rb32):
    bsz, s, d = x.shape
    do = o.shape[-1]
    tq = TM_WIDE
    nt = s // tq
    tok = lambda w_: pl.BlockSpec((1, tq, w_), lambda b, i: (b, i, 0))
    per_b = pl.BlockSpec((1, 1, d), lambda b, i: (b, 0, 0))
    full = lambda a: pl.BlockSpec(a.shape, lambda b, i: (0,) * a.ndim)
    per_tile = pl.BlockSpec((1, 8, tq), lambda b, i: (b * nt + i, 0, 0))
    gf2 = gf.reshape(1, d)
    upper = (jnp.arange(RANK_BLOCK)[:, None] < jnp.arange(RANK_BLOCK)[None, :]).astype(BF16)
    return pl.pallas_call(
        _post_kernel,
        grid=(bsz, nt),
        in_specs=[tok(do), full(wo), tok(d), per_b, full(gf2), per_b, per_b, full(rw32), full(rb32),
                  full(upper)],
        out_specs=[tok(d), tok(d // 2), per_tile, per_tile, pl.BlockSpec((32, 128), lambda b, i: (0, 0))],
        out_shape=[jax.ShapeDtypeStruct((bsz, s, d), F32),
                   jax.ShapeDtypeStruct((bsz, s, d // 2), U32),
                   jax.ShapeDtypeStruct((bsz * nt, 8, tq), I32),
                   jax.ShapeDtypeStruct((bsz * nt, 8, tq), I32),
                   jax.ShapeDtypeStruct((32, 128), F32)],
        scratch_shapes=[pltpu.VMEM((32, 128), F32)],
        compiler_params=_cparams("arbitrary", "arbitrary"),
        name="post_attention",
    )(o, wo, x, gm, gf2, scf, shf, rw32, rb32, upper)


def _ffn_kernel(e1_ref, e2_ref, nused_ref, hs_ref, *refs):
    ys_ref = refs[-1]
    j = pl.program_id(0)
    tm = FFN_TM

    @pl.when(j * FFN_TILES_PER_STEP < nused_ref[0])
    def _():
        for sub in range(FFN_TILES_PER_STEP):
            wg1, wu1, wd1, rw1, wg2, wu2, wd2, rw2 = refs[8 * sub:8 * (sub + 1)]
            rows = slice(tm * sub, tm * (sub + 1))
            xf = _unpack_bf16_pairs(hs_ref[rows])
            x = xf.astype(BF16)
            gt1 = jnp.dot(x, wg1[0], preferred_element_type=F32)
            up1 = jnp.dot(x, wu1[0], preferred_element_type=F32)
            gt2 = jnp.dot(x, wg2[0], preferred_element_type=F32)
            up2 = jnp.dot(x, wu2[0], preferred_element_type=F32)
            a1 = ((gt1 * jax.nn.sigmoid(gt1)) * up1).astype(BF16)
            y1 = jnp.dot(a1, wd1[0], preferred_element_type=F32)
            a2 = ((gt2 * jax.nn.sigmoid(gt2)) * up2).astype(BF16)
            y2 = jnp.dot(a2, wd2[0], preferred_element_type=F32)
            s1 = jax.nn.sigmoid(jnp.sum(xf * rw1[0].astype(F32), axis=-1, keepdims=True))
            s2 = jax.nn.sigmoid(jnp.sum(xf * rw2[0].astype(F32), axis=-1, keepdims=True))
            den = s1 + s2
            ys_ref[rows] = _pack_bf16_pairs((s1 / den) * y1 + (s2 / den) * y2)

    @pl.when(j * FFN_TILES_PER_STEP >= nused_ref[0])
    def _():
        ys_ref[...] = jnp.zeros_like(ys_ref)


def _ffn(hs, e1, e2, nused, wg, wu, wd, rw3):
    npad, dh = hs.shape
    d = 2 * dh
    per = FFN_TILES_PER_STEP
    rows = per * FFN_TM
    de = wg.shape[-1]
    assert npad % rows == 0
    row = pl.BlockSpec((rows, dh), lambda j, e1, e2, nu: (jnp.minimum(j, (nu[0] - 1) // per), 0))
    w_specs = []
    for sub in range(per):
        for which in range(2):
            pick = lambda j, e1, e2, nu, sub=sub, which=which: ((e2 if which else e1)[per * j + sub], 0, 0)
            w_specs += [pl.BlockSpec((1, d, de), pick), pl.BlockSpec((1, d, de), pick),
                        pl.BlockSpec((1, de, d), pick), pl.BlockSpec((1, 1, d), pick)]
    grid_spec = pltpu.PrefetchScalarGridSpec(
        num_scalar_prefetch=3,
        grid=(npad // rows,),
        in_specs=[row] + w_specs,
        out_specs=pl.BlockSpec((rows, dh), lambda j, e1, e2, nu: (j, 0)),
    )
    return pl.pallas_call(
        _ffn_kernel,
        grid_spec=grid_spec,
        out_shape=jax.ShapeDtypeStruct((npad, dh), U32),
        compiler_params=_cparams("arbitrary"),
        name="expert_ffn",
    )(e1, e2, nused, hs, *([wg, wu, wd, rw3] * (2 * per)))


def _sc_mesh():
    return plsc.VectorSubcoreMesh(core_axis_name="c", subcore_axis_name="s",
                                  num_cores=SC_CORES_V7X, num_subcores=SC_SUBCORES_V7X)


def _sc_steps(n_rows):
    workers = SC_CORES_V7X * SC_SUBCORES_V7X
    per_w = n_rows // workers
    ch = min(SC_ROWS_PER_STEP, per_w // 2)
    n = per_w // ch
    assert per_w * workers == n_rows and n * ch == per_w and n % 2 == 0 and ch % 8 == 0
    return per_w, ch, n


def _sc_scratch(ch, d, dtype):
    return [pltpu.VMEM((2, ch), I32), pltpu.VMEM((2, ch, d), dtype),
            pltpu.SemaphoreType.DMA((2,)), pltpu.SemaphoreType.DMA((2,))]


def _sc_scatter_rows(src, idx, n_out):
    n_in, d = src.shape
    per_w, ch, n = _sc_steps(n_in)

    def body(src_hbm, idx_hbm, out_hbm, idx_v, rows_v, rsem, ssem):
        base = (lax.axis_index("s") * SC_CORES_V7X + lax.axis_index("c")) * per_w

        def read(i, b):
            return pltpu.make_async_copy(src_hbm.at[pl.ds(base + i * ch, ch)], rows_v.at[b], rsem.at[b])

        def scatter(b):
            return pltpu.make_async_copy(rows_v.at[b], out_hbm.at[idx_v.at[b]], ssem.at[b])

        def start_read(i, b):
            pltpu.sync_copy(idx_hbm.at[pl.ds(base + i * ch, ch)], idx_v.at[b])
            read(i, b).start()

        start_read(0, 0)

        @pl.loop(0, n, step=2)
        def _(i):
            for b in (0, 1):
                cur = i + b

                @pl.when(cur + 1 < n)
                def _():
                    @pl.when(cur >= 1)
                    def _():
                        scatter(1 - b).wait()
                    start_read(cur + 1, 1 - b)

                read(cur, b).wait()
                scatter(b).start()

        scatter(0).wait()
        scatter(1).wait()

    return pl.kernel(
        body, mesh=_sc_mesh(),
        out_type=jax.ShapeDtypeStruct((n_out, d), src.dtype),
        scratch_types=_sc_scratch(ch, d, src.dtype),
        name="sc_scatter_rows",
    )(src, idx)


def _sc_gather_rows(table, idx):
    n_out = idx.shape[0]
    d = table.shape[1]
    per_w, ch, n = _sc_steps(n_out)

    def body(table_hbm, idx_hbm, out_hbm, idx_v, rows_v, gsem, wsem):
        base = (lax.axis_index("s") * SC_CORES_V7X + lax.axis_index("c")) * per_w

        def gather(b):
            return pltpu.make_async_copy(table_hbm.at[idx_v.at[b]], rows_v.at[b], gsem.at[b])

        def write(i, b):
            return pltpu.make_async_copy(rows_v.at[b], out_hbm.at[pl.ds(base + i * ch, ch)], wsem.at[b])

        def start_gather(i, b):
            pltpu.sync_copy(idx_hbm.at[pl.ds(base + i * ch, ch)], idx_v.at[b])
            gather(b).start()

        start_gather(0, 0)

        @pl.loop(0, n, step=2)
        def _(i):
            for b in (0, 1):
                cur = i + b

                @pl.when(cur + 1 < n)
                def _():
                    @pl.when(cur >= 1)
                    def _():
                        write(cur - 1, 1 - b).wait()
                    start_gather(cur + 1, 1 - b)

                gather(b).wait()
                write(cur, b).start()

        write(n - 2, 0).wait()
        write(n - 1, 1).wait()

    return pl.kernel(
        body, mesh=_sc_mesh(),
        out_type=jax.ShapeDtypeStruct((n_out, d), table.dtype),
        scratch_types=_sc_scratch(ch, d, table.dtype),
        name="sc_gather_rows",
    )(table, idx)


def _moe(h2p, cls3, rank3, cnt, wg, wu, wd, rw3):
    bsz, s, dh = h2p.shape
    t = bsz * s
    tm = FFN_TM
    counts = cnt[:N_CLASSES, 0].astype(I32)
    padded = ((counts + tm - 1) // tm) * tm
    upto = jnp.arange(N_CLASSES)[:, None] >= jnp.arange(N_CLASSES)[None, :]
    ends = jnp.sum(jnp.where(upto, padded[None, :], 0), axis=1)
    base = ends - padded
    cls2 = cls3[:, 0, :]
    pos2 = rank3[:, 0, :]
    for c in range(N_CLASSES):
        pos2 = pos2 + jnp.where(cls2 == c, base[c], 0)
    pos = pos2.reshape(t)
    ntiles = t // tm + N_CLASSES
    tile_start = jnp.arange(ntiles, dtype=I32) * tm
    tcls = jnp.minimum(jnp.sum((ends[None, :] <= tile_start[:, None]).astype(I32), axis=1), N_CLASSES - 1)
    nused = (ends[-1:] // tm).astype(I32)
    grp = tcls // len(PAIRS)
    pr = tcls % len(PAIRS)
    first = sum(jnp.where(pr == i, p[0], 0) for i, p in enumerate(PAIRS))
    second = sum(jnp.where(pr == i, p[1], 0) for i, p in enumerate(PAIRS))
    e1 = (grp * PER_GROUP + first).astype(I32)
    e2 = (grp * PER_GROUP + second).astype(I32)
    hs = _sc_scatter_rows(h2p.reshape(t, dh), pos, ntiles * tm)
    ys = _ffn(hs, e1, e2, nused, wg, wu, wd, rw3)
    return _sc_gather_rows(ys, pos).reshape(bsz, s, dh)


def _qkv1_kernel(x_ref, y_ref, gfp_ref, g_ref, sc_ref, sh_ref, w_ref, cos_ref, sin_ref,
                 wg_ref, wu_ref, wd_ref, x2_ref, *rest):
    outs = rest[:9]
    _side_cast((wg_ref, wu_ref, wd_ref), rest[9:12])
    hb_scr, ysc, ysc2 = rest[12:]
    tm = x_ref.shape[1]
    x2 = x_ref[0] + gfp_ref[0] * _unpack_bf16_pairs(y_ref[0])
    x2_ref[0] = x2
    hb_scr[...] = _rms_mod(x2, g_ref[...], sc_ref[0], sh_ref[0]).astype(BF16)
    cos = cos_ref[0]
    sin = sin_ref[0]
    lane_lo = (lax.broadcasted_iota(I32, cos.shape, 1) % HEAD_DIM) < (HEAD_DIM // 2)
    for gi, (_, dil) in reversed(list(enumerate(B_GROUPS))):
        for j in range(3):
            c = gi * 3 + j
            y = jnp.dot(hb_scr[...], w_ref[:, B_WIDTH * c:B_WIDTH * (c + 1)], preferred_element_type=F32)
            out = outs[c]
            for cc in range(B_WIDTH // 128):
                lanes = slice(128 * cc, 128 * (cc + 1))
                r = y[:, lanes]
                if j < 2:
                    r = _rope_apply(r, cos, sin, lane_lo)
                if j == 0:
                    r = r * Q_SCALE
                if dil == 1:
                    out[0, 0, :, lanes] = r.astype(BF16)
                    continue
                ysc[cc] = r
                if dil % 16:
                    for rr in range(dil):
                        out[0, rr, :, lanes] = ysc[cc, pl.ds(rr, tm // dil, stride=dil), :].astype(BF16)
                    continue
                quarter = tm // 4
                inner = dil // 4
                for r1 in range(4):
                    ysc2[cc, r1 * quarter:(r1 + 1) * quarter] = ysc[cc, pl.ds(r1, quarter, stride=4), :]
                for r1 in range(4):
                    for r2 in range(inner):
                        out[0, r1 + 4 * r2, :, lanes] = ysc2[
                            cc, pl.ds(r1 * quarter + r2, tm // dil, stride=inner), :].astype(BF16)


def _qkv1(x, y, gfp, g, sc, sh, w, cos, sin, expert_ws, layer):
    bsz, s, d = x.shape
    tm = TM
    grid = (bsz, s // tm)
    w_in, w_out, w_shape, w_args = _side_cast_specs([(w_, layer) for w_ in expert_ws], grid)
    tok = lambda w_: pl.BlockSpec((1, tm, w_), lambda b, i: (b, i, 0))
    per_b = pl.BlockSpec((1, 1, d), lambda b, i: (b, 0, 0))
    out_specs = [tok(d)]
    out_shape = [jax.ShapeDtypeStruct((bsz, s, d), F32)]
    for _, dil in B_GROUPS:
        for _ in range(3):
            out_specs.append(pl.BlockSpec((1, dil, tm // dil, B_WIDTH), lambda b, i: (b, 0, i, 0)))
            out_shape.append(jax.ShapeDtypeStruct((bsz, dil, s // dil, B_WIDTH), BF16))
    outs = pl.pallas_call(
        _qkv1_kernel,
        grid=grid,
        in_specs=[tok(d), tok(d // 2), per_b, pl.BlockSpec((1, d), lambda b, i: (0, 0)), per_b, per_b,
                  pl.BlockSpec(w.shape, lambda b, i: (0, 0)), tok(128), tok(128)] + w_in,
        out_specs=out_specs + w_out,
        out_shape=out_shape + w_shape,
        scratch_shapes=[pltpu.VMEM((tm, d), BF16), pltpu.VMEM((B_WIDTH // 128, tm, 128), F32),
                        pltpu.VMEM((B_WIDTH // 128, tm, 128), F32)],
        compiler_params=_cparams("parallel", "parallel"),
        name="qkv1",
    )(x, y, gfp, g.reshape(1, d), sc, sh, w, cos, sin, *w_args)
    return outs[:10], [c.reshape(w_.shape[1:]) for c, w_ in zip(outs[10:], expert_ws)]


def _att1_window_pad(lseg):
    qb = min(2 * B_RADIUS, lseg)
    return -(qb + 2 * B_RADIUS) % 128


def _att1_kernel(*refs, chunk, seq):
    ins = refs[:21]
    o_ref = refs[21]
    scr = refs[22:]
    nat_o, nat_d, nat_m = scr[0:3], scr[3:6], scr[6:9]
    n = pl.program_id(1)
    rad = B_RADIUS
    npair = B_WIDTH // 128

    def window(cur, prev, nxt, res, lo, hi, lseg, lanes):
        parts = []
        if lo < 0:
            parts.append(prev[0, res, :, lanes])
        parts.append(cur[0, res, max(lo, 0):min(hi, lseg), lanes])
        if hi > lseg:
            parts.append(nxt[0, res, :, lanes])
        if hi > lseg + rad:
            parts.append(nxt[0, res, :, lanes])
        return jnp.concatenate(parts, axis=0) if len(parts) > 1 else parts[0]

    for c in range(npair):
        lanes = slice(128 * c, 128 * (c + 1))
        for gi, (_, dil) in enumerate(B_GROUPS):
            q_ref, kc, kp, kn, vc, vp, vn = ins[7 * gi:7 * gi + 7]
            lseg = chunk // dil
            qb = min(2 * rad, lseg)
            kb_n = qb + 2 * rad + _att1_window_pad(lseg)
            lstr = seq // dil
            rel = lax.broadcasted_iota(I32, (qb, kb_n), 1) - lax.broadcasted_iota(I32, (qb, kb_n), 0)
            band_bias = jnp.where((rel >= 0) & (rel <= 2 * rad), 0.0, NEG)
            jrow = lax.broadcasted_iota(I32, (1, kb_n), 1)
            first_head = lax.broadcasted_iota(I32, (qb, 128), 1) < HEAD_DIM
            zero = jnp.zeros((qb, 128), BF16)
            ones = jnp.ones((kb_n, 128), BF16)
            for res in range(dil):
                for q0 in range(0, lseg, qb):
                    apos = n * lseg + q0 - rad + jrow
                    bias = band_bias + jnp.where((apos >= 0) & (apos < lstr), 0.0, NEG)
                    rows = pl.ds(q0, qb) if dil == 1 else pl.ds(res + dil * q0, qb, stride=dil)
                    qc = q_ref[0, res, q0:q0 + qb, lanes]
                    kb = window(kc, kp, kn, res, q0 - rad, q0 - rad + kb_n, lseg, lanes)
                    vb = jnp.concatenate(
                        [window(vc, vp, vn, res, q0 - rad, q0 - rad + kb_n, lseg, lanes), ones], axis=1)
                    qs = jnp.concatenate(
                        [jnp.where(first_head, qc, zero), jnp.where(first_head, zero, qc)], axis=0)
                    s2 = lax.dot_general(qs, kb, (((1,), (1,)), ((), ())), preferred_element_type=F32)
                    ps, ms = [], []
                    for t2 in range(2):
                        s = s2[qb * t2:qb * (t2 + 1)] + bias
                        m = jnp.max(s, axis=-1, keepdims=True)
                        ps.append(jnp.exp2((s - m).astype(BF16)))
                        ms.append(m)
                    pv = jnp.dot(jnp.concatenate(ps, axis=0), vb, preferred_element_type=F32)
                    nat_o[gi][rows, :] = jnp.where(first_head, pv[0:qb, 0:128], pv[qb:2 * qb, 0:128])
                    nat_d[gi][rows, :] = jnp.where(first_head, pv[0:qb, 128:256], pv[qb:2 * qb, 128:256])
                    nat_m[gi][rows, :] = jnp.where(first_head, ms[0], ms[1])

        ms = [nat_m[gi][...] for gi in range(3)]
        mx = jnp.maximum(jnp.maximum(ms[0], ms[1]), ms[2])
        w = [jnp.exp2(x - mx) for x in ms]
        num = w[0] * nat_o[0][...] + w[1] * nat_o[1][...] + w[2] * nat_o[2][...]
        den = w[0] * nat_d[0][...] + w[1] * nat_d[1][...] + w[2] * nat_d[2][...]
        o_ref[0, :, lanes] = (num / den).astype(BF16)


def _att1(qkv):
    bsz = qkv[0].shape[0]
    seq = qkv[0].shape[1] * qkv[0].shape[2]
    chunk = ATT1_CHUNK
    rad = B_RADIUS
    in_specs, args, scratch = [], [], []
    for gi, (_, dil) in enumerate(B_GROUPS):
        q, k, v = qkv[3 * gi:3 * gi + 3]
        lseg = chunk // dil
        per = lseg // rad
        last = seq // dil // rad - 1
        cur = pl.BlockSpec((1, dil, lseg, B_WIDTH), lambda b, n: (b, 0, n, 0))
        prev = pl.BlockSpec((1, dil, rad, B_WIDTH),
                            lambda b, n, per=per: (b, 0, jnp.maximum(n * per - 1, 0), 0))
        nxt = pl.BlockSpec((1, dil, rad, B_WIDTH),
                           lambda b, n, per=per, last=last: (b, 0, jnp.minimum(n * per + per, last), 0))
        in_specs += [cur, cur, prev, nxt, cur, prev, nxt]
        args += [q, k, k, k, v, v, v]
    for _ in range(3):
        for _ in B_GROUPS:
            scratch.append(pltpu.VMEM((chunk, 128), F32))
    return pl.pallas_call(
        functools.partial(_att1_kernel, chunk=chunk, seq=seq),
        grid=(bsz, seq // chunk),
        in_specs=in_specs,
        out_specs=pl.BlockSpec((1, chunk, B_WIDTH), lambda b, n: (b, n, 0)),
        out_shape=jax.ShapeDtypeStruct((bsz, seq, B_WIDTH), BF16),
        scratch_shapes=scratch,
        compiler_params=_cparams("parallel", "parallel"),
        name="att1",
    )(*args)


def _final_kernel(x_ref, y_ref, gfp_ref, g_ref, o_ref):
    x = x_ref[0] + gfp_ref[0] * _unpack_bf16_pairs(y_ref[0])
    ms = jnp.mean(x * x, axis=-1, keepdims=True)
    o_ref[0] = (x * lax.rsqrt(ms + NORM_EPS)) * g_ref[...]


def _final(x, y, gfp, g):
    bsz, s, d = x.shape
    tm = TM_WIDE
    tok = pl.BlockSpec((1, tm, d), lambda b, i: (b, i, 0))
    return pl.pallas_call(
        _final_kernel,
        grid=(bsz, s // tm),
        in_specs=[tok, pl.BlockSpec((1, tm, d // 2), lambda b, i: (b, i, 0)),
                  pl.BlockSpec((1, 1, d), lambda b, i: (b, 0, 0)),
                  pl.BlockSpec((1, d), lambda b, i: (0, 0))],
        out_specs=tok,
        out_shape=jax.ShapeDtypeStruct((bsz, s, d), F32),
        compiler_params=_cparams("parallel", "parallel"),
        name="final_norm",
    )(x, y, gfp, g.reshape(1, d))


def kernel(x, c, positions, ada_w, ada_b, norm_mix_g, norm_ffn_g, a_w_qkv, a_w_o, a_sink, b_w_qkv, b_w_o,
           router_w, router_bias, exp_w_gate, exp_w_up, exp_w_down, final_norm_g):
    bsz, s, d = x.shape
    assert d == D_MODEL and ada_w.shape[0] == 2 and s % ATT1_CHUNK == 0
    assert all(w_ // (2 * dil) == B_RADIUS for w_, dil in B_GROUPS)

    mod = _modulation(c, ada_w, ada_b)
    mods = [[mod[l][:, k * d:(k + 1) * d].reshape(bsz, 1, d) for k in range(6)] for l in range(2)]
    cos, sin, sin_paired = _rope_tables(positions)

    rw_t = router_w.T.astype(BF16)
    rows = jnp.array([PER_GROUP * g + j if g < N_EXPERT_GROUPS else 0
                      for j in range(PER_GROUP) for g in range(8)], I32)
    live = jnp.array([1.0 if g < N_EXPERT_GROUPS else 0.0
                      for j in range(PER_GROUP) for g in range(8)], F32)
    rw32 = (rw_t[rows].astype(F32) * live[:, None]).astype(BF16)
    rb32 = jnp.where(live > 0, router_bias.astype(F32)[rows], NEG).reshape(32, 1)
    rw3 = rw_t.reshape(N_EXPERTS, 1, d)
    expert_ws = (exp_w_gate, exp_w_up, exp_w_down)

    sh_m, sc_m, g_m, sh_f, sc_f, g_f = mods[0]
    nq, nkv = A_Q_HEADS * HEAD_DIM, A_KV_HEADS * HEAD_DIM
    columns = np.concatenate([_paired_layout_columns(nq), nq + _paired_layout_columns(nkv),
                              nq + nkv + np.arange(nkv)])
    q, k, v = _qkv0(x, norm_mix_g[0], sc_m, sh_m, a_w_qkv[0][:, columns].astype(BF16), cos, sin_paired)
    o, cast = _att0(q, k, v, a_sink[0].astype(F32),
                    [(w_, 0) for w_ in expert_ws] + [(a_w_o, 0), (b_w_o, 0), (b_w_qkv, 0)])
    experts, (a_wo, b_wo, b_wqkv) = cast[:3], cast[3:]
    x1, *routed = _post(o, a_wo, x, g_m, norm_ffn_g[0], sc_f, sh_f, rw32, rb32)
    y = _moe(*routed, *experts, rw3)
    g_f_prev = g_f

    sh_m, sc_m, g_m, sh_f, sc_f, g_f = mods[1]
    outs, experts = _qkv1(x1, y, g_f_prev, norm_mix_g[1], sc_m, sh_m, b_wqkv, cos, sin, expert_ws, 1)
    x2, qkv = outs[0], outs[1:]
    o = _att1(qkv)
    x3, *routed = _post(o, b_wo, x2, g_m, norm_ffn_g[1], sc_f, sh_f, rw32, rb32)
    y = _moe(*routed, *experts, rw3)

    return _final(x3, y, g_f, final_norm_g)
```

```python
import functools

import jax
import jax.numpy as jnp
from jax import lax
from jax.experimental import pallas as pl
from jax.experimental.pallas import tpu as pltpu
from jax.experimental.pallas import tpu_sc as plsc

F32, BF16, I32, U32 = jnp.float32, jnp.bfloat16, jnp.int32, jnp.uint32

D_MODEL = 1024
HEAD_DIM = 64
ROPE_THETA = 10000.0
NORM_EPS = 1e-6
A_Q_HEADS = 16
A_KV_HEADS = 4
A_GROUP = A_Q_HEADS // A_KV_HEADS
A_RADIUS = 128
B_GROUPS = ((128, 1), (512, 4), (2048, 16))
B_HEADS = 8
B_RADIUS = 64
B_WIDTH = B_HEADS * HEAD_DIM
N_EXPERTS = 16
N_EXPERT_GROUPS = 4
PER_GROUP = N_EXPERTS // N_EXPERT_GROUPS
D_EXPERT = D_MODEL // 2
PAIRS = ((0, 1), (0, 2), (0, 3), (1, 2), (1, 3), (2, 3))
N_CLASSES = N_EXPERT_GROUPS * len(PAIRS)
NEG = -1e30
LOG2E = 1.4426950408889634
Q_SCALE = HEAD_DIM ** -0.5 * LOG2E

VMEM_LIMIT = 56 * 1024 * 1024
TM = 512
ROPE_TS = 2048
TM_WIDE = 1024
RANK_BLOCK = 512
ATT1_CHUNK = 1024
FFN_TM = 256
FFN_TILES_PER_STEP = 2
SC_CORES_V7X = 2
SC_SUBCORES_V7X = 16
SC_ROWS_PER_STEP = 64


def _cparams(*sem):
    return pltpu.CompilerParams(dimension_semantics=sem, vmem_limit_bytes=VMEM_LIMIT)


def _rms_mod(x, g, sc, sh):
    ms = jnp.mean(x * x, axis=-1, keepdims=True)
    return (x * lax.rsqrt(ms + NORM_EPS)) * (g * (1.0 + sc)) + sh


def _rope_apply(y, cos, sin_signed, lane_lo):
    sw = jnp.where(lane_lo, pltpu.roll(y, 96, 1), pltpu.roll(y, 32, 1))
    return y * cos + sw * sin_signed


def _rope_apply_paired(y, cos, sin_paired):
    return y * cos + pltpu.roll(y, HEAD_DIM, 1) * sin_paired


def _paired_layout(w):
    k, n = w.shape
    half = HEAD_DIM // 2
    return w.reshape(k, n // 128, 2, 2, half).transpose(0, 1, 3, 2, 4).reshape(k, n)


def _pack_bf16_pairs(x):
    n = x.shape[1] // 2
    lo = lax.bitcast_convert_type(x[:, :n].astype(BF16).astype(F32), U32)
    hi = lax.bitcast_convert_type(x[:, n:].astype(BF16).astype(F32), U32)
    return (hi & jnp.uint32(0xFFFF0000)) | (lo >> 16)


def _unpack_bf16_pairs(p):
    lo = lax.bitcast_convert_type(p << 16, F32)
    hi = lax.bitcast_convert_type(p & jnp.uint32(0xFFFF0000), F32)
    return jnp.concatenate([lo, hi], axis=1)


def _dup_head(x, first_head, which):
    rolled = pltpu.roll(x, HEAD_DIM, 1)
    return jnp.where(first_head, x, rolled) if which == 0 else jnp.where(first_head, rolled, x)


def _mod_kernel(ct_ref, w_ref, b_ref, o_ref, *, bsz):
    ct = ct_ref[...]
    s = ct * jax.nn.sigmoid(ct)
    w = w_ref[0]
    rows = [jnp.sum(w * s[:, b:b + 1], axis=0, keepdims=True) + b_ref[0] for b in range(bsz)]
    o_ref[0] = jnp.concatenate(rows + [jnp.zeros_like(rows[0])] * (8 - bsz), axis=0)


def _modulation(c, ada_w, ada_b):
    depth, d, n = ada_w.shape
    bsz = c.shape[0]
    assert bsz <= 8
    tn = 1536
    out = pl.pallas_call(
        functools.partial(_mod_kernel, bsz=bsz),
        grid=(depth, n // tn),
        in_specs=[pl.BlockSpec((d, bsz), lambda l, j: (0, 0)),
                  pl.BlockSpec((1, d, tn), lambda l, j: (l, 0, j)),
                  pl.BlockSpec((1, 1, tn), lambda l, j: (l, 0, j))],
        out_specs=pl.BlockSpec((1, 8, tn), lambda l, j: (l, 0, j)),
        out_shape=jax.ShapeDtypeStruct((depth, 8, n), F32),
        compiler_params=_cparams("arbitrary", "arbitrary"),
        name="modulation",
    )(c.T, ada_w, ada_b.reshape(depth, 1, n))
    return out[:, :bsz]


def _rope_kernel(pos_ref, invf_ref, sign_ref, cos_ref, sin_ref, sinp_ref):
    p = pos_ref[0].astype(F32)
    n = p.shape[0]
    quarter = lax.broadcasted_iota(I32, (n, 128), 1) // (HEAD_DIM // 2)
    pos4 = jnp.where(quarter == 0, p[:, 0:1],
                     jnp.where(quarter == 1, p[:, 1:2], jnp.where(quarter == 2, p[:, 2:3], p[:, 3:4])))
    ang = pos4 * invf_ref[...]
    dense = (jnp.cos(ang), jnp.sin(ang))
    for u in range(4):
        spread = []
        for d in dense:
            e = jnp.where(quarter == u, d, 0.0)
            e = e + pltpu.roll(e, HEAD_DIM // 2, 1)
            spread.append(e + pltpu.roll(e, HEAD_DIM, 1))
        rows = pl.ds(u, n, stride=4)
        cos_ref[0, rows, :] = spread[0]
        sin_ref[0, rows, :] = spread[1] * sign_ref[0:1]
        sinp_ref[0, rows, :] = spread[1] * sign_ref[1:2]


def _rope_tables(positions):
    bsz, s = positions.shape
    inv_freq = ROPE_THETA ** (-jnp.arange(0, HEAD_DIM, 2, dtype=F32) / HEAD_DIM)
    invf = jnp.tile(inv_freq, 4).reshape(1, 128)
    half = HEAD_DIM // 2
    minus, plus = -jnp.ones((half,), F32), jnp.ones((half,), F32)
    sign = jnp.stack([jnp.concatenate([minus, plus, minus, plus]),
                      jnp.concatenate([minus, minus, plus, plus])])
    ts = ROPE_TS
    return pl.pallas_call(
        _rope_kernel,
        grid=(bsz, s // ts),
        in_specs=[pl.BlockSpec((1, ts // 4, 4), lambda b, i: (b, i, 0)),
                  pl.BlockSpec((1, 128), lambda b, i: (0, 0)),
                  pl.BlockSpec((2, 128), lambda b, i: (0, 0))],
        out_specs=[pl.BlockSpec((1, ts, 128), lambda b, i: (b, i, 0))] * 3,
        out_shape=[jax.ShapeDtypeStruct((bsz, s, 128), F32)] * 3,
        compiler_params=_cparams("parallel", "parallel"),
        name="rope_tables",
    )(positions.reshape(bsz, s // 4, 4), invf, sign)


def _qkv0_kernel(x_ref, g_ref, sc_ref, sh_ref, w_ref, cos_ref, sin_ref,
                 q_ref, k_ref, v_ref, hb_scr):
    hb_scr[...] = _rms_mod(x_ref[0], g_ref[...], sc_ref[0], sh_ref[0]).astype(BF16)
    cos = cos_ref[0]
    sin = sin_ref[0]
    lane = lax.broadcasted_iota(I32, cos.shape, 1)
    x_lo = (lane % HEAD_DIM) < (HEAD_DIM // 2)
    nq = A_Q_HEADS * HEAD_DIM
    nkv = A_KV_HEADS * HEAD_DIM
    for c in range(nq // 256):
        y = jnp.dot(hb_scr[...], w_ref[:, 256 * c:256 * (c + 1)], preferred_element_type=F32)
        for cc in range(2):
            r = _rope_apply_paired(y[:, 128 * cc:128 * (cc + 1)], cos, sin) * Q_SCALE
            q_ref[0, :, 256 * c + 128 * cc:256 * c + 128 * (cc + 1)] = r.astype(BF16)
    first_head = lane < HEAD_DIM
    y = jnp.dot(hb_scr[...], w_ref[:, nq:nq + nkv], preferred_element_type=F32)
    for cc in range(nkv // 128):
        r = _rope_apply_paired(y[:, 128 * cc:128 * (cc + 1)], cos, sin)
        k_ref[0, :, 256 * cc:256 * cc + 128] = jnp.where(
            x_lo, r, pltpu.roll(r, HEAD_DIM // 2, 1)).astype(BF16)
        k_ref[0, :, 256 * cc + 128:256 * cc + 256] = jnp.where(
            x_lo, pltpu.roll(r, 128 - HEAD_DIM // 2, 1), r).astype(BF16)
    y = jnp.dot(hb_scr[...], w_ref[:, nq + nkv:nq + 2 * nkv], preferred_element_type=F32)
    for cc in range(nkv // 128):
        r = y[:, 128 * cc:128 * (cc + 1)]
        for which in range(2):
            g = 2 * cc + which
            v_ref[0, :, 128 * g:128 * (g + 1)] = _dup_head(r, first_head, which).astype(BF16)


def _qkv0(x, g, sc, sh, w, cos, sin):
    bsz, s, d = x.shape
    nq = A_Q_HEADS * HEAD_DIM
    nkv = A_KV_HEADS * HEAD_DIM
    tm = TM_WIDE
    tok = lambda w_: pl.BlockSpec((1, tm, w_), lambda b, i: (b, i, 0))
    per_b = pl.BlockSpec((1, 1, d), lambda b, i: (b, 0, 0))
    return pl.pallas_call(
        _qkv0_kernel,
        grid=(bsz, s // tm),
        in_specs=[tok(d), pl.BlockSpec((1, d), lambda b, i: (0, 0)), per_b, per_b,
                  pl.BlockSpec(w.shape, lambda b, i: (0, 0)), tok(128), tok(128)],
        out_specs=[tok(nq), tok(2 * nkv), tok(2 * nkv)],
        out_shape=[jax.ShapeDtypeStruct((bsz, s, nq), BF16),
                   jax.ShapeDtypeStruct((bsz, s, 2 * nkv), BF16),
                   jax.ShapeDtypeStruct((bsz, s, 2 * nkv), BF16)],
        scratch_shapes=[pltpu.VMEM((tm, d), BF16)],
        compiler_params=_cparams("parallel", "parallel"),
        name="qkv0",
    )(x, g.reshape(1, d), sc, sh, w, cos, sin)


def _side_cast_specs(items, grid):
    nsteps = grid[0] * grid[1]
    in_specs, out_specs, out_shape, args = [], [], [], []
    for w, layer in items:
        n = w.shape[-1]
        per_layer = w[0].size // n
        rows = per_layer // nsteps
        assert rows * nsteps == per_layer and rows % 16 == 0
        in_specs.append(pl.BlockSpec(
            (rows, n), lambda b, i, layer=layer: (layer * nsteps + b * grid[1] + i, 0)))
        out_specs.append(pl.BlockSpec((rows, n), lambda b, i: (b * grid[1] + i, 0)))
        out_shape.append(jax.ShapeDtypeStruct((per_layer, n), BF16))
        args.append(w.reshape(w.shape[0] * per_layer, n))
    return in_specs, out_specs, out_shape, args


def _side_cast(srcs, dsts):
    for src, dst in zip(srcs, dsts):
        dst[...] = src[...].astype(BF16)


def _att0_kernel(sink_ref, q_ref, kc_ref, kp_ref, kn_ref, vc_ref, vp_ref, vn_ref, *rest, tq, seq, ncast):
    o_ref = rest[ncast]
    kf, vf = rest[2 * ncast + 1:]
    _side_cast(rest[:ncast], rest[ncast + 1:2 * ncast + 1])
    i = pl.program_id(1)
    r = A_RADIUS
    ones = jnp.ones((tq + 2 * r, 128), BF16)
    for g in range(A_KV_HEADS):
        lanes = slice(128 * g, 128 * (g + 1))
        kf[g, 0:r] = kp_ref[0, :, lanes]
        kf[g, r:r + tq] = kc_ref[0, :, lanes]
        kf[g, r + tq:r + tq + r] = kn_ref[0, :, lanes]
        vf[g, 0:r, 0:128] = vp_ref[0, :, lanes]
        vf[g, r:r + tq, 0:128] = vc_ref[0, :, lanes]
        vf[g, r + tq:r + tq + r, 0:128] = vn_ref[0, :, lanes]
        vf[g, :, 128:256] = ones

    ii = lax.broadcasted_iota(I32, (r, r), 0)
    jj = lax.broadcasted_iota(I32, (r, r), 1)
    first_head = jj < HEAD_DIM
    q_first = (jj % HEAD_DIM) < (HEAD_DIM // 2)
    zero = jnp.zeros((r, 128), BF16)

    for j in range(tq // r):
        q0 = j * r
        base = i * tq + q0
        bias_lo = jnp.where((jj >= ii) & (base - r + jj >= 0), 0.0, NEG)
        bias_hi = jnp.where((jj <= ii) & (base + r + jj < seq), 0.0, NEG)
        for g in range(A_KV_HEADS):
            kb = kf[g, pl.ds(q0, 3 * r), :]
            vb = vf[g, pl.ds(q0, 3 * r), :]
            qs = []
            for cc in range(2):
                c = 2 * g + cc
                qc = q_ref[0, pl.ds(q0, r), 128 * c:128 * (c + 1)]
                qs += [jnp.where(q_first, qc, zero), jnp.where(q_first, zero, qc)]
            s4 = lax.dot_general(jnp.concatenate(qs, axis=0), kb, (((1,), (1,)), ((), ())),
                                 preferred_element_type=F32)
            ps, ms = [], []
            for t in range(A_GROUP):
                s = s4[r * t:r * (t + 1)]
                a0 = s[:, 0:r] + bias_lo
                a1 = s[:, r:2 * r]
                a2 = s[:, 2 * r:3 * r] + bias_hi
                sink = sink_ref[A_GROUP * g + t] * LOG2E
                m = jnp.max(jnp.maximum(jnp.maximum(a0, a1), a2), axis=-1, keepdims=True)
                m = jnp.maximum(m, sink)
                ps.append(jnp.exp2(jnp.concatenate([a0 - m, a1 - m, a2 - m], axis=1).astype(BF16)))
                ms.append(m)
            pv = jnp.dot(jnp.concatenate(ps, axis=0), vb, preferred_element_type=F32)
            os_ = []
            for t in range(A_GROUP):
                sink = sink_ref[A_GROUP * g + t] * LOG2E
                den = pv[r * t:r * (t + 1), 128:256] + jnp.exp2(sink - ms[t])
                os_.append(pv[r * t:r * (t + 1), 0:128] / den)
            for cc in range(2):
                c = 2 * g + cc
                o_ref[0, pl.ds(q0, r), 128 * c:128 * (c + 1)] = jnp.where(
                    first_head, os_[2 * cc], os_[2 * cc + 1]).astype(BF16)


def _att0(q, k2, v2, sink, cast_items):
    bsz, s, nq = q.shape
    nkv2 = k2.shape[-1]
    tq = TM_WIDE
    r = A_RADIUS
    per = tq // r
    last = s // r - 1
    grid = (bsz, s // tq)
    cur = lambda w_: pl.BlockSpec((1, tq, w_), lambda b, i: (b, i, 0))
    prev = pl.BlockSpec((1, r, nkv2), lambda b, i: (b, jnp.maximum(i * per - 1, 0), 0))
    nxt = pl.BlockSpec((1, r, nkv2), lambda b, i: (b, jnp.minimum(i * per + per, last), 0))
    w_in, w_out, w_shape, w_args = _side_cast_specs(cast_items, grid)
    o, *cast = pl.pallas_call(
        functools.partial(_att0_kernel, tq=tq, seq=s, ncast=len(cast_items)),
        grid=grid,
        in_specs=[pl.BlockSpec(memory_space=pltpu.SMEM),
                  cur(nq), cur(nkv2), prev, nxt, cur(nkv2), prev, nxt] + w_in,
        out_specs=[cur(nq)] + w_out,
        out_shape=[jax.ShapeDtypeStruct((bsz, s, nq), BF16)] + w_shape,
        scratch_shapes=[pltpu.VMEM((A_KV_HEADS, tq + 2 * r, 128), BF16),
                        pltpu.VMEM((A_KV_HEADS, tq + 2 * r, 256), BF16)],
        compiler_params=_cparams("parallel", "parallel"),
        name="att0",
    )(sink, q, k2, k2, k2, v2, v2, v2, *w_args)
    return o, [c.reshape(w.shape[1:]) for c, (w, _) in zip(cast, cast_items)]


def _post_kernel(o_ref, wo_ref, x_ref, gm_ref, gf_ref, scf_ref, shf_ref, rw_ref, rb_ref, upper_ref,
                 x1_ref, h2_ref, cls_ref, rank_ref, cnt_ref, carry):
    mix = jnp.dot(o_ref[0], wo_ref[...], preferred_element_type=F32)
    x1 = x_ref[0] + gm_ref[0] * mix
    x1_ref[0] = x1
    h2 = _rms_mod(x1, gf_ref[...], scf_ref[0], shf_ref[0]).astype(BF16)
    h2_ref[0] = _pack_bf16_pairs(h2.astype(F32))
    tq = h2.shape[0]
    logits = lax.dot_general(rw_ref[...], h2, (((1,), (1,)), ((), ())), preferred_element_type=F32)
    scores = jax.nn.sigmoid(logits)
    biased = scores + rb_ref[...]
    pj = [biased[8 * j:8 * (j + 1)] for j in range(PER_GROUP)]
    sel = []
    for j in range(PER_GROUP):
        beaten = jnp.zeros((8, tq), F32)
        for j2 in range(PER_GROUP):
            if j2 == j:
                continue
            beats = (pj[j2] > pj[j]) | (pj[j2] == pj[j]) if j2 < j else (pj[j2] > pj[j])
            beaten = beaten + jnp.where(beats, 1.0, 0.0)
        sel.append(beaten < 2.0)
    gscore = jnp.zeros((8, tq), F32)
    for j in range(PER_GROUP):
        gscore = gscore + jnp.where(sel[j], pj[j], 0.0)
    gi = lax.broadcasted_iota(I32, (8, tq), 0).astype(F32)
    gmax = jnp.max(gscore, axis=0, keepdims=True)
    gidx = jnp.min(jnp.where(gscore == gmax, gi, 8.0), axis=0, keepdims=True)
    onehot = gi == gidx
    f = [jnp.max(jnp.where(onehot & sel[j], 1.0, 0.0), axis=0, keepdims=True) > 0.5
         for j in range(PER_GROUP)]
    pair = jnp.where(f[0], jnp.where(f[1], 0.0, jnp.where(f[2], 1.0, 2.0)),
                     jnp.where(f[1], jnp.where(f[2], 3.0, 4.0), 5.0))
    cls = (gidx * float(len(PAIRS)) + pair).astype(I32)
    cls_ref[0] = cls

    @pl.when((pl.program_id(0) == 0) & (pl.program_id(1) == 0))
    def _():
        carry[...] = jnp.zeros_like(carry)

    c = carry[...]
    ranks = []
    for blk in range(tq // RANK_BLOCK):
        cls_b = cls[:, RANK_BLOCK * blk:RANK_BLOCK * (blk + 1)]
        oh = lax.broadcasted_iota(I32, (32, RANK_BLOCK), 0) == cls_b
        ohf = jnp.where(oh, 1.0, 0.0)
        within = jnp.dot(ohf.astype(BF16), upper_ref[...], preferred_element_type=F32)
        ranks.append(jnp.sum(jnp.where(oh, within + c[:, 0:1], 0.0), axis=0, keepdims=True))
        c = c + jnp.sum(ohf, axis=1, keepdims=True)
    rank_ref[0] = jnp.concatenate(ranks, axis=1).astype(I32)
    carry[...] = c
    cnt_ref[...] = c


def _post(o, wo, x, gm, gf, scf, shf, rw32, rb32):
    bsz, s, d = x.shape
    do = o.shape[-1]
    tq = TM_WIDE
    nt = s // tq
    tok = lambda w_: pl.BlockSpec((1, tq, w_), lambda b, i: (b, i, 0))
    per_b = pl.BlockSpec((1, 1, d), lambda b, i: (b, 0, 0))
    full = lambda a: pl.BlockSpec(a.shape, lambda b, i: (0,) * a.ndim)
    per_tile = pl.BlockSpec((1, 1, tq), lambda b, i: (b * nt + i, 0, 0))
    gf2 = gf.reshape(1, d)
    upper = (jnp.arange(RANK_BLOCK)[:, None] < jnp.arange(RANK_BLOCK)[None, :]).astype(BF16)
    return pl.pallas_call(
        _post_kernel,
        grid=(bsz, nt),
        in_specs=[tok(do), full(wo), tok(d), per_b, full(gf2), per_b, per_b, full(rw32), full(rb32),
                  full(upper)],
        out_specs=[tok(d), tok(d // 2), per_tile, per_tile, pl.BlockSpec((32, 128), lambda b, i: (0, 0))],
        out_shape=[jax.ShapeDtypeStruct((bsz, s, d), F32),
                   jax.ShapeDtypeStruct((bsz, s, d // 2), U32),
                   jax.ShapeDtypeStruct((bsz * nt, 1, tq), I32),
                   jax.ShapeDtypeStruct((bsz * nt, 1, tq), I32),
                   jax.ShapeDtypeStruct((32, 128), F32)],
        scratch_shapes=[pltpu.VMEM((32, 128), F32)],
        compiler_params=_cparams("arbitrary", "arbitrary"),
        name="post_attention",
    )(o, wo, x, gm, gf2, scf, shf, rw32, rb32, upper)


def _ffn_kernel(e1_ref, e2_ref, nused_ref, hs_ref, *refs):
    ys_ref = refs[-1]
    j = pl.program_id(0)
    tm = FFN_TM

    @pl.when(j * FFN_TILES_PER_STEP < nused_ref[0])
    def _():
        for sub in range(FFN_TILES_PER_STEP):
            wg1, wu1, wd1, rw1, wg2, wu2, wd2, rw2 = refs[8 * sub:8 * (sub + 1)]
            rows = slice(tm * sub, tm * (sub + 1))
            xf = _unpack_bf16_pairs(hs_ref[rows])
            x = xf.astype(BF16)
            gt1 = jnp.dot(x, wg1[0], preferred_element_type=F32)
            up1 = jnp.dot(x, wu1[0], preferred_element_type=F32)
            gt2 = jnp.dot(x, wg2[0], preferred_element_type=F32)
            up2 = jnp.dot(x, wu2[0], preferred_element_type=F32)
            a1 = ((gt1 * jax.nn.sigmoid(gt1)) * up1).astype(BF16)
            y1 = jnp.dot(a1, wd1[0], preferred_element_type=F32)
            a2 = ((gt2 * jax.nn.sigmoid(gt2)) * up2).astype(BF16)
            y2 = jnp.dot(a2, wd2[0], preferred_element_type=F32)
            s1 = jax.nn.sigmoid(jnp.sum(xf * rw1[0].astype(F32), axis=-1, keepdims=True))
            s2 = jax.nn.sigmoid(jnp.sum(xf * rw2[0].astype(F32), axis=-1, keepdims=True))
            den = s1 + s2
            ys_ref[rows] = _pack_bf16_pairs((s1 / den) * y1 + (s2 / den) * y2)

    @pl.when(j * FFN_TILES_PER_STEP >= nused_ref[0])
    def _():
        ys_ref[...] = jnp.zeros_like(ys_ref)


def _ffn(hs, e1, e2, nused, wg, wu, wd, rw3):
    npad, dh = hs.shape
    d = 2 * dh
    per = FFN_TILES_PER_STEP
    rows = per * FFN_TM
    de = wg.shape[-1]
    assert npad % rows == 0
    row = pl.BlockSpec((rows, dh), lambda j, e1, e2, nu: (jnp.minimum(j, (nu[0] - 1) // per), 0))
    w_specs = []
    for sub in range(per):
        for which in range(2):
            pick = lambda j, e1, e2, nu, sub=sub, which=which: ((e2 if which else e1)[per * j + sub], 0, 0)
            w_specs += [pl.BlockSpec((1, d, de), pick), pl.BlockSpec((1, d, de), pick),
                        pl.BlockSpec((1, de, d), pick), pl.BlockSpec((1, 1, d), pick)]
    grid_spec = pltpu.PrefetchScalarGridSpec(
        num_scalar_prefetch=3,
        grid=(npad // rows,),
        in_specs=[row] + w_specs,
        out_specs=pl.BlockSpec((rows, dh), lambda j, e1, e2, nu: (j, 0)),
    )
    return pl.pallas_call(
        _ffn_kernel,
        grid_spec=grid_spec,
        out_shape=jax.ShapeDtypeStruct((npad, dh), U32),
        compiler_params=_cparams("arbitrary"),
        name="expert_ffn",
    )(e1, e2, nused, hs, *([wg, wu, wd, rw3] * (2 * per)))


def _sc_mesh():
    return plsc.VectorSubcoreMesh(core_axis_name="c", subcore_axis_name="s",
                                  num_cores=SC_CORES_V7X, num_subcores=SC_SUBCORES_V7X)


def _sc_steps(n_rows):
    workers = SC_CORES_V7X * SC_SUBCORES_V7X
    per_w = n_rows // workers
    ch = min(SC_ROWS_PER_STEP, per_w // 2)
    n = per_w // ch
    assert per_w * workers == n_rows and n * ch == per_w and n % 2 == 0 and ch % 8 == 0
    return per_w, ch, n


def _sc_scratch(ch, d, dtype):
    return [pltpu.VMEM((2, ch), I32), pltpu.VMEM((2, ch, d), dtype),
            pltpu.SemaphoreType.DMA((2,)), pltpu.SemaphoreType.DMA((2,))]


def _sc_scatter_rows(src, idx, n_out):
    n_in, d = src.shape
    per_w, ch, n = _sc_steps(n_in)

    def body(src_hbm, idx_hbm, out_hbm, idx_v, rows_v, rsem, ssem):
        base = (lax.axis_index("s") * SC_CORES_V7X + lax.axis_index("c")) * per_w

        def read(i, b):
            return pltpu.make_async_copy(src_hbm.at[pl.ds(base + i * ch, ch)], rows_v.at[b], rsem.at[b])

        def scatter(b):
            return pltpu.make_async_copy(rows_v.at[b], out_hbm.at[idx_v.at[b]], ssem.at[b])

        def start_read(i, b):
            pltpu.sync_copy(idx_hbm.at[pl.ds(base + i * ch, ch)], idx_v.at[b])
            read(i, b).start()

        start_read(0, 0)

        @pl.loop(0, n, step=2)
        def _(i):
            for b in (0, 1):
                cur = i + b

                @pl.when(cur + 1 < n)
                def _():
                    @pl.when(cur >= 1)
                    def _():
                        scatter(1 - b).wait()
                    start_read(cur + 1, 1 - b)

                read(cur, b).wait()
                scatter(b).start()

        scatter(0).wait()
        scatter(1).wait()

    return pl.kernel(
        body, mesh=_sc_mesh(),
        out_type=jax.ShapeDtypeStruct((n_out, d), src.dtype),
        scratch_types=_sc_scratch(ch, d, src.dtype),
        name="sc_scatter_rows",
    )(src, idx)


def _sc_gather_rows(table, idx):
    n_out = idx.shape[0]
    d = table.shape[1]
    per_w, ch, n = _sc_steps(n_out)

    def body(table_hbm, idx_hbm, out_hbm, idx_v, rows_v, gsem, wsem):
        base = (lax.axis_index("s") * SC_CORES_V7X + lax.axis_index("c")) * per_w

        def gather(b):
            return pltpu.make_async_copy(table_hbm.at[idx_v.at[b]], rows_v.at[b], gsem.at[b])

        def write(i, b):
            return pltpu.make_async_copy(rows_v.at[b], out_hbm.at[pl.ds(base + i * ch, ch)], wsem.at[b])

        def start_gather(i, b):
            pltpu.sync_copy(idx_hbm.at[pl.ds(base + i * ch, ch)], idx_v.at[b])
            gather(b).start()

        start_gather(0, 0)

        @pl.loop(0, n, step=2)
        def _(i):
            for b in (0, 1):
                cur = i + b

                @pl.when(cur + 1 < n)
                def _():
                    @pl.when(cur >= 1)
                    def _():
                        write(cur - 1, 1 - b).wait()
                    start_gather(cur + 1, 1 - b)

                gather(b).wait()
                write(cur, b).start()

        write(n - 2, 0).wait()
        write(n - 1, 1).wait()

    return pl.kernel(
        body, mesh=_sc_mesh(),
        out_type=jax.ShapeDtypeStruct((n_out, d), table.dtype),
        scratch_types=_sc_scratch(ch, d, table.dtype),
        name="sc_gather_rows",
    )(table, idx)


def _moe(h2p, cls3, rank3, cnt, wg, wu, wd, rw3):
    bsz, s, dh = h2p.shape
    t = bsz * s
    tm = FFN_TM
    counts = cnt[:N_CLASSES, 0].astype(I32)
    padded = ((counts + tm - 1) // tm) * tm
    upto = jnp.arange(N_CLASSES)[:, None] >= jnp.arange(N_CLASSES)[None, :]
    ends = jnp.sum(jnp.where(upto, padded[None, :], 0), axis=1)
    base = ends - padded
    cls2 = cls3[:, 0, :]
    pos2 = rank3[:, 0, :]
    for c in range(N_CLASSES):
        pos2 = pos2 + jnp.where(cls2 == c, base[c], 0)
    pos = pos2.reshape(t)
    ntiles = t // tm + N_CLASSES
    tile_start = jnp.arange(ntiles, dtype=I32) * tm
    tcls = jnp.minimum(jnp.sum((ends[None, :] <= tile_start[:, None]).astype(I32), axis=1), N_CLASSES - 1)
    nused = (ends[-1:] // tm).astype(I32)
    grp = tcls // len(PAIRS)
    pr = tcls % len(PAIRS)
    first = sum(jnp.where(pr == i, p[0], 0) for i, p in enumerate(PAIRS))
    second = sum(jnp.where(pr == i, p[1], 0) for i, p in enumerate(PAIRS))
    e1 = (grp * PER_GROUP + first).astype(I32)
    e2 = (grp * PER_GROUP + second).astype(I32)
    hs = _sc_scatter_rows(h2p.reshape(t, dh), pos, ntiles * tm)
    ys = _ffn(hs, e1, e2, nused, wg, wu, wd, rw3)
    return _sc_gather_rows(ys, pos).reshape(bsz, s, dh)


def _qkv1_kernel(x_ref, y_ref, gfp_ref, g_ref, sc_ref, sh_ref, w_ref, cos_ref, sin_ref,
                 wg_ref, wu_ref, wd_ref, x2_ref, *rest):
    outs = rest[:9]
    _side_cast((wg_ref, wu_ref, wd_ref), rest[9:12])
    hb_scr, ysc, ysc2 = rest[12:]
    tm = x_ref.shape[1]
    x2 = x_ref[0] + gfp_ref[0] * _unpack_bf16_pairs(y_ref[0])
    x2_ref[0] = x2
    hb_scr[...] = _rms_mod(x2, g_ref[...], sc_ref[0], sh_ref[0]).astype(BF16)
    cos = cos_ref[0]
    sin = sin_ref[0]
    lane_lo = (lax.broadcasted_iota(I32, cos.shape, 1) % HEAD_DIM) < (HEAD_DIM // 2)
    for gi, (_, dil) in reversed(list(enumerate(B_GROUPS))):
        for j in range(3):
            c = gi * 3 + j
            y = jnp.dot(hb_scr[...], w_ref[:, B_WIDTH * c:B_WIDTH * (c + 1)], preferred_element_type=F32)
            out = outs[c]
            for cc in range(B_WIDTH // 128):
                lanes = slice(128 * cc, 128 * (cc + 1))
                r = y[:, lanes]
                if j < 2:
                    r = _rope_apply(r, cos, sin, lane_lo)
                if j == 0:
                    r = r * Q_SCALE
                if dil == 1:
                    out[0, 0, :, lanes] = r.astype(BF16)
                    continue
                ysc[cc] = r
                if dil % 16:
                    for rr in range(dil):
                        out[0, rr, :, lanes] = ysc[cc, pl.ds(rr, tm // dil, stride=dil), :].astype(BF16)
                    continue
                quarter = tm // 4
                inner = dil // 4
                for r1 in range(4):
                    ysc2[cc, r1 * quarter:(r1 + 1) * quarter] = ysc[cc, pl.ds(r1, quarter, stride=4), :]
                for r1 in range(4):
                    for r2 in range(inner):
                        out[0, r1 + 4 * r2, :, lanes] = ysc2[
                            cc, pl.ds(r1 * quarter + r2, tm // dil, stride=inner), :].astype(BF16)


def _qkv1(x, y, gfp, g, sc, sh, w, cos, sin, expert_ws, layer):
    bsz, s, d = x.shape
    tm = TM
    grid = (bsz, s // tm)
    w_in, w_out, w_shape, w_args = _side_cast_specs([(w_, layer) for w_ in expert_ws], grid)
    tok = lambda w_: pl.BlockSpec((1, tm, w_), lambda b, i: (b, i, 0))
    per_b = pl.BlockSpec((1, 1, d), lambda b, i: (b, 0, 0))
    out_specs = [tok(d)]
    out_shape = [jax.ShapeDtypeStruct((bsz, s, d), F32)]
    for _, dil in B_GROUPS:
        for _ in range(3):
            out_specs.append(pl.BlockSpec((1, dil, tm // dil, B_WIDTH), lambda b, i: (b, 0, i, 0)))
            out_shape.append(jax.ShapeDtypeStruct((bsz, dil, s // dil, B_WIDTH), BF16))
    outs = pl.pallas_call(
        _qkv1_kernel,
        grid=grid,
        in_specs=[tok(d), tok(d // 2), per_b, pl.BlockSpec((1, d), lambda b, i: (0, 0)), per_b, per_b,
                  pl.BlockSpec(w.shape, lambda b, i: (0, 0)), tok(128), tok(128)] + w_in,
        out_specs=out_specs + w_out,
        out_shape=out_shape + w_shape,
        scratch_shapes=[pltpu.VMEM((tm, d), BF16), pltpu.VMEM((B_WIDTH // 128, tm, 128), F32),
                        pltpu.VMEM((B_WIDTH // 128, tm, 128), F32)],
        compiler_params=_cparams("parallel", "parallel"),
        name="qkv1",
    )(x, y, gfp, g.reshape(1, d), sc, sh, w, cos, sin, *w_args)
    return outs[:10], [c.reshape(w_.shape[1:]) for c, w_ in zip(outs[10:], expert_ws)]


def _att1_window_pad(lseg):
    qb = min(2 * B_RADIUS, lseg)
    return -(qb + 2 * B_RADIUS) % 128


def _att1_kernel(*refs, chunk, seq):
    ins = refs[:21]
    o_ref = refs[21]
    scr = refs[22:]
    nat_o, nat_d, nat_m = scr[0:3], scr[3:6], scr[6:9]
    n = pl.program_id(1)
    rad = B_RADIUS
    npair = B_WIDTH // 128

    def window(cur, prev, nxt, res, lo, hi, lseg, lanes):
        parts = []
        if lo < 0:
            parts.append(prev[0, res, :, lanes])
        parts.append(cur[0, res, max(lo, 0):min(hi, lseg), lanes])
        if hi > lseg:
            parts.append(nxt[0, res, :, lanes])
        if hi > lseg + rad:
            parts.append(nxt[0, res, :, lanes])
        return jnp.concatenate(parts, axis=0) if len(parts) > 1 else parts[0]

    for c in range(npair):
        lanes = slice(128 * c, 128 * (c + 1))
        for gi, (_, dil) in enumerate(B_GROUPS):
            q_ref, kc, kp, kn, vc, vp, vn = ins[7 * gi:7 * gi + 7]
            lseg = chunk // dil
            qb = min(2 * rad, lseg)
            kb_n = qb + 2 * rad + _att1_window_pad(lseg)
            lstr = seq // dil
            rel = lax.broadcasted_iota(I32, (qb, kb_n), 1) - lax.broadcasted_iota(I32, (qb, kb_n), 0)
            band_bias = jnp.where((rel >= 0) & (rel <= 2 * rad), 0.0, NEG)
            jrow = lax.broadcasted_iota(I32, (1, kb_n), 1)
            first_head = lax.broadcasted_iota(I32, (qb, 128), 1) < HEAD_DIM
            zero = jnp.zeros((qb, 128), BF16)
            ones = jnp.ones((kb_n, 128), BF16)
            for res in range(dil):
                for q0 in range(0, lseg, qb):
                    apos = n * lseg + q0 - rad + jrow
                    bias = band_bias + jnp.where((apos >= 0) & (apos < lstr), 0.0, NEG)
                    rows = pl.ds(q0, qb) if dil == 1 else pl.ds(res + dil * q0, qb, stride=dil)
                    qc = q_ref[0, res, q0:q0 + qb, lanes]
                    kb = window(kc, kp, kn, res, q0 - rad, q0 - rad + kb_n, lseg, lanes)
                    vb = jnp.concatenate(
                        [window(vc, vp, vn, res, q0 - rad, q0 - rad + kb_n, lseg, lanes), ones], axis=1)
                    qs = jnp.concatenate(
                        [jnp.where(first_head, qc, zero), jnp.where(first_head, zero, qc)], axis=0)
                    s2 = lax.dot_general(qs, kb, (((1,), (1,)), ((), ())), preferred_element_type=F32)
                    ps, ms = [], []
                    for t2 in range(2):
                        s = s2[qb * t2:qb * (t2 + 1)] + bias
                        m = jnp.max(s, axis=-1, keepdims=True)
                        ps.append(jnp.exp2((s - m).astype(BF16)))
                        ms.append(m)
                    pv = jnp.dot(jnp.concatenate(ps, axis=0), vb, preferred_element_type=F32)
                    nat_o[gi][rows, :] = jnp.where(first_head, pv[0:qb, 0:128], pv[qb:2 * qb, 0:128])
                    nat_d[gi][rows, :] = jnp.where(first_head, pv[0:qb, 128:256], pv[qb:2 * qb, 128:256])
                    nat_m[gi][rows, :] = jnp.where(first_head, ms[0], ms[1])

        ms = [nat_m[gi][...] for gi in range(3)]
        mx = jnp.maximum(jnp.maximum(ms[0], ms[1]), ms[2])
        w = [jnp.exp2(x - mx) for x in ms]
        num = w[0] * nat_o[0][...] + w[1] * nat_o[1][...] + w[2] * nat_o[2][...]
        den = w[0] * nat_d[0][...] + w[1] * nat_d[1][...] + w[2] * nat_d[2][...]
        o_ref[0, :, lanes] = (num / den).astype(BF16)


def _att1(qkv):
    bsz = qkv[0].shape[0]
    seq = qkv[0].shape[1] * qkv[0].shape[2]
    chunk = ATT1_CHUNK
    rad = B_RADIUS
    in_specs, args, scratch = [], [], []
    for gi, (_, dil) in enumerate(B_GROUPS):
        q, k, v = qkv[3 * gi:3 * gi + 3]
        lseg = chunk // dil
        per = lseg // rad
        last = seq // dil // rad - 1
        cur = pl.BlockSpec((1, dil, lseg, B_WIDTH), lambda b, n: (b, 0, n, 0))
        prev = pl.BlockSpec((1, dil, rad, B_WIDTH),
                            lambda b, n, per=per: (b, 0, jnp.maximum(n * per - 1, 0), 0))
        nxt = pl.BlockSpec((1, dil, rad, B_WIDTH),
                           lambda b, n, per=per, last=last: (b, 0, jnp.minimum(n * per + per, last), 0))
        in_specs += [cur, cur, prev, nxt, cur, prev, nxt]
        args += [q, k, k, k, v, v, v]
    for _ in range(3):
        for _ in B_GROUPS:
            scratch.append(pltpu.VMEM((chunk, 128), F32))
    return pl.pallas_call(
        functools.partial(_att1_kernel, chunk=chunk, seq=seq),
        grid=(bsz, seq // chunk),
        in_specs=in_specs,
        out_specs=pl.BlockSpec((1, chunk, B_WIDTH), lambda b, n: (b, n, 0)),
        out_shape=jax.ShapeDtypeStruct((bsz, seq, B_WIDTH), BF16),
        scratch_shapes=scratch,
        compiler_params=_cparams("parallel", "parallel"),
        name="att1",
    )(*args)


def _final_kernel(x_ref, y_ref, gfp_ref, g_ref, o_ref):
    x = x_ref[0] + gfp_ref[0] * _unpack_bf16_pairs(y_ref[0])
    ms = jnp.mean(x * x, axis=-1, keepdims=True)
    o_ref[0] = (x * lax.rsqrt(ms + NORM_EPS)) * g_ref[...]


def _final(x, y, gfp, g):
    bsz, s, d = x.shape
    tm = TM_WIDE
    tok = pl.BlockSpec((1, tm, d), lambda b, i: (b, i, 0))
    return pl.pallas_call(
        _final_kernel,
        grid=(bsz, s // tm),
        in_specs=[tok, pl.BlockSpec((1, tm, d // 2), lambda b, i: (b, i, 0)),
                  pl.BlockSpec((1, 1, d), lambda b, i: (b, 0, 0)),
                  pl.BlockSpec((1, d), lambda b, i: (0, 0))],
        out_specs=tok,
        out_shape=jax.ShapeDtypeStruct((bsz, s, d), F32),
        compiler_params=_cparams("parallel", "parallel"),
        name="final_norm",
    )(x, y, gfp, g.reshape(1, d))


def kernel(x, c, positions, ada_w, ada_b, norm_mix_g, norm_ffn_g, a_w_qkv, a_w_o, a_sink, b_w_qkv, b_w_o,
           router_w, router_bias, exp_w_gate, exp_w_up, exp_w_down, final_norm_g):
    bsz, s, d = x.shape
    assert d == D_MODEL and ada_w.shape[0] == 2 and s % ATT1_CHUNK == 0
    assert all(w_ // (2 * dil) == B_RADIUS for w_, dil in B_GROUPS)

    mod = _modulation(c, ada_w, ada_b)
    mods = [[mod[l][:, k * d:(k + 1) * d].reshape(bsz, 1, d) for k in range(6)] for l in range(2)]
    cos, sin, sin_paired = _rope_tables(positions)

    rw_t = router_w.T.astype(BF16)
    rows = jnp.array([PER_GROUP * g + j if g < N_EXPERT_GROUPS else 0
                      for j in range(PER_GROUP) for g in range(8)], I32)
    live = jnp.array([1.0 if g < N_EXPERT_GROUPS else 0.0
                      for j in range(PER_GROUP) for g in range(8)], F32)
    rw32 = (rw_t[rows].astype(F32) * live[:, None]).astype(BF16)
    rb32 = jnp.where(live > 0, router_bias.astype(F32)[rows], NEG).reshape(32, 1)
    rw3 = rw_t.reshape(N_EXPERTS, 1, d)
    expert_ws = (exp_w_gate, exp_w_up, exp_w_down)

    sh_m, sc_m, g_m, sh_f, sc_f, g_f = mods[0]
    nq, nkv = A_Q_HEADS * HEAD_DIM, A_KV_HEADS * HEAD_DIM
    w0 = a_w_qkv[0].astype(BF16)
    w0 = jnp.concatenate([_paired_layout(w0[:, :nq]), _paired_layout(w0[:, nq:nq + nkv]), w0[:, nq + nkv:]],
                         axis=1)
    q, k, v = _qkv0(x, norm_mix_g[0], sc_m, sh_m, w0, cos, sin_paired)
    o, cast = _att0(q, k, v, a_sink[0].astype(F32),
                    [(w_, 0) for w_ in expert_ws] + [(a_w_o, 0), (b_w_o, 0), (b_w_qkv, 0)])
    experts, (a_wo, b_wo, b_wqkv) = cast[:3], cast[3:]
    x1, *routed = _post(o, a_wo, x, g_m, norm_ffn_g[0], sc_f, sh_f, rw32, rb32)
    y = _moe(*routed, *experts, rw3)
    g_f_prev = g_f

    sh_m, sc_m, g_m, sh_f, sc_f, g_f = mods[1]
    outs, experts = _qkv1(x1, y, g_f_prev, norm_mix_g[1], sc_m, sh_m, b_wqkv, cos, sin, expert_ws, 1)
    x2, qkv = outs[0], outs[1:]
    o = _att1(qkv)
    x3, *routed = _post(o, b_wo, x2, g_m, norm_ffn_g[1], sc_f, sh_f, rw32, rb32)
    y = _moe(*routed, *experts, rw3)

    return _final(x3, y, g_f, final_norm_g)
```

```python
import functools

import jax
import jax.numpy as jnp
from jax import lax
from jax.experimental import pallas as pl
from jax.experimental.pallas import tpu as pltpu
from jax.experimental.pallas import tpu_sc as plsc

F32, BF16, I32, U32 = jnp.float32, jnp.bfloat16, jnp.int32, jnp.uint32

D_MODEL = 1024
HEAD_DIM = 64
ROPE_THETA = 10000.0
NORM_EPS = 1e-6
A_Q_HEADS = 16
A_KV_HEADS = 4
A_GROUP = A_Q_HEADS // A_KV_HEADS
A_RADIUS = 128
B_GROUPS = ((128, 1), (512, 4), (2048, 16))
B_HEADS = 8
B_RADIUS = 64
B_WIDTH = B_HEADS * HEAD_DIM
N_EXPERTS = 16
N_EXPERT_GROUPS = 4
PER_GROUP = N_EXPERTS // N_EXPERT_GROUPS
D_EXPERT = D_MODEL // 2
PAIRS = ((0, 1), (0, 2), (0, 3), (1, 2), (1, 3), (2, 3))
N_CLASSES = N_EXPERT_GROUPS * len(PAIRS)
NEG = -1e30
LOG2E = 1.4426950408889634
Q_SCALE = HEAD_DIM ** -0.5 * LOG2E

VMEM_LIMIT = 56 * 1024 * 1024
TM = 512
ROPE_TS = 2048
TM_WIDE = 1024
RANK_BLOCK = 512
ATT1_CHUNK = 1024
FFN_TM = 256
FFN_TILES_PER_STEP = 2
SC_CORES_V7X = 2
SC_SUBCORES_V7X = 16
SC_ROWS_PER_STEP = 64


def _cparams(*sem):
    return pltpu.CompilerParams(dimension_semantics=sem, vmem_limit_bytes=VMEM_LIMIT)


def _rms_mod(x, g, sc, sh):
    ms = jnp.mean(x * x, axis=-1, keepdims=True)
    return (x * lax.rsqrt(ms + NORM_EPS)) * (g * (1.0 + sc)) + sh


def _rope_apply(y, cos, sin_signed, lane_lo):
    sw = jnp.where(lane_lo, pltpu.roll(y, 96, 1), pltpu.roll(y, 32, 1))
    return y * cos + sw * sin_signed


def _rope_apply_paired(y, cos, sin_paired):
    return y * cos + pltpu.roll(y, HEAD_DIM, 1) * sin_paired


def _paired_layout(w):
    k, n = w.shape
    half = HEAD_DIM // 2
    return w.reshape(k, n // 128, 2, 2, half).transpose(0, 1, 3, 2, 4).reshape(k, n)


def _pack_bf16_pairs(x):
    n = x.shape[1] // 2
    lo = lax.bitcast_convert_type(x[:, :n].astype(BF16).astype(F32), U32)
    hi = lax.bitcast_convert_type(x[:, n:].astype(BF16).astype(F32), U32)
    return (hi & jnp.uint32(0xFFFF0000)) | (lo >> 16)


def _unpack_bf16_pairs(p):
    lo = lax.bitcast_convert_type(p << 16, F32)
    hi = lax.bitcast_convert_type(p & jnp.uint32(0xFFFF0000), F32)
    return jnp.concatenate([lo, hi], axis=1)


def _dup_head(x, first_head, which):
    rolled = pltpu.roll(x, HEAD_DIM, 1)
    return jnp.where(first_head, x, rolled) if which == 0 else jnp.where(first_head, rolled, x)


def _mod_kernel(ct_ref, w_ref, b_ref, o_ref, *, bsz):
    ct = ct_ref[...]
    s = ct * jax.nn.sigmoid(ct)
    w = w_ref[0]
    rows = [jnp.sum(w * s[:, b:b + 1], axis=0, keepdims=True) + b_ref[0] for b in range(bsz)]
    o_ref[0] = jnp.concatenate(rows + [jnp.zeros_like(rows[0])] * (8 - bsz), axis=0)


def _modulation(c, ada_w, ada_b):
    depth, d, n = ada_w.shape
    bsz = c.shape[0]
    assert bsz <= 8
    tn = 1536
    out = pl.pallas_call(
        functools.partial(_mod_kernel, bsz=bsz),
        grid=(depth, n // tn),
        in_specs=[pl.BlockSpec((d, bsz), lambda l, j: (0, 0)),
                  pl.BlockSpec((1, d, tn), lambda l, j: (l, 0, j)),
                  pl.BlockSpec((1, 1, tn), lambda l, j: (l, 0, j))],
        out_specs=pl.BlockSpec((1, 8, tn), lambda l, j: (l, 0, j)),
        out_shape=jax.ShapeDtypeStruct((depth, 8, n), F32),
        compiler_params=_cparams("arbitrary", "arbitrary"),
        name="modulation",
    )(c.T, ada_w, ada_b.reshape(depth, 1, n))
    return out[:, :bsz]


def _rope_kernel(pos_ref, invf_ref, sign_ref, cos_ref, sin_ref, sinp_ref):
    p = pos_ref[0].astype(F32)
    n = p.shape[0]
    quarter = lax.broadcasted_iota(I32, (n, 128), 1) // (HEAD_DIM // 2)
    pos4 = jnp.where(quarter == 0, p[:, 0:1],
                     jnp.where(quarter == 1, p[:, 1:2], jnp.where(quarter == 2, p[:, 2:3], p[:, 3:4])))
    ang = pos4 * invf_ref[...]
    dense = (jnp.cos(ang), jnp.sin(ang))
    for u in range(4):
        spread = []
        for d in dense:
            e = jnp.where(quarter == u, d, 0.0)
            e = e + pltpu.roll(e, HEAD_DIM // 2, 1)
            spread.append(e + pltpu.roll(e, HEAD_DIM, 1))
        rows = pl.ds(u, n, stride=4)
        cos_ref[0, rows, :] = spread[0]
        sin_ref[0, rows, :] = spread[1] * sign_ref[0:1]
        sinp_ref[0, rows, :] = spread[1] * sign_ref[1:2]


def _rope_tables(positions):
    bsz, s = positions.shape
    inv_freq = ROPE_THETA ** (-jnp.arange(0, HEAD_DIM, 2, dtype=F32) / HEAD_DIM)
    invf = jnp.tile(inv_freq, 4).reshape(1, 128)
    half = HEAD_DIM // 2
    minus, plus = -jnp.ones((half,), F32), jnp.ones((half,), F32)
    sign = jnp.stack([jnp.concatenate([minus, plus, minus, plus]),
                      jnp.concatenate([minus, minus, plus, plus])])
    ts = ROPE_TS
    return pl.pallas_call(
        _rope_kernel,
        grid=(bsz, s // ts),
        in_specs=[pl.BlockSpec((1, ts // 4, 4), lambda b, i: (b, i, 0)),
                  pl.BlockSpec((1, 128), lambda b, i: (0, 0)),
                  pl.BlockSpec((2, 128), lambda b, i: (0, 0))],
        out_specs=[pl.BlockSpec((1, ts, 128), lambda b, i: (b, i, 0))] * 3,
        out_shape=[jax.ShapeDtypeStruct((bsz, s, 128), F32)] * 3,
        compiler_params=_cparams("parallel", "parallel"),
        name="rope_tables",
    )(positions.reshape(bsz, s // 4, 4), invf, sign)


def _qkv0_kernel(x_ref, g_ref, sc_ref, sh_ref, w_ref, cos_ref, sin_ref,
                 q_ref, k_ref, v_ref, hb_scr):
    hb_scr[...] = _rms_mod(x_ref[0], g_ref[...], sc_ref[0], sh_ref[0]).astype(BF16)
    cos = cos_ref[0]
    sin = sin_ref[0]
    lane = lax.broadcasted_iota(I32, cos.shape, 1)
    x_lo = (lane % HEAD_DIM) < (HEAD_DIM // 2)
    nq = A_Q_HEADS * HEAD_DIM
    nkv = A_KV_HEADS * HEAD_DIM
    for c in range(nq // 256):
        y = jnp.dot(hb_scr[...], w_ref[:, 256 * c:256 * (c + 1)], preferred_element_type=F32)
        for cc in range(2):
            r = _rope_apply_paired(y[:, 128 * cc:128 * (cc + 1)], cos, sin) * Q_SCALE
            q_ref[0, :, 256 * c + 128 * cc:256 * c + 128 * (cc + 1)] = r.astype(BF16)
    first_head = lane < HEAD_DIM
    y = jnp.dot(hb_scr[...], w_ref[:, nq:nq + nkv], preferred_element_type=F32)
    for cc in range(nkv // 128):
        r = _rope_apply_paired(y[:, 128 * cc:128 * (cc + 1)], cos, sin)
        k_ref[0, :, 256 * cc:256 * cc + 128] = jnp.where(
            x_lo, r, pltpu.roll(r, HEAD_DIM // 2, 1)).astype(BF16)
        k_ref[0, :, 256 * cc + 128:256 * cc + 256] = jnp.where(
            x_lo, pltpu.roll(r, 128 - HEAD_DIM // 2, 1), r).astype(BF16)
    y = jnp.dot(hb_scr[...], w_ref[:, nq + nkv:nq + 2 * nkv], preferred_element_type=F32)
    for cc in range(nkv // 128):
        r = y[:, 128 * cc:128 * (cc + 1)]
        for which in range(2):
            g = 2 * cc + which
            v_ref[0, :, 128 * g:128 * (g + 1)] = _dup_head(r, first_head, which).astype(BF16)


def _qkv0(x, g, sc, sh, w, cos, sin):
    bsz, s, d = x.shape
    nq = A_Q_HEADS * HEAD_DIM
    nkv = A_KV_HEADS * HEAD_DIM
    tm = TM_WIDE
    tok = lambda w_: pl.BlockSpec((1, tm, w_), lambda b, i: (b, i, 0))
    per_b = pl.BlockSpec((1, 1, d), lambda b, i: (b, 0, 0))
    return pl.pallas_call(
        _qkv0_kernel,
        grid=(bsz, s // tm),
        in_specs=[tok(d), pl.BlockSpec((1, d), lambda b, i: (0, 0)), per_b, per_b,
                  pl.BlockSpec(w.shape, lambda b, i: (0, 0)), tok(128), tok(128)],
        out_specs=[tok(nq), tok(2 * nkv), tok(2 * nkv)],
        out_shape=[jax.ShapeDtypeStruct((bsz, s, nq), BF16),
                   jax.ShapeDtypeStruct((bsz, s, 2 * nkv), BF16),
                   jax.ShapeDtypeStruct((bsz, s, 2 * nkv), BF16)],
        scratch_shapes=[pltpu.VMEM((tm, d), BF16)],
        compiler_params=_cparams("parallel", "parallel"),
        name="qkv0",
    )(x, g.reshape(1, d), sc, sh, w, cos, sin)


def _side_cast_specs(items, grid):
    nsteps = grid[0] * grid[1]
    in_specs, out_specs, out_shape, args = [], [], [], []
    for w, layer in items:
        n = w.shape[-1]
        per_layer = w[0].size // n
        rows = per_layer // nsteps
        assert rows * nsteps == per_layer and rows % 16 == 0
        in_specs.append(pl.BlockSpec(
            (rows, n), lambda b, i, layer=layer: (layer * nsteps + b * grid[1] + i, 0)))
        out_specs.append(pl.BlockSpec((rows, n), lambda b, i: (b * grid[1] + i, 0)))
        out_shape.append(jax.ShapeDtypeStruct((per_layer, n), BF16))
        args.append(w.reshape(w.shape[0] * per_layer, n))
    return in_specs, out_specs, out_shape, args


def _side_cast(srcs, dsts):
    for src, dst in zip(srcs, dsts):
        dst[...] = src[...].astype(BF16)


def _att0_kernel(sink_ref, q_ref, kc_ref, kp_ref, kn_ref, vc_ref, vp_ref, vn_ref, *rest, tq, seq, ncast):
    o_ref = rest[ncast]
    kf, vf = rest[2 * ncast + 1:]
    _side_cast(rest[:ncast], rest[ncast + 1:2 * ncast + 1])
    i = pl.program_id(1)
    r = A_RADIUS
    ones = jnp.ones((tq + 2 * r, 128), BF16)
    for g in range(A_KV_HEADS):
        lanes = slice(128 * g, 128 * (g + 1))
        kf[g, 0:r] = kp_ref[0, :, lanes]
        kf[g, r:r + tq] = kc_ref[0, :, lanes]
        kf[g, r + tq:r + tq + r] = kn_ref[0, :, lanes]
        vf[g, 0:r, 0:128] = vp_ref[0, :, lanes]
        vf[g, r:r + tq, 0:128] = vc_ref[0, :, lanes]
        vf[g, r + tq:r + tq + r, 0:128] = vn_ref[0, :, lanes]
        vf[g, :, 128:256] = ones

    ii = lax.broadcasted_iota(I32, (r, r), 0)
    jj = lax.broadcasted_iota(I32, (r, r), 1)
    first_head = jj < HEAD_DIM
    q_first = (jj % HEAD_DIM) < (HEAD_DIM // 2)
    zero = jnp.zeros((r, 128), BF16)

    for j in range(tq // r):
        q0 = j * r
        base = i * tq + q0
        bias_lo = jnp.where((jj >= ii) & (base - r + jj >= 0), 0.0, NEG)
        bias_hi = jnp.where((jj <= ii) & (base + r + jj < seq), 0.0, NEG)
        for g in range(A_KV_HEADS):
            kb = kf[g, pl.ds(q0, 3 * r), :]
            vb = vf[g, pl.ds(q0, 3 * r), :]
            qs = []
            for cc in range(2):
                c = 2 * g + cc
                qc = q_ref[0, pl.ds(q0, r), 128 * c:128 * (c + 1)]
                qs += [jnp.where(q_first, qc, zero), jnp.where(q_first, zero, qc)]
            s4 = lax.dot_general(jnp.concatenate(qs, axis=0), kb, (((1,), (1,)), ((), ())),
                                 preferred_element_type=F32)
            ps, ms = [], []
            for t in range(A_GROUP):
                s = s4[r * t:r * (t + 1)]
                a0 = s[:, 0:r] + bias_lo
                a1 = s[:, r:2 * r]
                a2 = s[:, 2 * r:3 * r] + bias_hi
                sink = sink_ref[A_GROUP * g + t] * LOG2E
                m = jnp.max(jnp.maximum(jnp.maximum(a0, a1), a2), axis=-1, keepdims=True)
                m = jnp.maximum(m, sink)
                ps.append(jnp.exp2(jnp.concatenate([a0 - m, a1 - m, a2 - m], axis=1).astype(BF16)))
                ms.append(m)
            pv = jnp.dot(jnp.concatenate(ps, axis=0), vb, preferred_element_type=F32)
            os_ = []
            for t in range(A_GROUP):
                sink = sink_ref[A_GROUP * g + t] * LOG2E
                den = pv[r * t:r * (t + 1), 128:256] + jnp.exp2(sink - ms[t])
                os_.append(pv[r * t:r * (t + 1), 0:128] / den)
            for cc in range(2):
                c = 2 * g + cc
                o_ref[0, pl.ds(q0, r), 128 * c:128 * (c + 1)] = jnp.where(
                    first_head, os_[2 * cc], os_[2 * cc + 1]).astype(BF16)


def _att0(q, k2, v2, sink, cast_items):
    bsz, s, nq = q.shape
    nkv2 = k2.shape[-1]
    tq = TM_WIDE
    r = A_RADIUS
    per = tq // r
    last = s // r - 1
    grid = (bsz, s // tq)
    cur = lambda w_: pl.BlockSpec((1, tq, w_), lambda b, i: (b, i, 0))
    prev = pl.BlockSpec((1, r, nkv2), lambda b, i: (b, jnp.maximum(i * per - 1, 0), 0))
    nxt = pl.BlockSpec((1, r, nkv2), lambda b, i: (b, jnp.minimum(i * per + per, last), 0))
    w_in, w_out, w_shape, w_args = _side_cast_specs(cast_items, grid)
    o, *cast = pl.pallas_call(
        functools.partial(_att0_kernel, tq=tq, seq=s, ncast=len(cast_items)),
        grid=grid,
        in_specs=[pl.BlockSpec(memory_space=pltpu.SMEM),
                  cur(nq), cur(nkv2), prev, nxt, cur(nkv2), prev, nxt] + w_in,
        out_specs=[cur(nq)] + w_out,
        out_shape=[jax.ShapeDtypeStruct((bsz, s, nq), BF16)] + w_shape,
        scratch_shapes=[pltpu.VMEM((A_KV_HEADS, tq + 2 * r, 128), BF16),
                        pltpu.VMEM((A_KV_HEADS, tq + 2 * r, 256), BF16)],
        compiler_params=_cparams("parallel", "parallel"),
        name="att0",
    )(sink, q, k2, k2, k2, v2, v2, v2, *w_args)
    return o, [c.reshape(w.shape[1:]) for c, (w, _) in zip(cast, cast_items)]


def _post_kernel(o_ref, wo_ref, x_ref, gm_ref, gf_ref, scf_ref, shf_ref, rw_ref, rb_ref, upper_ref,
                 x1_ref, h2_ref, cls_ref, rank_ref, cnt_ref, carry):
    mix = jnp.dot(o_ref[0], wo_ref[...], preferred_element_type=F32)
    x1 = x_ref[0] + gm_ref[0] * mix
    x1_ref[0] = x1
    h2 = _rms_mod(x1, gf_ref[...], scf_ref[0], shf_ref[0]).astype(BF16)
    h2_ref[0] = _pack_bf16_pairs(h2.astype(F32))
    tq = h2.shape[0]
    logits = lax.dot_general(rw_ref[...], h2, (((1,), (1,)), ((), ())), preferred_element_type=F32)
    scores = jax.nn.sigmoid(logits)
    biased = scores + rb_ref[...]
    pj = [biased[8 * j:8 * (j + 1)] for j in range(PER_GROUP)]
    sel = []
    for j in range(PER_GROUP):
        beaten = jnp.zeros((8, tq), F32)
        for j2 in range(PER_GROUP):
            if j2 == j:
                continue
            beats = (pj[j2] > pj[j]) | (pj[j2] == pj[j]) if j2 < j else (pj[j2] > pj[j])
            beaten = beaten + jnp.where(beats, 1.0, 0.0)
        sel.append(beaten < 2.0)
    gscore = jnp.zeros((8, tq), F32)
    for j in range(PER_GROUP):
        gscore = gscore + jnp.where(sel[j], pj[j], 0.0)
    gi = lax.broadcasted_iota(I32, (8, tq), 0).astype(F32)
    gmax = jnp.max(gscore, axis=0, keepdims=True)
    gidx = jnp.min(jnp.where(gscore == gmax, gi, 8.0), axis=0, keepdims=True)
    onehot = gi == gidx
    f = [jnp.max(jnp.where(onehot & sel[j], 1.0, 0.0), axis=0, keepdims=True) > 0.5
         for j in range(PER_GROUP)]
    pair = jnp.where(f[0], jnp.where(f[1], 0.0, jnp.where(f[2], 1.0, 2.0)),
                     jnp.where(f[1], jnp.where(f[2], 3.0, 4.0), 5.0))
    cls = (gidx * float(len(PAIRS)) + pair).astype(I32)
    cls_ref[0] = cls

    @pl.when((pl.program_id(0) == 0) & (pl.program_id(1) == 0))
    def _():
        carry[...] = jnp.zeros_like(carry)

    c = carry[...]
    ranks = []
    for blk in range(tq // RANK_BLOCK):
        cls_b = cls[:, RANK_BLOCK * blk:RANK_BLOCK * (blk + 1)]
        oh = lax.broadcasted_iota(I32, (32, RANK_BLOCK), 0) == cls_b
        ohf = jnp.where(oh, 1.0, 0.0)
        within = jnp.dot(ohf.astype(BF16), upper_ref[...], preferred_element_type=F32)
        ranks.append(jnp.sum(jnp.where(oh, within + c[:, 0:1], 0.0), axis=0, keepdims=True))
        c = c + jnp.sum(ohf, axis=1, keepdims=True)
    rank_ref[0] = jnp.concatenate(ranks, axis=1).astype(I32)
    carry[...] = c
    cnt_ref[...] = c


def _post(o, wo, x, gm, gf, scf, shf, rw32, rb32):
    bsz, s, d = x.shape
    do = o.shape[-1]
    tq = TM_WIDE
    nt = s // tq
    tok = lambda w_: pl.BlockSpec((1, tq, w_), lambda b, i: (b, i, 0))
    per_b = pl.BlockSpec((1, 1, d), lambda b, i: (b, 0, 0))
    full = lambda a: pl.BlockSpec(a.shape, lambda b, i: (0,) * a.ndim)
    per_tile = pl.BlockSpec((1, 1, tq), lambda b, i: (b * nt + i, 0, 0))
    gf2 = gf.reshape(1, d)
    upper = (jnp.arange(RANK_BLOCK)[:, None] < jnp.arange(RANK_BLOCK)[None, :]).astype(BF16)
    return pl.pallas_call(
        _post_kernel,
        grid=(bsz, nt),
        in_specs=[tok(do), full(wo), tok(d), per_b, full(gf2), per_b, per_b, full(rw32), full(rb32),
                  full(upper)],
        out_specs=[tok(d), tok(d // 2), per_tile, per_tile, pl.BlockSpec((32, 128), lambda b, i: (0, 0))],
        out_shape=[jax.ShapeDtypeStruct((bsz, s, d), F32),
                   jax.ShapeDtypeStruct((bsz, s, d // 2), U32),
                   jax.ShapeDtypeStruct((bsz * nt, 1, tq), I32),
                   jax.ShapeDtypeStruct((bsz * nt, 1, tq), I32),
                   jax.ShapeDtypeStruct((32, 128), F32)],
        scratch_shapes=[pltpu.VMEM((32, 128), F32)],
        compiler_params=_cparams("arbitrary", "arbitrary"),
        name="post_attention",
    )(o, wo, x, gm, gf2, scf, shf, rw32, rb32, upper)


def _ffn_kernel(e1_ref, e2_ref, nused_ref, hs_ref, *refs):
    ys_ref = refs[-1]
    j = pl.program_id(0)
    tm = FFN_TM

    @pl.when(j * FFN_TILES_PER_STEP < nused_ref[0])
    def _():
        for sub in range(FFN_TILES_PER_STEP):
            wg1, wu1, wd1, rw1, wg2, wu2, wd2, rw2 = refs[8 * sub:8 * (sub + 1)]
            rows = slice(tm * sub, tm * (sub + 1))
            xf = _unpack_bf16_pairs(hs_ref[rows])
            x = xf.astype(BF16)
            gt1 = jnp.dot(x, wg1[0], preferred_element_type=F32)
            up1 = jnp.dot(x, wu1[0], preferred_element_type=F32)
            gt2 = jnp.dot(x, wg2[0], preferred_element_type=F32)
            up2 = jnp.dot(x, wu2[0], preferred_element_type=F32)
            a1 = ((gt1 * jax.nn.sigmoid(gt1)) * up1).astype(BF16)
            y1 = jnp.dot(a1, wd1[0], preferred_element_type=F32)
            a2 = ((gt2 * jax.nn.sigmoid(gt2)) * up2).astype(BF16)
            y2 = jnp.dot(a2, wd2[0], preferred_element_type=F32)
            s1 = jax.nn.sigmoid(jnp.sum(xf * rw1[0].astype(F32), axis=-1, keepdims=True))
            s2 = jax.nn.sigmoid(jnp.sum(xf * rw2[0].astype(F32), axis=-1, keepdims=True))
            den = s1 + s2
            ys_ref[rows] = _pack_bf16_pairs((s1 / den) * y1 + (s2 / den) * y2)

    @pl.when(j * FFN_TILES_PER_STEP >= nused_ref[0])
    def _():
        ys_ref[...] = jnp.zeros_like(ys_ref)


def _ffn(hs, e1, e2, nused, wg, wu, wd, rw3):
    npad, dh = hs.shape
    d = 2 * dh
    per = FFN_TILES_PER_STEP
    rows = per * FFN_TM
    de = wg.shape[-1]
    assert npad % rows == 0
    row = pl.BlockSpec((rows, dh), lambda j, e1, e2, nu: (jnp.minimum(j, (nu[0] - 1) // per), 0))
    w_specs = []
    for sub in range(per):
        for which in range(2):
            pick = lambda j, e1, e2, nu, sub=sub, which=which: ((e2 if which else e1)[per * j + sub], 0, 0)
            w_specs += [pl.BlockSpec((1, d, de), pick), pl.BlockSpec((1, d, de), pick),
                        pl.BlockSpec((1, de, d), pick), pl.BlockSpec((1, 1, d), pick)]
    grid_spec = pltpu.PrefetchScalarGridSpec(
        num_scalar_prefetch=3,
        grid=(npad // rows,),
        in_specs=[row] + w_specs,
        out_specs=pl.BlockSpec((rows, dh), lambda j, e1, e2, nu: (j, 0)),
    )
    return pl.pallas_call(
        _ffn_kernel,
        grid_spec=grid_spec,
        out_shape=jax.ShapeDtypeStruct((npad, dh), U32),
        compiler_params=_cparams("arbitrary"),
        name="expert_ffn",
    )(e1, e2, nused, hs, *([wg, wu, wd, rw3] * (2 * per)))


def _sc_mesh():
    return plsc.VectorSubcoreMesh(core_axis_name="c", subcore_axis_name="s",
                                  num_cores=SC_CORES_V7X, num_subcores=SC_SUBCORES_V7X)


def _sc_steps(n_rows):
    workers = SC_CORES_V7X * SC_SUBCORES_V7X
    per_w = n_rows // workers
    ch = min(SC_ROWS_PER_STEP, per_w // 2)
    n = per_w // ch
    assert per_w * workers == n_rows and n * ch == per_w and n % 2 == 0 and ch % 8 == 0
    return per_w, ch, n


def _sc_scratch(ch, d, dtype):
    return [pltpu.VMEM((2, ch), I32), pltpu.VMEM((2, ch, d), dtype),
            pltpu.SemaphoreType.DMA((2,)), pltpu.SemaphoreType.DMA((2,))]


def _sc_scatter_rows(src, idx, n_out):
    n_in, d = src.shape
    per_w, ch, n = _sc_steps(n_in)

    def body(src_hbm, idx_hbm, out_hbm, idx_v, rows_v, rsem, ssem):
        base = (lax.axis_index("s") * SC_CORES_V7X + lax.axis_index("c")) * per_w

        def read(i, b):
            return pltpu.make_async_copy(src_hbm.at[pl.ds(base + i * ch, ch)], rows_v.at[b], rsem.at[b])

        def scatter(b):
            return pltpu.make_async_copy(rows_v.at[b], out_hbm.at[idx_v.at[b]], ssem.at[b])

        def start_read(i, b):
            pltpu.sync_copy(idx_hbm.at[pl.ds(base + i * ch, ch)], idx_v.at[b])
            read(i, b).start()

        start_read(0, 0)

        @pl.loop(0, n, step=2)
        def _(i):
            for b in (0, 1):
                cur = i + b

                @pl.when(cur + 1 < n)
                def _():
                    @pl.when(cur >= 1)
                    def _():
                        scatter(1 - b).wait()
                    start_read(cur + 1, 1 - b)

                read(cur, b).wait()
                scatter(b).start()

        scatter(0).wait()
        scatter(1).wait()

    return pl.kernel(
        body, mesh=_sc_mesh(),
        out_type=jax.ShapeDtypeStruct((n_out, d), src.dtype),
        scratch_types=_sc_scratch(ch, d, src.dtype),
        name="sc_scatter_rows",
    )(src, idx)


def _sc_gather_rows(table, idx):
    n_out = idx.shape[0]
    d = table.shape[1]
    per_w, ch, n = _sc_steps(n_out)

    def body(table_hbm, idx_hbm, out_hbm, idx_v, rows_v, gsem, wsem):
        base = (lax.axis_index("s") * SC_CORES_V7X + lax.axis_index("c")) * per_w

        def gather(b):
            return pltpu.make_async_copy(table_hbm.at[idx_v.at[b]], rows_v.at[b], gsem.at[b])

        def write(i, b):
            return pltpu.make_async_copy(rows_v.at[b], out_hbm.at[pl.ds(base + i * ch, ch)], wsem.at[b])

        def start_gather(i, b):
            pltpu.sync_copy(idx_hbm.at[pl.ds(base + i * ch, ch)], idx_v.at[b])
            gather(b).start()

        start_gather(0, 0)

        @pl.loop(0, n, step=2)
        def _(i):
            for b in (0, 1):
                cur = i + b

                @pl.when(cur + 1 < n)
                def _():
                    @pl.when(cur >= 1)
                    def _():
                        write(cur - 1, 1 - b).wait()
                    start_gather(cur + 1, 1 - b)

                gather(b).wait()
                write(cur, b).start()

        write(n - 2, 0).wait()
        write(n - 1, 1).wait()

    return pl.kernel(
        body, mesh=_sc_mesh(),
        out_type=jax.ShapeDtypeStruct((n_out, d), table.dtype),
        scratch_types=_sc_scratch(ch, d, table.dtype),
        name="sc_gather_rows",
    )(table, idx)


def _moe(h2p, cls3, rank3, cnt, wg, wu, wd, rw3, per_batch=False):
    bsz, s, dh = h2p.shape
    t = bsz * s
    tm = FFN_TM
    counts = cnt[:N_CLASSES, 0].astype(I32)
    padded = ((counts + tm - 1) // tm) * tm
    upto = jnp.arange(N_CLASSES)[:, None] >= jnp.arange(N_CLASSES)[None, :]
    ends = jnp.sum(jnp.where(upto, padded[None, :], 0), axis=1)
    base = ends - padded
    cls2 = cls3[:, 0, :]
    pos2 = rank3[:, 0, :]
    for c in range(N_CLASSES):
        pos2 = pos2 + jnp.where(cls2 == c, base[c], 0)
    pos = pos2.reshape(t)
    ntiles = t // tm + N_CLASSES
    tile_start = jnp.arange(ntiles, dtype=I32) * tm
    tcls = jnp.minimum(jnp.sum((ends[None, :] <= tile_start[:, None]).astype(I32), axis=1), N_CLASSES - 1)
    nused = (ends[-1:] // tm).astype(I32)
    grp = tcls // len(PAIRS)
    pr = tcls % len(PAIRS)
    first = sum(jnp.where(pr == i, p[0], 0) for i, p in enumerate(PAIRS))
    second = sum(jnp.where(pr == i, p[1], 0) for i, p in enumerate(PAIRS))
    e1 = (grp * PER_GROUP + first).astype(I32)
    e2 = (grp * PER_GROUP + second).astype(I32)
    hs = _sc_scatter_rows(h2p.reshape(t, dh), pos, ntiles * tm)
    ys = _ffn(hs, e1, e2, nused, wg, wu, wd, rw3)
    if per_batch:
        return [_sc_gather_rows(ys, pos[b * s:(b + 1) * s]) for b in range(bsz)]
    return _sc_gather_rows(ys, pos).reshape(bsz, s, dh)


def _qkv1_kernel(x_ref, y_ref, gfp_ref, g_ref, sc_ref, sh_ref, w_ref, cos_ref, sin_ref,
                 wg_ref, wu_ref, wd_ref, x2_ref, *rest):
    outs = rest[:9]
    _side_cast((wg_ref, wu_ref, wd_ref), rest[9:12])
    hb_scr, ysc, ysc2 = rest[12:]
    tm = x_ref.shape[1]
    x2 = x_ref[0] + gfp_ref[0] * _unpack_bf16_pairs(y_ref[0])
    x2_ref[0] = x2
    hb_scr[...] = _rms_mod(x2, g_ref[...], sc_ref[0], sh_ref[0]).astype(BF16)
    cos = cos_ref[0]
    sin = sin_ref[0]
    lane_lo = (lax.broadcasted_iota(I32, cos.shape, 1) % HEAD_DIM) < (HEAD_DIM // 2)
    for gi, (_, dil) in reversed(list(enumerate(B_GROUPS))):
        for j in range(3):
            c = gi * 3 + j
            y = jnp.dot(hb_scr[...], w_ref[:, B_WIDTH * c:B_WIDTH * (c + 1)], preferred_element_type=F32)
            out = outs[c]
            for cc in range(B_WIDTH // 128):
                lanes = slice(128 * cc, 128 * (cc + 1))
                r = y[:, lanes]
                if j < 2:
                    r = _rope_apply(r, cos, sin, lane_lo)
                if j == 0:
                    r = r * Q_SCALE
                if dil == 1:
                    out[0, 0, :, lanes] = r.astype(BF16)
                    continue
                ysc[cc] = r
                if dil % 16:
                    for rr in range(dil):
                        out[0, rr, :, lanes] = ysc[cc, pl.ds(rr, tm // dil, stride=dil), :].astype(BF16)
                    continue
                quarter = tm // 4
                inner = dil // 4
                for r1 in range(4):
                    ysc2[cc, r1 * quarter:(r1 + 1) * quarter] = ysc[cc, pl.ds(r1, quarter, stride=4), :]
                for r1 in range(4):
                    for r2 in range(inner):
                        out[0, r1 + 4 * r2, :, lanes] = ysc2[
                            cc, pl.ds(r1 * quarter + r2, tm // dil, stride=inner), :].astype(BF16)


def _qkv1(x, y, gfp, g, sc, sh, w, cos, sin, expert_ws, layer):
    bsz, s, d = x.shape
    tm = TM
    grid = (bsz, s // tm)
    w_in, w_out, w_shape, w_args = _side_cast_specs([(w_, layer) for w_ in expert_ws], grid)
    tok = lambda w_: pl.BlockSpec((1, tm, w_), lambda b, i: (b, i, 0))
    per_b = pl.BlockSpec((1, 1, d), lambda b, i: (b, 0, 0))
    out_specs = [tok(d)]
    out_shape = [jax.ShapeDtypeStruct((bsz, s, d), F32)]
    for _, dil in B_GROUPS:
        for _ in range(3):
            out_specs.append(pl.BlockSpec((1, dil, tm // dil, B_WIDTH), lambda b, i: (b, 0, i, 0)))
            out_shape.append(jax.ShapeDtypeStruct((bsz, dil, s // dil, B_WIDTH), BF16))
    outs = pl.pallas_call(
        _qkv1_kernel,
        grid=grid,
        in_specs=[tok(d), tok(d // 2), per_b, pl.BlockSpec((1, d), lambda b, i: (0, 0)), per_b, per_b,
                  pl.BlockSpec(w.shape, lambda b, i: (0, 0)), tok(128), tok(128)] + w_in,
        out_specs=out_specs + w_out,
        out_shape=out_shape + w_shape,
        scratch_shapes=[pltpu.VMEM((tm, d), BF16), pltpu.VMEM((B_WIDTH // 128, tm, 128), F32),
                        pltpu.VMEM((B_WIDTH // 128, tm, 128), F32)],
        compiler_params=_cparams("parallel", "parallel"),
        name="qkv1",
    )(x, y, gfp, g.reshape(1, d), sc, sh, w, cos, sin, *w_args)
    return outs[:10], [c.reshape(w_.shape[1:]) for c, w_ in zip(outs[10:], expert_ws)]


def _att1_window_pad(lseg):
    qb = min(2 * B_RADIUS, lseg)
    return -(qb + 2 * B_RADIUS) % 128


def _att1_kernel(*refs, chunk, seq):
    ins = refs[:21]
    o_ref = refs[21]
    scr = refs[22:]
    nat_o, nat_d, nat_m = scr[0:3], scr[3:6], scr[6:9]
    n = pl.program_id(1)
    rad = B_RADIUS
    npair = B_WIDTH // 128

    def window(cur, prev, nxt, res, lo, hi, lseg, lanes):
        parts = []
        if lo < 0:
            parts.append(prev[0, res, :, lanes])
        parts.append(cur[0, res, max(lo, 0):min(hi, lseg), lanes])
        if hi > lseg:
            parts.append(nxt[0, res, :, lanes])
        if hi > lseg + rad:
            parts.append(nxt[0, res, :, lanes])
        return jnp.concatenate(parts, axis=0) if len(parts) > 1 else parts[0]

    for c in range(npair):
        lanes = slice(128 * c, 128 * (c + 1))
        for gi, (_, dil) in enumerate(B_GROUPS):
            q_ref, kc, kp, kn, vc, vp, vn = ins[7 * gi:7 * gi + 7]
            lseg = chunk // dil
            qb = min(2 * rad, lseg)
            kb_n = qb + 2 * rad + _att1_window_pad(lseg)
            lstr = seq // dil
            rel = lax.broadcasted_iota(I32, (qb, kb_n), 1) - lax.broadcasted_iota(I32, (qb, kb_n), 0)
            band_bias = jnp.where((rel >= 0) & (rel <= 2 * rad), 0.0, NEG)
            jrow = lax.broadcasted_iota(I32, (1, kb_n), 1)
            first_head = lax.broadcasted_iota(I32, (qb, 128), 1) < HEAD_DIM
            zero = jnp.zeros((qb, 128), BF16)
            ones = jnp.ones((kb_n, 128), BF16)
            for res in range(dil):
                for q0 in range(0, lseg, qb):
                    apos = n * lseg + q0 - rad + jrow
                    bias = band_bias + jnp.where((apos >= 0) & (apos < lstr), 0.0, NEG)
                    rows = pl.ds(q0, qb) if dil == 1 else pl.ds(res + dil * q0, qb, stride=dil)
                    qc = q_ref[0, res, q0:q0 + qb, lanes]
                    kb = window(kc, kp, kn, res, q0 - rad, q0 - rad + kb_n, lseg, lanes)
                    vb = jnp.concatenate(
                        [window(vc, vp, vn, res, q0 - rad, q0 - rad + kb_n, lseg, lanes), ones], axis=1)
                    qs = jnp.concatenate(
                        [jnp.where(first_head, qc, zero), jnp.where(first_head, zero, qc)], axis=0)
                    s2 = lax.dot_general(qs, kb, (((1,), (1,)), ((), ())), preferred_element_type=F32)
                    ps, ms = [], []
                    for t2 in range(2):
                        s = s2[qb * t2:qb * (t2 + 1)] + bias
                        m = jnp.max(s, axis=-1, keepdims=True)
                        ps.append(jnp.exp2((s - m).astype(BF16)))
                        ms.append(m)
                    pv = jnp.dot(jnp.concatenate(ps, axis=0), vb, preferred_element_type=F32)
                    nat_o[gi][rows, :] = jnp.where(first_head, pv[0:qb, 0:128], pv[qb:2 * qb, 0:128])
                    nat_d[gi][rows, :] = jnp.where(first_head, pv[0:qb, 128:256], pv[qb:2 * qb, 128:256])
                    nat_m[gi][rows, :] = jnp.where(first_head, ms[0], ms[1])

        ms = [nat_m[gi][...] for gi in range(3)]
        mx = jnp.maximum(jnp.maximum(ms[0], ms[1]), ms[2])
        w = [jnp.exp2(x - mx) for x in ms]
        num = w[0] * nat_o[0][...] + w[1] * nat_o[1][...] + w[2] * nat_o[2][...]
        den = w[0] * nat_d[0][...] + w[1] * nat_d[1][...] + w[2] * nat_d[2][...]
        o_ref[0, :, lanes] = (num / den).astype(BF16)


def _att1(qkv):
    bsz = qkv[0].shape[0]
    seq = qkv[0].shape[1] * qkv[0].shape[2]
    chunk = ATT1_CHUNK
    rad = B_RADIUS
    in_specs, args, scratch = [], [], []
    for gi, (_, dil) in enumerate(B_GROUPS):
        q, k, v = qkv[3 * gi:3 * gi + 3]
        lseg = chunk // dil
        per = lseg // rad
        last = seq // dil // rad - 1
        cur = pl.BlockSpec((1, dil, lseg, B_WIDTH), lambda b, n: (b, 0, n, 0))
        prev = pl.BlockSpec((1, dil, rad, B_WIDTH),
                            lambda b, n, per=per: (b, 0, jnp.maximum(n * per - 1, 0), 0))
        nxt = pl.BlockSpec((1, dil, rad, B_WIDTH),
                           lambda b, n, per=per, last=last: (b, 0, jnp.minimum(n * per + per, last), 0))
        in_specs += [cur, cur, prev, nxt, cur, prev, nxt]
        args += [q, k, k, k, v, v, v]
    for _ in range(3):
        for _ in B_GROUPS:
            scratch.append(pltpu.VMEM((chunk, 128), F32))
    return pl.pallas_call(
        functools.partial(_att1_kernel, chunk=chunk, seq=seq),
        grid=(bsz, seq // chunk),
        in_specs=in_specs,
        out_specs=pl.BlockSpec((1, chunk, B_WIDTH), lambda b, n: (b, n, 0)),
        out_shape=jax.ShapeDtypeStruct((bsz, seq, B_WIDTH), BF16),
        scratch_shapes=scratch,
        compiler_params=_cparams("parallel", "parallel"),
        name="att1",
    )(*args)


def _final_kernel(x_ref, y_ref, gfp_ref, g_ref, *rest):
    o_ref = rest[-1]
    x = x_ref[0] + gfp_ref[0] * _unpack_bf16_pairs(y_ref[...])
    ms = jnp.mean(x * x, axis=-1, keepdims=True)
    o_ref[0] = (x * lax.rsqrt(ms + NORM_EPS)) * g_ref[...]


def _final(x, ys, gfp, g):
    bsz, s, d = x.shape
    tm = TM_WIDE
    out = None
    for b in range(bsz):
        tok = pl.BlockSpec((1, tm, d), lambda i, b=b: (b, i, 0))
        in_specs = [tok, pl.BlockSpec((tm, d // 2), lambda i: (i, 0)),
                    pl.BlockSpec((1, 1, d), lambda i, b=b: (b, 0, 0)),
                    pl.BlockSpec((1, d), lambda i: (0, 0))]
        args = [x, ys[b], gfp, g.reshape(1, d)]
        if out is not None:
            in_specs.append(pl.BlockSpec(memory_space=pl.ANY))
            args.append(out)
        out = pl.pallas_call(
            _final_kernel,
            grid=(s // tm,),
            in_specs=in_specs,
            out_specs=tok,
            out_shape=jax.ShapeDtypeStruct((bsz, s, d), F32),
            input_output_aliases={} if len(args) == 4 else {4: 0},
            compiler_params=_cparams("parallel"),
            name="final_norm",
        )(*args)
    return out


def kernel(x, c, positions, ada_w, ada_b, norm_mix_g, norm_ffn_g, a_w_qkv, a_w_o, a_sink, b_w_qkv, b_w_o,
           router_w, router_bias, exp_w_gate, exp_w_up, exp_w_down, final_norm_g):
    bsz, s, d = x.shape
    assert d == D_MODEL and ada_w.shape[0] == 2 and s % ATT1_CHUNK == 0
    assert all(w_ // (2 * dil) == B_RADIUS for w_, dil in B_GROUPS)

    mod = _modulation(c, ada_w, ada_b)
    mods = [[mod[l][:, k * d:(k + 1) * d].reshape(bsz, 1, d) for k in range(6)] for l in range(2)]
    cos, sin, sin_paired = _rope_tables(positions)

    rw_t = router_w.T.astype(BF16)
    rows = jnp.array([PER_GROUP * g + j if g < N_EXPERT_GROUPS else 0
                      for j in range(PER_GROUP) for g in range(8)], I32)
    live = jnp.array([1.0 if g < N_EXPERT_GROUPS else 0.0
                      for j in range(PER_GROUP) for g in range(8)], F32)
    rw32 = (rw_t[rows].astype(F32) * live[:, None]).astype(BF16)
    rb32 = jnp.where(live > 0, router_bias.astype(F32)[rows], NEG).reshape(32, 1)
    rw3 = rw_t.reshape(N_EXPERTS, 1, d)
    expert_ws = (exp_w_gate, exp_w_up, exp_w_down)

    sh_m, sc_m, g_m, sh_f, sc_f, g_f = mods[0]
    nq, nkv = A_Q_HEADS * HEAD_DIM, A_KV_HEADS * HEAD_DIM
    w0 = a_w_qkv[0].astype(BF16)
    w0 = jnp.concatenate([_paired_layout(w0[:, :nq]), _paired_layout(w0[:, nq:nq + nkv]), w0[:, nq + nkv:]],
                         axis=1)
    q, k, v = _qkv0(x, norm_mix_g[0], sc_m, sh_m, w0, cos, sin_paired)
    o, cast = _att0(q, k, v, a_sink[0].astype(F32),
                    [(w_, 0) for w_ in expert_ws] + [(a_w_o, 0), (b_w_o, 0), (b_w_qkv, 0)])
    experts, (a_wo, b_wo, b_wqkv) = cast[:3], cast[3:]
    x1, *routed = _post(o, a_wo, x, g_m, norm_ffn_g[0], sc_f, sh_f, rw32, rb32)
    y = _moe(*routed, *experts, rw3)
    g_f_prev = g_f

    sh_m, sc_m, g_m, sh_f, sc_f, g_f = mods[1]
    outs, experts = _qkv1(x1, y, g_f_prev, norm_mix_g[1], sc_m, sh_m, b_wqkv, cos, sin, expert_ws, 1)
    x2, qkv = outs[0], outs[1:]
    o = _att1(qkv)
    x3, *routed = _post(o, b_wo, x2, g_m, norm_ffn_g[1], sc_f, sh_f, rw32, rb32)
    ys = _moe(*routed, *experts, rw3, per_batch=True)

    return _final(x3, ys, g_f, final_norm_g)
```

```python
import functools

import jax
import jax.numpy as jnp
from jax import lax
from jax.experimental import pallas as pl
from jax.experimental.pallas import tpu as pltpu
from jax.experimental.pallas import tpu_sc as plsc

F32, BF16, I32, U32 = jnp.float32, jnp.bfloat16, jnp.int32, jnp.uint32

D_MODEL = 1024
HEAD_DIM = 64
ROPE_THETA = 10000.0
NORM_EPS = 1e-6
A_Q_HEADS = 16
A_KV_HEADS = 4
A_GROUP = A_Q_HEADS // A_KV_HEADS
A_RADIUS = 128
B_GROUPS = ((128, 1), (512, 4), (2048, 16))
B_HEADS = 8
B_RADIUS = 64
B_WIDTH = B_HEADS * HEAD_DIM
N_EXPERTS = 16
N_EXPERT_GROUPS = 4
PER_GROUP = N_EXPERTS // N_EXPERT_GROUPS
D_EXPERT = D_MODEL // 2
PAIRS = ((0, 1), (0, 2), (0, 3), (1, 2), (1, 3), (2, 3))
N_CLASSES = N_EXPERT_GROUPS * len(PAIRS)
NEG = -1e30
LOG2E = 1.4426950408889634
Q_SCALE = HEAD_DIM ** -0.5 * LOG2E

VMEM_LIMIT = 56 * 1024 * 1024
TM = 512
ROPE_TS = 2048
TM_WIDE = 1024
RANK_BLOCK = 512
ATT1_CHUNK = 1024
FFN_TM = 256
FFN_TILES_PER_STEP = 2
SC_CORES_V7X = 2
SC_SUBCORES_V7X = 16
SC_ROWS_PER_STEP = 64


def _cparams(*sem):
    return pltpu.CompilerParams(dimension_semantics=sem, vmem_limit_bytes=VMEM_LIMIT)


def _rms_mod(x, g, sc, sh):
    ms = jnp.mean(x * x, axis=-1, keepdims=True)
    return (x * lax.rsqrt(ms + NORM_EPS)) * (g * (1.0 + sc)) + sh


def _rope_apply_paired(y, cos, sin_paired):
    return y * cos + pltpu.roll(y, HEAD_DIM, 1) * sin_paired


def _paired_layout(w):
    k, n = w.shape
    half = HEAD_DIM // 2
    return w.reshape(k, n // 128, 2, 2, half).transpose(0, 1, 3, 2, 4).reshape(k, n)


def _silu(x):
    half = 0.5 * x
    return half + half * jnp.tanh(half)


def _pack_bf16_pairs(x):
    n = x.shape[1] // 2
    lo = lax.bitcast_convert_type(x[:, :n].astype(BF16).astype(F32), U32)
    hi = lax.bitcast_convert_type(x[:, n:].astype(BF16).astype(F32), U32)
    return (hi & jnp.uint32(0xFFFF0000)) | (lo >> 16)


def _unpack_bf16_pairs(p):
    lo = lax.bitcast_convert_type(p << 16, F32)
    hi = lax.bitcast_convert_type(p & jnp.uint32(0xFFFF0000), F32)
    return jnp.concatenate([lo, hi], axis=1)


def _dup_head(x, first_head, which):
    rolled = pltpu.roll(x, HEAD_DIM, 1)
    return jnp.where(first_head, x, rolled) if which == 0 else jnp.where(first_head, rolled, x)


def _mod_kernel(ct_ref, w_ref, b_ref, o_ref, *, bsz):
    ct = ct_ref[...]
    s = ct * jax.nn.sigmoid(ct)
    w = w_ref[0]
    rows = [jnp.sum(w * s[:, b:b + 1], axis=0, keepdims=True) + b_ref[0] for b in range(bsz)]
    o_ref[0] = jnp.concatenate(rows + [jnp.zeros_like(rows[0])] * (8 - bsz), axis=0)


def _modulation(c, ada_w, ada_b):
    depth, d, n = ada_w.shape
    bsz = c.shape[0]
    assert bsz <= 8
    tn = 1536
    out = pl.pallas_call(
        functools.partial(_mod_kernel, bsz=bsz),
        grid=(depth, n // tn),
        in_specs=[pl.BlockSpec((d, bsz), lambda l, j: (0, 0)),
                  pl.BlockSpec((1, d, tn), lambda l, j: (l, 0, j)),
                  pl.BlockSpec((1, 1, tn), lambda l, j: (l, 0, j))],
        out_specs=pl.BlockSpec((1, 8, tn), lambda l, j: (l, 0, j)),
        out_shape=jax.ShapeDtypeStruct((depth, 8, n), F32),
        compiler_params=_cparams("arbitrary", "arbitrary"),
        name="modulation",
    )(c.T, ada_w, ada_b.reshape(depth, 1, n))
    return out[:, :bsz]


def _rope_kernel(pos_ref, invf_ref, sign_ref, cos_ref, sin_ref):
    p = pos_ref[0].astype(F32)
    n = p.shape[0]
    quarter = lax.broadcasted_iota(I32, (n, 128), 1) // (HEAD_DIM // 2)
    pos4 = jnp.where(quarter == 0, p[:, 0:1],
                     jnp.where(quarter == 1, p[:, 1:2], jnp.where(quarter == 2, p[:, 2:3], p[:, 3:4])))
    ang = pos4 * invf_ref[...]
    dense = (jnp.cos(ang), jnp.sin(ang))
    for u in range(4):
        spread = []
        for d in dense:
            e = jnp.where(quarter == u, d, 0.0)
            e = e + pltpu.roll(e, HEAD_DIM // 2, 1)
            spread.append(e + pltpu.roll(e, HEAD_DIM, 1))
        rows = pl.ds(u, n, stride=4)
        cos_ref[0, rows, :] = spread[0]
        sin_ref[0, rows, :] = spread[1] * sign_ref[...]


def _rope_tables(positions):
    bsz, s = positions.shape
    inv_freq = ROPE_THETA ** (-jnp.arange(0, HEAD_DIM, 2, dtype=F32) / HEAD_DIM)
    invf = jnp.tile(inv_freq, 4).reshape(1, 128)
    sign = jnp.concatenate([-jnp.ones((HEAD_DIM,), F32), jnp.ones((HEAD_DIM,), F32)]).reshape(1, 128)
    ts = ROPE_TS
    return pl.pallas_call(
        _rope_kernel,
        grid=(bsz, s // ts),
        in_specs=[pl.BlockSpec((1, ts // 4, 4), lambda b, i: (b, i, 0)),
                  pl.BlockSpec((1, 128), lambda b, i: (0, 0)),
                  pl.BlockSpec((1, 128), lambda b, i: (0, 0))],
        out_specs=[pl.BlockSpec((1, ts, 128), lambda b, i: (b, i, 0))] * 2,
        out_shape=[jax.ShapeDtypeStruct((bsz, s, 128), F32)] * 2,
        compiler_params=_cparams("parallel", "parallel"),
        name="rope_tables",
    )(positions.reshape(bsz, s // 4, 4), invf, sign)


def _qkv0_kernel(x_ref, g_ref, sc_ref, sh_ref, w_ref, cos_ref, sin_ref,
                 q_ref, k_ref, v_ref, hb_scr):
    hb_scr[...] = _rms_mod(x_ref[0], g_ref[...], sc_ref[0], sh_ref[0]).astype(BF16)
    cos = cos_ref[0]
    sin = sin_ref[0]
    lane = lax.broadcasted_iota(I32, cos.shape, 1)
    x_lo = (lane % HEAD_DIM) < (HEAD_DIM // 2)
    nq = A_Q_HEADS * HEAD_DIM
    nkv = A_KV_HEADS * HEAD_DIM
    for c in range(nq // 256):
        y = jnp.dot(hb_scr[...], w_ref[:, 256 * c:256 * (c + 1)], preferred_element_type=F32)
        for cc in range(2):
            r = _rope_apply_paired(y[:, 128 * cc:128 * (cc + 1)], cos, sin) * Q_SCALE
            q_ref[0, :, 256 * c + 128 * cc:256 * c + 128 * (cc + 1)] = r.astype(BF16)
    first_head = lane < HEAD_DIM
    y = jnp.dot(hb_scr[...], w_ref[:, nq:nq + nkv], preferred_element_type=F32)
    for cc in range(nkv // 128):
        r = _rope_apply_paired(y[:, 128 * cc:128 * (cc + 1)], cos, sin)
        k_ref[0, :, 256 * cc:256 * cc + 128] = jnp.where(
            x_lo, r, pltpu.roll(r, HEAD_DIM // 2, 1)).astype(BF16)
        k_ref[0, :, 256 * cc + 128:256 * cc + 256] = jnp.where(
            x_lo, pltpu.roll(r, 128 - HEAD_DIM // 2, 1), r).astype(BF16)
    y = jnp.dot(hb_scr[...], w_ref[:, nq + nkv:nq + 2 * nkv], preferred_element_type=F32)
    for cc in range(nkv // 128):
        r = y[:, 128 * cc:128 * (cc + 1)]
        for which in range(2):
            g = 2 * cc + which
            v_ref[0, :, 128 * g:128 * (g + 1)] = _dup_head(r, first_head, which).astype(BF16)


def _qkv0(x, g, sc, sh, w, cos, sin):
    bsz, s, d = x.shape
    nq = A_Q_HEADS * HEAD_DIM
    nkv = A_KV_HEADS * HEAD_DIM
    tm = TM_WIDE
    tok = lambda w_: pl.BlockSpec((1, tm, w_), lambda b, i: (b, i, 0))
    per_b = pl.BlockSpec((1, 1, d), lambda b, i: (b, 0, 0))
    return pl.pallas_call(
        _qkv0_kernel,
        grid=(bsz, s // tm),
        in_specs=[tok(d), pl.BlockSpec((1, d), lambda b, i: (0, 0)), per_b, per_b,
                  pl.BlockSpec(w.shape, lambda b, i: (0, 0)), tok(128), tok(128)],
        out_specs=[tok(nq), tok(2 * nkv), tok(2 * nkv)],
        out_shape=[jax.ShapeDtypeStruct((bsz, s, nq), BF16),
                   jax.ShapeDtypeStruct((bsz, s, 2 * nkv), BF16),
                   jax.ShapeDtypeStruct((bsz, s, 2 * nkv), BF16)],
        scratch_shapes=[pltpu.VMEM((tm, d), BF16)],
        compiler_params=_cparams("parallel", "parallel"),
        name="qkv0",
    )(x, g.reshape(1, d), sc, sh, w, cos, sin)


def _side_cast_specs(items, grid):
    nsteps = grid[0] * grid[1]
    in_specs, out_specs, out_shape, args = [], [], [], []
    for w, layer in items:
        n = w.shape[-1]
        per_layer = w[0].size // n
        rows = per_layer // nsteps
        assert rows * nsteps == per_layer and rows % 16 == 0
        in_specs.append(pl.BlockSpec(
            (rows, n), lambda b, i, layer=layer: (layer * nsteps + b * grid[1] + i, 0)))
        out_specs.append(pl.BlockSpec((rows, n), lambda b, i: (b * grid[1] + i, 0)))
        out_shape.append(jax.ShapeDtypeStruct((per_layer, n), BF16))
        args.append(w.reshape(w.shape[0] * per_layer, n))
    return in_specs, out_specs, out_shape, args


def _side_cast(srcs, dsts, paired_chunks=None):
    for n, (src, dst) in enumerate(zip(srcs, dsts)):
        chunks = paired_chunks[n] if paired_chunks else ()
        if not chunks:
            dst[...] = src[...].astype(BF16)
            continue
        quarter = lax.broadcasted_iota(I32, (src.shape[0], 128), 1) // (HEAD_DIM // 2)
        for c in range(src.shape[1] // 128):
            x = src[:, 128 * c:128 * (c + 1)]
            if c in chunks:
                x = jnp.where(quarter == 1, pltpu.roll(x, 128 - HEAD_DIM // 2, 1),
                              jnp.where(quarter == 2, pltpu.roll(x, HEAD_DIM // 2, 1), x))
            dst[:, 128 * c:128 * (c + 1)] = x.astype(BF16)


def _att0_kernel(sink_ref, q_ref, kc_ref, kp_ref, kn_ref, vc_ref, vp_ref, vn_ref, *rest, tq, seq, paired_chunks):
    ncast = len(paired_chunks)
    o_ref = rest[ncast]
    kf, vf = rest[2 * ncast + 1:]
    _side_cast(rest[:ncast], rest[ncast + 1:2 * ncast + 1], paired_chunks)
    i = pl.program_id(1)
    r = A_RADIUS
    ones = jnp.ones((tq + 2 * r, 128), BF16)
    for g in range(A_KV_HEADS):
        lanes = slice(128 * g, 128 * (g + 1))
        kf[g, 0:r] = kp_ref[0, :, lanes]
        kf[g, r:r + tq] = kc_ref[0, :, lanes]
        kf[g, r + tq:r + tq + r] = kn_ref[0, :, lanes]
        vf[g, 0:r, 0:128] = vp_ref[0, :, lanes]
        vf[g, r:r + tq, 0:128] = vc_ref[0, :, lanes]
        vf[g, r + tq:r + tq + r, 0:128] = vn_ref[0, :, lanes]
        vf[g, :, 128:256] = ones

    ii = lax.broadcasted_iota(I32, (r, r), 0)
    jj = lax.broadcasted_iota(I32, (r, r), 1)
    first_head = jj < HEAD_DIM
    q_first = (jj % HEAD_DIM) < (HEAD_DIM // 2)
    zero = jnp.zeros((r, 128), BF16)

    for j in range(tq // r):
        q0 = j * r
        base = i * tq + q0
        bias_lo = jnp.where((jj >= ii) & (base - r + jj >= 0), 0.0, NEG)
        bias_hi = jnp.where((jj <= ii) & (base + r + jj < seq), 0.0, NEG)
        for g in range(A_KV_HEADS):
            kb = kf[g, pl.ds(q0, 3 * r), :]
            vb = vf[g, pl.ds(q0, 3 * r), :]
            qs = []
            for cc in range(2):
                c = 2 * g + cc
                qc = q_ref[0, pl.ds(q0, r), 128 * c:128 * (c + 1)]
                qs += [jnp.where(q_first, qc, zero), jnp.where(q_first, zero, qc)]
            s4 = lax.dot_general(jnp.concatenate(qs, axis=0), kb, (((1,), (1,)), ((), ())),
                                 preferred_element_type=F32)
            ps, ms = [], []
            for t in range(A_GROUP):
                s = s4[r * t:r * (t + 1)]
                a0 = s[:, 0:r] + bias_lo
                a1 = s[:, r:2 * r]
                a2 = s[:, 2 * r:3 * r] + bias_hi
                sink = sink_ref[A_GROUP * g + t] * LOG2E
                m = jnp.max(jnp.maximum(jnp.maximum(a0, a1), a2), axis=-1, keepdims=True)
                m = jnp.maximum(m, sink)
                ps.append(jnp.exp2(jnp.concatenate([a0 - m, a1 - m, a2 - m], axis=1).astype(BF16)))
                ms.append(m)
            pv = jnp.dot(jnp.concatenate(ps, axis=0), vb, preferred_element_type=F32)
            os_ = []
            for t in range(A_GROUP):
                sink = sink_ref[A_GROUP * g + t] * LOG2E
                den = pv[r * t:r * (t + 1), 128:256] + jnp.exp2(sink - ms[t])
                os_.append(pv[r * t:r * (t + 1), 0:128] / den)
            for cc in range(2):
                c = 2 * g + cc
                o_ref[0, pl.ds(q0, r), 128 * c:128 * (c + 1)] = jnp.where(
                    first_head, os_[2 * cc], os_[2 * cc + 1]).astype(BF16)


def _att0(q, k2, v2, sink, cast_items, paired_chunks):
    bsz, s, nq = q.shape
    nkv2 = k2.shape[-1]
    tq = TM_WIDE
    r = A_RADIUS
    per = tq // r
    last = s // r - 1
    grid = (bsz, s // tq)
    cur = lambda w_: pl.BlockSpec((1, tq, w_), lambda b, i: (b, i, 0))
    prev = pl.BlockSpec((1, r, nkv2), lambda b, i: (b, jnp.maximum(i * per - 1, 0), 0))
    nxt = pl.BlockSpec((1, r, nkv2), lambda b, i: (b, jnp.minimum(i * per + per, last), 0))
    w_in, w_out, w_shape, w_args = _side_cast_specs(cast_items, grid)
    o, *cast = pl.pallas_call(
        functools.partial(_att0_kernel, tq=tq, seq=s, paired_chunks=paired_chunks),
        grid=grid,
        in_specs=[pl.BlockSpec(memory_space=pltpu.SMEM),
                  cur(nq), cur(nkv2), prev, nxt, cur(nkv2), prev, nxt] + w_in,
        out_specs=[cur(nq)] + w_out,
        out_shape=[jax.ShapeDtypeStruct((bsz, s, nq), BF16)] + w_shape,
        scratch_shapes=[pltpu.VMEM((A_KV_HEADS, tq + 2 * r, 128), BF16),
                        pltpu.VMEM((A_KV_HEADS, tq + 2 * r, 256), BF16)],
        compiler_params=_cparams("parallel", "parallel"),
        name="att0",
    )(sink, q, k2, k2, k2, v2, v2, v2, *w_args)
    return o, [c.reshape(w.shape[1:]) for c, (w, _) in zip(cast, cast_items)]


def _post_kernel(o_ref, wo_ref, x_ref, gm_ref, gf_ref, scf_ref, shf_ref, rw_ref, rb_ref, upper_ref,
                 x1_ref, h2_ref, cls_ref, rank_ref, cnt_ref, carry):
    mix = jnp.dot(o_ref[0], wo_ref[...], preferred_element_type=F32)
    x1 = x_ref[0] + gm_ref[0] * mix
    x1_ref[0] = x1
    h2 = _rms_mod(x1, gf_ref[...], scf_ref[0], shf_ref[0]).astype(BF16)
    h2_ref[0] = _pack_bf16_pairs(h2.astype(F32))
    tq = h2.shape[0]
    logits = lax.dot_general(rw_ref[...], h2, (((1,), (1,)), ((), ())), preferred_element_type=F32)
    scores = jax.nn.sigmoid(logits)
    biased = scores + rb_ref[...]
    pj = [biased[8 * j:8 * (j + 1)] for j in range(PER_GROUP)]
    sel = []
    for j in range(PER_GROUP):
        beaten = jnp.zeros((8, tq), F32)
        for j2 in range(PER_GROUP):
            if j2 == j:
                continue
            beats = (pj[j2] > pj[j]) | (pj[j2] == pj[j]) if j2 < j else (pj[j2] > pj[j])
            beaten = beaten + jnp.where(beats, 1.0, 0.0)
        sel.append(beaten < 2.0)
    gscore = jnp.zeros((8, tq), F32)
    for j in range(PER_GROUP):
        gscore = gscore + jnp.where(sel[j], pj[j], 0.0)
    gi = lax.broadcasted_iota(I32, (8, tq), 0).astype(F32)
    gmax = jnp.max(gscore, axis=0, keepdims=True)
    gidx = jnp.min(jnp.where(gscore == gmax, gi, 8.0), axis=0, keepdims=True)
    onehot = gi == gidx
    f = [jnp.max(jnp.where(onehot & sel[j], 1.0, 0.0), axis=0, keepdims=True) > 0.5
         for j in range(PER_GROUP)]
    pair = jnp.where(f[0], jnp.where(f[1], 0.0, jnp.where(f[2], 1.0, 2.0)),
                     jnp.where(f[1], jnp.where(f[2], 3.0, 4.0), 5.0))
    cls = (gidx * float(len(PAIRS)) + pair).astype(I32)
    cls_ref[0] = cls

    @pl.when((pl.program_id(0) == 0) & (pl.program_id(1) == 0))
    def _():
        carry[...] = jnp.zeros_like(carry)

    c = carry[...]
    ranks = []
    for blk in range(tq // RANK_BLOCK):
        cls_b = cls[:, RANK_BLOCK * blk:RANK_BLOCK * (blk + 1)]
        oh = lax.broadcasted_iota(I32, (32, RANK_BLOCK), 0) == cls_b
        ohf = jnp.where(oh, 1.0, 0.0)
        within = jnp.dot(ohf.astype(BF16), upper_ref[...], preferred_element_type=F32)
        ranks.append(jnp.sum(jnp.where(oh, within + c[:, 0:1], 0.0), axis=0, keepdims=True))
        c = c + jnp.sum(ohf, axis=1, keepdims=True)
    rank_ref[0] = jnp.concatenate(ranks, axis=1).astype(I32)
    carry[...] = c
    cnt_ref[...] = c


def _post(o, wo, x, gm, gf, scf, shf, rw32, rb32):
    bsz, s, d = x.shape
    do = o.shape[-1]
    tq = TM_WIDE
    nt = s // tq
    tok = lambda w_: pl.BlockSpec((1, tq, w_), lambda b, i: (b, i, 0))
    per_b = pl.BlockSpec((1, 1, d), lambda b, i: (b, 0, 0))
    full = lambda a: pl.BlockSpec(a.shape, lambda b, i: (0,) * a.ndim)
    per_tile = pl.BlockSpec((1, 1, tq), lambda b, i: (b * nt + i, 0, 0))
    gf2 = gf.reshape(1, d)
    upper = (jnp.arange(RANK_BLOCK)[:, None] < jnp.arange(RANK_BLOCK)[None, :]).astype(BF16)
    return pl.pallas_call(
        _post_kernel,
        grid=(bsz, nt),
        in_specs=[tok(do), full(wo), tok(d), per_b, full(gf2), per_b, per_b, full(rw32), full(rb32),
                  full(upper)],
        out_specs=[tok(d), tok(d // 2), per_tile, per_tile, pl.BlockSpec((32, 128), lambda b, i: (0, 0))],
        out_shape=[jax.ShapeDtypeStruct((bsz, s, d), F32),
                   jax.ShapeDtypeStruct((bsz, s, d // 2), U32),
                   jax.ShapeDtypeStruct((bsz * nt, 1, tq), I32),
                   jax.ShapeDtypeStruct((bsz * nt, 1, tq), I32),
                   jax.ShapeDtypeStruct((32, 128), F32)],
        scratch_shapes=[pltpu.VMEM((32, 128), F32)],
        compiler_params=_cparams("arbitrary", "arbitrary"),
        name="post_attention",
    )(o, wo, x, gm, gf2, scf, shf, rw32, rb32, upper)


def _ffn_kernel(e1_ref, e2_ref, nused_ref, hs_ref, *refs):
    ys_ref = refs[-1]
    j = pl.program_id(0)
    tm = FFN_TM

    @pl.when(j * FFN_TILES_PER_STEP < nused_ref[0])
    def _():
        for sub in range(FFN_TILES_PER_STEP):
            wg1, wu1, wd1, rw1, wg2, wu2, wd2, rw2 = refs[8 * sub:8 * (sub + 1)]
            rows = slice(tm * sub, tm * (sub + 1))
            xf = _unpack_bf16_pairs(hs_ref[rows])
            x = xf.astype(BF16)
            gt1 = jnp.dot(x, wg1[0], preferred_element_type=F32)
            up1 = jnp.dot(x, wu1[0], preferred_element_type=F32)
            gt2 = jnp.dot(x, wg2[0], preferred_element_type=F32)
            up2 = jnp.dot(x, wu2[0], preferred_element_type=F32)
            a1 = (_silu(gt1) * up1).astype(BF16)
            y1 = jnp.dot(a1, wd1[0], preferred_element_type=F32)
            a2 = (_silu(gt2) * up2).astype(BF16)
            y2 = jnp.dot(a2, wd2[0], preferred_element_type=F32)
            s1 = jax.nn.sigmoid(jnp.sum(xf * rw1[0].astype(F32), axis=-1, keepdims=True))
            s2 = jax.nn.sigmoid(jnp.sum(xf * rw2[0].astype(F32), axis=-1, keepdims=True))
            den = s1 + s2
            ys_ref[rows] = _pack_bf16_pairs((s1 / den) * y1 + (s2 / den) * y2)

    @pl.when(j * FFN_TILES_PER_STEP >= nused_ref[0])
    def _():
        ys_ref[...] = jnp.zeros_like(ys_ref)


def _ffn(hs, e1, e2, nused, wg, wu, wd, rw3):
    npad, dh = hs.shape
    d = 2 * dh
    per = FFN_TILES_PER_STEP
    rows = per * FFN_TM
    de = wg.shape[-1]
    assert npad % rows == 0
    row = pl.BlockSpec((rows, dh), lambda j, e1, e2, nu: (jnp.minimum(j, (nu[0] - 1) // per), 0))
    w_specs = []
    for sub in range(per):
        for which in range(2):
            pick = lambda j, e1, e2, nu, sub=sub, which=which: ((e2 if which else e1)[per * j + sub], 0, 0)
            w_specs += [pl.BlockSpec((1, d, de), pick), pl.BlockSpec((1, d, de), pick),
                        pl.BlockSpec((1, de, d), pick), pl.BlockSpec((1, 1, d), pick)]
    grid_spec = pltpu.PrefetchScalarGridSpec(
        num_scalar_prefetch=3,
        grid=(npad // rows,),
        in_specs=[row] + w_specs,
        out_specs=pl.BlockSpec((rows, dh), lambda j, e1, e2, nu: (j, 0)),
    )
    return pl.pallas_call(
        _ffn_kernel,
        grid_spec=grid_spec,
        out_shape=jax.ShapeDtypeStruct((npad, dh), U32),
        compiler_params=_cparams("arbitrary"),
        name="expert_ffn",
    )(e1, e2, nused, hs, *([wg, wu, wd, rw3] * (2 * per)))


def _sc_mesh():
    return plsc.VectorSubcoreMesh(core_axis_name="c", subcore_axis_name="s",
                                  num_cores=SC_CORES_V7X, num_subcores=SC_SUBCORES_V7X)


def _sc_steps(n_rows):
    workers = SC_CORES_V7X * SC_SUBCORES_V7X
    per_w = n_rows // workers
    ch = min(SC_ROWS_PER_STEP, per_w // 2)
    n = per_w // ch
    assert per_w * workers == n_rows and n * ch == per_w and n % 2 == 0 and ch % 8 == 0
    return per_w, ch, n


def _sc_scratch(ch, d, dtype):
    return [pltpu.VMEM((2, ch), I32), pltpu.VMEM((2, ch, d), dtype),
            pltpu.SemaphoreType.DMA((2,)), pltpu.SemaphoreType.DMA((2,))]


def _sc_scatter_rows(src, idx, n_out):
    n_in, d = src.shape
    per_w, ch, n = _sc_steps(n_in)

    def body(src_hbm, idx_hbm, out_hbm, idx_v, rows_v, rsem, ssem):
        base = (lax.axis_index("s") * SC_CORES_V7X + lax.axis_index("c")) * per_w

        def read(i, b):
            return pltpu.make_async_copy(src_hbm.at[pl.ds(base + i * ch, ch)], rows_v.at[b], rsem.at[b])

        def scatter(b):
            return pltpu.make_async_copy(rows_v.at[b], out_hbm.at[idx_v.at[b]], ssem.at[b])

        def start_read(i, b):
            pltpu.sync_copy(idx_hbm.at[pl.ds(base + i * ch, ch)], idx_v.at[b])
            read(i, b).start()

        start_read(0, 0)

        @pl.loop(0, n, step=2)
        def _(i):
            for b in (0, 1):
                cur = i + b

                @pl.when(cur + 1 < n)
                def _():
                    @pl.when(cur >= 1)
                    def _():
                        scatter(1 - b).wait()
                    start_read(cur + 1, 1 - b)

                read(cur, b).wait()
                scatter(b).start()

        scatter(0).wait()
        scatter(1).wait()

    return pl.kernel(
        body, mesh=_sc_mesh(),
        out_type=jax.ShapeDtypeStruct((n_out, d), src.dtype),
        scratch_types=_sc_scratch(ch, d, src.dtype),
        name="sc_scatter_rows",
    )(src, idx)


def _sc_gather_rows(table, idx):
    n_out = idx.shape[0]
    d = table.shape[1]
    per_w, ch, n = _sc_steps(n_out)

    def body(table_hbm, idx_hbm, out_hbm, idx_v, rows_v, gsem, wsem):
        base = (lax.axis_index("s") * SC_CORES_V7X + lax.axis_index("c")) * per_w

        def gather(b):
            return pltpu.make_async_copy(table_hbm.at[idx_v.at[b]], rows_v.at[b], gsem.at[b])

        def write(i, b):
            return pltpu.make_async_copy(rows_v.at[b], out_hbm.at[pl.ds(base + i * ch, ch)], wsem.at[b])

        def start_gather(i, b):
            pltpu.sync_copy(idx_hbm.at[pl.ds(base + i * ch, ch)], idx_v.at[b])
            gather(b).start()

        start_gather(0, 0)

        @pl.loop(0, n, step=2)
        def _(i):
            for b in (0, 1):
                cur = i + b

                @pl.when(cur + 1 < n)
                def _():
                    @pl.when(cur >= 1)
                    def _():
                        write(cur - 1, 1 - b).wait()
                    start_gather(cur + 1, 1 - b)

                gather(b).wait()
                write(cur, b).start()

        write(n - 2, 0).wait()
        write(n - 1, 1).wait()

    return pl.kernel(
        body, mesh=_sc_mesh(),
        out_type=jax.ShapeDtypeStruct((n_out, d), table.dtype),
        scratch_types=_sc_scratch(ch, d, table.dtype),
        name="sc_gather_rows",
    )(table, idx)


def _moe(h2p, cls3, rank3, cnt, wg, wu, wd, rw3):
    bsz, s, dh = h2p.shape
    t = bsz * s
    tm = FFN_TM
    counts = cnt[:N_CLASSES, 0].astype(I32)
    padded = ((counts + tm - 1) // tm) * tm
    upto = jnp.arange(N_CLASSES)[:, None] >= jnp.arange(N_CLASSES)[None, :]
    ends = jnp.sum(jnp.where(upto, padded[None, :], 0), axis=1)
    base = ends - padded
    cls2 = cls3[:, 0, :]
    pos2 = rank3[:, 0, :]
    for c in range(N_CLASSES):
        pos2 = pos2 + jnp.where(cls2 == c, base[c], 0)
    pos = pos2.reshape(t)
    ntiles = t // tm + N_CLASSES
    tile_start = jnp.arange(ntiles, dtype=I32) * tm
    tcls = jnp.minimum(jnp.sum((ends[None, :] <= tile_start[:, None]).astype(I32), axis=1), N_CLASSES - 1)
    nused = (ends[-1:] // tm).astype(I32)
    grp = tcls // len(PAIRS)
    pr = tcls % len(PAIRS)
    first = sum(jnp.where(pr == i, p[0], 0) for i, p in enumerate(PAIRS))
    second = sum(jnp.where(pr == i, p[1], 0) for i, p in enumerate(PAIRS))
    e1 = (grp * PER_GROUP + first).astype(I32)
    e2 = (grp * PER_GROUP + second).astype(I32)
    hs = _sc_scatter_rows(h2p.reshape(t, dh), pos, ntiles * tm)
    ys = _ffn(hs, e1, e2, nused, wg, wu, wd, rw3)
    return _sc_gather_rows(ys, pos).reshape(bsz, s, dh)


def _qkv1_kernel(x_ref, y_ref, gfp_ref, g_ref, sc_ref, sh_ref, w_ref, cos_ref, sin_ref,
                 wg_ref, wu_ref, wd_ref, x2_ref, *rest):
    outs = rest[:9]
    _side_cast((wg_ref, wu_ref, wd_ref), rest[9:12])
    hb_scr, ysc, ysc2 = rest[12:]
    tm = x_ref.shape[1]
    x2 = x_ref[0] + gfp_ref[0] * _unpack_bf16_pairs(y_ref[0])
    x2_ref[0] = x2
    hb_scr[...] = _rms_mod(x2, g_ref[...], sc_ref[0], sh_ref[0]).astype(BF16)
    cos = cos_ref[0]
    sin = sin_ref[0]
    for gi, (_, dil) in reversed(list(enumerate(B_GROUPS))):
        for j in range(3):
            c = gi * 3 + j
            y = jnp.dot(hb_scr[...], w_ref[:, B_WIDTH * c:B_WIDTH * (c + 1)], preferred_element_type=F32)
            out = outs[c]
            for cc in range(B_WIDTH // 128):
                lanes = slice(128 * cc, 128 * (cc + 1))
                r = y[:, lanes]
                if j < 2:
                    r = _rope_apply_paired(r, cos, sin)
                if j == 0:
                    r = r * Q_SCALE
                if dil == 1:
                    out[0, 0, :, lanes] = r.astype(BF16)
                    continue
                ysc[cc] = r
                if dil % 16:
                    for rr in range(dil):
                        out[0, rr, :, lanes] = ysc[cc, pl.ds(rr, tm // dil, stride=dil), :].astype(BF16)
                    continue
                quarter = tm // 4
                inner = dil // 4
                for r1 in range(4):
                    ysc2[cc, r1 * quarter:(r1 + 1) * quarter] = ysc[cc, pl.ds(r1, quarter, stride=4), :]
                for r1 in range(4):
                    for r2 in range(inner):
                        out[0, r1 + 4 * r2, :, lanes] = ysc2[
                            cc, pl.ds(r1 * quarter + r2, tm // dil, stride=inner), :].astype(BF16)


def _qkv1(x, y, gfp, g, sc, sh, w, cos, sin, expert_ws, layer):
    bsz, s, d = x.shape
    tm = TM
    grid = (bsz, s // tm)
    w_in, w_out, w_shape, w_args = _side_cast_specs([(w_, layer) for w_ in expert_ws], grid)
    tok = lambda w_: pl.BlockSpec((1, tm, w_), lambda b, i: (b, i, 0))
    per_b = pl.BlockSpec((1, 1, d), lambda b, i: (b, 0, 0))
    out_specs = [tok(d)]
    out_shape = [jax.ShapeDtypeStruct((bsz, s, d), F32)]
    for _, dil in B_GROUPS:
        for _ in range(3):
            out_specs.append(pl.BlockSpec((1, dil, tm // dil, B_WIDTH), lambda b, i: (b, 0, i, 0)))
            out_shape.append(jax.ShapeDtypeStruct((bsz, dil, s // dil, B_WIDTH), BF16))
    outs = pl.pallas_call(
        _qkv1_kernel,
        grid=grid,
        in_specs=[tok(d), tok(d // 2), per_b, pl.BlockSpec((1, d), lambda b, i: (0, 0)), per_b, per_b,
                  pl.BlockSpec(w.shape, lambda b, i: (0, 0)), tok(128), tok(128)] + w_in,
        out_specs=out_specs + w_out,
        out_shape=out_shape + w_shape,
        scratch_shapes=[pltpu.VMEM((tm, d), BF16), pltpu.VMEM((B_WIDTH // 128, tm, 128), F32),
                        pltpu.VMEM((B_WIDTH // 128, tm, 128), F32)],
        compiler_params=_cparams("parallel", "parallel"),
        name="qkv1",
    )(x, y, gfp, g.reshape(1, d), sc, sh, w, cos, sin, *w_args)
    return outs[:10], [c.reshape(w_.shape[1:]) for c, w_ in zip(outs[10:], expert_ws)]


def _att1_window_pad(lseg):
    qb = min(2 * B_RADIUS, lseg)
    return -(qb + 2 * B_RADIUS) % 128


def _att1_kernel(*refs, chunk, seq):
    ins = refs[:21]
    o_ref = refs[21]
    scr = refs[22:]
    nat_o, nat_d, nat_m = scr[0:3], scr[3:6], scr[6:9]
    n = pl.program_id(1)
    rad = B_RADIUS
    npair = B_WIDTH // 128

    def window(cur, prev, nxt, res, lo, hi, lseg, lanes):
        parts = []
        if lo < 0:
            parts.append(prev[0, res, :, lanes])
        parts.append(cur[0, res, max(lo, 0):min(hi, lseg), lanes])
        if hi > lseg:
            parts.append(nxt[0, res, :, lanes])
        if hi > lseg + rad:
            parts.append(nxt[0, res, :, lanes])
        return jnp.concatenate(parts, axis=0) if len(parts) > 1 else parts[0]

    for c in range(npair):
        lanes = slice(128 * c, 128 * (c + 1))
        for gi, (_, dil) in enumerate(B_GROUPS):
            q_ref, kc, kp, kn, vc, vp, vn = ins[7 * gi:7 * gi + 7]
            lseg = chunk // dil
            qb = min(2 * rad, lseg)
            kb_n = qb + 2 * rad + _att1_window_pad(lseg)
            lstr = seq // dil
            rel = lax.broadcasted_iota(I32, (qb, kb_n), 1) - lax.broadcasted_iota(I32, (qb, kb_n), 0)
            band_bias = jnp.where((rel >= 0) & (rel <= 2 * rad), 0.0, NEG)
            jrow = lax.broadcasted_iota(I32, (1, kb_n), 1)
            lane = lax.broadcasted_iota(I32, (qb, 128), 1)
            first_head = lane < HEAD_DIM
            q_first = (lane % HEAD_DIM) < (HEAD_DIM // 2)
            zero = jnp.zeros((qb, 128), BF16)
            ones = jnp.ones((kb_n, 128), BF16)
            for res in range(dil):
                for q0 in range(0, lseg, qb):
                    apos = n * lseg + q0 - rad + jrow
                    bias = band_bias + jnp.where((apos >= 0) & (apos < lstr), 0.0, NEG)
                    rows = pl.ds(q0, qb) if dil == 1 else pl.ds(res + dil * q0, qb, stride=dil)
                    qc = q_ref[0, res, q0:q0 + qb, lanes]
                    kb = window(kc, kp, kn, res, q0 - rad, q0 - rad + kb_n, lseg, lanes)
                    vb = jnp.concatenate(
                        [window(vc, vp, vn, res, q0 - rad, q0 - rad + kb_n, lseg, lanes), ones], axis=1)
                    qs = jnp.concatenate(
                        [jnp.where(q_first, qc, zero), jnp.where(q_first, zero, qc)], axis=0)
                    s2 = lax.dot_general(qs, kb, (((1,), (1,)), ((), ())), preferred_element_type=F32)
                    ps, ms = [], []
                    for t2 in range(2):
                        s = s2[qb * t2:qb * (t2 + 1)] + bias
                        m = jnp.max(s, axis=-1, keepdims=True)
                        ps.append(jnp.exp2((s - m).astype(BF16)))
                        ms.append(m)
                    pv = jnp.dot(jnp.concatenate(ps, axis=0), vb, preferred_element_type=F32)
                    nat_o[gi][rows, :] = jnp.where(first_head, pv[0:qb, 0:128], pv[qb:2 * qb, 0:128])
                    nat_d[gi][rows, :] = jnp.where(first_head, pv[0:qb, 128:256], pv[qb:2 * qb, 128:256])
                    nat_m[gi][rows, :] = jnp.where(first_head, ms[0], ms[1])

        ms = [nat_m[gi][...] for gi in range(3)]
        mx = jnp.maximum(jnp.maximum(ms[0], ms[1]), ms[2])
        w = [jnp.exp2(x - mx) for x in ms]
        num = w[0] * nat_o[0][...] + w[1] * nat_o[1][...] + w[2] * nat_o[2][...]
        den = w[0] * nat_d[0][...] + w[1] * nat_d[1][...] + w[2] * nat_d[2][...]
        o_ref[0, :, lanes] = (num / den).astype(BF16)


def _att1(qkv):
    bsz = qkv[0].shape[0]
    seq = qkv[0].shape[1] * qkv[0].shape[2]
    chunk = ATT1_CHUNK
    rad = B_RADIUS
    in_specs, args, scratch = [], [], []
    for gi, (_, dil) in enumerate(B_GROUPS):
        q, k, v = qkv[3 * gi:3 * gi + 3]
        lseg = chunk // dil
        per = lseg // rad
        last = seq // dil // rad - 1
        cur = pl.BlockSpec((1, dil, lseg, B_WIDTH), lambda b, n: (b, 0, n, 0))
        prev = pl.BlockSpec((1, dil, rad, B_WIDTH),
                            lambda b, n, per=per: (b, 0, jnp.maximum(n * per - 1, 0), 0))
        nxt = pl.BlockSpec((1, dil, rad, B_WIDTH),
                           lambda b, n, per=per, last=last: (b, 0, jnp.minimum(n * per + per, last), 0))
        in_specs += [cur, cur, prev, nxt, cur, prev, nxt]
        args += [q, k, k, k, v, v, v]
    for _ in range(3):
        for _ in B_GROUPS:
            scratch.append(pltpu.VMEM((chunk, 128), F32))
    return pl.pallas_call(
        functools.partial(_att1_kernel, chunk=chunk, seq=seq),
        grid=(bsz, seq // chunk),
        in_specs=in_specs,
        out_specs=pl.BlockSpec((1, chunk, B_WIDTH), lambda b, n: (b, n, 0)),
        out_shape=jax.ShapeDtypeStruct((bsz, seq, B_WIDTH), BF16),
        scratch_shapes=scratch,
        compiler_params=_cparams("parallel", "parallel"),
        name="att1",
    )(*args)


def _final_kernel(x_ref, y_ref, gfp_ref, g_ref, o_ref):
    x = x_ref[0] + gfp_ref[0] * _unpack_bf16_pairs(y_ref[0])
    ms = jnp.mean(x * x, axis=-1, keepdims=True)
    o_ref[0] = (x * lax.rsqrt(ms + NORM_EPS)) * g_ref[...]


def _final(x, y, gfp, g):
    bsz, s, d = x.shape
    tm = TM_WIDE
    tok = pl.BlockSpec((1, tm, d), lambda b, i: (b, i, 0))
    return pl.pallas_call(
        _final_kernel,
        grid=(bsz, s // tm),
        in_specs=[tok, pl.BlockSpec((1, tm, d // 2), lambda b, i: (b, i, 0)),
                  pl.BlockSpec((1, 1, d), lambda b, i: (b, 0, 0)),
                  pl.BlockSpec((1, d), lambda b, i: (0, 0))],
        out_specs=tok,
        out_shape=jax.ShapeDtypeStruct((bsz, s, d), F32),
        compiler_params=_cparams("parallel", "parallel"),
        name="final_norm",
    )(x, y, gfp, g.reshape(1, d))


def kernel(x, c, positions, ada_w, ada_b, norm_mix_g, norm_ffn_g, a_w_qkv, a_w_o, a_sink, b_w_qkv, b_w_o,
           router_w, router_bias, exp_w_gate, exp_w_up, exp_w_down, final_norm_g):
    bsz, s, d = x.shape
    assert d == D_MODEL and ada_w.shape[0] == 2 and s % ATT1_CHUNK == 0
    assert all(w_ // (2 * dil) == B_RADIUS for w_, dil in B_GROUPS)

    mod = _modulation(c, ada_w, ada_b)
    mods = [[mod[l][:, k * d:(k + 1) * d].reshape(bsz, 1, d) for k in range(6)] for l in range(2)]
    cos, sin = _rope_tables(positions)

    rw_t = router_w.T.astype(BF16)
    rows = jnp.array([PER_GROUP * g + j if g < N_EXPERT_GROUPS else 0
                      for j in range(PER_GROUP) for g in range(8)], I32)
    live = jnp.array([1.0 if g < N_EXPERT_GROUPS else 0.0
                      for j in range(PER_GROUP) for g in range(8)], F32)
    rw32 = (rw_t[rows].astype(F32) * live[:, None]).astype(BF16)
    rb32 = jnp.where(live > 0, router_bias.astype(F32)[rows], NEG).reshape(32, 1)
    rw3 = rw_t.reshape(N_EXPERTS, 1, d)
    expert_ws = (exp_w_gate, exp_w_up, exp_w_down)

    sh_m, sc_m, g_m, sh_f, sc_f, g_f = mods[0]
    nq, nkv = A_Q_HEADS * HEAD_DIM, A_KV_HEADS * HEAD_DIM
    w0 = a_w_qkv[0].astype(BF16)
    w0 = jnp.concatenate([_paired_layout(w0[:, :nq]), _paired_layout(w0[:, nq:nq + nkv]), w0[:, nq + nkv:]],
                         axis=1)
    q, k, v = _qkv0(x, norm_mix_g[0], sc_m, sh_m, w0, cos, sin)
    qk_chunks = tuple(c for c in range(b_w_qkv.shape[-1] // 128) if (c * 128 // B_WIDTH) % 3 < 2)
    o, cast = _att0(q, k, v, a_sink[0].astype(F32),
                    [(w_, 0) for w_ in expert_ws] + [(a_w_o, 0), (b_w_o, 0), (b_w_qkv, 0)],
                    ((), (), (), (), (), qk_chunks))
    experts, (a_wo, b_wo, b_wqkv) = cast[:3], cast[3:]
    x1, *routed = _post(o, a_wo, x, g_m, norm_ffn_g[0], sc_f, sh_f, rw32, rb32)
    y = _moe(*routed, *experts, rw3)
    g_f_prev = g_f

    sh_m, sc_m, g_m, sh_f, sc_f, g_f = mods[1]
    outs, experts = _qkv1(x1, y, g_f_prev, norm_mix_g[1], sc_m, sh_m, b_wqkv, cos, sin, expert_ws, 1)
    x2, qkv = outs[0], outs[1:]
    o = _att1(qkv)
    x3, *routed = _post(o, b_wo, x2, g_m, norm_ffn_g[1], sc_f, sh_f, rw32, rb32)
    y = _moe(*routed, *experts, rw3)

    return _final(x3, y, g_f, final_norm_g)
```

```python
import functools

import jax
import jax.numpy as jnp
from jax import lax
from jax.experimental import pallas as pl
from jax.experimental.pallas import tpu as pltpu
from jax.experimental.pallas import tpu_sc as plsc

F32, BF16, I32, U32 = jnp.float32, jnp.bfloat16, jnp.int32, jnp.uint32

D_MODEL = 1024
HEAD_DIM = 64
ROPE_THETA = 10000.0
NORM_EPS = 1e-6
A_Q_HEADS = 16
A_KV_HEADS = 4
A_GROUP = A_Q_HEADS // A_KV_HEADS
A_RADIUS = 128
B_GROUPS = ((128, 1), (512, 4), (2048, 16))
B_HEADS = 8
B_RADIUS = 64
B_WIDTH = B_HEADS * HEAD_DIM
N_EXPERTS = 16
N_EXPERT_GROUPS = 4
PER_GROUP = N_EXPERTS // N_EXPERT_GROUPS
PAIRS = ((0, 1), (0, 2), (0, 3), (1, 2), (1, 3), (2, 3))
N_CLASSES = N_EXPERT_GROUPS * len(PAIRS)
NEG = -1e30
LOG2E = 1.4426950408889634
Q_SCALE = HEAD_DIM ** -0.5 * LOG2E

VMEM_LIMIT = 56 * 1024 * 1024
TM = 512
ROPE_TS = 2048
TM_WIDE = 1024
RANK_BLOCK = 512
ATT1_CHUNK = 1024
FFN_TM = 256
FFN_TILES_PER_STEP = 2
SC_CORES_V7X = 2
SC_SUBCORES_V7X = 16
SC_ROWS_PER_STEP = 64


def _cparams(*sem):
    return pltpu.CompilerParams(dimension_semantics=sem, vmem_limit_bytes=VMEM_LIMIT)


def _rms_mod(x, g, sc, sh):
    ms = jnp.mean(x * x, axis=-1, keepdims=True)
    return (x * lax.rsqrt(ms + NORM_EPS)) * (g * (1.0 + sc)) + sh


def _rope_apply_paired(y, cos, sin_paired):
    return y * cos + pltpu.roll(y, HEAD_DIM, 1) * sin_paired


def _paired_layout(w):
    k, n = w.shape
    half = HEAD_DIM // 2
    return w.reshape(k, n // 128, 2, 2, half).transpose(0, 1, 3, 2, 4).reshape(k, n)


def _silu(x):
    half = 0.5 * x
    return half + half * jnp.tanh(half)


def _pack_bf16_pairs(x):
    n = x.shape[1] // 2
    x = x.astype(BF16).astype(F32)
    lo = lax.bitcast_convert_type(x[:, :n], U32)
    hi = lax.bitcast_convert_type(x[:, n:], U32)
    return (hi & jnp.uint32(0xFFFF0000)) | (lo >> 16)


def _unpack_bf16_pairs(p):
    lo = lax.bitcast_convert_type(p << 16, F32)
    hi = lax.bitcast_convert_type(p & jnp.uint32(0xFFFF0000), F32)
    return jnp.concatenate([lo, hi], axis=1)


def _dup_head(x, first_head, which):
    rolled = pltpu.roll(x, HEAD_DIM, 1)
    return jnp.where(first_head, x, rolled) if which == 0 else jnp.where(first_head, rolled, x)


def _mod_kernel(ct_ref, w_ref, b_ref, o_ref, *, bsz):
    ct = ct_ref[...]
    s = ct * jax.nn.sigmoid(ct)
    w = w_ref[0]
    rows = [jnp.sum(w * s[:, b:b + 1], axis=0, keepdims=True) + b_ref[0] for b in range(bsz)]
    o_ref[0] = jnp.concatenate(rows + [jnp.zeros_like(rows[0])] * (8 - bsz), axis=0)


def _modulation(c, ada_w, ada_b):
    depth, d, n = ada_w.shape
    bsz = c.shape[0]
    assert bsz <= 8
    tn = 1536
    out = pl.pallas_call(
        functools.partial(_mod_kernel, bsz=bsz),
        grid=(depth, n // tn),
        in_specs=[pl.BlockSpec((d, bsz), lambda l, j: (0, 0)),
                  pl.BlockSpec((1, d, tn), lambda l, j: (l, 0, j)),
                  pl.BlockSpec((1, 1, tn), lambda l, j: (l, 0, j))],
        out_specs=pl.BlockSpec((1, 8, tn), lambda l, j: (l, 0, j)),
        out_shape=jax.ShapeDtypeStruct((depth, 8, n), F32),
        compiler_params=_cparams("arbitrary", "arbitrary"),
        name="modulation",
    )(c.T, ada_w, ada_b.reshape(depth, 1, n))
    return out[:, :bsz]


def _rope_kernel(pos_ref, invf_ref, sign_ref, cos_ref, sin_ref):
    p = pos_ref[0].astype(F32)
    n = p.shape[0]
    quarter = lax.broadcasted_iota(I32, (n, 128), 1) // (HEAD_DIM // 2)
    pos4 = jnp.where(quarter == 0, p[:, 0:1],
                     jnp.where(quarter == 1, p[:, 1:2], jnp.where(quarter == 2, p[:, 2:3], p[:, 3:4])))
    ang = pos4 * invf_ref[...]
    dense = (jnp.cos(ang), jnp.sin(ang))
    for u in range(4):
        spread = []
        for d in dense:
            e = jnp.where(quarter == u, d, 0.0)
            e = e + pltpu.roll(e, HEAD_DIM // 2, 1)
            spread.append(e + pltpu.roll(e, HEAD_DIM, 1))
        rows = pl.ds(u, n, stride=4)
        cos_ref[0, rows, :] = spread[0]
        sin_ref[0, rows, :] = spread[1] * sign_ref[...]


def _rope_tables(positions):
    bsz, s = positions.shape
    inv_freq = ROPE_THETA ** (-jnp.arange(0, HEAD_DIM, 2, dtype=F32) / HEAD_DIM)
    invf = jnp.tile(inv_freq, 4).reshape(1, 128)
    sign = jnp.concatenate([-jnp.ones((HEAD_DIM,), F32), jnp.ones((HEAD_DIM,), F32)]).reshape(1, 128)
    ts = ROPE_TS
    return pl.pallas_call(
        _rope_kernel,
        grid=(bsz, s // ts),
        in_specs=[pl.BlockSpec((1, ts // 4, 4), lambda b, i: (b, i, 0)),
                  pl.BlockSpec((1, 128), lambda b, i: (0, 0)),
                  pl.BlockSpec((1, 128), lambda b, i: (0, 0))],
        out_specs=[pl.BlockSpec((1, ts, 128), lambda b, i: (b, i, 0))] * 2,
        out_shape=[jax.ShapeDtypeStruct((bsz, s, 128), F32)] * 2,
        compiler_params=_cparams("parallel", "parallel"),
        name="rope_tables",
    )(positions.reshape(bsz, s // 4, 4), invf, sign)


def _qkv0_kernel(x_ref, g_ref, sc_ref, sh_ref, w_ref, cos_ref, sin_ref,
                 q_ref, k_ref, v_ref, hb_scr):
    hb_scr[...] = _rms_mod(x_ref[0], g_ref[...], sc_ref[0], sh_ref[0]).astype(BF16)
    cos = cos_ref[0]
    sin = sin_ref[0]
    lane = lax.broadcasted_iota(I32, cos.shape, 1)
    x_lo = (lane % HEAD_DIM) < (HEAD_DIM // 2)
    nq = A_Q_HEADS * HEAD_DIM
    nkv = A_KV_HEADS * HEAD_DIM
    for c in range(nq // 256):
        y = jnp.dot(hb_scr[...], w_ref[:, 256 * c:256 * (c + 1)], preferred_element_type=F32)
        for cc in range(2):
            r = _rope_apply_paired(y[:, 128 * cc:128 * (cc + 1)], cos, sin) * Q_SCALE
            q_ref[0, :, 256 * c + 128 * cc:256 * c + 128 * (cc + 1)] = r.astype(BF16)
    first_head = lane < HEAD_DIM
    y = jnp.dot(hb_scr[...], w_ref[:, nq:nq + nkv], preferred_element_type=F32)
    for cc in range(nkv // 128):
        r = _rope_apply_paired(y[:, 128 * cc:128 * (cc + 1)], cos, sin)
        k_ref[0, :, 256 * cc:256 * cc + 128] = jnp.where(
            x_lo, r, pltpu.roll(r, HEAD_DIM // 2, 1)).astype(BF16)
        k_ref[0, :, 256 * cc + 128:256 * cc + 256] = jnp.where(
            x_lo, pltpu.roll(r, 128 - HEAD_DIM // 2, 1), r).astype(BF16)
    y = jnp.dot(hb_scr[...], w_ref[:, nq + nkv:nq + 2 * nkv], preferred_element_type=F32)
    for cc in range(nkv // 128):
        r = y[:, 128 * cc:128 * (cc + 1)]
        for which in range(2):
            g = 2 * cc + which
            v_ref[0, :, 128 * g:128 * (g + 1)] = _dup_head(r, first_head, which).astype(BF16)


def _qkv0(x, g, sc, sh, w, cos, sin):
    bsz, s, d = x.shape
    nq = A_Q_HEADS * HEAD_DIM
    nkv = A_KV_HEADS * HEAD_DIM
    tm = TM_WIDE
    tok = lambda w_: pl.BlockSpec((1, tm, w_), lambda b, i: (b, i, 0))
    per_b = pl.BlockSpec((1, 1, d), lambda b, i: (b, 0, 0))
    return pl.pallas_call(
        _qkv0_kernel,
        grid=(bsz, s // tm),
        in_specs=[tok(d), pl.BlockSpec((1, d), lambda b, i: (0, 0)), per_b, per_b,
                  pl.BlockSpec(w.shape, lambda b, i: (0, 0)), tok(128), tok(128)],
        out_specs=[tok(nq), tok(2 * nkv), tok(2 * nkv)],
        out_shape=[jax.ShapeDtypeStruct((bsz, s, nq), BF16),
                   jax.ShapeDtypeStruct((bsz, s, 2 * nkv), BF16),
                   jax.ShapeDtypeStruct((bsz, s, 2 * nkv), BF16)],
        scratch_shapes=[pltpu.VMEM((tm, d), BF16)],
        compiler_params=_cparams("parallel", "parallel"),
        name="qkv0",
    )(x, g.reshape(1, d), sc, sh, w, cos, sin)


def _side_cast_specs(items, grid):
    nsteps = grid[0] * grid[1]
    in_specs, out_specs, out_shape, args = [], [], [], []
    for w, layer in items:
        n = w.shape[-1]
        per_layer = w[0].size // n
        rows = per_layer // nsteps
        assert rows * nsteps == per_layer and rows % 16 == 0
        in_specs.append(pl.BlockSpec(
            (rows, n), lambda b, i, layer=layer: (layer * nsteps + b * grid[1] + i, 0)))
        out_specs.append(pl.BlockSpec((rows, n), lambda b, i: (b * grid[1] + i, 0)))
        out_shape.append(jax.ShapeDtypeStruct((per_layer, n), BF16))
        args.append(w.reshape(w.shape[0] * per_layer, n))
    return in_specs, out_specs, out_shape, args


def _side_cast(srcs, dsts, paired_chunks=None):
    for n, (src, dst) in enumerate(zip(srcs, dsts)):
        chunks = paired_chunks[n] if paired_chunks else ()
        if not chunks:
            dst[...] = src[...].astype(BF16)
            continue
        quarter = lax.broadcasted_iota(I32, (src.shape[0], 128), 1) // (HEAD_DIM // 2)
        for c in range(src.shape[1] // 128):
            x = src[:, 128 * c:128 * (c + 1)]
            if c in chunks:
                x = jnp.where(quarter == 1, pltpu.roll(x, 128 - HEAD_DIM // 2, 1),
                              jnp.where(quarter == 2, pltpu.roll(x, HEAD_DIM // 2, 1), x))
            dst[:, 128 * c:128 * (c + 1)] = x.astype(BF16)


def _att0_kernel(sink_ref, q_ref, kc_ref, kp_ref, kn_ref, vc_ref, vp_ref, vn_ref, *rest, tq, seq, paired_chunks):
    ncast = len(paired_chunks)
    o_ref = rest[ncast]
    kf, vf = rest[2 * ncast + 1:]
    _side_cast(rest[:ncast], rest[ncast + 1:2 * ncast + 1], paired_chunks)
    i = pl.program_id(1)
    r = A_RADIUS
    ones = jnp.ones((tq + 2 * r, 128), BF16)
    for g in range(A_KV_HEADS):
        lanes = slice(128 * g, 128 * (g + 1))
        kf[g, 0:r] = kp_ref[0, :, lanes]
        kf[g, r:r + tq] = kc_ref[0, :, lanes]
        kf[g, r + tq:r + tq + r] = kn_ref[0, :, lanes]
        vf[g, 0:r, 0:128] = vp_ref[0, :, lanes]
        vf[g, r:r + tq, 0:128] = vc_ref[0, :, lanes]
        vf[g, r + tq:r + tq + r, 0:128] = vn_ref[0, :, lanes]
        vf[g, :, 128:256] = ones

    ii = lax.broadcasted_iota(I32, (r, r), 0)
    jj = lax.broadcasted_iota(I32, (r, r), 1)
    first_head = jj < HEAD_DIM
    q_first = (jj % HEAD_DIM) < (HEAD_DIM // 2)
    zero = jnp.zeros((r, 128), BF16)

    for j in range(tq // r):
        q0 = j * r
        base = i * tq + q0
        bias_lo = jnp.where((jj >= ii) & (base - r + jj >= 0), 0.0, NEG)
        bias_hi = jnp.where((jj <= ii) & (base + r + jj < seq), 0.0, NEG)
        for g in range(A_KV_HEADS):
            kb = kf[g, pl.ds(q0, 3 * r), :]
            vb = vf[g, pl.ds(q0, 3 * r), :]
            qs = []
            for cc in range(2):
                c = 2 * g + cc
                qc = q_ref[0, pl.ds(q0, r), 128 * c:128 * (c + 1)]
                qs += [jnp.where(q_first, qc, zero), jnp.where(q_first, zero, qc)]
            s4 = lax.dot_general(jnp.concatenate(qs, axis=0), kb, (((1,), (1,)), ((), ())),
                                 preferred_element_type=F32)
            ps, ms = [], []
            for t in range(A_GROUP):
                s = s4[r * t:r * (t + 1)]
                a0 = s[:, 0:r] + bias_lo
                a1 = s[:, r:2 * r]
                a2 = s[:, 2 * r:3 * r] + bias_hi
                sink = sink_ref[A_GROUP * g + t] * LOG2E
                m = jnp.max(jnp.maximum(jnp.maximum(a0, a1), a2), axis=-1, keepdims=True)
                m = jnp.maximum(m, sink)
                ps.append(jnp.exp2(jnp.concatenate([a0 - m, a1 - m, a2 - m], axis=1).astype(BF16)))
                ms.append(m)
            pv = jnp.dot(jnp.concatenate(ps, axis=0), vb, preferred_element_type=F32)
            os_ = []
            for t in range(A_GROUP):
                sink = sink_ref[A_GROUP * g + t] * LOG2E
                den = pv[r * t:r * (t + 1), 128:256] + jnp.exp2(sink - ms[t])
                os_.append(pv[r * t:r * (t + 1), 0:128] / den)
            for cc in range(2):
                c = 2 * g + cc
                o_ref[0, pl.ds(q0, r), 128 * c:128 * (c + 1)] = jnp.where(
                    first_head, os_[2 * cc], os_[2 * cc + 1]).astype(BF16)


def _att0(q, k2, v2, sink, cast_items, paired_chunks):
    bsz, s, nq = q.shape
    nkv2 = k2.shape[-1]
    tq = TM_WIDE
    r = A_RADIUS
    per = tq // r
    last = s // r - 1
    grid = (bsz, s // tq)
    cur = lambda w_: pl.BlockSpec((1, tq, w_), lambda b, i: (b, i, 0))
    prev = pl.BlockSpec((1, r, nkv2), lambda b, i: (b, jnp.maximum(i * per - 1, 0), 0))
    nxt = pl.BlockSpec((1, r, nkv2), lambda b, i: (b, jnp.minimum(i * per + per, last), 0))
    w_in, w_out, w_shape, w_args = _side_cast_specs(cast_items, grid)
    o, *cast = pl.pallas_call(
        functools.partial(_att0_kernel, tq=tq, seq=s, paired_chunks=paired_chunks),
        grid=grid,
        in_specs=[pl.BlockSpec(memory_space=pltpu.SMEM),
                  cur(nq), cur(nkv2), prev, nxt, cur(nkv2), prev, nxt] + w_in,
        out_specs=[cur(nq)] + w_out,
        out_shape=[jax.ShapeDtypeStruct((bsz, s, nq), BF16)] + w_shape,
        scratch_shapes=[pltpu.VMEM((A_KV_HEADS, tq + 2 * r, 128), BF16),
                        pltpu.VMEM((A_KV_HEADS, tq + 2 * r, 256), BF16)],
        compiler_params=_cparams("parallel", "parallel"),
        name="att0",
    )(sink, q, k2, k2, k2, v2, v2, v2, *w_args)
    return o, [c.reshape(w.shape[1:]) for c, (w, _) in zip(cast, cast_items)]


def _post_kernel(o_ref, wo_ref, x_ref, gm_ref, gf_ref, scf_ref, shf_ref, rw_ref, rb_ref, upper_ref,
                 x1_ref, h2_ref, cls_ref, rank_ref, cnt_ref, carry):
    mix = jnp.dot(o_ref[0], wo_ref[...], preferred_element_type=F32)
    x1 = x_ref[0] + gm_ref[0] * mix
    x1_ref[0] = x1
    h2 = _rms_mod(x1, gf_ref[...], scf_ref[0], shf_ref[0]).astype(BF16)
    h2_ref[0] = _pack_bf16_pairs(h2)
    tq = h2.shape[0]
    logits = lax.dot_general(rw_ref[...], h2, (((1,), (1,)), ((), ())), preferred_element_type=F32)
    scores = jax.nn.sigmoid(logits)
    biased = scores + rb_ref[...]
    pj = [biased[8 * j:8 * (j + 1)] for j in range(PER_GROUP)]
    sel = []
    for j in range(PER_GROUP):
        beaten = jnp.zeros((8, tq), F32)
        for j2 in range(PER_GROUP):
            if j2 == j:
                continue
            beats = (pj[j2] > pj[j]) | (pj[j2] == pj[j]) if j2 < j else (pj[j2] > pj[j])
            beaten = beaten + jnp.where(beats, 1.0, 0.0)
        sel.append(beaten < 2.0)
    gscore = jnp.zeros((8, tq), F32)
    for j in range(PER_GROUP):
        gscore = gscore + jnp.where(sel[j], pj[j], 0.0)
    gi = lax.broadcasted_iota(I32, (8, tq), 0).astype(F32)
    gmax = jnp.max(gscore, axis=0, keepdims=True)
    gidx = jnp.min(jnp.where(gscore == gmax, gi, 8.0), axis=0, keepdims=True)
    onehot = gi == gidx
    f = [jnp.max(jnp.where(onehot & sel[j], 1.0, 0.0), axis=0, keepdims=True) > 0.5
         for j in range(PER_GROUP)]
    pair = jnp.where(f[0], jnp.where(f[1], 0.0, jnp.where(f[2], 1.0, 2.0)),
                     jnp.where(f[1], jnp.where(f[2], 3.0, 4.0), 5.0))
    cls = (gidx * float(len(PAIRS)) + pair).astype(I32)
    cls_ref[0] = cls

    @pl.when((pl.program_id(0) == 0) & (pl.program_id(1) == 0))
    def _():
        carry[...] = jnp.zeros_like(carry)

    c = carry[...]
    ranks = []
    for blk in range(tq // RANK_BLOCK):
        cls_b = cls[:, RANK_BLOCK * blk:RANK_BLOCK * (blk + 1)]
        oh = lax.broadcasted_iota(I32, (32, RANK_BLOCK), 0) == cls_b
        ohf = jnp.where(oh, 1.0, 0.0)
        within = jnp.dot(ohf.astype(BF16), upper_ref[...], preferred_element_type=F32)
        ranks.append(jnp.sum(jnp.where(oh, within + c[:, 0:1], 0.0), axis=0, keepdims=True))
        c = c + jnp.sum(ohf, axis=1, keepdims=True)
    rank_ref[0] = jnp.concatenate(ranks, axis=1).astype(I32)
    carry[...] = c
    cnt_ref[...] = c


def _post(o, wo, x, gm, gf, scf, shf, rw32, rb32):
    bsz, s, d = x.shape
    do = o.shape[-1]
    tq = TM_WIDE
    nt = s // tq
    tok = lambda w_: pl.BlockSpec((1, tq, w_), lambda b, i: (b, i, 0))
    per_b = pl.BlockSpec((1, 1, d), lambda b, i: (b, 0, 0))
    full = lambda a: pl.BlockSpec(a.shape, lambda b, i: (0,) * a.ndim)
    per_tile = pl.BlockSpec((1, 1, tq), lambda b, i: (b * nt + i, 0, 0))
    gf2 = gf.reshape(1, d)
    upper = (jnp.arange(RANK_BLOCK)[:, None] < jnp.arange(RANK_BLOCK)[None, :]).astype(BF16)
    return pl.pallas_call(
        _post_kernel,
        grid=(bsz, nt),
        in_specs=[tok(do), full(wo), tok(d), per_b, full(gf2), per_b, per_b, full(rw32), full(rb32),
                  full(upper)],
        out_specs=[tok(d), tok(d // 2), per_tile, per_tile, pl.BlockSpec((32, 128), lambda b, i: (0, 0))],
        out_shape=[jax.ShapeDtypeStruct((bsz, s, d), F32),
                   jax.ShapeDtypeStruct((bsz, s, d // 2), U32),
                   jax.ShapeDtypeStruct((bsz * nt, 1, tq), I32),
                   jax.ShapeDtypeStruct((bsz * nt, 1, tq), I32),
                   jax.ShapeDtypeStruct((32, 128), F32)],
        scratch_shapes=[pltpu.VMEM((32, 128), F32)],
        compiler_params=_cparams("arbitrary", "arbitrary"),
        name="post_attention",
    )(o, wo, x, gm, gf2, scf, shf, rw32, rb32, upper)


def _ffn_kernel(e1_ref, e2_ref, nused_ref, hs_ref, *refs):
    ys_ref = refs[-1]
    j = pl.program_id(0)
    tm = FFN_TM

    @pl.when(j * FFN_TILES_PER_STEP < nused_ref[0])
    def _():
        for sub in range(FFN_TILES_PER_STEP):
            wg1, wu1, wd1, rw1, wg2, wu2, wd2, rw2 = refs[8 * sub:8 * (sub + 1)]
            rows = slice(tm * sub, tm * (sub + 1))
            xf = _unpack_bf16_pairs(hs_ref[rows])
            x = xf.astype(BF16)
            gt1 = jnp.dot(x, wg1[0], preferred_element_type=F32)
            up1 = jnp.dot(x, wu1[0], preferred_element_type=F32)
            gt2 = jnp.dot(x, wg2[0], preferred_element_type=F32)
            up2 = jnp.dot(x, wu2[0], preferred_element_type=F32)
            a1 = (_silu(gt1) * up1).astype(BF16)
            y1 = jnp.dot(a1, wd1[0], preferred_element_type=F32)
            a2 = (_silu(gt2) * up2).astype(BF16)
            y2 = jnp.dot(a2, wd2[0], preferred_element_type=F32)
            s1 = jax.nn.sigmoid(jnp.sum(xf * rw1[0].astype(F32), axis=-1, keepdims=True))
            s2 = jax.nn.sigmoid(jnp.sum(xf * rw2[0].astype(F32), axis=-1, keepdims=True))
            den = s1 + s2
            ys_ref[rows] = _pack_bf16_pairs((s1 / den) * y1 + (s2 / den) * y2)

    @pl.when(j * FFN_TILES_PER_STEP >= nused_ref[0])
    def _():
        ys_ref[...] = jnp.zeros_like(ys_ref)


def _ffn(hs, e1, e2, nused, wg, wu, wd, rw3):
    npad, dh = hs.shape
    d = 2 * dh
    per = FFN_TILES_PER_STEP
    rows = per * FFN_TM
    de = wg.shape[-1]
    assert npad % rows == 0
    row = pl.BlockSpec((rows, dh), lambda j, e1, e2, nu: (jnp.minimum(j, (nu[0] - 1) // per), 0))
    w_specs = []
    for sub in range(per):
        for which in range(2):
            pick = lambda j, e1, e2, nu, sub=sub, which=which: ((e2 if which else e1)[per * j + sub], 0, 0)
            w_specs += [pl.BlockSpec((1, d, de), pick), pl.BlockSpec((1, d, de), pick),
                        pl.BlockSpec((1, de, d), pick), pl.BlockSpec((1, 1, d), pick)]
    grid_spec = pltpu.PrefetchScalarGridSpec(
        num_scalar_prefetch=3,
        grid=(npad // rows,),
        in_specs=[row] + w_specs,
        out_specs=pl.BlockSpec((rows, dh), lambda j, e1, e2, nu: (j, 0)),
    )
    return pl.pallas_call(
        _ffn_kernel,
        grid_spec=grid_spec,
        out_shape=jax.ShapeDtypeStruct((npad, dh), U32),
        compiler_params=_cparams("arbitrary"),
        name="expert_ffn",
    )(e1, e2, nused, hs, *([wg, wu, wd, rw3] * (2 * per)))


def _sc_mesh():
    return plsc.VectorSubcoreMesh(core_axis_name="c", subcore_axis_name="s",
                                  num_cores=SC_CORES_V7X, num_subcores=SC_SUBCORES_V7X)


def _sc_steps(n_rows):
    workers = SC_CORES_V7X * SC_SUBCORES_V7X
    per_w = n_rows // workers
    ch = min(SC_ROWS_PER_STEP, per_w // 2)
    n = per_w // ch
    assert per_w * workers == n_rows and n * ch == per_w and n % 2 == 0 and ch % 8 == 0
    return per_w, ch, n


def _sc_scratch(ch, d, dtype):
    return [pltpu.VMEM((2, ch), I32), pltpu.VMEM((2, ch, d), dtype),
            pltpu.SemaphoreType.DMA((2,)), pltpu.SemaphoreType.DMA((2,))]


def _sc_scatter_rows(src, idx, n_out):
    n_in, d = src.shape
    per_w, ch, n = _sc_steps(n_in)

    def body(src_hbm, idx_hbm, out_hbm, idx_v, rows_v, rsem, ssem):
        base = (lax.axis_index("s") * SC_CORES_V7X + lax.axis_index("c")) * per_w

        def read(i, b):
            return pltpu.make_async_copy(src_hbm.at[pl.ds(base + i * ch, ch)], rows_v.at[b], rsem.at[b])

        def scatter(b):
            return pltpu.make_async_copy(rows_v.at[b], out_hbm.at[idx_v.at[b]], ssem.at[b])

        def start_read(i, b):
            pltpu.sync_copy(idx_hbm.at[pl.ds(base + i * ch, ch)], idx_v.at[b])
            read(i, b).start()

        start_read(0, 0)

        @pl.loop(0, n, step=2)
        def _(i):
            for b in (0, 1):
                cur = i + b

                @pl.when(cur + 1 < n)
                def _():
                    @pl.when(cur >= 1)
                    def _():
                        scatter(1 - b).wait()
                    start_read(cur + 1, 1 - b)

                read(cur, b).wait()
                scatter(b).start()

        scatter(0).wait()
        scatter(1).wait()

    return pl.kernel(
        body, mesh=_sc_mesh(),
        out_type=jax.ShapeDtypeStruct((n_out, d), src.dtype),
        scratch_types=_sc_scratch(ch, d, src.dtype),
        name="sc_scatter_rows",
    )(src, idx)


def _sc_gather_rows(table, idx):
    n_out = idx.shape[0]
    d = table.shape[1]
    per_w, ch, n = _sc_steps(n_out)

    def body(table_hbm, idx_hbm, out_hbm, idx_v, rows_v, gsem, wsem):
        base = (lax.axis_index("s") * SC_CORES_V7X + lax.axis_index("c")) * per_w

        def gather(b):
            return pltpu.make_async_copy(table_hbm.at[idx_v.at[b]], rows_v.at[b], gsem.at[b])

        def write(i, b):
            return pltpu.make_async_copy(rows_v.at[b], out_hbm.at[pl.ds(base + i * ch, ch)], wsem.at[b])

        def start_gather(i, b):
            pltpu.sync_copy(idx_hbm.at[pl.ds(base + i * ch, ch)], idx_v.at[b])
            gather(b).start()

        start_gather(0, 0)

        @pl.loop(0, n, step=2)
        def _(i):
            for b in (0, 1):
                cur = i + b

                @pl.when(cur + 1 < n)
                def _():
                    @pl.when(cur >= 1)
                    def _():
                        write(cur - 1, 1 - b).wait()
                    start_gather(cur + 1, 1 - b)

                gather(b).wait()
                write(cur, b).start()

        write(n - 2, 0).wait()
        write(n - 1, 1).wait()

    return pl.kernel(
        body, mesh=_sc_mesh(),
        out_type=jax.ShapeDtypeStruct((n_out, d), table.dtype),
        scratch_types=_sc_scratch(ch, d, table.dtype),
        name="sc_gather_rows",
    )(table, idx)


def _moe(h2p, cls3, rank3, cnt, wg, wu, wd, rw3):
    bsz, s, dh = h2p.shape
    t = bsz * s
    tm = FFN_TM
    counts = cnt[:N_CLASSES, 0].astype(I32)
    padded = ((counts + tm - 1) // tm) * tm
    upto = jnp.arange(N_CLASSES)[:, None] >= jnp.arange(N_CLASSES)[None, :]
    ends = jnp.sum(jnp.where(upto, padded[None, :], 0), axis=1)
    base = ends - padded
    cls2 = cls3[:, 0, :]
    pos2 = rank3[:, 0, :]
    for c in range(N_CLASSES):
        pos2 = pos2 + jnp.where(cls2 == c, base[c], 0)
    pos = pos2.reshape(t)
    ntiles = t // tm + N_CLASSES
    tile_start = jnp.arange(ntiles, dtype=I32) * tm
    tcls = jnp.minimum(jnp.sum((ends[None, :] <= tile_start[:, None]).astype(I32), axis=1), N_CLASSES - 1)
    nused = (ends[-1:] // tm).astype(I32)
    grp = tcls // len(PAIRS)
    pr = tcls % len(PAIRS)
    first = sum(jnp.where(pr == i, p[0], 0) for i, p in enumerate(PAIRS))
    second = sum(jnp.where(pr == i, p[1], 0) for i, p in enumerate(PAIRS))
    e1 = (grp * PER_GROUP + first).astype(I32)
    e2 = (grp * PER_GROUP + second).astype(I32)
    hs = _sc_scatter_rows(h2p.reshape(t, dh), pos, ntiles * tm)
    ys = _ffn(hs, e1, e2, nused, wg, wu, wd, rw3)
    return _sc_gather_rows(ys, pos).reshape(bsz, s, dh)


def _qkv1_kernel(x_ref, y_ref, gfp_ref, g_ref, sc_ref, sh_ref, w_ref, cos_ref, sin_ref,
                 wg_ref, wu_ref, wd_ref, x2_ref, *rest):
    outs = rest[:9]
    _side_cast((wg_ref, wu_ref, wd_ref), rest[9:12])
    hb_scr, ysc, ysc2 = rest[12:]
    tm = x_ref.shape[1]
    x2 = x_ref[0] + gfp_ref[0] * _unpack_bf16_pairs(y_ref[0])
    x2_ref[0] = x2
    hb_scr[...] = _rms_mod(x2, g_ref[...], sc_ref[0], sh_ref[0]).astype(BF16)
    cos = cos_ref[0]
    sin = sin_ref[0]
    for gi, (_, dil) in reversed(list(enumerate(B_GROUPS))):
        for j in range(3):
            c = gi * 3 + j
            y = jnp.dot(hb_scr[...], w_ref[:, B_WIDTH * c:B_WIDTH * (c + 1)], preferred_element_type=F32)
            out = outs[c]
            for cc in range(B_WIDTH // 128):
                lanes = slice(128 * cc, 128 * (cc + 1))
                r = y[:, lanes]
                if j < 2:
                    r = _rope_apply_paired(r, cos, sin)
                if j == 0:
                    r = r * Q_SCALE
                if dil == 1:
                    out[0, 0, :, lanes] = r.astype(BF16)
                    continue
                ysc[cc] = r
                if dil % 16:
                    for rr in range(dil):
                        out[0, rr, :, lanes] = ysc[cc, pl.ds(rr, tm // dil, stride=dil), :].astype(BF16)
                    continue
                quarter = tm // 4
                inner = dil // 4
                for r1 in range(4):
                    ysc2[cc, r1 * quarter:(r1 + 1) * quarter] = ysc[cc, pl.ds(r1, quarter, stride=4), :]
                for r1 in range(4):
                    for r2 in range(inner):
                        out[0, r1 + 4 * r2, :, lanes] = ysc2[
                            cc, pl.ds(r1 * quarter + r2, tm // dil, stride=inner), :].astype(BF16)


def _qkv1(x, y, gfp, g, sc, sh, w, cos, sin, expert_ws, layer):
    bsz, s, d = x.shape
    tm = TM
    grid = (bsz, s // tm)
    w_in, w_out, w_shape, w_args = _side_cast_specs([(w_, layer) for w_ in expert_ws], grid)
    tok = lambda w_: pl.BlockSpec((1, tm, w_), lambda b, i: (b, i, 0))
    per_b = pl.BlockSpec((1, 1, d), lambda b, i: (b, 0, 0))
    out_specs = [tok(d)]
    out_shape = [jax.ShapeDtypeStruct((bsz, s, d), F32)]
    for _, dil in B_GROUPS:
        for _ in range(3):
            out_specs.append(pl.BlockSpec((1, dil, tm // dil, B_WIDTH), lambda b, i: (b, 0, i, 0)))
            out_shape.append(jax.ShapeDtypeStruct((bsz, dil, s // dil, B_WIDTH), BF16))
    outs = pl.pallas_call(
        _qkv1_kernel,
        grid=grid,
        in_specs=[tok(d), tok(d // 2), per_b, pl.BlockSpec((1, d), lambda b, i: (0, 0)), per_b, per_b,
                  pl.BlockSpec(w.shape, lambda b, i: (0, 0)), tok(128), tok(128)] + w_in,
        out_specs=out_specs + w_out,
        out_shape=out_shape + w_shape,
        scratch_shapes=[pltpu.VMEM((tm, d), BF16), pltpu.VMEM((B_WIDTH // 128, tm, 128), F32),
                        pltpu.VMEM((B_WIDTH // 128, tm, 128), F32)],
        compiler_params=_cparams("parallel", "parallel"),
        name="qkv1",
    )(x, y, gfp, g.reshape(1, d), sc, sh, w, cos, sin, *w_args)
    return outs[:10], [c.reshape(w_.shape[1:]) for c, w_ in zip(outs[10:], expert_ws)]


def _att1_window_pad(lseg):
    qb = min(2 * B_RADIUS, lseg)
    return -(qb + 2 * B_RADIUS) % 128


def _att1_kernel(*refs, chunk, seq):
    ins = refs[:21]
    o_ref = refs[21]
    scr = refs[22:]
    nat_o, nat_d, nat_m = scr[0:3], scr[3:6], scr[6:9]
    n = pl.program_id(1)
    rad = B_RADIUS
    npair = B_WIDTH // 128

    def window(cur, prev, nxt, res, lo, hi, lseg, lanes):
        parts = []
        if lo < 0:
            parts.append(prev[0, res, :, lanes])
        parts.append(cur[0, res, max(lo, 0):min(hi, lseg), lanes])
        if hi > lseg:
            parts.append(nxt[0, res, :, lanes])
        if hi > lseg + rad:
            parts.append(nxt[0, res, :, lanes])
        return jnp.concatenate(parts, axis=0) if len(parts) > 1 else parts[0]

    for c in range(npair):
        lanes = slice(128 * c, 128 * (c + 1))
        for gi, (_, dil) in enumerate(B_GROUPS):
            q_ref, kc, kp, kn, vc, vp, vn = ins[7 * gi:7 * gi + 7]
            lseg = chunk // dil
            qb = min(2 * rad, lseg)
            kb_n = qb + 2 * rad + _att1_window_pad(lseg)
            lstr = seq // dil
            rel = lax.broadcasted_iota(I32, (qb, kb_n), 1) - lax.broadcasted_iota(I32, (qb, kb_n), 0)
            band_bias = jnp.where((rel >= 0) & (rel <= 2 * rad), 0.0, NEG)
            jrow = lax.broadcasted_iota(I32, (1, kb_n), 1)
            lane = lax.broadcasted_iota(I32, (qb, 128), 1)
            first_head = lane < HEAD_DIM
            q_first = (lane % HEAD_DIM) < (HEAD_DIM // 2)
            zero = jnp.zeros((qb, 128), BF16)
            ones = jnp.ones((kb_n, 128), BF16)
            for res in range(dil):
                for q0 in range(0, lseg, qb):
                    apos = n * lseg + q0 - rad + jrow
                    bias = band_bias + jnp.where((apos >= 0) & (apos < lstr), 0.0, NEG)
                    rows = pl.ds(q0, qb) if dil == 1 else pl.ds(res + dil * q0, qb, stride=dil)
                    qc = q_ref[0, res, q0:q0 + qb, lanes]
                    kb = window(kc, kp, kn, res, q0 - rad, q0 - rad + kb_n, lseg, lanes)
                    vb = jnp.concatenate(
                        [window(vc, vp, vn, res, q0 - rad, q0 - rad + kb_n, lseg, lanes), ones], axis=1)
                    qs = jnp.concatenate(
                        [jnp.where(q_first, qc, zero), jnp.where(q_first, zero, qc)], axis=0)
                    s2 = lax.dot_general(qs, kb, (((1,), (1,)), ((), ())), preferred_element_type=F32)
                    ps, ms = [], []
                    for t2 in range(2):
                        s = s2[qb * t2:qb * (t2 + 1)] + bias
                        m = jnp.max(s, axis=-1, keepdims=True)
                        ps.append(jnp.exp2((s - m).astype(BF16)))
                        ms.append(m)
                    pv = jnp.dot(jnp.concatenate(ps, axis=0), vb, preferred_element_type=F32)
                    nat_o[gi][rows, :] = jnp.where(first_head, pv[0:qb, 0:128], pv[qb:2 * qb, 0:128])
                    nat_d[gi][rows, :] = jnp.where(first_head, pv[0:qb, 128:256], pv[qb:2 * qb, 128:256])
                    nat_m[gi][rows, :] = jnp.where(first_head, ms[0], ms[1])

        ms = [nat_m[gi][...] for gi in range(3)]
        mx = jnp.maximum(jnp.maximum(ms[0], ms[1]), ms[2])
        w = [jnp.exp2(x - mx) for x in ms]
        num = w[0] * nat_o[0][...] + w[1] * nat_o[1][...] + w[2] * nat_o[2][...]
        den = w[0] * nat_d[0][...] + w[1] * nat_d[1][...] + w[2] * nat_d[2][...]
        o_ref[0, :, lanes] = (num / den).astype(BF16)


def _att1(qkv):
    bsz = qkv[0].shape[0]
    seq = qkv[0].shape[1] * qkv[0].shape[2]
    chunk = ATT1_CHUNK
    rad = B_RADIUS
    in_specs, args, scratch = [], [], []
    for gi, (_, dil) in enumerate(B_GROUPS):
        q, k, v = qkv[3 * gi:3 * gi + 3]
        lseg = chunk // dil
        per = lseg // rad
        last = seq // dil // rad - 1
        cur = pl.BlockSpec((1, dil, lseg, B_WIDTH), lambda b, n: (b, 0, n, 0))
        prev = pl.BlockSpec((1, dil, rad, B_WIDTH),
                            lambda b, n, per=per: (b, 0, jnp.maximum(n * per - 1, 0), 0))
        nxt = pl.BlockSpec((1, dil, rad, B_WIDTH),
                           lambda b, n, per=per, last=last: (b, 0, jnp.minimum(n * per + per, last), 0))
        in_specs += [cur, cur, prev, nxt, cur, prev, nxt]
        args += [q, k, k, k, v, v, v]
    for _ in range(3):
        for _ in B_GROUPS:
            scratch.append(pltpu.VMEM((chunk, 128), F32))
    return pl.pallas_call(
        functools.partial(_att1_kernel, chunk=chunk, seq=seq),
        grid=(bsz, seq // chunk),
        in_specs=in_specs,
        out_specs=pl.BlockSpec((1, chunk, B_WIDTH), lambda b, n: (b, n, 0)),
        out_shape=jax.ShapeDtypeStruct((bsz, seq, B_WIDTH), BF16),
        scratch_shapes=scratch,
        compiler_params=_cparams("parallel", "parallel"),
        name="att1",
    )(*args)


def _final_kernel(x_ref, y_ref, gfp_ref, g_ref, o_ref):
    x = x_ref[0] + gfp_ref[0] * _unpack_bf16_pairs(y_ref[0])
    ms = jnp.mean(x * x, axis=-1, keepdims=True)
    o_ref[0] = (x * lax.rsqrt(ms + NORM_EPS)) * g_ref[...]


def _final(x, y, gfp, g):
    bsz, s, d = x.shape
    tm = TM_WIDE
    tok = pl.BlockSpec((1, tm, d), lambda b, i: (b, i, 0))
    return pl.pallas_call(
        _final_kernel,
        grid=(bsz, s // tm),
        in_specs=[tok, pl.BlockSpec((1, tm, d // 2), lambda b, i: (b, i, 0)),
                  pl.BlockSpec((1, 1, d), lambda b, i: (b, 0, 0)),
                  pl.BlockSpec((1, d), lambda b, i: (0, 0))],
        out_specs=tok,
        out_shape=jax.ShapeDtypeStruct((bsz, s, d), F32),
        compiler_params=_cparams("parallel", "parallel"),
        name="final_norm",
    )(x, y, gfp, g.reshape(1, d))


def kernel(x, c, positions, ada_w, ada_b, norm_mix_g, norm_ffn_g, a_w_qkv, a_w_o, a_sink, b_w_qkv, b_w_o,
           router_w, router_bias, exp_w_gate, exp_w_up, exp_w_down, final_norm_g):
    bsz, s, d = x.shape
    assert d == D_MODEL and ada_w.shape[0] == 2 and s % ATT1_CHUNK == 0
    assert all(w_ // (2 * dil) == B_RADIUS for w_, dil in B_GROUPS)

    mod = _modulation(c, ada_w, ada_b)
    mods = [[mod[l][:, k * d:(k + 1) * d].reshape(bsz, 1, d) for k in range(6)] for l in range(2)]
    cos, sin = _rope_tables(positions)

    rw_t = router_w.T.astype(BF16)
    rows = jnp.array([PER_GROUP * g + j if g < N_EXPERT_GROUPS else 0
                      for j in range(PER_GROUP) for g in range(8)], I32)
    live = jnp.array([1.0 if g < N_EXPERT_GROUPS else 0.0
                      for j in range(PER_GROUP) for g in range(8)], F32)
    rw32 = (rw_t[rows].astype(F32) * live[:, None]).astype(BF16)
    rb32 = jnp.where(live > 0, router_bias.astype(F32)[rows], NEG).reshape(32, 1)
    rw3 = rw_t.reshape(N_EXPERTS, 1, d)
    expert_ws = (exp_w_gate, exp_w_up, exp_w_down)

    sh_m, sc_m, g_m, sh_f, sc_f, g_f = mods[0]
    nq, nkv = A_Q_HEADS * HEAD_DIM, A_KV_HEADS * HEAD_DIM
    w0 = a_w_qkv[0].astype(BF16)
    w0 = jnp.concatenate([_paired_layout(w0[:, :nq]), _paired_layout(w0[:, nq:nq + nkv]), w0[:, nq + nkv:]],
                         axis=1)
    q, k, v = _qkv0(x, norm_mix_g[0], sc_m, sh_m, w0, cos, sin)
    qk_chunks = tuple(c for c in range(b_w_qkv.shape[-1] // 128) if (c * 128 // B_WIDTH) % 3 < 2)
    o, cast = _att0(q, k, v, a_sink[0].astype(F32),
                    [(w_, 0) for w_ in expert_ws] + [(a_w_o, 0), (b_w_o, 0), (b_w_qkv, 0)],
                    ((), (), (), (), (), qk_chunks))
    experts, (a_wo, b_wo, b_wqkv) = cast[:3], cast[3:]
    x1, *routed = _post(o, a_wo, x, g_m, norm_ffn_g[0], sc_f, sh_f, rw32, rb32)
    y = _moe(*routed, *experts, rw3)
    g_f_prev = g_f

    sh_m, sc_m, g_m, sh_f, sc_f, g_f = mods[1]
    outs, experts = _qkv1(x1, y, g_f_prev, norm_mix_g[1], sc_m, sh_m, b_wqkv, cos, sin, expert_ws, 1)
    x2, qkv = outs[0], outs[1:]
    o = _att1(qkv)
    x3, *routed = _post(o, b_wo, x2, g_m, norm_ffn_g[1], sc_f, sh_f, rw32, rb32)
    y = _moe(*routed, *experts, rw3)

    return _final(x3, y, g_f, final_norm_g)
```

```python
import functools
from typing import NamedTuple

import jax
import jax.numpy as jnp
from jax import lax
from jax.experimental import pallas as pl
from jax.experimental.pallas import tpu as pltpu
from jax.experimental.pallas import tpu_sc as plsc

F32, BF16, I32, U32 = jnp.float32, jnp.bfloat16, jnp.int32, jnp.uint32

D_MODEL = 1024
HEAD_DIM = 64
ROPE_THETA = 10000.0
NORM_EPS = 1e-6
A_Q_HEADS = 16
A_KV_HEADS = 4
A_GROUP = A_Q_HEADS // A_KV_HEADS
A_RADIUS = 128
B_GROUPS = ((128, 1), (512, 4), (2048, 16))
B_HEADS = 8
B_RADIUS = 64
B_WIDTH = B_HEADS * HEAD_DIM
N_EXPERTS = 16
N_EXPERT_GROUPS = 4
PER_GROUP = N_EXPERTS // N_EXPERT_GROUPS
PAIRS = ((0, 1), (0, 2), (0, 3), (1, 2), (1, 3), (2, 3))
N_CLASSES = N_EXPERT_GROUPS * len(PAIRS)
NEG = -1e30
LOG2E = 1.4426950408889634
Q_SCALE = HEAD_DIM ** -0.5 * LOG2E

VMEM_LIMIT = 56 * 1024 * 1024
TM = 512
ROPE_TS = 2048
TM_WIDE = 1024
RANK_BLOCK = 512
ATT1_CHUNK = 1024
FFN_TM = 256
FFN_TILES_PER_STEP = 2
SC_CORES_V7X = 2
SC_SUBCORES_V7X = 16
SC_ROWS_PER_STEP = 64


def _cparams(*sem):
    return pltpu.CompilerParams(dimension_semantics=sem, vmem_limit_bytes=VMEM_LIMIT)


class ModVec(NamedTuple):
    table: jax.Array
    row: int


def _mod_spec(v):
    return pl.BlockSpec((1, 1, v.table.shape[-1]), lambda b, i: (v.row + b, 0, 0))


def _rms_mod(x, g, sc, sh):
    ms = jnp.mean(x * x, axis=-1, keepdims=True)
    return (x * lax.rsqrt(ms + NORM_EPS)) * (g * (1.0 + sc)) + sh


def _rope_apply_paired(y, cos, sin_paired):
    return y * cos + pltpu.roll(y, HEAD_DIM, 1) * sin_paired


def _paired_layout(w):
    k, n = w.shape
    half = HEAD_DIM // 2
    return w.reshape(k, n // 128, 2, 2, half).transpose(0, 1, 3, 2, 4).reshape(k, n)


def _silu(x):
    half = 0.5 * x
    return half + half * jnp.tanh(half)


def _pack_bf16_pairs(x):
    n = x.shape[1] // 2
    x = x.astype(BF16).astype(F32)
    lo = lax.bitcast_convert_type(x[:, :n], U32)
    hi = lax.bitcast_convert_type(x[:, n:], U32)
    return (hi & jnp.uint32(0xFFFF0000)) | (lo >> 16)


def _unpack_bf16_pairs(p):
    lo = lax.bitcast_convert_type(p << 16, F32)
    hi = lax.bitcast_convert_type(p & jnp.uint32(0xFFFF0000), F32)
    return jnp.concatenate([lo, hi], axis=1)


def _dup_head(x, first_head, which):
    rolled = pltpu.roll(x, HEAD_DIM, 1)
    return jnp.where(first_head, x, rolled) if which == 0 else jnp.where(first_head, rolled, x)


def _mod_kernel(ct_ref, w_ref, b_ref, o_ref, *, bsz):
    ct = ct_ref[...]
    s = ct * jax.nn.sigmoid(ct)
    w = w_ref[0]
    rows = [jnp.sum(w * s[:, b:b + 1], axis=0, keepdims=True) + b_ref[0] for b in range(bsz)]
    o_ref[0] = jnp.concatenate(rows + [jnp.zeros_like(rows[0])] * (8 - bsz), axis=0)


def _modulation(c, ada_w, ada_b):
    depth, d, n = ada_w.shape
    bsz = c.shape[0]
    assert bsz <= 8
    tn = 1536
    out = pl.pallas_call(
        functools.partial(_mod_kernel, bsz=bsz),
        grid=(depth, n // tn),
        in_specs=[pl.BlockSpec((d, bsz), lambda l, j: (0, 0)),
                  pl.BlockSpec((1, d, tn), lambda l, j: (l, 0, j)),
                  pl.BlockSpec((1, 1, tn), lambda l, j: (l, 0, j))],
        out_specs=pl.BlockSpec((1, 8, tn), lambda l, j: (l, 0, j)),
        out_shape=jax.ShapeDtypeStruct((depth, 8, n), F32),
        compiler_params=_cparams("arbitrary", "arbitrary"),
        name="modulation",
    )(c.T, ada_w, ada_b.reshape(depth, 1, n))
    return out[:, :bsz]


def _rope_kernel(pos_ref, invf_ref, sign_ref, cos_ref, sin_ref):
    p = pos_ref[0].astype(F32)
    n = p.shape[0]
    quarter = lax.broadcasted_iota(I32, (n, 128), 1) // (HEAD_DIM // 2)
    pos4 = jnp.where(quarter == 0, p[:, 0:1],
                     jnp.where(quarter == 1, p[:, 1:2], jnp.where(quarter == 2, p[:, 2:3], p[:, 3:4])))
    ang = pos4 * invf_ref[...]
    dense = (jnp.cos(ang), jnp.sin(ang))
    for u in range(4):
        spread = []
        for d in dense:
            e = jnp.where(quarter == u, d, 0.0)
            e = e + pltpu.roll(e, HEAD_DIM // 2, 1)
            spread.append(e + pltpu.roll(e, HEAD_DIM, 1))
        rows = pl.ds(u, n, stride=4)
        cos_ref[0, rows, :] = spread[0]
        sin_ref[0, rows, :] = spread[1] * sign_ref[...]


def _rope_tables(positions):
    bsz, s = positions.shape
    inv_freq = ROPE_THETA ** (-jnp.arange(0, HEAD_DIM, 2, dtype=F32) / HEAD_DIM)
    invf = jnp.tile(inv_freq, 4).reshape(1, 128)
    sign = jnp.concatenate([-jnp.ones((HEAD_DIM,), F32), jnp.ones((HEAD_DIM,), F32)]).reshape(1, 128)
    ts = ROPE_TS
    return pl.pallas_call(
        _rope_kernel,
        grid=(bsz, s // ts),
        in_specs=[pl.BlockSpec((1, ts // 4, 4), lambda b, i: (b, i, 0)),
                  pl.BlockSpec((1, 128), lambda b, i: (0, 0)),
                  pl.BlockSpec((1, 128), lambda b, i: (0, 0))],
        out_specs=[pl.BlockSpec((1, ts, 128), lambda b, i: (b, i, 0))] * 2,
        out_shape=[jax.ShapeDtypeStruct((bsz, s, 128), F32)] * 2,
        compiler_params=_cparams("parallel", "parallel"),
        name="rope_tables",
    )(positions.reshape(bsz, s // 4, 4), invf, sign)


def _qkv0_kernel(x_ref, g_ref, sc_ref, sh_ref, w_ref, cos_ref, sin_ref,
                 q_ref, k_ref, v_ref, hb_scr):
    hb_scr[...] = _rms_mod(x_ref[0], g_ref[...], sc_ref[0], sh_ref[0]).astype(BF16)
    cos = cos_ref[0]
    sin = sin_ref[0]
    lane = lax.broadcasted_iota(I32, cos.shape, 1)
    x_lo = (lane % HEAD_DIM) < (HEAD_DIM // 2)
    nq = A_Q_HEADS * HEAD_DIM
    nkv = A_KV_HEADS * HEAD_DIM
    for c in range(nq // 256):
        y = jnp.dot(hb_scr[...], w_ref[:, 256 * c:256 * (c + 1)], preferred_element_type=F32)
        for cc in range(2):
            r = _rope_apply_paired(y[:, 128 * cc:128 * (cc + 1)], cos, sin) * Q_SCALE
            q_ref[0, :, 256 * c + 128 * cc:256 * c + 128 * (cc + 1)] = r.astype(BF16)
    first_head = lane < HEAD_DIM
    y = jnp.dot(hb_scr[...], w_ref[:, nq:nq + nkv], preferred_element_type=F32)
    for cc in range(nkv // 128):
        r = _rope_apply_paired(y[:, 128 * cc:128 * (cc + 1)], cos, sin)
        k_ref[0, :, 256 * cc:256 * cc + 128] = jnp.where(
            x_lo, r, pltpu.roll(r, HEAD_DIM // 2, 1)).astype(BF16)
        k_ref[0, :, 256 * cc + 128:256 * cc + 256] = jnp.where(
            x_lo, pltpu.roll(r, 128 - HEAD_DIM // 2, 1), r).astype(BF16)
    y = jnp.dot(hb_scr[...], w_ref[:, nq + nkv:nq + 2 * nkv], preferred_element_type=F32)
    for cc in range(nkv // 128):
        r = y[:, 128 * cc:128 * (cc + 1)]
        for which in range(2):
            g = 2 * cc + which
            v_ref[0, :, 128 * g:128 * (g + 1)] = _dup_head(r, first_head, which).astype(BF16)


def _qkv0(x, g, sc, sh, w, cos, sin):
    bsz, s, d = x.shape
    nq = A_Q_HEADS * HEAD_DIM
    nkv = A_KV_HEADS * HEAD_DIM
    tm = TM_WIDE
    tok = lambda w_: pl.BlockSpec((1, tm, w_), lambda b, i: (b, i, 0))
    return pl.pallas_call(
        _qkv0_kernel,
        grid=(bsz, s // tm),
        in_specs=[tok(d), pl.BlockSpec((1, d), lambda b, i: (0, 0)), _mod_spec(sc), _mod_spec(sh),
                  pl.BlockSpec(w.shape, lambda b, i: (0, 0)), tok(128), tok(128)],
        out_specs=[tok(nq), tok(2 * nkv), tok(2 * nkv)],
        out_shape=[jax.ShapeDtypeStruct((bsz, s, nq), BF16),
                   jax.ShapeDtypeStruct((bsz, s, 2 * nkv), BF16),
                   jax.ShapeDtypeStruct((bsz, s, 2 * nkv), BF16)],
        scratch_shapes=[pltpu.VMEM((tm, d), BF16)],
        compiler_params=_cparams("parallel", "parallel"),
        name="qkv0",
    )(x, g.reshape(1, d), sc.table, sh.table, w, cos, sin)


def _side_cast_specs(items, grid):
    nsteps = grid[0] * grid[1]
    in_specs, out_specs, out_shape, args = [], [], [], []
    for w, layer in items:
        n = w.shape[-1]
        per_layer = w[0].size // n
        rows = per_layer // nsteps
        assert rows * nsteps == per_layer and rows % 16 == 0
        in_specs.append(pl.BlockSpec(
            (rows, n), lambda b, i, layer=layer: (layer * nsteps + b * grid[1] + i, 0)))
        out_specs.append(pl.BlockSpec((rows, n), lambda b, i: (b * grid[1] + i, 0)))
        out_shape.append(jax.ShapeDtypeStruct((per_layer, n), BF16))
        args.append(w.reshape(w.shape[0] * per_layer, n))
    return in_specs, out_specs, out_shape, args


def _side_cast(srcs, dsts, paired_chunks=None):
    for n, (src, dst) in enumerate(zip(srcs, dsts)):
        chunks = paired_chunks[n] if paired_chunks else ()
        if not chunks:
            dst[...] = src[...].astype(BF16)
            continue
        quarter = lax.broadcasted_iota(I32, (src.shape[0], 128), 1) // (HEAD_DIM // 2)
        for c in range(src.shape[1] // 128):
            x = src[:, 128 * c:128 * (c + 1)]
            if c in chunks:
                x = jnp.where(quarter == 1, pltpu.roll(x, 128 - HEAD_DIM // 2, 1),
                              jnp.where(quarter == 2, pltpu.roll(x, HEAD_DIM // 2, 1), x))
            dst[:, 128 * c:128 * (c + 1)] = x.astype(BF16)


def _att0_kernel(sink_ref, q_ref, kc_ref, kp_ref, kn_ref, vc_ref, vp_ref, vn_ref, *rest, tq, seq, paired_chunks):
    ncast = len(paired_chunks)
    o_ref = rest[ncast]
    kf, vf = rest[2 * ncast + 1:]
    _side_cast(rest[:ncast], rest[ncast + 1:2 * ncast + 1], paired_chunks)
    i = pl.program_id(1)
    r = A_RADIUS
    ones = jnp.ones((tq + 2 * r, 128), BF16)
    for g in range(A_KV_HEADS):
        lanes = slice(128 * g, 128 * (g + 1))
        kf[g, 0:r] = kp_ref[0, :, lanes]
        kf[g, r:r + tq] = kc_ref[0, :, lanes]
        kf[g, r + tq:r + tq + r] = kn_ref[0, :, lanes]
        vf[g, 0:r, 0:128] = vp_ref[0, :, lanes]
        vf[g, r:r + tq, 0:128] = vc_ref[0, :, lanes]
        vf[g, r + tq:r + tq + r, 0:128] = vn_ref[0, :, lanes]
        vf[g, :, 128:256] = ones

    ii = lax.broadcasted_iota(I32, (r, r), 0)
    jj = lax.broadcasted_iota(I32, (r, r), 1)
    first_head = jj < HEAD_DIM
    q_first = (jj % HEAD_DIM) < (HEAD_DIM // 2)
    zero = jnp.zeros((r, 128), BF16)

    for j in range(tq // r):
        q0 = j * r
        base = i * tq + q0
        bias_lo = jnp.where((jj >= ii) & (base - r + jj >= 0), 0.0, NEG)
        bias_hi = jnp.where((jj <= ii) & (base + r + jj < seq), 0.0, NEG)
        for g in range(A_KV_HEADS):
            kb = kf[g, pl.ds(q0, 3 * r), :]
            vb = vf[g, pl.ds(q0, 3 * r), :]
            qs = []
            for cc in range(2):
                c = 2 * g + cc
                qc = q_ref[0, pl.ds(q0, r), 128 * c:128 * (c + 1)]
                qs += [jnp.where(q_first, qc, zero), jnp.where(q_first, zero, qc)]
            s4 = lax.dot_general(jnp.concatenate(qs, axis=0), kb, (((1,), (1,)), ((), ())),
                                 preferred_element_type=F32)
            ps, ms = [], []
            for t in range(A_GROUP):
                s = s4[r * t:r * (t + 1)]
                a0 = s[:, 0:r] + bias_lo
                a1 = s[:, r:2 * r]
                a2 = s[:, 2 * r:3 * r] + bias_hi
                sink = sink_ref[A_GROUP * g + t] * LOG2E
                m = jnp.max(jnp.maximum(jnp.maximum(a0, a1), a2), axis=-1, keepdims=True)
                m = jnp.maximum(m, sink)
                ps.append(jnp.exp2(jnp.concatenate([a0 - m, a1 - m, a2 - m], axis=1).astype(BF16)))
                ms.append(m)
            pv = jnp.dot(jnp.concatenate(ps, axis=0), vb, preferred_element_type=F32)
            os_ = []
            for t in range(A_GROUP):
                sink = sink_ref[A_GROUP * g + t] * LOG2E
                den = pv[r * t:r * (t + 1), 128:256] + jnp.exp2(sink - ms[t])
                os_.append(pv[r * t:r * (t + 1), 0:128] / den)
            for cc in range(2):
                c = 2 * g + cc
                o_ref[0, pl.ds(q0, r), 128 * c:128 * (c + 1)] = jnp.where(
                    first_head, os_[2 * cc], os_[2 * cc + 1]).astype(BF16)


def _att0(q, k2, v2, sink, cast_items, paired_chunks):
    bsz, s, nq = q.shape
    nkv2 = k2.shape[-1]
    tq = TM_WIDE
    r = A_RADIUS
    per = tq // r
    last = s // r - 1
    grid = (bsz, s // tq)
    cur = lambda w_: pl.BlockSpec((1, tq, w_), lambda b, i: (b, i, 0))
    prev = pl.BlockSpec((1, r, nkv2), lambda b, i: (b, jnp.maximum(i * per - 1, 0), 0))
    nxt = pl.BlockSpec((1, r, nkv2), lambda b, i: (b, jnp.minimum(i * per + per, last), 0))
    w_in, w_out, w_shape, w_args = _side_cast_specs(cast_items, grid)
    o, *cast = pl.pallas_call(
        functools.partial(_att0_kernel, tq=tq, seq=s, paired_chunks=paired_chunks),
        grid=grid,
        in_specs=[pl.BlockSpec(memory_space=pltpu.SMEM),
                  cur(nq), cur(nkv2), prev, nxt, cur(nkv2), prev, nxt] + w_in,
        out_specs=[cur(nq)] + w_out,
        out_shape=[jax.ShapeDtypeStruct((bsz, s, nq), BF16)] + w_shape,
        scratch_shapes=[pltpu.VMEM((A_KV_HEADS, tq + 2 * r, 128), BF16),
                        pltpu.VMEM((A_KV_HEADS, tq + 2 * r, 256), BF16)],
        compiler_params=_cparams("parallel", "parallel"),
        name="att0",
    )(sink, q, k2, k2, k2, v2, v2, v2, *w_args)
    return o, [c.reshape(w.shape[1:]) for c, (w, _) in zip(cast, cast_items)]


def _post_kernel(o_ref, wo_ref, x_ref, gm_ref, gf_ref, scf_ref, shf_ref, rw_ref, rb_ref, upper_ref,
                 x1_ref, h2_ref, cls_ref, rank_ref, cnt_ref, carry):
    mix = jnp.dot(o_ref[0], wo_ref[...], preferred_element_type=F32)
    x1 = x_ref[0] + gm_ref[0] * mix
    x1_ref[0] = x1
    h2 = _rms_mod(x1, gf_ref[...], scf_ref[0], shf_ref[0]).astype(BF16)
    h2_ref[0] = _pack_bf16_pairs(h2)
    tq = h2.shape[0]
    logits = lax.dot_general(rw_ref[...], h2, (((1,), (1,)), ((), ())), preferred_element_type=F32)
    scores = jax.nn.sigmoid(logits)
    biased = scores + rb_ref[...]
    pj = [biased[8 * j:8 * (j + 1)] for j in range(PER_GROUP)]
    sel = []
    for j in range(PER_GROUP):
        beaten = jnp.zeros((8, tq), F32)
        for j2 in range(PER_GROUP):
            if j2 == j:
                continue
            beats = (pj[j2] > pj[j]) | (pj[j2] == pj[j]) if j2 < j else (pj[j2] > pj[j])
            beaten = beaten + jnp.where(beats, 1.0, 0.0)
        sel.append(beaten < 2.0)
    gscore = jnp.zeros((8, tq), F32)
    for j in range(PER_GROUP):
        gscore = gscore + jnp.where(sel[j], pj[j], 0.0)
    gi = lax.broadcasted_iota(I32, (8, tq), 0).astype(F32)
    gmax = jnp.max(gscore, axis=0, keepdims=True)
    gidx = jnp.min(jnp.where(gscore == gmax, gi, 8.0), axis=0, keepdims=True)
    onehot = gi == gidx
    f = [jnp.max(jnp.where(onehot & sel[j], 1.0, 0.0), axis=0, keepdims=True) > 0.5
         for j in range(PER_GROUP)]
    pair = jnp.where(f[0], jnp.where(f[1], 0.0, jnp.where(f[2], 1.0, 2.0)),
                     jnp.where(f[1], jnp.where(f[2], 3.0, 4.0), 5.0))
    cls = (gidx * float(len(PAIRS)) + pair).astype(I32)
    cls_ref[0] = cls

    @pl.when((pl.program_id(0) == 0) & (pl.program_id(1) == 0))
    def _():
        carry[...] = jnp.zeros_like(carry)

    c = carry[...]
    ranks = []
    for blk in range(tq // RANK_BLOCK):
        cls_b = cls[:, RANK_BLOCK * blk:RANK_BLOCK * (blk + 1)]
        oh = lax.broadcasted_iota(I32, (32, RANK_BLOCK), 0) == cls_b
        ohf = jnp.where(oh, 1.0, 0.0)
        within = jnp.dot(ohf.astype(BF16), upper_ref[...], preferred_element_type=F32)
        ranks.append(jnp.sum(jnp.where(oh, within + c[:, 0:1], 0.0), axis=0, keepdims=True))
        c = c + jnp.sum(ohf, axis=1, keepdims=True)
    rank_ref[0] = jnp.concatenate(ranks, axis=1).astype(I32)
    carry[...] = c
    cnt_ref[...] = c


def _post(o, wo, x, gm, gf, scf, shf, rw32, rb32):
    bsz, s, d = x.shape
    do = o.shape[-1]
    tq = TM_WIDE
    nt = s // tq
    tok = lambda w_: pl.BlockSpec((1, tq, w_), lambda b, i: (b, i, 0))
    full = lambda a: pl.BlockSpec(a.shape, lambda b, i: (0,) * a.ndim)
    per_tile = pl.BlockSpec((1, 1, tq), lambda b, i: (b * nt + i, 0, 0))
    gf2 = gf.reshape(1, d)
    upper = (jnp.arange(RANK_BLOCK)[:, None] < jnp.arange(RANK_BLOCK)[None, :]).astype(BF16)
    return pl.pallas_call(
        _post_kernel,
        grid=(bsz, nt),
        in_specs=[tok(do), full(wo), tok(d), _mod_spec(gm), full(gf2), _mod_spec(scf), _mod_spec(shf),
                  full(rw32), full(rb32),
                  full(upper)],
        out_specs=[tok(d), tok(d // 2), per_tile, per_tile, pl.BlockSpec((32, 128), lambda b, i: (0, 0))],
        out_shape=[jax.ShapeDtypeStruct((bsz, s, d), F32),
                   jax.ShapeDtypeStruct((bsz, s, d // 2), U32),
                   jax.ShapeDtypeStruct((bsz * nt, 1, tq), I32),
                   jax.ShapeDtypeStruct((bsz * nt, 1, tq), I32),
                   jax.ShapeDtypeStruct((32, 128), F32)],
        scratch_shapes=[pltpu.VMEM((32, 128), F32)],
        compiler_params=_cparams("arbitrary", "arbitrary"),
        name="post_attention",
    )(o, wo, x, gm.table, gf2, scf.table, shf.table, rw32, rb32, upper)


def _ffn_kernel(e1_ref, e2_ref, nused_ref, hs_ref, *refs):
    ys_ref = refs[-1]
    j = pl.program_id(0)
    tm = FFN_TM

    @pl.when(j * FFN_TILES_PER_STEP < nused_ref[0])
    def _():
        for sub in range(FFN_TILES_PER_STEP):
            wg1, wu1, wd1, rw1, wg2, wu2, wd2, rw2 = refs[8 * sub:8 * (sub + 1)]
            rows = slice(tm * sub, tm * (sub + 1))
            xf = _unpack_bf16_pairs(hs_ref[rows])
            x = xf.astype(BF16)
            gt1 = jnp.dot(x, wg1[0], preferred_element_type=F32)
            up1 = jnp.dot(x, wu1[0], preferred_element_type=F32)
            gt2 = jnp.dot(x, wg2[0], preferred_element_type=F32)
            up2 = jnp.dot(x, wu2[0], preferred_element_type=F32)
            a1 = (_silu(gt1) * up1).astype(BF16)
            y1 = jnp.dot(a1, wd1[0], preferred_element_type=F32)
            a2 = (_silu(gt2) * up2).astype(BF16)
            y2 = jnp.dot(a2, wd2[0], preferred_element_type=F32)
            s1 = jax.nn.sigmoid(jnp.sum(xf * rw1[0].astype(F32), axis=-1, keepdims=True))
            s2 = jax.nn.sigmoid(jnp.sum(xf * rw2[0].astype(F32), axis=-1, keepdims=True))
            den = s1 + s2
            ys_ref[rows] = _pack_bf16_pairs((s1 / den) * y1 + (s2 / den) * y2)

    @pl.when(j * FFN_TILES_PER_STEP >= nused_ref[0])
    def _():
        ys_ref[...] = jnp.zeros_like(ys_ref)


def _ffn(hs, e1, e2, nused, wg, wu, wd, rw3):
    npad, dh = hs.shape
    d = 2 * dh
    per = FFN_TILES_PER_STEP
    rows = per * FFN_TM
    de = wg.shape[-1]
    assert npad % rows == 0
    row = pl.BlockSpec((rows, dh), lambda j, e1, e2, nu: (jnp.minimum(j, (nu[0] - 1) // per), 0))
    w_specs = []
    for sub in range(per):
        for which in range(2):
            pick = lambda j, e1, e2, nu, sub=sub, which=which: ((e2 if which else e1)[per * j + sub], 0, 0)
            w_specs += [pl.BlockSpec((1, d, de), pick), pl.BlockSpec((1, d, de), pick),
                        pl.BlockSpec((1, de, d), pick), pl.BlockSpec((1, 1, d), pick)]
    grid_spec = pltpu.PrefetchScalarGridSpec(
        num_scalar_prefetch=3,
        grid=(npad // rows,),
        in_specs=[row] + w_specs,
        out_specs=pl.BlockSpec((rows, dh), lambda j, e1, e2, nu: (j, 0)),
    )
    return pl.pallas_call(
        _ffn_kernel,
        grid_spec=grid_spec,
        out_shape=jax.ShapeDtypeStruct((npad, dh), U32),
        compiler_params=_cparams("arbitrary"),
        name="expert_ffn",
    )(e1, e2, nused, hs, *([wg, wu, wd, rw3] * (2 * per)))


def _sc_mesh():
    return plsc.VectorSubcoreMesh(core_axis_name="c", subcore_axis_name="s",
                                  num_cores=SC_CORES_V7X, num_subcores=SC_SUBCORES_V7X)


def _sc_steps(n_rows):
    workers = SC_CORES_V7X * SC_SUBCORES_V7X
    per_w = n_rows // workers
    ch = min(SC_ROWS_PER_STEP, per_w // 2)
    n = per_w // ch
    assert per_w * workers == n_rows and n * ch == per_w and n % 2 == 0 and ch % 8 == 0
    return per_w, ch, n


def _sc_scratch(ch, d, dtype):
    return [pltpu.VMEM((2, ch), I32), pltpu.VMEM((2, ch, d), dtype),
            pltpu.SemaphoreType.DMA((2,)), pltpu.SemaphoreType.DMA((2,))]


def _sc_scatter_rows(src, idx, n_out):
    n_in, d = src.shape
    per_w, ch, n = _sc_steps(n_in)

    def body(src_hbm, idx_hbm, out_hbm, idx_v, rows_v, rsem, ssem):
        base = (lax.axis_index("s") * SC_CORES_V7X + lax.axis_index("c")) * per_w

        def read(i, b):
            return pltpu.make_async_copy(src_hbm.at[pl.ds(base + i * ch, ch)], rows_v.at[b], rsem.at[b])

        def scatter(b):
            return pltpu.make_async_copy(rows_v.at[b], out_hbm.at[idx_v.at[b]], ssem.at[b])

        def start_read(i, b):
            pltpu.sync_copy(idx_hbm.at[pl.ds(base + i * ch, ch)], idx_v.at[b])
            read(i, b).start()

        start_read(0, 0)

        @pl.loop(0, n, step=2)
        def _(i):
            for b in (0, 1):
                cur = i + b

                @pl.when(cur + 1 < n)
                def _():
                    @pl.when(cur >= 1)
                    def _():
                        scatter(1 - b).wait()
                    start_read(cur + 1, 1 - b)

                read(cur, b).wait()
                scatter(b).start()

        scatter(0).wait()
        scatter(1).wait()

    return pl.kernel(
        body, mesh=_sc_mesh(),
        out_type=jax.ShapeDtypeStruct((n_out, d), src.dtype),
        scratch_types=_sc_scratch(ch, d, src.dtype),
        name="sc_scatter_rows",
    )(src, idx)


def _sc_gather_rows(table, idx):
    n_out = idx.shape[0]
    d = table.shape[1]
    per_w, ch, n = _sc_steps(n_out)

    def body(table_hbm, idx_hbm, out_hbm, idx_v, rows_v, gsem, wsem):
        base = (lax.axis_index("s") * SC_CORES_V7X + lax.axis_index("c")) * per_w

        def gather(b):
            return pltpu.make_async_copy(table_hbm.at[idx_v.at[b]], rows_v.at[b], gsem.at[b])

        def write(i, b):
            return pltpu.make_async_copy(rows_v.at[b], out_hbm.at[pl.ds(base + i * ch, ch)], wsem.at[b])

        def start_gather(i, b):
            pltpu.sync_copy(idx_hbm.at[pl.ds(base + i * ch, ch)], idx_v.at[b])
            gather(b).start()

        start_gather(0, 0)

        @pl.loop(0, n, step=2)
        def _(i):
            for b in (0, 1):
                cur = i + b

                @pl.when(cur + 1 < n)
                def _():
                    @pl.when(cur >= 1)
                    def _():
                        write(cur - 1, 1 - b).wait()
                    start_gather(cur + 1, 1 - b)

                gather(b).wait()
                write(cur, b).start()

        write(n - 2, 0).wait()
        write(n - 1, 1).wait()

    return pl.kernel(
        body, mesh=_sc_mesh(),
        out_type=jax.ShapeDtypeStruct((n_out, d), table.dtype),
        scratch_types=_sc_scratch(ch, d, table.dtype),
        name="sc_gather_rows",
    )(table, idx)


def _moe(h2p, cls3, rank3, cnt, wg, wu, wd, rw3):
    bsz, s, dh = h2p.shape
    t = bsz * s
    tm = FFN_TM
    counts = cnt[:N_CLASSES, 0].astype(I32)
    padded = ((counts + tm - 1) // tm) * tm
    upto = jnp.arange(N_CLASSES)[:, None] >= jnp.arange(N_CLASSES)[None, :]
    ends = jnp.sum(jnp.where(upto, padded[None, :], 0), axis=1)
    base = ends - padded
    cls2 = cls3[:, 0, :]
    pos2 = rank3[:, 0, :]
    for c in range(N_CLASSES):
        pos2 = pos2 + jnp.where(cls2 == c, base[c], 0)
    pos = pos2.reshape(t)
    ntiles = t // tm + N_CLASSES
    tile_start = jnp.arange(ntiles, dtype=I32) * tm
    tcls = jnp.minimum(jnp.sum((ends[None, :] <= tile_start[:, None]).astype(I32), axis=1), N_CLASSES - 1)
    nused = (ends[-1:] // tm).astype(I32)
    grp = tcls // len(PAIRS)
    pr = tcls % len(PAIRS)
    first = sum(jnp.where(pr == i, p[0], 0) for i, p in enumerate(PAIRS))
    second = sum(jnp.where(pr == i, p[1], 0) for i, p in enumerate(PAIRS))
    e1 = (grp * PER_GROUP + first).astype(I32)
    e2 = (grp * PER_GROUP + second).astype(I32)
    hs = _sc_scatter_rows(h2p.reshape(t, dh), pos, ntiles * tm)
    ys = _ffn(hs, e1, e2, nused, wg, wu, wd, rw3)
    return _sc_gather_rows(ys, pos).reshape(bsz, s, dh)


def _qkv1_kernel(x_ref, y_ref, gfp_ref, g_ref, sc_ref, sh_ref, w_ref, cos_ref, sin_ref,
                 wg_ref, wu_ref, wd_ref, x2_ref, *rest):
    outs = rest[:9]
    _side_cast((wg_ref, wu_ref, wd_ref), rest[9:12])
    hb_scr, ysc, ysc2 = rest[12:]
    tm = x_ref.shape[1]
    x2 = x_ref[0] + gfp_ref[0] * _unpack_bf16_pairs(y_ref[0])
    x2_ref[0] = x2
    hb_scr[...] = _rms_mod(x2, g_ref[...], sc_ref[0], sh_ref[0]).astype(BF16)
    cos = cos_ref[0]
    sin = sin_ref[0]
    for gi, (_, dil) in reversed(list(enumerate(B_GROUPS))):
        for j in range(3):
            c = gi * 3 + j
            y = jnp.dot(hb_scr[...], w_ref[:, B_WIDTH * c:B_WIDTH * (c + 1)], preferred_element_type=F32)
            out = outs[c]
            for cc in range(B_WIDTH // 128):
                lanes = slice(128 * cc, 128 * (cc + 1))
                r = y[:, lanes]
                if j < 2:
                    r = _rope_apply_paired(r, cos, sin)
                if j == 0:
                    r = r * Q_SCALE
                if dil == 1:
                    out[0, 0, :, lanes] = r.astype(BF16)
                    continue
                ysc[cc] = r
                if dil % 16:
                    for rr in range(dil):
                        out[0, rr, :, lanes] = ysc[cc, pl.ds(rr, tm // dil, stride=dil), :].astype(BF16)
                    continue
                quarter = tm // 4
                inner = dil // 4
                for r1 in range(4):
                    ysc2[cc, r1 * quarter:(r1 + 1) * quarter] = ysc[cc, pl.ds(r1, quarter, stride=4), :]
                for r1 in range(4):
                    for r2 in range(inner):
                        out[0, r1 + 4 * r2, :, lanes] = ysc2[
                            cc, pl.ds(r1 * quarter + r2, tm // dil, stride=inner), :].astype(BF16)


def _qkv1(x, y, gfp, g, sc, sh, w, cos, sin, expert_ws, layer):
    bsz, s, d = x.shape
    tm = TM
    grid = (bsz, s // tm)
    w_in, w_out, w_shape, w_args = _side_cast_specs([(w_, layer) for w_ in expert_ws], grid)
    tok = lambda w_: pl.BlockSpec((1, tm, w_), lambda b, i: (b, i, 0))
    out_specs = [tok(d)]
    out_shape = [jax.ShapeDtypeStruct((bsz, s, d), F32)]
    for _, dil in B_GROUPS:
        for _ in range(3):
            out_specs.append(pl.BlockSpec((1, dil, tm // dil, B_WIDTH), lambda b, i: (b, 0, i, 0)))
            out_shape.append(jax.ShapeDtypeStruct((bsz, dil, s // dil, B_WIDTH), BF16))
    outs = pl.pallas_call(
        _qkv1_kernel,
        grid=grid,
        in_specs=[tok(d), tok(d // 2), _mod_spec(gfp), pl.BlockSpec((1, d), lambda b, i: (0, 0)),
                  _mod_spec(sc), _mod_spec(sh),
                  pl.BlockSpec(w.shape, lambda b, i: (0, 0)), tok(128), tok(128)] + w_in,
        out_specs=out_specs + w_out,
        out_shape=out_shape + w_shape,
        scratch_shapes=[pltpu.VMEM((tm, d), BF16), pltpu.VMEM((B_WIDTH // 128, tm, 128), F32),
                        pltpu.VMEM((B_WIDTH // 128, tm, 128), F32)],
        compiler_params=_cparams("parallel", "parallel"),
        name="qkv1",
    )(x, y, gfp.table, g.reshape(1, d), sc.table, sh.table, w, cos, sin, *w_args)
    return outs[:10], [c.reshape(w_.shape[1:]) for c, w_ in zip(outs[10:], expert_ws)]


def _att1_window_pad(lseg):
    qb = min(2 * B_RADIUS, lseg)
    return -(qb + 2 * B_RADIUS) % 128


def _att1_kernel(*refs, chunk, seq):
    ins = refs[:21]
    o_ref = refs[21]
    scr = refs[22:]
    nat_o, nat_d, nat_m = scr[0:3], scr[3:6], scr[6:9]
    n = pl.program_id(1)
    rad = B_RADIUS
    npair = B_WIDTH // 128

    def window(cur, prev, nxt, res, lo, hi, lseg, lanes):
        parts = []
        if lo < 0:
            parts.append(prev[0, res, :, lanes])
        parts.append(cur[0, res, max(lo, 0):min(hi, lseg), lanes])
        if hi > lseg:
            parts.append(nxt[0, res, :, lanes])
        if hi > lseg + rad:
            parts.append(nxt[0, res, :, lanes])
        return jnp.concatenate(parts, axis=0) if len(parts) > 1 else parts[0]

    for c in range(npair):
        lanes = slice(128 * c, 128 * (c + 1))
        for gi, (_, dil) in enumerate(B_GROUPS):
            q_ref, kc, kp, kn, vc, vp, vn = ins[7 * gi:7 * gi + 7]
            lseg = chunk // dil
            qb = min(2 * rad, lseg)
            kb_n = qb + 2 * rad + _att1_window_pad(lseg)
            lstr = seq // dil
            rel = lax.broadcasted_iota(I32, (qb, kb_n), 1) - lax.broadcasted_iota(I32, (qb, kb_n), 0)
            band_bias = jnp.where((rel >= 0) & (rel <= 2 * rad), 0.0, NEG)
            jrow = lax.broadcasted_iota(I32, (1, kb_n), 1)
            lane = lax.broadcasted_iota(I32, (qb, 128), 1)
            first_head = lane < HEAD_DIM
            q_first = (lane % HEAD_DIM) < (HEAD_DIM // 2)
            zero = jnp.zeros((qb, 128), BF16)
            ones = jnp.ones((kb_n, 128), BF16)
            for res in range(dil):
                for q0 in range(0, lseg, qb):
                    apos = n * lseg + q0 - rad + jrow
                    bias = band_bias + jnp.where((apos >= 0) & (apos < lstr), 0.0, NEG)
                    rows = pl.ds(q0, qb) if dil == 1 else pl.ds(res + dil * q0, qb, stride=dil)
                    qc = q_ref[0, res, q0:q0 + qb, lanes]
                    kb = window(kc, kp, kn, res, q0 - rad, q0 - rad + kb_n, lseg, lanes)
                    vb = jnp.concatenate(
                        [window(vc, vp, vn, res, q0 - rad, q0 - rad + kb_n, lseg, lanes), ones], axis=1)
                    qs = jnp.concatenate(
                        [jnp.where(q_first, qc, zero), jnp.where(q_first, zero, qc)], axis=0)
                    s2 = lax.dot_general(qs, kb, (((1,), (1,)), ((), ())), preferred_element_type=F32)
                    ps, ms = [], []
                    for t2 in range(2):
                        s = s2[qb * t2:qb * (t2 + 1)] + bias
                        m = jnp.max(s, axis=-1, keepdims=True)
                        ps.append(jnp.exp2((s - m).astype(BF16)))
                        ms.append(m)
                    pv = jnp.dot(jnp.concatenate(ps, axis=0), vb, preferred_element_type=F32)
                    nat_o[gi][rows, :] = jnp.where(first_head, pv[0:qb, 0:128], pv[qb:2 * qb, 0:128])
                    nat_d[gi][rows, :] = jnp.where(first_head, pv[0:qb, 128:256], pv[qb:2 * qb, 128:256])
                    nat_m[gi][rows, :] = jnp.where(first_head, ms[0], ms[1])

        ms = [nat_m[gi][...] for gi in range(3)]
        mx = jnp.maximum(jnp.maximum(ms[0], ms[1]), ms[2])
        w = [jnp.exp2(x - mx) for x in ms]
        num = w[0] * nat_o[0][...] + w[1] * nat_o[1][...] + w[2] * nat_o[2][...]
        den = w[0] * nat_d[0][...] + w[1] * nat_d[1][...] + w[2] * nat_d[2][...]
        o_ref[0, :, lanes] = (num / den).astype(BF16)


def _att1(qkv):
    bsz = qkv[0].shape[0]
    seq = qkv[0].shape[1] * qkv[0].shape[2]
    chunk = ATT1_CHUNK
    rad = B_RADIUS
    in_specs, args, scratch = [], [], []
    for gi, (_, dil) in enumerate(B_GROUPS):
        q, k, v = qkv[3 * gi:3 * gi + 3]
        lseg = chunk // dil
        per = lseg // rad
        last = seq // dil // rad - 1
        cur = pl.BlockSpec((1, dil, lseg, B_WIDTH), lambda b, n: (b, 0, n, 0))
        prev = pl.BlockSpec((1, dil, rad, B_WIDTH),
                            lambda b, n, per=per: (b, 0, jnp.maximum(n * per - 1, 0), 0))
        nxt = pl.BlockSpec((1, dil, rad, B_WIDTH),
                           lambda b, n, per=per, last=last: (b, 0, jnp.minimum(n * per + per, last), 0))
        in_specs += [cur, cur, prev, nxt, cur, prev, nxt]
        args += [q, k, k, k, v, v, v]
    for _ in range(3):
        for _ in B_GROUPS:
            scratch.append(pltpu.VMEM((chunk, 128), F32))
    return pl.pallas_call(
        functools.partial(_att1_kernel, chunk=chunk, seq=seq),
        grid=(bsz, seq // chunk),
        in_specs=in_specs,
        out_specs=pl.BlockSpec((1, chunk, B_WIDTH), lambda b, n: (b, n, 0)),
        out_shape=jax.ShapeDtypeStruct((bsz, seq, B_WIDTH), BF16),
        scratch_shapes=scratch,
        compiler_params=_cparams("parallel", "parallel"),
        name="att1",
    )(*args)


def _final_kernel(x_ref, y_ref, gfp_ref, g_ref, o_ref):
    x = x_ref[0] + gfp_ref[0] * _unpack_bf16_pairs(y_ref[0])
    ms = jnp.mean(x * x, axis=-1, keepdims=True)
    o_ref[0] = (x * lax.rsqrt(ms + NORM_EPS)) * g_ref[...]


def _final(x, y, gfp, g):
    bsz, s, d = x.shape
    tm = TM_WIDE
    tok = pl.BlockSpec((1, tm, d), lambda b, i: (b, i, 0))
    return pl.pallas_call(
        _final_kernel,
        grid=(bsz, s // tm),
        in_specs=[tok, pl.BlockSpec((1, tm, d // 2), lambda b, i: (b, i, 0)),
                  _mod_spec(gfp),
                  pl.BlockSpec((1, d), lambda b, i: (0, 0))],
        out_specs=tok,
        out_shape=jax.ShapeDtypeStruct((bsz, s, d), F32),
        compiler_params=_cparams("parallel", "parallel"),
        name="final_norm",
    )(x, y, gfp.table, g.reshape(1, d))


def kernel(x, c, positions, ada_w, ada_b, norm_mix_g, norm_ffn_g, a_w_qkv, a_w_o, a_sink, b_w_qkv, b_w_o,
           router_w, router_bias, exp_w_gate, exp_w_up, exp_w_down, final_norm_g):
    bsz, s, d = x.shape
    assert d == D_MODEL and ada_w.shape[0] == 2 and s % ATT1_CHUNK == 0
    assert all(w_ // (2 * dil) == B_RADIUS for w_, dil in B_GROUPS)

    mod = _modulation(c, ada_w, ada_b)
    table = mod.reshape(2, bsz, 6, d).transpose(0, 2, 1, 3).reshape(2 * 6 * bsz, 1, d)
    mods = [[ModVec(table, (l * 6 + k) * bsz) for k in range(6)] for l in range(2)]
    cos, sin = _rope_tables(positions)

    rw_t = router_w.T.astype(BF16)
    rows = jnp.array([PER_GROUP * g + j if g < N_EXPERT_GROUPS else 0
                      for j in range(PER_GROUP) for g in range(8)], I32)
    live = jnp.array([1.0 if g < N_EXPERT_GROUPS else 0.0
                      for j in range(PER_GROUP) for g in range(8)], F32)
    rw32 = (rw_t[rows].astype(F32) * live[:, None]).astype(BF16)
    rb32 = jnp.where(live > 0, router_bias.astype(F32)[rows], NEG).reshape(32, 1)
    rw3 = rw_t.reshape(N_EXPERTS, 1, d)
    expert_ws = (exp_w_gate, exp_w_up, exp_w_down)

    sh_m, sc_m, g_m, sh_f, sc_f, g_f = mods[0]
    nq, nkv = A_Q_HEADS * HEAD_DIM, A_KV_HEADS * HEAD_DIM
    w0 = a_w_qkv[0].astype(BF16)
    w0 = jnp.concatenate([_paired_layout(w0[:, :nq]), _paired_layout(w0[:, nq:nq + nkv]), w0[:, nq + nkv:]],
                         axis=1)
    q, k, v = _qkv0(x, norm_mix_g[0], sc_m, sh_m, w0, cos, sin)
    qk_chunks = tuple(c for c in range(b_w_qkv.shape[-1] // 128) if (c * 128 // B_WIDTH) % 3 < 2)
    o, cast = _att0(q, k, v, a_sink[0].astype(F32),
                    [(w_, 0) for w_ in expert_ws] + [(a_w_o, 0), (b_w_o, 0), (b_w_qkv, 0)],
                    ((), (), (), (), (), qk_chunks))
    experts, (a_wo, b_wo, b_wqkv) = cast[:3], cast[3:]
    x1, *routed = _post(o, a_wo, x, g_m, norm_ffn_g[0], sc_f, sh_f, rw32, rb32)
    y = _moe(*routed, *experts, rw3)
    g_f_prev = g_f

    sh_m, sc_m, g_m, sh_f, sc_f, g_f = mods[1]
    outs, experts = _qkv1(x1, y, g_f_prev, norm_mix_g[1], sc_m, sh_m, b_wqkv, cos, sin, expert_ws, 1)
    x2, qkv = outs[0], outs[1:]
    o = _att1(qkv)
    x3, *routed = _post(o, b_wo, x2, g_m, norm_ffn_g[1], sc_f, sh_f, rw32, rb32)
    y = _moe(*routed, *experts, rw3)

    return _final(x3, y, g_f, final_norm_g)
```

```python
import functools

import jax
import jax.numpy as jnp
from jax import lax
from jax.experimental import pallas as pl
from jax.experimental.pallas import tpu as pltpu
from jax.experimental.pallas import tpu_sc as plsc

F32, BF16, I32, U32 = jnp.float32, jnp.bfloat16, jnp.int32, jnp.uint32

D_MODEL = 1024
HEAD_DIM = 64
ROPE_THETA = 10000.0
NORM_EPS = 1e-6
A_Q_HEADS = 16
A_KV_HEADS = 4
A_GROUP = A_Q_HEADS // A_KV_HEADS
A_RADIUS = 128
B_GROUPS = ((128, 1), (512, 4), (2048, 16))
B_HEADS = 8
B_RADIUS = 64
B_WIDTH = B_HEADS * HEAD_DIM
N_EXPERTS = 16
N_EXPERT_GROUPS = 4
PER_GROUP = N_EXPERTS // N_EXPERT_GROUPS
D_EXPERT = D_MODEL // 2
PAIRS = ((0, 1), (0, 2), (0, 3), (1, 2), (1, 3), (2, 3))
N_CLASSES = N_EXPERT_GROUPS * len(PAIRS)
NEG = -1e30
LOG2E = 1.4426950408889634
Q_SCALE = HEAD_DIM ** -0.5 * LOG2E

VMEM_LIMIT = 56 * 1024 * 1024
TM = 512
ROPE_TS = 2048
TM_WIDE = 1024
RANK_BLOCK = 512
ATT1_CHUNK = 1024
FFN_TM = 256
FFN_TILES_PER_STEP = 3
SC_CORES_V7X = 2
SC_SUBCORES_V7X = 16
SC_ROWS_PER_STEP = 64


def _cparams(*sem):
    return pltpu.CompilerParams(dimension_semantics=sem, vmem_limit_bytes=VMEM_LIMIT)


def _rms_mod(x, g, sc, sh):
    ms = jnp.mean(x * x, axis=-1, keepdims=True)
    return (x * lax.rsqrt(ms + NORM_EPS)) * (g * (1.0 + sc)) + sh


def _rope_apply_paired(y, cos, sin_paired):
    return y * cos + pltpu.roll(y, HEAD_DIM, 1) * sin_paired


def _paired_layout(w):
    k, n = w.shape
    half = HEAD_DIM // 2
    return w.reshape(k, n // 128, 2, 2, half).transpose(0, 1, 3, 2, 4).reshape(k, n)


def _silu(x):
    half = 0.5 * x
    return half + half * jnp.tanh(half)


def _pack_bf16_pairs(x):
    n = x.shape[1] // 2
    lo = lax.bitcast_convert_type(x[:, :n].astype(BF16).astype(F32), U32)
    hi = lax.bitcast_convert_type(x[:, n:].astype(BF16).astype(F32), U32)
    return (hi & jnp.uint32(0xFFFF0000)) | (lo >> 16)


def _unpack_bf16_pairs(p):
    lo = lax.bitcast_convert_type(p << 16, F32)
    hi = lax.bitcast_convert_type(p & jnp.uint32(0xFFFF0000), F32)
    return jnp.concatenate([lo, hi], axis=1)


def _dup_head(x, first_head, which):
    rolled = pltpu.roll(x, HEAD_DIM, 1)
    return jnp.where(first_head, x, rolled) if which == 0 else jnp.where(first_head, rolled, x)


def _mod_kernel(ct_ref, w_ref, b_ref, o_ref, *, bsz):
    ct = ct_ref[...]
    s = ct * jax.nn.sigmoid(ct)
    w = w_ref[0]
    rows = [jnp.sum(w * s[:, b:b + 1], axis=0, keepdims=True) + b_ref[0] for b in range(bsz)]
    o_ref[0] = jnp.concatenate(rows + [jnp.zeros_like(rows[0])] * (8 - bsz), axis=0)


def _modulation(c, ada_w, ada_b):
    depth, d, n = ada_w.shape
    bsz = c.shape[0]
    assert bsz <= 8
    tn = 1536
    out = pl.pallas_call(
        functools.partial(_mod_kernel, bsz=bsz),
        grid=(depth, n // tn),
        in_specs=[pl.BlockSpec((d, bsz), lambda l, j: (0, 0)),
                  pl.BlockSpec((1, d, tn), lambda l, j: (l, 0, j)),
                  pl.BlockSpec((1, 1, tn), lambda l, j: (l, 0, j))],
        out_specs=pl.BlockSpec((1, 8, tn), lambda l, j: (l, 0, j)),
        out_shape=jax.ShapeDtypeStruct((depth, 8, n), F32),
        compiler_params=_cparams("arbitrary", "arbitrary"),
        name="modulation",
    )(c.T, ada_w, ada_b.reshape(depth, 1, n))
    return out[:, :bsz]


def _rope_kernel(pos_ref, invf_ref, sign_ref, cos_ref, sin_ref):
    p = pos_ref[0].astype(F32)
    n = p.shape[0]
    quarter = lax.broadcasted_iota(I32, (n, 128), 1) // (HEAD_DIM // 2)
    pos4 = jnp.where(quarter == 0, p[:, 0:1],
                     jnp.where(quarter == 1, p[:, 1:2], jnp.where(quarter == 2, p[:, 2:3], p[:, 3:4])))
    ang = pos4 * invf_ref[...]
    dense = (jnp.cos(ang), jnp.sin(ang))
    for u in range(4):
        spread = []
        for d in dense:
            e = jnp.where(quarter == u, d, 0.0)
            e = e + pltpu.roll(e, HEAD_DIM // 2, 1)
            spread.append(e + pltpu.roll(e, HEAD_DIM, 1))
        rows = pl.ds(u, n, stride=4)
        cos_ref[0, rows, :] = spread[0]
        sin_ref[0, rows, :] = spread[1] * sign_ref[...]


def _rope_tables(positions):
    bsz, s = positions.shape
    inv_freq = ROPE_THETA ** (-jnp.arange(0, HEAD_DIM, 2, dtype=F32) / HEAD_DIM)
    invf = jnp.tile(inv_freq, 4).reshape(1, 128)
    sign = jnp.concatenate([-jnp.ones((HEAD_DIM,), F32), jnp.ones((HEAD_DIM,), F32)]).reshape(1, 128)
    ts = ROPE_TS
    return pl.pallas_call(
        _rope_kernel,
        grid=(bsz, s // ts),
        in_specs=[pl.BlockSpec((1, ts // 4, 4), lambda b, i: (b, i, 0)),
                  pl.BlockSpec((1, 128), lambda b, i: (0, 0)),
                  pl.BlockSpec((1, 128), lambda b, i: (0, 0))],
        out_specs=[pl.BlockSpec((1, ts, 128), lambda b, i: (b, i, 0))] * 2,
        out_shape=[jax.ShapeDtypeStruct((bsz, s, 128), F32)] * 2,
        compiler_params=_cparams("parallel", "parallel"),
        name="rope_tables",
    )(positions.reshape(bsz, s // 4, 4), invf, sign)


def _qkv0_kernel(x_ref, g_ref, sc_ref, sh_ref, w_ref, cos_ref, sin_ref,
                 q_ref, k_ref, v_ref, hb_scr):
    hb_scr[...] = _rms_mod(x_ref[0], g_ref[...], sc_ref[0], sh_ref[0]).astype(BF16)
    cos = cos_ref[0]
    sin = sin_ref[0]
    lane = lax.broadcasted_iota(I32, cos.shape, 1)
    x_lo = (lane % HEAD_DIM) < (HEAD_DIM // 2)
    nq = A_Q_HEADS * HEAD_DIM
    nkv = A_KV_HEADS * HEAD_DIM
    for c in range(nq // 256):
        y = jnp.dot(hb_scr[...], w_ref[:, 256 * c:256 * (c + 1)], preferred_element_type=F32)
        for cc in range(2):
            r = _rope_apply_paired(y[:, 128 * cc:128 * (cc + 1)], cos, sin) * Q_SCALE
            q_ref[0, :, 256 * c + 128 * cc:256 * c + 128 * (cc + 1)] = r.astype(BF16)
    first_head = lane < HEAD_DIM
    y = jnp.dot(hb_scr[...], w_ref[:, nq:nq + nkv], preferred_element_type=F32)
    for cc in range(nkv // 128):
        r = _rope_apply_paired(y[:, 128 * cc:128 * (cc + 1)], cos, sin)
        k_ref[0, :, 256 * cc:256 * cc + 128] = jnp.where(
            x_lo, r, pltpu.roll(r, HEAD_DIM // 2, 1)).astype(BF16)
        k_ref[0, :, 256 * cc + 128:256 * cc + 256] = jnp.where(
            x_lo, pltpu.roll(r, 128 - HEAD_DIM // 2, 1), r).astype(BF16)
    y = jnp.dot(hb_scr[...], w_ref[:, nq + nkv:nq + 2 * nkv], preferred_element_type=F32)
    for cc in range(nkv // 128):
        r = y[:, 128 * cc:128 * (cc + 1)]
        for which in range(2):
            g = 2 * cc + which
            v_ref[0, :, 128 * g:128 * (g + 1)] = _dup_head(r, first_head, which).astype(BF16)


def _qkv0(x, g, sc, sh, w, cos, sin):
    bsz, s, d = x.shape
    nq = A_Q_HEADS * HEAD_DIM
    nkv = A_KV_HEADS * HEAD_DIM
    tm = TM_WIDE
    tok = lambda w_: pl.BlockSpec((1, tm, w_), lambda b, i: (b, i, 0))
    per_b = pl.BlockSpec((1, 1, d), lambda b, i: (b, 0, 0))
    return pl.pallas_call(
        _qkv0_kernel,
        grid=(bsz, s // tm),
        in_specs=[tok(d), pl.BlockSpec((1, d), lambda b, i: (0, 0)), per_b, per_b,
                  pl.BlockSpec(w.shape, lambda b, i: (0, 0)), tok(128), tok(128)],
        out_specs=[tok(nq), tok(2 * nkv), tok(2 * nkv)],
        out_shape=[jax.ShapeDtypeStruct((bsz, s, nq), BF16),
                   jax.ShapeDtypeStruct((bsz, s, 2 * nkv), BF16),
                   jax.ShapeDtypeStruct((bsz, s, 2 * nkv), BF16)],
        scratch_shapes=[pltpu.VMEM((tm, d), BF16)],
        compiler_params=_cparams("parallel", "parallel"),
        name="qkv0",
    )(x, g.reshape(1, d), sc, sh, w, cos, sin)


def _side_cast_specs(items, grid):
    nsteps = grid[0] * grid[1]
    in_specs, out_specs, out_shape, args = [], [], [], []
    for w, layer in items:
        n = w.shape[-1]
        per_layer = w[0].size // n
        rows = per_layer // nsteps
        assert rows * nsteps == per_layer and rows % 16 == 0
        in_specs.append(pl.BlockSpec(
            (rows, n), lambda b, i, layer=layer: (layer * nsteps + b * grid[1] + i, 0)))
        out_specs.append(pl.BlockSpec((rows, n), lambda b, i: (b * grid[1] + i, 0)))
        out_shape.append(jax.ShapeDtypeStruct((per_layer, n), BF16))
        args.append(w.reshape(w.shape[0] * per_layer, n))
    return in_specs, out_specs, out_shape, args


def _side_cast(srcs, dsts, paired_chunks=None):
    for n, (src, dst) in enumerate(zip(srcs, dsts)):
        chunks = paired_chunks[n] if paired_chunks else ()
        if not chunks:
            dst[...] = src[...].astype(BF16)
            continue
        quarter = lax.broadcasted_iota(I32, (src.shape[0], 128), 1) // (HEAD_DIM // 2)
        for c in range(src.shape[1] // 128):
            x = src[:, 128 * c:128 * (c + 1)]
            if c in chunks:
                x = jnp.where(quarter == 1, pltpu.roll(x, 128 - HEAD_DIM // 2, 1),
                              jnp.where(quarter == 2, pltpu.roll(x, HEAD_DIM // 2, 1), x))
            dst[:, 128 * c:128 * (c + 1)] = x.astype(BF16)


def _att0_kernel(sink_ref, q_ref, kc_ref, kp_ref, kn_ref, vc_ref, vp_ref, vn_ref, *rest, tq, seq, paired_chunks):
    ncast = len(paired_chunks)
    o_ref = rest[ncast]
    kf, vf = rest[2 * ncast + 1:]
    _side_cast(rest[:ncast], rest[ncast + 1:2 * ncast + 1], paired_chunks)
    i = pl.program_id(1)
    r = A_RADIUS
    ones = jnp.ones((tq + 2 * r, 128), BF16)
    for g in range(A_KV_HEADS):
        lanes = slice(128 * g, 128 * (g + 1))
        kf[g, 0:r] = kp_ref[0, :, lanes]
        kf[g, r:r + tq] = kc_ref[0, :, lanes]
        kf[g, r + tq:r + tq + r] = kn_ref[0, :, lanes]
        vf[g, 0:r, 0:128] = vp_ref[0, :, lanes]
        vf[g, r:r + tq, 0:128] = vc_ref[0, :, lanes]
        vf[g, r + tq:r + tq + r, 0:128] = vn_ref[0, :, lanes]
        vf[g, :, 128:256] = ones

    ii = lax.broadcasted_iota(I32, (r, r), 0)
    jj = lax.broadcasted_iota(I32, (r, r), 1)
    first_head = jj < HEAD_DIM
    q_first = (jj % HEAD_DIM) < (HEAD_DIM // 2)
    zero = jnp.zeros((r, 128), BF16)

    for j in range(tq // r):
        q0 = j * r
        base = i * tq + q0
        bias_lo = jnp.where((jj >= ii) & (base - r + jj >= 0), 0.0, NEG)
        bias_hi = jnp.where((jj <= ii) & (base + r + jj < seq), 0.0, NEG)
        for g in range(A_KV_HEADS):
            kb = kf[g, pl.ds(q0, 3 * r), :]
            vb = vf[g, pl.ds(q0, 3 * r), :]
            qs = []
            for cc in range(2):
                c = 2 * g + cc
                qc = q_ref[0, pl.ds(q0, r), 128 * c:128 * (c + 1)]
                qs += [jnp.where(q_first, qc, zero), jnp.where(q_first, zero, qc)]
            s4 = lax.dot_general(jnp.concatenate(qs, axis=0), kb, (((1,), (1,)), ((), ())),
                                 preferred_element_type=F32)
            ps, ms = [], []
            for t in range(A_GROUP):
                s = s4[r * t:r * (t + 1)]
                a0 = s[:, 0:r] + bias_lo
                a1 = s[:, r:2 * r]
                a2 = s[:, 2 * r:3 * r] + bias_hi
                sink = sink_ref[A_GROUP * g + t] * LOG2E
                m = jnp.max(jnp.maximum(jnp.maximum(a0, a1), a2), axis=-1, keepdims=True)
                m = jnp.maximum(m, sink)
                ps.append(jnp.exp2(jnp.concatenate([a0 - m, a1 - m, a2 - m], axis=1).astype(BF16)))
                ms.append(m)
            pv = jnp.dot(jnp.concatenate(ps, axis=0), vb, preferred_element_type=F32)
            os_ = []
            for t in range(A_GROUP):
                sink = sink_ref[A_GROUP * g + t] * LOG2E
                den = pv[r * t:r * (t + 1), 128:256] + jnp.exp2(sink - ms[t])
                os_.append(pv[r * t:r * (t + 1), 0:128] / den)
            for cc in range(2):
                c = 2 * g + cc
                o_ref[0, pl.ds(q0, r), 128 * c:128 * (c + 1)] = jnp.where(
                    first_head, os_[2 * cc], os_[2 * cc + 1]).astype(BF16)


def _att0(q, k2, v2, sink, cast_items, paired_chunks):
    bsz, s, nq = q.shape
    nkv2 = k2.shape[-1]
    tq = TM_WIDE
    r = A_RADIUS
    per = tq // r
    last = s // r - 1
    grid = (bsz, s // tq)
    cur = lambda w_: pl.BlockSpec((1, tq, w_), lambda b, i: (b, i, 0))
    prev = pl.BlockSpec((1, r, nkv2), lambda b, i: (b, jnp.maximum(i * per - 1, 0), 0))
    nxt = pl.BlockSpec((1, r, nkv2), lambda b, i: (b, jnp.minimum(i * per + per, last), 0))
    w_in, w_out, w_shape, w_args = _side_cast_specs(cast_items, grid)
    o, *cast = pl.pallas_call(
        functools.partial(_att0_kernel, tq=tq, seq=s, paired_chunks=paired_chunks),
        grid=grid,
        in_specs=[pl.BlockSpec(memory_space=pltpu.SMEM),
                  cur(nq), cur(nkv2), prev, nxt, cur(nkv2), prev, nxt] + w_in,
        out_specs=[cur(nq)] + w_out,
        out_shape=[jax.ShapeDtypeStruct((bsz, s, nq), BF16)] + w_shape,
        scratch_shapes=[pltpu.VMEM((A_KV_HEADS, tq + 2 * r, 128), BF16),
                        pltpu.VMEM((A_KV_HEADS, tq + 2 * r, 256), BF16)],
        compiler_params=_cparams("parallel", "parallel"),
        name="att0",
    )(sink, q, k2, k2, k2, v2, v2, v2, *w_args)
    return o, [c.reshape(w.shape[1:]) for c, (w, _) in zip(cast, cast_items)]


def _post_kernel(o_ref, wo_ref, x_ref, gm_ref, gf_ref, scf_ref, shf_ref, rw_ref, rb_ref, upper_ref,
                 x1_ref, h2_ref, cls_ref, rank_ref, cnt_ref, carry):
    mix = jnp.dot(o_ref[0], wo_ref[...], preferred_element_type=F32)
    x1 = x_ref[0] + gm_ref[0] * mix
    x1_ref[0] = x1
    h2 = _rms_mod(x1, gf_ref[...], scf_ref[0], shf_ref[0]).astype(BF16)
    h2_ref[0] = _pack_bf16_pairs(h2.astype(F32))
    tq = h2.shape[0]
    logits = lax.dot_general(rw_ref[...], h2, (((1,), (1,)), ((), ())), preferred_element_type=F32)
    scores = jax.nn.sigmoid(logits)
    biased = scores + rb_ref[...]
    pj = [biased[8 * j:8 * (j + 1)] for j in range(PER_GROUP)]
    sel = []
    for j in range(PER_GROUP):
        beaten = jnp.zeros((8, tq), F32)
        for j2 in range(PER_GROUP):
            if j2 == j:
                continue
            beats = (pj[j2] > pj[j]) | (pj[j2] == pj[j]) if j2 < j else (pj[j2] > pj[j])
            beaten = beaten + jnp.where(beats, 1.0, 0.0)
        sel.append(beaten < 2.0)
    gscore = jnp.zeros((8, tq), F32)
    for j in range(PER_GROUP):
        gscore = gscore + jnp.where(sel[j], pj[j], 0.0)
    gi = lax.broadcasted_iota(I32, (8, tq), 0).astype(F32)
    gmax = jnp.max(gscore, axis=0, keepdims=True)
    gidx = jnp.min(jnp.where(gscore == gmax, gi, 8.0), axis=0, keepdims=True)
    onehot = gi == gidx
    f = [jnp.max(jnp.where(onehot & sel[j], 1.0, 0.0), axis=0, keepdims=True) > 0.5
         for j in range(PER_GROUP)]
    pair = jnp.where(f[0], jnp.where(f[1], 0.0, jnp.where(f[2], 1.0, 2.0)),
                     jnp.where(f[1], jnp.where(f[2], 3.0, 4.0), 5.0))
    cls = (gidx * float(len(PAIRS)) + pair).astype(I32)
    cls_ref[0] = cls

    @pl.when((pl.program_id(0) == 0) & (pl.program_id(1) == 0))
    def _():
        carry[...] = jnp.zeros_like(carry)

    c = carry[...]
    ranks = []
    for blk in range(tq // RANK_BLOCK):
        cls_b = cls[:, RANK_BLOCK * blk:RANK_BLOCK * (blk + 1)]
        oh = lax.broadcasted_iota(I32, (32, RANK_BLOCK), 0) == cls_b
        ohf = jnp.where(oh, 1.0, 0.0)
        within = jnp.dot(ohf.astype(BF16), upper_ref[...], preferred_element_type=F32)
        ranks.append(jnp.sum(jnp.where(oh, within + c[:, 0:1], 0.0), axis=0, keepdims=True))
        c = c + jnp.sum(ohf, axis=1, keepdims=True)
    rank_ref[0] = jnp.concatenate(ranks, axis=1).astype(I32)
    carry[...] = c
    cnt_ref[...] = c


def _post(o, wo, x, gm, gf, scf, shf, rw32, rb32):
    bsz, s, d = x.shape
    do = o.shape[-1]
    tq = TM_WIDE
    nt = s // tq
    tok = lambda w_: pl.BlockSpec((1, tq, w_), lambda b, i: (b, i, 0))
    per_b = pl.BlockSpec((1, 1, d), lambda b, i: (b, 0, 0))
    full = lambda a: pl.BlockSpec(a.shape, lambda b, i: (0,) * a.ndim)
    per_tile = pl.BlockSpec((1, 1, tq), lambda b, i: (b * nt + i, 0, 0))
    gf2 = gf.reshape(1, d)
    upper = (jnp.arange(RANK_BLOCK)[:, None] < jnp.arange(RANK_BLOCK)[None, :]).astype(BF16)
    return pl.pallas_call(
        _post_kernel,
        grid=(bsz, nt),
        in_specs=[tok(do), full(wo), tok(d), per_b, full(gf2), per_b, per_b, full(rw32), full(rb32),
                  full(upper)],
        out_specs=[tok(d), tok(d // 2), per_tile, per_tile, pl.BlockSpec((32, 128), lambda b, i: (0, 0))],
        out_shape=[jax.ShapeDtypeStruct((bsz, s, d), F32),
                   jax.ShapeDtypeStruct((bsz, s, d // 2), U32),
                   jax.ShapeDtypeStruct((bsz * nt, 1, tq), I32),
                   jax.ShapeDtypeStruct((bsz * nt, 1, tq), I32),
                   jax.ShapeDtypeStruct((32, 128), F32)],
        scratch_shapes=[pltpu.VMEM((32, 128), F32)],
        compiler_params=_cparams("arbitrary", "arbitrary"),
        name="post_attention",
    )(o, wo, x, gm, gf2, scf, shf, rw32, rb32, upper)


def _ffn_kernel(e1_ref, e2_ref, nused_ref, hs_ref, *refs):
    ys_ref = refs[-1]
    j = pl.program_id(0)
    tm = FFN_TM

    @pl.when(j * FFN_TILES_PER_STEP < nused_ref[0])
    def _():
        for sub in range(FFN_TILES_PER_STEP):
            wg1, wu1, wd1, rw1, wg2, wu2, wd2, rw2 = refs[8 * sub:8 * (sub + 1)]
            rows = slice(tm * sub, tm * (sub + 1))
            xf = _unpack_bf16_pairs(hs_ref[rows])
            x = xf.astype(BF16)
            gt1 = jnp.dot(x, wg1[0], preferred_element_type=F32)
            up1 = jnp.dot(x, wu1[0], preferred_element_type=F32)
            gt2 = jnp.dot(x, wg2[0], preferred_element_type=F32)
            up2 = jnp.dot(x, wu2[0], preferred_element_type=F32)
            a1 = (_silu(gt1) * up1).astype(BF16)
            y1 = jnp.dot(a1, wd1[0], preferred_element_type=F32)
            a2 = (_silu(gt2) * up2).astype(BF16)
            y2 = jnp.dot(a2, wd2[0], preferred_element_type=F32)
            s1 = jax.nn.sigmoid(jnp.sum(xf * rw1[0].astype(F32), axis=-1, keepdims=True))
            s2 = jax.nn.sigmoid(jnp.sum(xf * rw2[0].astype(F32), axis=-1, keepdims=True))
            den = s1 + s2
            ys_ref[rows] = _pack_bf16_pairs((s1 / den) * y1 + (s2 / den) * y2)

    @pl.when(j * FFN_TILES_PER_STEP >= nused_ref[0])
    def _():
        ys_ref[...] = jnp.zeros_like(ys_ref)


def _ffn(hs, e1, e2, nused, wg, wu, wd, rw3):
    npad, dh = hs.shape
    d = 2 * dh
    per = FFN_TILES_PER_STEP
    rows = per * FFN_TM
    de = wg.shape[-1]
    assert npad % rows == 0
    row = pl.BlockSpec((rows, dh), lambda j, e1, e2, nu: (jnp.minimum(j, (nu[0] - 1) // per), 0))
    w_specs = []
    for sub in range(per):
        for which in range(2):
            pick = lambda j, e1, e2, nu, sub=sub, which=which: ((e2 if which else e1)[per * j + sub], 0, 0)
            w_specs += [pl.BlockSpec((1, d, de), pick), pl.BlockSpec((1, d, de), pick),
                        pl.BlockSpec((1, de, d), pick), pl.BlockSpec((1, 1, d), pick)]
    grid_spec = pltpu.PrefetchScalarGridSpec(
        num_scalar_prefetch=3,
        grid=(npad // rows,),
        in_specs=[row] + w_specs,
        out_specs=pl.BlockSpec((rows, dh), lambda j, e1, e2, nu: (j, 0)),
    )
    return pl.pallas_call(
        _ffn_kernel,
        grid_spec=grid_spec,
        out_shape=jax.ShapeDtypeStruct((npad, dh), U32),
        compiler_params=_cparams("arbitrary"),
        name="expert_ffn",
    )(e1, e2, nused, hs, *([wg, wu, wd, rw3] * (2 * per)))


def _sc_mesh():
    return plsc.VectorSubcoreMesh(core_axis_name="c", subcore_axis_name="s",
                                  num_cores=SC_CORES_V7X, num_subcores=SC_SUBCORES_V7X)


def _sc_steps(n_rows):
    workers = SC_CORES_V7X * SC_SUBCORES_V7X
    per_w = n_rows // workers
    ch = min(SC_ROWS_PER_STEP, per_w // 2)
    n = per_w // ch
    assert per_w * workers == n_rows and n * ch == per_w and n % 2 == 0 and ch % 8 == 0
    return per_w, ch, n


def _sc_scratch(ch, d, dtype):
    return [pltpu.VMEM((2, ch), I32), pltpu.VMEM((2, ch, d), dtype),
            pltpu.SemaphoreType.DMA((2,)), pltpu.SemaphoreType.DMA((2,))]


def _sc_scatter_rows(src, idx, n_out):
    n_in, d = src.shape
    per_w, ch, n = _sc_steps(n_in)

    def body(src_hbm, idx_hbm, out_hbm, idx_v, rows_v, rsem, ssem):
        base = (lax.axis_index("s") * SC_CORES_V7X + lax.axis_index("c")) * per_w

        def read(i, b):
            return pltpu.make_async_copy(src_hbm.at[pl.ds(base + i * ch, ch)], rows_v.at[b], rsem.at[b])

        def scatter(b):
            return pltpu.make_async_copy(rows_v.at[b], out_hbm.at[idx_v.at[b]], ssem.at[b])

        def start_read(i, b):
            pltpu.sync_copy(idx_hbm.at[pl.ds(base + i * ch, ch)], idx_v.at[b])
            read(i, b).start()

        start_read(0, 0)

        @pl.loop(0, n, step=2)
        def _(i):
            for b in (0, 1):
                cur = i + b

                @pl.when(cur + 1 < n)
                def _():
                    @pl.when(cur >= 1)
                    def _():
                        scatter(1 - b).wait()
                    start_read(cur + 1, 1 - b)

                read(cur, b).wait()
                scatter(b).start()

        scatter(0).wait()
        scatter(1).wait()

    return pl.kernel(
        body, mesh=_sc_mesh(),
        out_type=jax.ShapeDtypeStruct((n_out, d), src.dtype),
        scratch_types=_sc_scratch(ch, d, src.dtype),
        name="sc_scatter_rows",
    )(src, idx)


def _sc_gather_rows(table, idx):
    n_out = idx.shape[0]
    d = table.shape[1]
    per_w, ch, n = _sc_steps(n_out)

    def body(table_hbm, idx_hbm, out_hbm, idx_v, rows_v, gsem, wsem):
        base = (lax.axis_index("s") * SC_CORES_V7X + lax.axis_index("c")) * per_w

        def gather(b):
            return pltpu.make_async_copy(table_hbm.at[idx_v.at[b]], rows_v.at[b], gsem.at[b])

        def write(i, b):
            return pltpu.make_async_copy(rows_v.at[b], out_hbm.at[pl.ds(base + i * ch, ch)], wsem.at[b])

        def start_gather(i, b):
            pltpu.sync_copy(idx_hbm.at[pl.ds(base + i * ch, ch)], idx_v.at[b])
            gather(b).start()

        start_gather(0, 0)

        @pl.loop(0, n, step=2)
        def _(i):
            for b in (0, 1):
                cur = i + b

                @pl.when(cur + 1 < n)
                def _():
                    @pl.when(cur >= 1)
                    def _():
                        write(cur - 1, 1 - b).wait()
                    start_gather(cur + 1, 1 - b)

                gather(b).wait()
                write(cur, b).start()

        write(n - 2, 0).wait()
        write(n - 1, 1).wait()

    return pl.kernel(
        body, mesh=_sc_mesh(),
        out_type=jax.ShapeDtypeStruct((n_out, d), table.dtype),
        scratch_types=_sc_scratch(ch, d, table.dtype),
        name="sc_gather_rows",
    )(table, idx)


def _moe(h2p, cls3, rank3, cnt, wg, wu, wd, rw3):
    bsz, s, dh = h2p.shape
    t = bsz * s
    tm = FFN_TM
    counts = cnt[:N_CLASSES, 0].astype(I32)
    padded = ((counts + tm - 1) // tm) * tm
    upto = jnp.arange(N_CLASSES)[:, None] >= jnp.arange(N_CLASSES)[None, :]
    ends = jnp.sum(jnp.where(upto, padded[None, :], 0), axis=1)
    base = ends - padded
    cls2 = cls3[:, 0, :]
    pos2 = rank3[:, 0, :]
    for c in range(N_CLASSES):
        pos2 = pos2 + jnp.where(cls2 == c, base[c], 0)
    pos = pos2.reshape(t)
    ntiles = -(-(t // tm + N_CLASSES) // FFN_TILES_PER_STEP) * FFN_TILES_PER_STEP
    tile_start = jnp.arange(ntiles, dtype=I32) * tm
    tcls = jnp.minimum(jnp.sum((ends[None, :] <= tile_start[:, None]).astype(I32), axis=1), N_CLASSES - 1)
    nused = (ends[-1:] // tm).astype(I32)
    grp = tcls // len(PAIRS)
    pr = tcls % len(PAIRS)
    first = sum(jnp.where(pr == i, p[0], 0) for i, p in enumerate(PAIRS))
    second = sum(jnp.where(pr == i, p[1], 0) for i, p in enumerate(PAIRS))
    e1 = (grp * PER_GROUP + first).astype(I32)
    e2 = (grp * PER_GROUP + second).astype(I32)
    hs = _sc_scatter_rows(h2p.reshape(t, dh), pos, ntiles * tm)
    ys = _ffn(hs, e1, e2, nused, wg, wu, wd, rw3)
    return _sc_gather_rows(ys, pos).reshape(bsz, s, dh)


def _qkv1_kernel(x_ref, y_ref, gfp_ref, g_ref, sc_ref, sh_ref, w_ref, cos_ref, sin_ref,
                 wg_ref, wu_ref, wd_ref, x2_ref, *rest):
    outs = rest[:9]
    _side_cast((wg_ref, wu_ref, wd_ref), rest[9:12])
    hb_scr, ysc, ysc2 = rest[12:]
    tm = x_ref.shape[1]
    x2 = x_ref[0] + gfp_ref[0] * _unpack_bf16_pairs(y_ref[0])
    x2_ref[0] = x2
    hb_scr[...] = _rms_mod(x2, g_ref[...], sc_ref[0], sh_ref[0]).astype(BF16)
    cos = cos_ref[0]
    sin = sin_ref[0]
    for gi, (_, dil) in reversed(list(enumerate(B_GROUPS))):
        for j in range(3):
            c = gi * 3 + j
            y = jnp.dot(hb_scr[...], w_ref[:, B_WIDTH * c:B_WIDTH * (c + 1)], preferred_element_type=F32)
            out = outs[c]
            for cc in range(B_WIDTH // 128):
                lanes = slice(128 * cc, 128 * (cc + 1))
                r = y[:, lanes]
                if j < 2:
                    r = _rope_apply_paired(r, cos, sin)
                if j == 0:
                    r = r * Q_SCALE
                if dil == 1:
                    out[0, 0, :, lanes] = r.astype(BF16)
                    continue
                ysc[cc] = r
                if dil % 16:
                    for rr in range(dil):
                        out[0, rr, :, lanes] = ysc[cc, pl.ds(rr, tm // dil, stride=dil), :].astype(BF16)
                    continue
                quarter = tm // 4
                inner = dil // 4
                for r1 in range(4):
                    ysc2[cc, r1 * quarter:(r1 + 1) * quarter] = ysc[cc, pl.ds(r1, quarter, stride=4), :]
                for r1 in range(4):
                    for r2 in range(inner):
                        out[0, r1 + 4 * r2, :, lanes] = ysc2[
                            cc, pl.ds(r1 * quarter + r2, tm // dil, stride=inner), :].astype(BF16)


def _qkv1(x, y, gfp, g, sc, sh, w, cos, sin, expert_ws, layer):
    bsz, s, d = x.shape
    tm = TM
    grid = (bsz, s // tm)
    w_in, w_out, w_shape, w_args = _side_cast_specs([(w_, layer) for w_ in expert_ws], grid)
    tok = lambda w_: pl.BlockSpec((1, tm, w_), lambda b, i: (b, i, 0))
    per_b = pl.BlockSpec((1, 1, d), lambda b, i: (b, 0, 0))
    out_specs = [tok(d)]
    out_shape = [jax.ShapeDtypeStruct((bsz, s, d), F32)]
    for _, dil in B_GROUPS:
        for _ in range(3):
            out_specs.append(pl.BlockSpec((1, dil, tm // dil, B_WIDTH), lambda b, i: (b, 0, i, 0)))
            out_shape.append(jax.ShapeDtypeStruct((bsz, dil, s // dil, B_WIDTH), BF16))
    outs = pl.pallas_call(
        _qkv1_kernel,
        grid=grid,
        in_specs=[tok(d), tok(d // 2), per_b, pl.BlockSpec((1, d), lambda b, i: (0, 0)), per_b, per_b,
                  pl.BlockSpec(w.shape, lambda b, i: (0, 0)), tok(128), tok(128)] + w_in,
        out_specs=out_specs + w_out,
        out_shape=out_shape + w_shape,
        scratch_shapes=[pltpu.VMEM((tm, d), BF16), pltpu.VMEM((B_WIDTH // 128, tm, 128), F32),
                        pltpu.VMEM((B_WIDTH // 128, tm, 128), F32)],
        compiler_params=_cparams("parallel", "parallel"),
        name="qkv1",
    )(x, y, gfp, g.reshape(1, d), sc, sh, w, cos, sin, *w_args)
    return outs[:10], [c.reshape(w_.shape[1:]) for c, w_ in zip(outs[10:], expert_ws)]


def _att1_window_pad(lseg):
    qb = min(2 * B_RADIUS, lseg)
    return -(qb + 2 * B_RADIUS) % 128


def _att1_kernel(*refs, chunk, seq):
    ins = refs[:21]
    o_ref = refs[21]
    scr = refs[22:]
    nat_o, nat_d, nat_m = scr[0:3], scr[3:6], scr[6:9]
    n = pl.program_id(1)
    rad = B_RADIUS
    npair = B_WIDTH // 128

    def window(cur, prev, nxt, res, lo, hi, lseg, lanes):
        parts = []
        if lo < 0:
            parts.append(prev[0, res, :, lanes])
        parts.append(cur[0, res, max(lo, 0):min(hi, lseg), lanes])
        if hi > lseg:
            parts.append(nxt[0, res, :, lanes])
        if hi > lseg + rad:
            parts.append(nxt[0, res, :, lanes])
        return jnp.concatenate(parts, axis=0) if len(parts) > 1 else parts[0]

    for c in range(npair):
        lanes = slice(128 * c, 128 * (c + 1))
        for gi, (_, dil) in enumerate(B_GROUPS):
            q_ref, kc, kp, kn, vc, vp, vn = ins[7 * gi:7 * gi + 7]
            lseg = chunk // dil
            qb = min(2 * rad, lseg)
            kb_n = qb + 2 * rad + _att1_window_pad(lseg)
            lstr = seq // dil
            rel = lax.broadcasted_iota(I32, (qb, kb_n), 1) - lax.broadcasted_iota(I32, (qb, kb_n), 0)
            band_bias = jnp.where((rel >= 0) & (rel <= 2 * rad), 0.0, NEG)
            jrow = lax.broadcasted_iota(I32, (1, kb_n), 1)
            lane = lax.broadcasted_iota(I32, (qb, 128), 1)
            first_head = lane < HEAD_DIM
            q_first = (lane % HEAD_DIM) < (HEAD_DIM // 2)
            zero = jnp.zeros((qb, 128), BF16)
            ones = jnp.ones((kb_n, 128), BF16)
            for res in range(dil):
                for q0 in range(0, lseg, qb):
                    apos = n * lseg + q0 - rad + jrow
                    bias = band_bias + jnp.where((apos >= 0) & (apos < lstr), 0.0, NEG)
                    rows = pl.ds(q0, qb) if dil == 1 else pl.ds(res + dil * q0, qb, stride=dil)
                    qc = q_ref[0, res, q0:q0 + qb, lanes]
                    kb = window(kc, kp, kn, res, q0 - rad, q0 - rad + kb_n, lseg, lanes)
                    vb = jnp.concatenate(
                        [window(vc, vp, vn, res, q0 - rad, q0 - rad + kb_n, lseg, lanes), ones], axis=1)
                    qs = jnp.concatenate(
                        [jnp.where(q_first, qc, zero), jnp.where(q_first, zero, qc)], axis=0)
                    s2 = lax.dot_general(qs, kb, (((1,), (1,)), ((), ())), preferred_element_type=F32)
                    ps, ms = [], []
                    for t2 in range(2):
                        s = s2[qb * t2:qb * (t2 + 1)] + bias
                        m = jnp.max(s, axis=-1, keepdims=True)
                        ps.append(jnp.exp2((s - m).astype(BF16)))
                        ms.append(m)
                    pv = jnp.dot(jnp.concatenate(ps, axis=0), vb, preferred_element_type=F32)
                    nat_o[gi][rows, :] = jnp.where(first_head, pv[0:qb, 0:128], pv[qb:2 * qb, 0:128])
                    nat_d[gi][rows, :] = jnp.where(first_head, pv[0:qb, 128:256], pv[qb:2 * qb, 128:256])
                    nat_m[gi][rows, :] = jnp.where(first_head, ms[0], ms[1])

        ms = [nat_m[gi][...] for gi in range(3)]
        mx = jnp.maximum(jnp.maximum(ms[0], ms[1]), ms[2])
        w = [jnp.exp2(x - mx) for x in ms]
        num = w[0] * nat_o[0][...] + w[1] * nat_o[1][...] + w[2] * nat_o[2][...]
        den = w[0] * nat_d[0][...] + w[1] * nat_d[1][...] + w[2] * nat_d[2][...]
        o_ref[0, :, lanes] = (num / den).astype(BF16)


def _att1(qkv):
    bsz = qkv[0].shape[0]
    seq = qkv[0].shape[1] * qkv[0].shape[2]
    chunk = ATT1_CHUNK
    rad = B_RADIUS
    in_specs, args, scratch = [], [], []
    for gi, (_, dil) in enumerate(B_GROUPS):
        q, k, v = qkv[3 * gi:3 * gi + 3]
        lseg = chunk // dil
        per = lseg // rad
        last = seq // dil // rad - 1
        cur = pl.BlockSpec((1, dil, lseg, B_WIDTH), lambda b, n: (b, 0, n, 0))
        prev = pl.BlockSpec((1, dil, rad, B_WIDTH),
                            lambda b, n, per=per: (b, 0, jnp.maximum(n * per - 1, 0), 0))
        nxt = pl.BlockSpec((1, dil, rad, B_WIDTH),
                           lambda b, n, per=per, last=last: (b, 0, jnp.minimum(n * per + per, last), 0))
        in_specs += [cur, cur, prev, nxt, cur, prev, nxt]
        args += [q, k, k, k, v, v, v]
    for _ in range(3):
        for _ in B_GROUPS:
            scratch.append(pltpu.VMEM((chunk, 128), F32))
    return pl.pallas_call(
        functools.partial(_att1_kernel, chunk=chunk, seq=seq),
        grid=(bsz, seq // chunk),
        in_specs=in_specs,
        out_specs=pl.BlockSpec((1, chunk, B_WIDTH), lambda b, n: (b, n, 0)),
        out_shape=jax.ShapeDtypeStruct((bsz, seq, B_WIDTH), BF16),
        scratch_shapes=scratch,
        compiler_params=_cparams("parallel", "parallel"),
        name="att1",
    )(*args)


def _final_kernel(x_ref, y_ref, gfp_ref, g_ref, o_ref):
    x = x_ref[0] + gfp_ref[0] * _unpack_bf16_pairs(y_ref[0])
    ms = jnp.mean(x * x, axis=-1, keepdims=True)
    o_ref[0] = (x * lax.rsqrt(ms + NORM_EPS)) * g_ref[...]


def _final(x, y, gfp, g):
    bsz, s, d = x.shape
    tm = TM_WIDE
    tok = pl.BlockSpec((1, tm, d), lambda b, i: (b, i, 0))
    return pl.pallas_call(
        _final_kernel,
        grid=(bsz, s // tm),
        in_specs=[tok, pl.BlockSpec((1, tm, d // 2), lambda b, i: (b, i, 0)),
                  pl.BlockSpec((1, 1, d), lambda b, i: (b, 0, 0)),
                  pl.BlockSpec((1, d), lambda b, i: (0, 0))],
        out_specs=tok,
        out_shape=jax.ShapeDtypeStruct((bsz, s, d), F32),
        compiler_params=_cparams("parallel", "parallel"),
        name="final_norm",
    )(x, y, gfp, g.reshape(1, d))


def kernel(x, c, positions, ada_w, ada_b, norm_mix_g, norm_ffn_g, a_w_qkv, a_w_o, a_sink, b_w_qkv, b_w_o,
           router_w, router_bias, exp_w_gate, exp_w_up, exp_w_down, final_norm_g):
    bsz, s, d = x.shape
    assert d == D_MODEL and ada_w.shape[0] == 2 and s % ATT1_CHUNK == 0
    assert all(w_ // (2 * dil) == B_RADIUS for w_, dil in B_GROUPS)

    mod = _modulation(c, ada_w, ada_b)
    mods = [[mod[l][:, k * d:(k + 1) * d].reshape(bsz, 1, d) for k in range(6)] for l in range(2)]
    cos, sin = _rope_tables(positions)

    rw_t = router_w.T.astype(BF16)
    rows = jnp.array([PER_GROUP * g + j if g < N_EXPERT_GROUPS else 0
                      for j in range(PER_GROUP) for g in range(8)], I32)
    live = jnp.array([1.0 if g < N_EXPERT_GROUPS else 0.0
                      for j in range(PER_GROUP) for g in range(8)], F32)
    rw32 = (rw_t[rows].astype(F32) * live[:, None]).astype(BF16)
    rb32 = jnp.where(live > 0, router_bias.astype(F32)[rows], NEG).reshape(32, 1)
    rw3 = rw_t.reshape(N_EXPERTS, 1, d)
    expert_ws = (exp_w_gate, exp_w_up, exp_w_down)

    sh_m, sc_m, g_m, sh_f, sc_f, g_f = mods[0]
    nq, nkv = A_Q_HEADS * HEAD_DIM, A_KV_HEADS * HEAD_DIM
    w0 = a_w_qkv[0].astype(BF16)
    w0 = jnp.concatenate([_paired_layout(w0[:, :nq]), _paired_layout(w0[:, nq:nq + nkv]), w0[:, nq + nkv:]],
                         axis=1)
    q, k, v = _qkv0(x, norm_mix_g[0], sc_m, sh_m, w0, cos, sin)
    qk_chunks = tuple(c for c in range(b_w_qkv.shape[-1] // 128) if (c * 128 // B_WIDTH) % 3 < 2)
    o, cast = _att0(q, k, v, a_sink[0].astype(F32),
                    [(w_, 0) for w_ in expert_ws] + [(a_w_o, 0), (b_w_o, 0), (b_w_qkv, 0)],
                    ((), (), (), (), (), qk_chunks))
    experts, (a_wo, b_wo, b_wqkv) = cast[:3], cast[3:]
    x1, *routed = _post(o, a_wo, x, g_m, norm_ffn_g[0], sc_f, sh_f, rw32, rb32)
    y = _moe(*routed, *experts, rw3)
    g_f_prev = g_f

    sh_m, sc_m, g_m, sh_f, sc_f, g_f = mods[1]
    outs, experts = _qkv1(x1, y, g_f_prev, norm_mix_g[1], sc_m, sh_m, b_wqkv, cos, sin, expert_ws, 1)
    x2, qkv = outs[0], outs[1:]
    o = _att1(qkv)
    x3, *routed = _post(o, b_wo, x2, g_m, norm_ffn_g[1], sc_f, sh_f, rw32, rb32)
    y = _moe(*routed, *experts, rw3)

    return _final(x3, y, g_f, final_norm_g)
```

```python
import functools

import jax
import jax.numpy as jnp
from jax import lax
from jax.experimental import pallas as pl
from jax.experimental.pallas import tpu as pltpu
from jax.experimental.pallas import tpu_sc as plsc

F32, BF16, I32, U32 = jnp.float32, jnp.bfloat16, jnp.int32, jnp.uint32

D_MODEL = 1024
HEAD_DIM = 64
ROPE_THETA = 10000.0
NORM_EPS = 1e-6
A_Q_HEADS = 16
A_KV_HEADS = 4
A_GROUP = A_Q_HEADS // A_KV_HEADS
A_RADIUS = 128
B_GROUPS = ((128, 1), (512, 4), (2048, 16))
B_HEADS = 8
B_RADIUS = 64
B_WIDTH = B_HEADS * HEAD_DIM
N_EXPERTS = 16
N_EXPERT_GROUPS = 4
PER_GROUP = N_EXPERTS // N_EXPERT_GROUPS
D_EXPERT = D_MODEL // 2
PAIRS = ((0, 1), (0, 2), (0, 3), (1, 2), (1, 3), (2, 3))
N_CLASSES = N_EXPERT_GROUPS * len(PAIRS)
NEG = -1e30
LOG2E = 1.4426950408889634
Q_SCALE = HEAD_DIM ** -0.5 * LOG2E

VMEM_LIMIT = 56 * 1024 * 1024
TM = 512
ROPE_TS = 4096
FINAL_TM = 2048
TM_WIDE = 1024
RANK_BLOCK = 512
ATT1_CHUNK = 1024
FFN_TM = 256
FFN_TILES_PER_STEP = 2
SC_CORES_V7X = 2
SC_SUBCORES_V7X = 16
SC_ROWS_PER_STEP = 64


def _cparams(*sem):
    return pltpu.CompilerParams(dimension_semantics=sem, vmem_limit_bytes=VMEM_LIMIT)


def _rms_mod(x, g, sc, sh):
    ms = jnp.mean(x * x, axis=-1, keepdims=True)
    return (x * lax.rsqrt(ms + NORM_EPS)) * (g * (1.0 + sc)) + sh


def _rope_apply_paired(y, cos, sin_paired):
    return y * cos + pltpu.roll(y, HEAD_DIM, 1) * sin_paired


def _paired_layout(w):
    k, n = w.shape
    half = HEAD_DIM // 2
    return w.reshape(k, n // 128, 2, 2, half).transpose(0, 1, 3, 2, 4).reshape(k, n)


def _silu(x):
    half = 0.5 * x
    return half + half * jnp.tanh(half)


def _pack_bf16_pairs(x):
    n = x.shape[1] // 2
    lo = lax.bitcast_convert_type(x[:, :n].astype(BF16).astype(F32), U32)
    hi = lax.bitcast_convert_type(x[:, n:].astype(BF16).astype(F32), U32)
    return (hi & jnp.uint32(0xFFFF0000)) | (lo >> 16)


def _unpack_bf16_pairs(p):
    lo = lax.bitcast_convert_type(p << 16, F32)
    hi = lax.bitcast_convert_type(p & jnp.uint32(0xFFFF0000), F32)
    return jnp.concatenate([lo, hi], axis=1)


def _dup_head(x, first_head, which):
    rolled = pltpu.roll(x, HEAD_DIM, 1)
    return jnp.where(first_head, x, rolled) if which == 0 else jnp.where(first_head, rolled, x)


def _mod_kernel(ct_ref, w_ref, b_ref, o_ref, *, bsz):
    ct = ct_ref[...]
    s = ct * jax.nn.sigmoid(ct)
    w = w_ref[0]
    rows = [jnp.sum(w * s[:, b:b + 1], axis=0, keepdims=True) + b_ref[0] for b in range(bsz)]
    o_ref[0] = jnp.concatenate(rows + [jnp.zeros_like(rows[0])] * (8 - bsz), axis=0)


def _modulation(c, ada_w, ada_b):
    depth, d, n = ada_w.shape
    bsz = c.shape[0]
    assert bsz <= 8
    tn = 1536
    out = pl.pallas_call(
        functools.partial(_mod_kernel, bsz=bsz),
        grid=(depth, n // tn),
        in_specs=[pl.BlockSpec((d, bsz), lambda l, j: (0, 0)),
                  pl.BlockSpec((1, d, tn), lambda l, j: (l, 0, j)),
                  pl.BlockSpec((1, 1, tn), lambda l, j: (l, 0, j))],
        out_specs=pl.BlockSpec((1, 8, tn), lambda l, j: (l, 0, j)),
        out_shape=jax.ShapeDtypeStruct((depth, 8, n), F32),
        compiler_params=_cparams("arbitrary", "arbitrary"),
        name="modulation",
    )(c.T, ada_w, ada_b.reshape(depth, 1, n))
    return out[:, :bsz]


def _rope_kernel(pos_ref, invf_ref, sign_ref, cos_ref, sin_ref):
    p = pos_ref[0].astype(F32)
    n = p.shape[0]
    quarter = lax.broadcasted_iota(I32, (n, 128), 1) // (HEAD_DIM // 2)
    pos4 = jnp.where(quarter == 0, p[:, 0:1],
                     jnp.where(quarter == 1, p[:, 1:2], jnp.where(quarter == 2, p[:, 2:3], p[:, 3:4])))
    ang = pos4 * invf_ref[...]
    dense = (jnp.cos(ang), jnp.sin(ang))
    for u in range(4):
        spread = []
        for d in dense:
            e = jnp.where(quarter == u, d, 0.0)
            e = e + pltpu.roll(e, HEAD_DIM // 2, 1)
            spread.append(e + pltpu.roll(e, HEAD_DIM, 1))
        rows = pl.ds(u, n, stride=4)
        cos_ref[0, rows, :] = spread[0]
        sin_ref[0, rows, :] = spread[1] * sign_ref[...]


def _rope_tables(positions):
    bsz, s = positions.shape
    inv_freq = ROPE_THETA ** (-jnp.arange(0, HEAD_DIM, 2, dtype=F32) / HEAD_DIM)
    invf = jnp.tile(inv_freq, 4).reshape(1, 128)
    sign = jnp.concatenate([-jnp.ones((HEAD_DIM,), F32), jnp.ones((HEAD_DIM,), F32)]).reshape(1, 128)
    ts = min(ROPE_TS, s)
    return pl.pallas_call(
        _rope_kernel,
        grid=(bsz, s // ts),
        in_specs=[pl.BlockSpec((1, ts // 4, 4), lambda b, i: (b, i, 0)),
                  pl.BlockSpec((1, 128), lambda b, i: (0, 0)),
                  pl.BlockSpec((1, 128), lambda b, i: (0, 0))],
        out_specs=[pl.BlockSpec((1, ts, 128), lambda b, i: (b, i, 0))] * 2,
        out_shape=[jax.ShapeDtypeStruct((bsz, s, 128), F32)] * 2,
        compiler_params=_cparams("parallel", "parallel"),
        name="rope_tables",
    )(positions.reshape(bsz, s // 4, 4), invf, sign)


def _qkv0_kernel(x_ref, g_ref, sc_ref, sh_ref, w_ref, cos_ref, sin_ref,
                 q_ref, k_ref, v_ref, hb_scr):
    hb_scr[...] = _rms_mod(x_ref[0], g_ref[...], sc_ref[0], sh_ref[0]).astype(BF16)
    cos = cos_ref[0]
    sin = sin_ref[0]
    lane = lax.broadcasted_iota(I32, cos.shape, 1)
    x_lo = (lane % HEAD_DIM) < (HEAD_DIM // 2)
    nq = A_Q_HEADS * HEAD_DIM
    nkv = A_KV_HEADS * HEAD_DIM
    for c in range(nq // 256):
        y = jnp.dot(hb_scr[...], w_ref[:, 256 * c:256 * (c + 1)], preferred_element_type=F32)
        for cc in range(2):
            r = _rope_apply_paired(y[:, 128 * cc:128 * (cc + 1)], cos, sin) * Q_SCALE
            q_ref[0, :, 256 * c + 128 * cc:256 * c + 128 * (cc + 1)] = r.astype(BF16)
    first_head = lane < HEAD_DIM
    y = jnp.dot(hb_scr[...], w_ref[:, nq:nq + nkv], preferred_element_type=F32)
    for cc in range(nkv // 128):
        r = _rope_apply_paired(y[:, 128 * cc:128 * (cc + 1)], cos, sin)
        k_ref[0, :, 256 * cc:256 * cc + 128] = jnp.where(
            x_lo, r, pltpu.roll(r, HEAD_DIM // 2, 1)).astype(BF16)
        k_ref[0, :, 256 * cc + 128:256 * cc + 256] = jnp.where(
            x_lo, pltpu.roll(r, 128 - HEAD_DIM // 2, 1), r).astype(BF16)
    y = jnp.dot(hb_scr[...], w_ref[:, nq + nkv:nq + 2 * nkv], preferred_element_type=F32)
    for cc in range(nkv // 128):
        r = y[:, 128 * cc:128 * (cc + 1)]
        for which in range(2):
            g = 2 * cc + which
            v_ref[0, :, 128 * g:128 * (g + 1)] = _dup_head(r, first_head, which).astype(BF16)


def _qkv0(x, g, sc, sh, w, cos, sin):
    bsz, s, d = x.shape
    nq = A_Q_HEADS * HEAD_DIM
    nkv = A_KV_HEADS * HEAD_DIM
    tm = TM_WIDE
    tok = lambda w_: pl.BlockSpec((1, tm, w_), lambda b, i: (b, i, 0))
    per_b = pl.BlockSpec((1, 1, d), lambda b, i: (b, 0, 0))
    return pl.pallas_call(
        _qkv0_kernel,
        grid=(bsz, s // tm),
        in_specs=[tok(d), pl.BlockSpec((1, d), lambda b, i: (0, 0)), per_b, per_b,
                  pl.BlockSpec(w.shape, lambda b, i: (0, 0)), tok(128), tok(128)],
        out_specs=[tok(nq), tok(2 * nkv), tok(2 * nkv)],
        out_shape=[jax.ShapeDtypeStruct((bsz, s, nq), BF16),
                   jax.ShapeDtypeStruct((bsz, s, 2 * nkv), BF16),
                   jax.ShapeDtypeStruct((bsz, s, 2 * nkv), BF16)],
        scratch_shapes=[pltpu.VMEM((tm, d), BF16)],
        compiler_params=_cparams("parallel", "parallel"),
        name="qkv0",
    )(x, g.reshape(1, d), sc, sh, w, cos, sin)


def _side_cast_specs(items, grid):
    nsteps = grid[0] * grid[1]
    in_specs, out_specs, out_shape, args = [], [], [], []
    for w, layer in items:
        n = w.shape[-1]
        per_layer = w[0].size // n
        rows = per_layer // nsteps
        assert rows * nsteps == per_layer and rows % 16 == 0
        in_specs.append(pl.BlockSpec(
            (rows, n), lambda b, i, layer=layer: (layer * nsteps + b * grid[1] + i, 0)))
        out_specs.append(pl.BlockSpec((rows, n), lambda b, i: (b * grid[1] + i, 0)))
        out_shape.append(jax.ShapeDtypeStruct((per_layer, n), BF16))
        args.append(w.reshape(w.shape[0] * per_layer, n))
    return in_specs, out_specs, out_shape, args


def _side_cast(srcs, dsts, paired_chunks=None):
    for n, (src, dst) in enumerate(zip(srcs, dsts)):
        chunks = paired_chunks[n] if paired_chunks else ()
        if not chunks:
            dst[...] = src[...].astype(BF16)
            continue
        quarter = lax.broadcasted_iota(I32, (src.shape[0], 128), 1) // (HEAD_DIM // 2)
        for c in range(src.shape[1] // 128):
            x = src[:, 128 * c:128 * (c + 1)]
            if c in chunks:
                x = jnp.where(quarter == 1, pltpu.roll(x, 128 - HEAD_DIM // 2, 1),
                              jnp.where(quarter == 2, pltpu.roll(x, HEAD_DIM // 2, 1), x))
            dst[:, 128 * c:128 * (c + 1)] = x.astype(BF16)


def _att0_kernel(sink_ref, q_ref, kc_ref, kp_ref, kn_ref, vc_ref, vp_ref, vn_ref, *rest, tq, seq, paired_chunks):
    ncast = len(paired_chunks)
    o_ref = rest[ncast]
    kf, vf = rest[2 * ncast + 1:]
    _side_cast(rest[:ncast], rest[ncast + 1:2 * ncast + 1], paired_chunks)
    i = pl.program_id(1)
    r = A_RADIUS
    ones = jnp.ones((tq + 2 * r, 128), BF16)
    for g in range(A_KV_HEADS):
        lanes = slice(128 * g, 128 * (g + 1))
        kf[g, 0:r] = kp_ref[0, :, lanes]
        kf[g, r:r + tq] = kc_ref[0, :, lanes]
        kf[g, r + tq:r + tq + r] = kn_ref[0, :, lanes]
        vf[g, 0:r, 0:128] = vp_ref[0, :, lanes]
        vf[g, r:r + tq, 0:128] = vc_ref[0, :, lanes]
        vf[g, r + tq:r + tq + r, 0:128] = vn_ref[0, :, lanes]
        vf[g, :, 128:256] = ones

    ii = lax.broadcasted_iota(I32, (r, r), 0)
    jj = lax.broadcasted_iota(I32, (r, r), 1)
    first_head = jj < HEAD_DIM
    q_first = (jj % HEAD_DIM) < (HEAD_DIM // 2)
    zero = jnp.zeros((r, 128), BF16)

    for j in range(tq // r):
        q0 = j * r
        base = i * tq + q0
        bias_lo = jnp.where((jj >= ii) & (base - r + jj >= 0), 0.0, NEG)
        bias_hi = jnp.where((jj <= ii) & (base + r + jj < seq), 0.0, NEG)
        for g in range(A_KV_HEADS):
            kb = kf[g, pl.ds(q0, 3 * r), :]
            vb = vf[g, pl.ds(q0, 3 * r), :]
            qs = []
            for cc in range(2):
                c = 2 * g + cc
                qc = q_ref[0, pl.ds(q0, r), 128 * c:128 * (c + 1)]
                qs += [jnp.where(q_first, qc, zero), jnp.where(q_first, zero, qc)]
            s4 = lax.dot_general(jnp.concatenate(qs, axis=0), kb, (((1,), (1,)), ((), ())),
                                 preferred_element_type=F32)
            ps, ms = [], []
            for t in range(A_GROUP):
                s = s4[r * t:r * (t + 1)]
                a0 = s[:, 0:r] + bias_lo
                a1 = s[:, r:2 * r]
                a2 = s[:, 2 * r:3 * r] + bias_hi
                sink = sink_ref[A_GROUP * g + t] * LOG2E
                m = jnp.max(jnp.maximum(jnp.maximum(a0, a1), a2), axis=-1, keepdims=True)
                m = jnp.maximum(m, sink)
                ps.append(jnp.exp2(jnp.concatenate([a0 - m, a1 - m, a2 - m], axis=1).astype(BF16)))
                ms.append(m)
            pv = jnp.dot(jnp.concatenate(ps, axis=0), vb, preferred_element_type=F32)
            os_ = []
            for t in range(A_GROUP):
                sink = sink_ref[A_GROUP * g + t] * LOG2E
                den = pv[r * t:r * (t + 1), 128:256] + jnp.exp2(sink - ms[t])
                os_.append(pv[r * t:r * (t + 1), 0:128] / den)
            for cc in range(2):
                c = 2 * g + cc
                o_ref[0, pl.ds(q0, r), 128 * c:128 * (c + 1)] = jnp.where(
                    first_head, os_[2 * cc], os_[2 * cc + 1]).astype(BF16)


def _att0(q, k2, v2, sink, cast_items, paired_chunks):
    bsz, s, nq = q.shape
    nkv2 = k2.shape[-1]
    tq = TM_WIDE
    r = A_RADIUS
    per = tq // r
    last = s // r - 1
    grid = (bsz, s // tq)
    cur = lambda w_: pl.BlockSpec((1, tq, w_), lambda b, i: (b, i, 0))
    prev = pl.BlockSpec((1, r, nkv2), lambda b, i: (b, jnp.maximum(i * per - 1, 0), 0))
    nxt = pl.BlockSpec((1, r, nkv2), lambda b, i: (b, jnp.minimum(i * per + per, last), 0))
    w_in, w_out, w_shape, w_args = _side_cast_specs(cast_items, grid)
    o, *cast = pl.pallas_call(
        functools.partial(_att0_kernel, tq=tq, seq=s, paired_chunks=paired_chunks),
        grid=grid,
        in_specs=[pl.BlockSpec(memory_space=pltpu.SMEM),
                  cur(nq), cur(nkv2), prev, nxt, cur(nkv2), prev, nxt] + w_in,
        out_specs=[cur(nq)] + w_out,
        out_shape=[jax.ShapeDtypeStruct((bsz, s, nq), BF16)] + w_shape,
        scratch_shapes=[pltpu.VMEM((A_KV_HEADS, tq + 2 * r, 128), BF16),
                        pltpu.VMEM((A_KV_HEADS, tq + 2 * r, 256), BF16)],
        compiler_params=_cparams("parallel", "parallel"),
        name="att0",
    )(sink, q, k2, k2, k2, v2, v2, v2, *w_args)
    return o, [c.reshape(w.shape[1:]) for c, (w, _) in zip(cast, cast_items)]


def _post_kernel(o_ref, wo_ref, x_ref, gm_ref, gf_ref, scf_ref, shf_ref, rw_ref, rb_ref, upper_ref,
                 x1_ref, h2_ref, cls_ref, rank_ref, cnt_ref, carry):
    mix = jnp.dot(o_ref[0], wo_ref[...], preferred_element_type=F32)
    x1 = x_ref[0] + gm_ref[0] * mix
    x1_ref[0] = x1
    h2 = _rms_mod(x1, gf_ref[...], scf_ref[0], shf_ref[0]).astype(BF16)
    h2_ref[0] = _pack_bf16_pairs(h2.astype(F32))
    tq = h2.shape[0]
    logits = lax.dot_general(rw_ref[...], h2, (((1,), (1,)), ((), ())), preferred_element_type=F32)
    scores = jax.nn.sigmoid(logits)
    biased = scores + rb_ref[...]
    pj = [biased[8 * j:8 * (j + 1)] for j in range(PER_GROUP)]
    sel = []
    for j in range(PER_GROUP):
        beaten = jnp.zeros((8, tq), F32)
        for j2 in range(PER_GROUP):
            if j2 == j:
                continue
            beats = (pj[j2] > pj[j]) | (pj[j2] == pj[j]) if j2 < j else (pj[j2] > pj[j])
            beaten = beaten + jnp.where(beats, 1.0, 0.0)
        sel.append(beaten < 2.0)
    gscore = jnp.zeros((8, tq), F32)
    for j in range(PER_GROUP):
        gscore = gscore + jnp.where(sel[j], pj[j], 0.0)
    gi = lax.broadcasted_iota(I32, (8, tq), 0).astype(F32)
    gmax = jnp.max(gscore, axis=0, keepdims=True)
    gidx = jnp.min(jnp.where(gscore == gmax, gi, 8.0), axis=0, keepdims=True)
    onehot = gi == gidx
    f = [jnp.max(jnp.where(onehot & sel[j], 1.0, 0.0), axis=0, keepdims=True) > 0.5
         for j in range(PER_GROUP)]
    pair = jnp.where(f[0], jnp.where(f[1], 0.0, jnp.where(f[2], 1.0, 2.0)),
                     jnp.where(f[1], jnp.where(f[2], 3.0, 4.0), 5.0))
    cls = (gidx * float(len(PAIRS)) + pair).astype(I32)
    cls_ref[0] = cls

    @pl.when((pl.program_id(0) == 0) & (pl.program_id(1) == 0))
    def _():
        carry[...] = jnp.zeros_like(carry)

    c = carry[...]
    ranks = []
    for blk in range(tq // RANK_BLOCK):
        cls_b = cls[:, RANK_BLOCK * blk:RANK_BLOCK * (blk + 1)]
        oh = lax.broadcasted_iota(I32, (32, RANK_BLOCK), 0) == cls_b
        ohf = jnp.where(oh, 1.0, 0.0)
        within = jnp.dot(ohf.astype(BF16), upper_ref[...], preferred_element_type=F32)
        ranks.append(jnp.sum(jnp.where(oh, within + c[:, 0:1], 0.0), axis=0, keepdims=True))
        c = c + jnp.sum(ohf, axis=1, keepdims=True)
    rank_ref[0] = jnp.concatenate(ranks, axis=1).astype(I32)
    carry[...] = c
    cnt_ref[...] = c


def _post(o, wo, x, gm, gf, scf, shf, rw32, rb32):
    bsz, s, d = x.shape
    do = o.shape[-1]
    tq = TM_WIDE
    nt = s // tq
    tok = lambda w_: pl.BlockSpec((1, tq, w_), lambda b, i: (b, i, 0))
    per_b = pl.BlockSpec((1, 1, d), lambda b, i: (b, 0, 0))
    full = lambda a: pl.BlockSpec(a.shape, lambda b, i: (0,) * a.ndim)
    per_tile = pl.BlockSpec((1, 1, tq), lambda b, i: (b * nt + i, 0, 0))
    gf2 = gf.reshape(1, d)
    upper = (jnp.arange(RANK_BLOCK)[:, None] < jnp.arange(RANK_BLOCK)[None, :]).astype(BF16)
    return pl.pallas_call(
        _post_kernel,
        grid=(bsz, nt),
        in_specs=[tok(do), full(wo), tok(d), per_b, full(gf2), per_b, per_b, full(rw32), full(rb32),
                  full(upper)],
        out_specs=[tok(d), tok(d // 2), per_tile, per_tile, pl.BlockSpec((32, 128), lambda b, i: (0, 0))],
        out_shape=[jax.ShapeDtypeStruct((bsz, s, d), F32),
                   jax.ShapeDtypeStruct((bsz, s, d // 2), U32),
                   jax.ShapeDtypeStruct((bsz * nt, 1, tq), I32),
                   jax.ShapeDtypeStruct((bsz * nt, 1, tq), I32),
                   jax.ShapeDtypeStruct((32, 128), F32)],
        scratch_shapes=[pltpu.VMEM((32, 128), F32)],
        compiler_params=_cparams("arbitrary", "arbitrary"),
        name="post_attention",
    )(o, wo, x, gm, gf2, scf, shf, rw32, rb32, upper)


def _ffn_kernel(e1_ref, e2_ref, nused_ref, hs_ref, *refs):
    ys_ref = refs[-1]
    j = pl.program_id(0)
    tm = FFN_TM

    @pl.when(j * FFN_TILES_PER_STEP < nused_ref[0])
    def _():
        for sub in range(FFN_TILES_PER_STEP):
            wg1, wu1, wd1, rw1, wg2, wu2, wd2, rw2 = refs[8 * sub:8 * (sub + 1)]
            rows = slice(tm * sub, tm * (sub + 1))
            xf = _unpack_bf16_pairs(hs_ref[rows])
            x = xf.astype(BF16)
            gt1 = jnp.dot(x, wg1[0], preferred_element_type=F32)
            up1 = jnp.dot(x, wu1[0], preferred_element_type=F32)
            gt2 = jnp.dot(x, wg2[0], preferred_element_type=F32)
            up2 = jnp.dot(x, wu2[0], preferred_element_type=F32)
            a1 = (_silu(gt1) * up1).astype(BF16)
            y1 = jnp.dot(a1, wd1[0], preferred_element_type=F32)
            a2 = (_silu(gt2) * up2).astype(BF16)
            y2 = jnp.dot(a2, wd2[0], preferred_element_type=F32)
            s1 = jax.nn.sigmoid(jnp.sum(xf * rw1[0].astype(F32), axis=-1, keepdims=True))
            s2 = jax.nn.sigmoid(jnp.sum(xf * rw2[0].astype(F32), axis=-1, keepdims=True))
            den = s1 + s2
            ys_ref[rows] = _pack_bf16_pairs((s1 / den) * y1 + (s2 / den) * y2)

    @pl.when(j * FFN_TILES_PER_STEP >= nused_ref[0])
    def _():
        ys_ref[...] = jnp.zeros_like(ys_ref)


def _ffn(hs, e1, e2, nused, wg, wu, wd, rw3):
    npad, dh = hs.shape
    d = 2 * dh
    per = FFN_TILES_PER_STEP
    rows = per * FFN_TM
    de = wg.shape[-1]
    assert npad % rows == 0
    row = pl.BlockSpec((rows, dh), lambda j, e1, e2, nu: (jnp.minimum(j, (nu[0] - 1) // per), 0))
    w_specs = []
    for sub in range(per):
        for which in range(2):
            pick = lambda j, e1, e2, nu, sub=sub, which=which: ((e2 if which else e1)[per * j + sub], 0, 0)
            w_specs += [pl.BlockSpec((1, d, de), pick), pl.BlockSpec((1, d, de), pick),
                        pl.BlockSpec((1, de, d), pick), pl.BlockSpec((1, 1, d), pick)]
    grid_spec = pltpu.PrefetchScalarGridSpec(
        num_scalar_prefetch=3,
        grid=(npad // rows,),
        in_specs=[row] + w_specs,
        out_specs=pl.BlockSpec((rows, dh), lambda j, e1, e2, nu: (j, 0)),
    )
    return pl.pallas_call(
        _ffn_kernel,
        grid_spec=grid_spec,
        out_shape=jax.ShapeDtypeStruct((npad, dh), U32),
        compiler_params=_cparams("arbitrary"),
        name="expert_ffn",
    )(e1, e2, nused, hs, *([wg, wu, wd, rw3] * (2 * per)))


def _sc_mesh():
    return plsc.VectorSubcoreMesh(core_axis_name="c", subcore_axis_name="s",
                                  num_cores=SC_CORES_V7X, num_subcores=SC_SUBCORES_V7X)


def _sc_steps(n_rows):
    workers = SC_CORES_V7X * SC_SUBCORES_V7X
    per_w = n_rows // workers
    ch = min(SC_ROWS_PER_STEP, per_w // 2)
    n = per_w // ch
    assert per_w * workers == n_rows and n * ch == per_w and n % 2 == 0 and ch % 8 == 0
    return per_w, ch, n


def _sc_scratch(ch, d, dtype):
    return [pltpu.VMEM((2, ch), I32), pltpu.VMEM((2, ch, d), dtype),
            pltpu.SemaphoreType.DMA((2,)), pltpu.SemaphoreType.DMA((2,))]


def _sc_scatter_rows(src, idx, n_out):
    n_in, d = src.shape
    per_w, ch, n = _sc_steps(n_in)

    def body(src_hbm, idx_hbm, out_hbm, idx_v, rows_v, rsem, ssem):
        base = (lax.axis_index("s") * SC_CORES_V7X + lax.axis_index("c")) * per_w

        def read(i, b):
            return pltpu.make_async_copy(src_hbm.at[pl.ds(base + i * ch, ch)], rows_v.at[b], rsem.at[b])

        def scatter(b):
            return pltpu.make_async_copy(rows_v.at[b], out_hbm.at[idx_v.at[b]], ssem.at[b])

        def start_read(i, b):
            pltpu.sync_copy(idx_hbm.at[pl.ds(base + i * ch, ch)], idx_v.at[b])
            read(i, b).start()

        start_read(0, 0)

        @pl.loop(0, n, step=2)
        def _(i):
            for b in (0, 1):
                cur = i + b

                @pl.when(cur + 1 < n)
                def _():
                    @pl.when(cur >= 1)
                    def _():
                        scatter(1 - b).wait()
                    start_read(cur + 1, 1 - b)

                read(cur, b).wait()
                scatter(b).start()

        scatter(0).wait()
        scatter(1).wait()

    return pl.kernel(
        body, mesh=_sc_mesh(),
        out_type=jax.ShapeDtypeStruct((n_out, d), src.dtype),
        scratch_types=_sc_scratch(ch, d, src.dtype),
        name="sc_scatter_rows",
    )(src, idx)


def _sc_gather_rows(table, idx):
    n_out = idx.shape[0]
    d = table.shape[1]
    per_w, ch, n = _sc_steps(n_out)

    def body(table_hbm, idx_hbm, out_hbm, idx_v, rows_v, gsem, wsem):
        base = (lax.axis_index("s") * SC_CORES_V7X + lax.axis_index("c")) * per_w

        def gather(b):
            return pltpu.make_async_copy(table_hbm.at[idx_v.at[b]], rows_v.at[b], gsem.at[b])

        def write(i, b):
            return pltpu.make_async_copy(rows_v.at[b], out_hbm.at[pl.ds(base + i * ch, ch)], wsem.at[b])

        def start_gather(i, b):
            pltpu.sync_copy(idx_hbm.at[pl.ds(base + i * ch, ch)], idx_v.at[b])
            gather(b).start()

        start_gather(0, 0)

        @pl.loop(0, n, step=2)
        def _(i):
            for b in (0, 1):
                cur = i + b

                @pl.when(cur + 1 < n)
                def _():
                    @pl.when(cur >= 1)
                    def _():
                        write(cur - 1, 1 - b).wait()
                    start_gather(cur + 1, 1 - b)

                gather(b).wait()
                write(cur, b).start()

        write(n - 2, 0).wait()
        write(n - 1, 1).wait()

    return pl.kernel(
        body, mesh=_sc_mesh(),
        out_type=jax.ShapeDtypeStruct((n_out, d), table.dtype),
        scratch_types=_sc_scratch(ch, d, table.dtype),
        name="sc_gather_rows",
    )(table, idx)


def _moe(h2p, cls3, rank3, cnt, wg, wu, wd, rw3):
    bsz, s, dh = h2p.shape
    t = bsz * s
    tm = FFN_TM
    counts = cnt[:N_CLASSES, 0].astype(I32)
    padded = ((counts + tm - 1) // tm) * tm
    upto = jnp.arange(N_CLASSES)[:, None] >= jnp.arange(N_CLASSES)[None, :]
    ends = jnp.sum(jnp.where(upto, padded[None, :], 0), axis=1)
    base = ends - padded
    cls2 = cls3[:, 0, :]
    pos2 = rank3[:, 0, :]
    for c in range(N_CLASSES):
        pos2 = pos2 + jnp.where(cls2 == c, base[c], 0)
    pos = pos2.reshape(t)
    ntiles = t // tm + N_CLASSES
    tile_start = jnp.arange(ntiles, dtype=I32) * tm
    tcls = jnp.minimum(jnp.sum((ends[None, :] <= tile_start[:, None]).astype(I32), axis=1), N_CLASSES - 1)
    nused = (ends[-1:] // tm).astype(I32)
    grp = tcls // len(PAIRS)
    pr = tcls % len(PAIRS)
    first = sum(jnp.where(pr == i, p[0], 0) for i, p in enumerate(PAIRS))
    second = sum(jnp.where(pr == i, p[1], 0) for i, p in enumerate(PAIRS))
    e1 = (grp * PER_GROUP + first).astype(I32)
    e2 = (grp * PER_GROUP + second).astype(I32)
    hs = _sc_scatter_rows(h2p.reshape(t, dh), pos, ntiles * tm)
    ys = _ffn(hs, e1, e2, nused, wg, wu, wd, rw3)
    return _sc_gather_rows(ys, pos).reshape(bsz, s, dh)


def _qkv1_kernel(x_ref, y_ref, gfp_ref, g_ref, sc_ref, sh_ref, w_ref, cos_ref, sin_ref,
                 wg_ref, wu_ref, wd_ref, x2_ref, *rest):
    outs = rest[:9]
    _side_cast((wg_ref, wu_ref, wd_ref), rest[9:12])
    hb_scr, ysc, ysc2 = rest[12:]
    tm = x_ref.shape[1]
    x2 = x_ref[0] + gfp_ref[0] * _unpack_bf16_pairs(y_ref[0])
    x2_ref[0] = x2
    hb_scr[...] = _rms_mod(x2, g_ref[...], sc_ref[0], sh_ref[0]).astype(BF16)
    cos = cos_ref[0]
    sin = sin_ref[0]
    for gi, (_, dil) in reversed(list(enumerate(B_GROUPS))):
        for j in range(3):
            c = gi * 3 + j
            y = jnp.dot(hb_scr[...], w_ref[:, B_WIDTH * c:B_WIDTH * (c + 1)], preferred_element_type=F32)
            out = outs[c]
            for cc in range(B_WIDTH // 128):
                lanes = slice(128 * cc, 128 * (cc + 1))
                r = y[:, lanes]
                if j < 2:
                    r = _rope_apply_paired(r, cos, sin)
                if j == 0:
                    r = r * Q_SCALE
                if dil == 1:
                    out[0, 0, :, lanes] = r.astype(BF16)
                    continue
                ysc[cc] = r
                if dil % 16:
                    for rr in range(dil):
                        out[0, rr, :, lanes] = ysc[cc, pl.ds(rr, tm // dil, stride=dil), :].astype(BF16)
                    continue
                quarter = tm // 4
                inner = dil // 4
                for r1 in range(4):
                    ysc2[cc, r1 * quarter:(r1 + 1) * quarter] = ysc[cc, pl.ds(r1, quarter, stride=4), :]
                for r1 in range(4):
                    for r2 in range(inner):
                        out[0, r1 + 4 * r2, :, lanes] = ysc2[
                            cc, pl.ds(r1 * quarter + r2, tm // dil, stride=inner), :].astype(BF16)


def _qkv1(x, y, gfp, g, sc, sh, w, cos, sin, expert_ws, layer):
    bsz, s, d = x.shape
    tm = TM
    grid = (bsz, s // tm)
    w_in, w_out, w_shape, w_args = _side_cast_specs([(w_, layer) for w_ in expert_ws], grid)
    tok = lambda w_: pl.BlockSpec((1, tm, w_), lambda b, i: (b, i, 0))
    per_b = pl.BlockSpec((1, 1, d), lambda b, i: (b, 0, 0))
    out_specs = [tok(d)]
    out_shape = [jax.ShapeDtypeStruct((bsz, s, d), F32)]
    for _, dil in B_GROUPS:
        for _ in range(3):
            out_specs.append(pl.BlockSpec((1, dil, tm // dil, B_WIDTH), lambda b, i: (b, 0, i, 0)))
            out_shape.append(jax.ShapeDtypeStruct((bsz, dil, s // dil, B_WIDTH), BF16))
    outs = pl.pallas_call(
        _qkv1_kernel,
        grid=grid,
        in_specs=[tok(d), tok(d // 2), per_b, pl.BlockSpec((1, d), lambda b, i: (0, 0)), per_b, per_b,
                  pl.BlockSpec(w.shape, lambda b, i: (0, 0)), tok(128), tok(128)] + w_in,
        out_specs=out_specs + w_out,
        out_shape=out_shape + w_shape,
        scratch_shapes=[pltpu.VMEM((tm, d), BF16), pltpu.VMEM((B_WIDTH // 128, tm, 128), F32),
                        pltpu.VMEM((B_WIDTH // 128, tm, 128), F32)],
        compiler_params=_cparams("parallel", "parallel"),
        name="qkv1",
    )(x, y, gfp, g.reshape(1, d), sc, sh, w, cos, sin, *w_args)
    return outs[:10], [c.reshape(w_.shape[1:]) for c, w_ in zip(outs[10:], expert_ws)]


def _att1_window_pad(lseg):
    qb = min(2 * B_RADIUS, lseg)
    return -(qb + 2 * B_RADIUS) % 128


def _att1_kernel(*refs, chunk, seq):
    ins = refs[:21]
    o_ref = refs[21]
    scr = refs[22:]
    nat_o, nat_d, nat_m = scr[0:3], scr[3:6], scr[6:9]
    n = pl.program_id(1)
    rad = B_RADIUS
    npair = B_WIDTH // 128

    def window(cur, prev, nxt, res, lo, hi, lseg, lanes):
        parts = []
        if lo < 0:
            parts.append(prev[0, res, :, lanes])
        parts.append(cur[0, res, max(lo, 0):min(hi, lseg), lanes])
        if hi > lseg:
            parts.append(nxt[0, res, :, lanes])
        if hi > lseg + rad:
            parts.append(nxt[0, res, :, lanes])
        return jnp.concatenate(parts, axis=0) if len(parts) > 1 else parts[0]

    for c in range(npair):
        lanes = slice(128 * c, 128 * (c + 1))
        for gi, (_, dil) in enumerate(B_GROUPS):
            q_ref, kc, kp, kn, vc, vp, vn = ins[7 * gi:7 * gi + 7]
            lseg = chunk // dil
            qb = min(2 * rad, lseg)
            kb_n = qb + 2 * rad + _att1_window_pad(lseg)
            lstr = seq // dil
            rel = lax.broadcasted_iota(I32, (qb, kb_n), 1) - lax.broadcasted_iota(I32, (qb, kb_n), 0)
            band_bias = jnp.where((rel >= 0) & (rel <= 2 * rad), 0.0, NEG)
            jrow = lax.broadcasted_iota(I32, (1, kb_n), 1)
            lane = lax.broadcasted_iota(I32, (qb, 128), 1)
            first_head = lane < HEAD_DIM
            q_first = (lane % HEAD_DIM) < (HEAD_DIM // 2)
            zero = jnp.zeros((qb, 128), BF16)
            ones = jnp.ones((kb_n, 128), BF16)
            for res in range(dil):
                for q0 in range(0, lseg, qb):
                    apos = n * lseg + q0 - rad + jrow
                    bias = band_bias + jnp.where((apos >= 0) & (apos < lstr), 0.0, NEG)
                    rows = pl.ds(q0, qb) if dil == 1 else pl.ds(res + dil * q0, qb, stride=dil)
                    qc = q_ref[0, res, q0:q0 + qb, lanes]
                    kb = window(kc, kp, kn, res, q0 - rad, q0 - rad + kb_n, lseg, lanes)
                    vb = jnp.concatenate(
                        [window(vc, vp, vn, res, q0 - rad, q0 - rad + kb_n, lseg, lanes), ones], axis=1)
                    qs = jnp.concatenate(
                        [jnp.where(q_first, qc, zero), jnp.where(q_first, zero, qc)], axis=0)
                    s2 = lax.dot_general(qs, kb, (((1,), (1,)), ((), ())), preferred_element_type=F32)
                    ps, ms = [], []
                    for t2 in range(2):
                        s = s2[qb * t2:qb * (t2 + 1)] + bias
                        m = jnp.max(s, axis=-1, keepdims=True)
                        ps.append(jnp.exp2((s - m).astype(BF16)))
                        ms.append(m)
                    pv = jnp.dot(jnp.concatenate(ps, axis=0), vb, preferred_element_type=F32)
                    nat_o[gi][rows, :] = jnp.where(first_head, pv[0:qb, 0:128], pv[qb:2 * qb, 0:128])
                    nat_d[gi][rows, :] = jnp.where(first_head, pv[0:qb, 128:256], pv[qb:2 * qb, 128:256])
                    nat_m[gi][rows, :] = jnp.where(first_head, ms[0], ms[1])

        ms = [nat_m[gi][...] for gi in range(3)]
        mx = jnp.maximum(jnp.maximum(ms[0], ms[1]), ms[2])
        w = [jnp.exp2(x - mx) for x in ms]
        num = w[0] * nat_o[0][...] + w[1] * nat_o[1][...] + w[2] * nat_o[2][...]
        den = w[0] * nat_d[0][...] + w[1] * nat_d[1][...] + w[2] * nat_d[2][...]
        o_ref[0, :, lanes] = (num / den).astype(BF16)


def _att1(qkv):
    bsz = qkv[0].shape[0]
    seq = qkv[0].shape[1] * qkv[0].shape[2]
    chunk = ATT1_CHUNK
    rad = B_RADIUS
    in_specs, args, scratch = [], [], []
    for gi, (_, dil) in enumerate(B_GROUPS):
        q, k, v = qkv[3 * gi:3 * gi + 3]
        lseg = chunk // dil
        per = lseg // rad
        last = seq // dil // rad - 1
        cur = pl.BlockSpec((1, dil, lseg, B_WIDTH), lambda b, n: (b, 0, n, 0))
        prev = pl.BlockSpec((1, dil, rad, B_WIDTH),
                            lambda b, n, per=per: (b, 0, jnp.maximum(n * per - 1, 0), 0))
        nxt = pl.BlockSpec((1, dil, rad, B_WIDTH),
                           lambda b, n, per=per, last=last: (b, 0, jnp.minimum(n * per + per, last), 0))
        in_specs += [cur, cur, prev, nxt, cur, prev, nxt]
        args += [q, k, k, k, v, v, v]
    for _ in range(3):
        for _ in B_GROUPS:
            scratch.append(pltpu.VMEM((chunk, 128), F32))
    return pl.pallas_call(
        functools.partial(_att1_kernel, chunk=chunk, seq=seq),
        grid=(bsz, seq // chunk),
        in_specs=in_specs,
        out_specs=pl.BlockSpec((1, chunk, B_WIDTH), lambda b, n: (b, n, 0)),
        out_shape=jax.ShapeDtypeStruct((bsz, seq, B_WIDTH), BF16),
        scratch_shapes=scratch,
        compiler_params=_cparams("parallel", "parallel"),
        name="att1",
    )(*args)


def _final_kernel(x_ref, y_ref, gfp_ref, g_ref, o_ref):
    x = x_ref[0] + gfp_ref[0] * _unpack_bf16_pairs(y_ref[0])
    ms = jnp.mean(x * x, axis=-1, keepdims=True)
    o_ref[0] = (x * lax.rsqrt(ms + NORM_EPS)) * g_ref[...]


def _final(x, y, gfp, g):
    bsz, s, d = x.shape
    tm = min(FINAL_TM, s)
    tok = pl.BlockSpec((1, tm, d), lambda b, i: (b, i, 0))
    return pl.pallas_call(
        _final_kernel,
        grid=(bsz, s // tm),
        in_specs=[tok, pl.BlockSpec((1, tm, d // 2), lambda b, i: (b, i, 0)),
                  pl.BlockSpec((1, 1, d), lambda b, i: (b, 0, 0)),
                  pl.BlockSpec((1, d), lambda b, i: (0, 0))],
        out_specs=tok,
        out_shape=jax.ShapeDtypeStruct((bsz, s, d), F32),
        compiler_params=_cparams("parallel", "parallel"),
        name="final_norm",
    )(x, y, gfp, g.reshape(1, d))


def kernel(x, c, positions, ada_w, ada_b, norm_mix_g, norm_ffn_g, a_w_qkv, a_w_o, a_sink, b_w_qkv, b_w_o,
           router_w, router_bias, exp_w_gate, exp_w_up, exp_w_down, final_norm_g):
    bsz, s, d = x.shape
    assert d == D_MODEL and ada_w.shape[0] == 2 and s % ATT1_CHUNK == 0
    assert all(w_ // (2 * dil) == B_RADIUS for w_, dil in B_GROUPS)

    mod = _modulation(c, ada_w, ada_b)
    mods = [[mod[l][:, k * d:(k + 1) * d].reshape(bsz, 1, d) for k in range(6)] for l in range(2)]
    cos, sin = _rope_tables(positions)

    rw_t = router_w.T.astype(BF16)
    rows = jnp.array([PER_GROUP * g + j if g < N_EXPERT_GROUPS else 0
                      for j in range(PER_GROUP) for g in range(8)], I32)
    live = jnp.array([1.0 if g < N_EXPERT_GROUPS else 0.0
                      for j in range(PER_GROUP) for g in range(8)], F32)
    rw32 = (rw_t[rows].astype(F32) * live[:, None]).astype(BF16)
    rb32 = jnp.where(live > 0, router_bias.astype(F32)[rows], NEG).reshape(32, 1)
    rw3 = rw_t.reshape(N_EXPERTS, 1, d)
    expert_ws = (exp_w_gate, exp_w_up, exp_w_down)

    sh_m, sc_m, g_m, sh_f, sc_f, g_f = mods[0]
    nq, nkv = A_Q_HEADS * HEAD_DIM, A_KV_HEADS * HEAD_DIM
    w0 = a_w_qkv[0].astype(BF16)
    w0 = jnp.concatenate([_paired_layout(w0[:, :nq]), _paired_layout(w0[:, nq:nq + nkv]), w0[:, nq + nkv:]],
                         axis=1)
    q, k, v = _qkv0(x, norm_mix_g[0], sc_m, sh_m, w0, cos, sin)
    qk_chunks = tuple(c for c in range(b_w_qkv.shape[-1] // 128) if (c * 128 // B_WIDTH) % 3 < 2)
    o, cast = _att0(q, k, v, a_sink[0].astype(F32),
                    [(w_, 0) for w_ in expert_ws] + [(a_w_o, 0), (b_w_o, 0), (b_w_qkv, 0)],
                    ((), (), (), (), (), qk_chunks))
    experts, (a_wo, b_wo, b_wqkv) = cast[:3], cast[3:]
    x1, *routed = _post(o, a_wo, x, g_m, norm_ffn_g[0], sc_f, sh_f, rw32, rb32)
    y = _moe(*routed, *experts, rw3)
    g_f_prev = g_f

    sh_m, sc_m, g_m, sh_f, sc_f, g_f = mods[1]
    outs, experts = _qkv1(x1, y, g_f_prev, norm_mix_g[1], sc_m, sh_m, b_wqkv, cos, sin, expert_ws, 1)
    x2, qkv = outs[0], outs[1:]
    o = _att1(qkv)
    x3, *routed = _post(o, b_wo, x2, g_m, norm_ffn_g[1], sc_f, sh_f, rw32, rb32)
    y = _moe(*routed, *experts, rw3)

    return _final(x3, y, g_f, final_norm_g)
```
